```python
import math
import jax
import jax.numpy as jnp
from jax import lax
import numpy as np

D_MODEL = 1024
BATCH = 1
SEQ = 16384
DEPTH = 1
DEC_BATCH = 32
DEC_SEQ = 32
PAST_LEN = 4096

CHUNK = 64
N_META = 16
META_PAD = CHUNK - N_META
DN_HEADS = 8
DN_DK = 64
DN_DV = 64
DN_QK = DN_HEADS * DN_DK
DN_V = DN_HEADS * DN_DV
CONV_W = 4
CONV_CH = 2 * DN_QK + DN_V
SWA_HEADS = 8
SWA_KV = 2
SWA_G = SWA_HEADS // SWA_KV
SWA_HD = 64
WINDOW = 128
WIN_CHUNKS = WINDOW // CHUNK
N_BUCKETS = 32
MAX_DIST = 128
D_FF = 4 * D_MODEL
MIX_WIDTH = DN_V + SWA_HEADS * SWA_HD
OFF_Z = CONV_CH
OFF_A = OFF_Z + DN_V
OFF_B = OFF_A + DN_HEADS
OFF_SQ = OFF_B + DN_HEADS
OFF_SK = OFF_SQ + SWA_HEADS * SWA_HD
OFF_SV = OFF_SK + SWA_KV * SWA_HD
PROJ_WIDTH = OFF_SV + SWA_KV * SWA_HD
DEEP_ALPHA = (2 * DEPTH) ** 0.25
DEEP_BETA = (8 * DEPTH) ** -0.25
LN_EPS = 1e-5
NORM_EPS = 1e-6

kernel_name = 'hybrid_deltanet_swa_streaming_step'


def layer_norm(x, g, b):
    xf = x.astype(jnp.float32)
    mu = jnp.mean(xf, -1, keepdims=True)
    var = jnp.mean(jnp.square(xf - mu), -1, keepdims=True)
    return ((xf - mu) * lax.rsqrt(var + LN_EPS) * g.astype(jnp.float32) + b.astype(jnp.float32)).astype(x.dtype)


def l2_normalize(x):
    return x * lax.rsqrt(jnp.sum(jnp.square(x), -1, keepdims=True) + NORM_EPS)


def t5_bucket(rel):
    half = N_BUCKETS // 2
    max_exact = half // 2
    n = jnp.abs(rel)
    large = max_exact + (jnp.log(jnp.maximum(n, 1).astype(jnp.float32) / max_exact)
                         / math.log(MAX_DIST / max_exact) * (half - max_exact)).astype(jnp.int32)
    large = jnp.minimum(large, half - 1)
    return jnp.where(rel > 0, half, 0) + jnp.where(n < max_exact, n, large)


def rel_bias(table, n_q, n_k, key_offset):
    rel = (jnp.arange(n_k)[None, :] - key_offset) - jnp.arange(n_q)[:, None]
    b = table.astype(jnp.float32)[t5_bucket(rel)]
    return b.reshape(n_q, n_k, SWA_KV, SWA_G).transpose(2, 3, 0, 1)


def short_conv(x, buf, w):
    L = x.shape[1]
    xp = jnp.concatenate([buf.astype(x.dtype), x], axis=1)
    y = sum(xp[:, j:j + L] * w[j] for j in range(CONV_W))
    return jax.nn.silu(y), xp[:, -(CONV_W - 1):]


def gated_delta_rule(q, k, v, g, beta, S0, chunk):
    B, L, H, DK = q.shape
    DV = v.shape[-1]
    n = L // chunk

    def blk(t):
        return jnp.moveaxis(t.reshape((B, n, chunk, H) + t.shape[3:]), 3, 1)

    q, k, v, g, beta = blk(q), blk(k), blk(v), blk(g), blk(beta)
    G = jnp.cumsum(g, axis=-1)
    idx = jnp.arange(chunk)
    causal = idx[:, None] >= idx[None, :]
    strict = idx[:, None] > idx[None, :]
    decay = jnp.exp(jnp.where(causal, G[..., :, None] - G[..., None, :], -jnp.inf))
    kk = jnp.einsum('bhnid,bhnjd->bhnij', k, k)
    A = jnp.where(strict, beta[..., :, None] * kk * decay, 0.0)
    T = A + jnp.eye(chunk, dtype=A.dtype)
    W = lax.linalg.triangular_solve(T, (beta * jnp.exp(G))[..., None] * k,
                                    left_side=True, lower=True, unit_diagonal=True)
    U = lax.linalg.triangular_solve(T, beta[..., None] * v,
                                    left_side=True, lower=True, unit_diagonal=True)
    qk = jnp.einsum('bhnid,bhnjd->bhnij', q, k) * decay
    q_dec = q * jnp.exp(G)[..., None]
    k_dec = k * jnp.exp(G[..., -1:] - G)[..., None]
    g_tot = jnp.exp(G[..., -1])

    def step(S, xs):
        W_c, U_c, qk_c, qd_c, kd_c, gt_c = xs
        u = U_c - jnp.einsum('bhck,bhkv->bhcv', W_c, S)
        o = jnp.einsum('bhck,bhkv->bhcv', qd_c, S) + jnp.einsum('bhij,bhjv->bhiv', qk_c, u)
        S = gt_c[..., None, None] * S + jnp.einsum('bhck,bhcv->bhkv', kd_c, u)
        return S, o

    xs = tuple(jnp.moveaxis(t, 2, 0) for t in (W, U, qk, q_dec, k_dec, g_tot))
    S, o = lax.scan(step, S0, xs)
    o = jnp.moveaxis(jnp.moveaxis(o, 0, 2), 1, 3).reshape(B, L, H, DV)
    return o, S


def deltanet_mixer(proj, conv_buf, S0, w_conv, a_log, dt_bias, norm_w, front_pad, chunk):
    B, L, _ = proj.shape
    f32 = jnp.float32
    qkv, conv_state = short_conv(proj[..., :CONV_CH], conv_buf, w_conv)
    qkv = qkv.astype(f32)
    q = l2_normalize(qkv[..., :DN_QK].reshape(B, L, DN_HEADS, DN_DK)) * DN_DK ** -0.5
    k = l2_normalize(qkv[..., DN_QK:2 * DN_QK].reshape(B, L, DN_HEADS, DN_DK))
    v = qkv[..., 2 * DN_QK:].reshape(B, L, DN_HEADS, DN_DV)
    a = proj[..., OFF_A:OFF_B].astype(f32)
    b = proj[..., OFF_B:OFF_SQ].astype(f32)
    g = -jnp.exp(a_log.astype(f32)) * jax.nn.softplus(a + dt_bias.astype(f32))
    beta = jax.nn.sigmoid(b)

    def pad(t):
        return jnp.pad(t, ((0, 0), (front_pad, 0)) + ((0, 0),) * (t.ndim - 2))

    o, S = gated_delta_rule(pad(q), pad(k), pad(v), pad(g), pad(beta), S0.astype(f32), chunk)
    o = o[:, front_pad:]
    z = proj[..., OFF_Z:OFF_A].astype(f32).reshape(B, L, DN_HEADS, DN_DV)
    o = o * lax.rsqrt(jnp.mean(jnp.square(o), -1, keepdims=True) + NORM_EPS) \
        * norm_w.astype(f32) * jax.nn.silu(z)
    return o.reshape(B, L, DN_V).astype(proj.dtype), conv_state, S.astype(S0.dtype)


def sink_attention(q, k, v, bias, key_valid, sinks):
    s = jnp.einsum('bnqhgd,bnshd->bnhgqs', q, k).astype(jnp.float32) * SWA_HD ** -0.5 + bias
    s = jnp.where(key_valid[None, :, None, None, None, :], s, -jnp.inf)
    sink = jnp.broadcast_to(sinks.astype(jnp.float32).reshape(SWA_KV, SWA_G)[None, None, :, :, None, None],
                            s.shape[:-1] + (1,))
    p = jax.nn.softmax(jnp.concatenate([s, sink], axis=-1), axis=-1)[..., :-1]
    return jnp.einsum('bnhgqs,bnshd->bnqhgd', p.astype(v.dtype), v)


def swa_prompt(proj, bias, key_valid, sinks):
    B, Lr, _ = proj.shape
    L = Lr + META_PAD
    n = L // CHUNK
    span = WIN_CHUNKS * CHUNK
    q = jnp.pad(proj[..., OFF_SQ:OFF_SK], ((0, 0), (META_PAD, 0), (0, 0)))
    q = q.reshape(B, n, CHUNK, SWA_KV, SWA_G, SWA_HD)
    k = proj[..., OFF_SK:OFF_SV].reshape(B, Lr, SWA_KV, SWA_HD)
    v = proj[..., OFF_SV:PROJ_WIDTH].reshape(B, Lr, SWA_KV, SWA_HD)
    pads = ((0, 0), (META_PAD + span, 0), (0, 0), (0, 0))
    kp, vp = jnp.pad(k, pads), jnp.pad(v, pads)
    kb = jnp.concatenate([kp[:, j * CHUNK:j * CHUNK + L].reshape(B, n, CHUNK, SWA_KV, SWA_HD)
                          for j in range(WIN_CHUNKS + 1)], axis=2)
    vb = jnp.concatenate([vp[:, j * CHUNK:j * CHUNK + L].reshape(B, n, CHUNK, SWA_KV, SWA_HD)
                          for j in range(WIN_CHUNKS + 1)], axis=2)
    o = sink_attention(q, kb, vb, bias, key_valid, sinks).reshape(B, L, SWA_HEADS * SWA_HD)
    return o[:, META_PAD:], k[:, -WINDOW:], v[:, -WINDOW:]


def swa_sample(proj, cache_k, cache_v, bias, sinks):
    B, D, _ = proj.shape
    q = proj[..., OFF_SQ:OFF_SK].reshape(B, 1, D, SWA_KV, SWA_G, SWA_HD)
    k = proj[..., OFF_SK:OFF_SV].reshape(B, D, SWA_KV, SWA_HD)
    v = proj[..., OFF_SV:PROJ_WIDTH].reshape(B, D, SWA_KV, SWA_HD)
    kb = jnp.concatenate([cache_k.astype(k.dtype), k], axis=1)[:, None]
    vb = jnp.concatenate([cache_v.astype(v.dtype), v], axis=1)[:, None]
    valid = jnp.ones((1, kb.shape[2]), bool)
    o = sink_attention(q, kb, vb, bias, valid, sinks)
    return o.reshape(B, D, SWA_HEADS * SWA_HD), k, v


def post_sublayers(h, mix, w_o, ln1_g, ln1_b, w_ff1, w_ff2, ln2_g, ln2_b):
    h = layer_norm(DEEP_ALPHA * h + mix @ w_o, ln1_g, ln1_b)
    f = jnp.square(jax.nn.relu(h @ w_ff1)) @ w_ff2
    return layer_norm(DEEP_ALPHA * h + f, ln2_g, ln2_b)


def setup_inputs(seed: int = 0) -> dict:
    key = jax.random.key(seed)
    ks = jax.random.split(key, 24)
    f32 = jnp.float32

    def nrm(k, shape, s):
        return jax.random.normal(k, shape, f32) * s

    cache_len = min(WINDOW, PAST_LEN)
    dt = jnp.exp(jax.random.uniform(ks[10], (DEPTH, DN_HEADS), f32, math.log(1e-3), math.log(1e-1)))
    return {
        'x_prompt': nrm(ks[0], (BATCH, SEQ, D_MODEL), 1.0),
        'x_sample': nrm(ks[1], (DEC_BATCH, DEC_SEQ, D_MODEL), 1.0),
        'state_delta': nrm(ks[2], (DEPTH, DEC_BATCH, DN_HEADS, DN_DK, DN_DV), 0.125),
        'state_conv': nrm(ks[3], (DEPTH, DEC_BATCH, CONV_W - 1, CONV_CH), 1.0),
        'cache_swa_k': nrm(ks[4], (DEPTH, DEC_BATCH, cache_len, SWA_KV, SWA_HD), 1.0),
        'cache_swa_v': nrm(ks[5], (DEPTH, DEC_BATCH, cache_len, SWA_KV, SWA_HD), 1.0),
        'meta_tokens': nrm(ks[6], (N_META, D_MODEL), 1.0),
        'ln_in_g': 1.0 + nrm(ks[7], (D_MODEL,), 0.02),
        'ln_in_b': nrm(ks[8], (D_MODEL,), 0.02),
        'w_in': nrm(ks[9], (DEPTH, D_MODEL, PROJ_WIDTH), D_MODEL ** -0.5),
        'w_conv': nrm(ks[11], (DEPTH, CONV_W, CONV_CH), CONV_W ** -0.5),
        'dn_a_log': jnp.log(jax.random.uniform(ks[12], (DEPTH, DN_HEADS), f32, 1.0, 16.0)),
        'dn_dt_bias': dt + jnp.log(-jnp.expm1(-dt)),
        'dn_norm': 1.0 + nrm(ks[13], (DEPTH, DN_DV), 0.02),
        'swa_sinks': nrm(ks[14], (DEPTH, SWA_HEADS), 0.5),
        'rel_bias_table': nrm(ks[15], (N_BUCKETS, SWA_HEADS), 0.5),
        'w_o': nrm(ks[16], (DEPTH, MIX_WIDTH, D_MODEL), MIX_WIDTH ** -0.5 * DEEP_BETA),
        'ln1_g': 1.0 + nrm(ks[17], (DEPTH, D_MODEL), 0.02),
        'ln1_b': nrm(ks[18], (DEPTH, D_MODEL), 0.02),
        'w_ff1': nrm(ks[19], (DEPTH, D_MODEL, D_FF), D_MODEL ** -0.5),
        'w_ff2': nrm(ks[20], (DEPTH, D_FF, D_MODEL), D_FF ** -0.5 * DEEP_BETA),
        'ln2_g': 1.0 + nrm(ks[21], (DEPTH, D_MODEL), 0.02),
        'ln2_b': nrm(ks[22], (DEPTH, D_MODEL), 0.02),
    }


def reference(x_prompt, x_sample, state_delta, state_conv, cache_swa_k, cache_swa_v,
              meta_tokens, ln_in_g, ln_in_b, w_in, w_conv, dn_a_log, dn_dt_bias, dn_norm,
              swa_sinks, rel_bias_table, w_o, ln1_g, ln1_b, w_ff1, w_ff2, ln2_g, ln2_b):
    B = x_prompt.shape[0]
    D = x_sample.shape[1]
    Wc = cache_swa_k.shape[2]
    span = WIN_CHUNKS * CHUNK
    meta = jnp.broadcast_to(meta_tokens.astype(x_prompt.dtype)[None], (B, N_META, D_MODEL))
    hp = layer_norm(jnp.concatenate([meta, x_prompt], axis=1), ln_in_g, ln_in_b)
    hs = layer_norm(x_sample, ln_in_g, ln_in_b)
    n_blk = (hp.shape[1] + META_PAD) // CHUNK
    key_idx = jnp.arange(n_blk)[:, None] * CHUNK + jnp.arange(span + CHUNK)[None, :] - span
    prompt_key_valid = key_idx >= META_PAD
    bias_prompt = rel_bias(rel_bias_table, CHUNK, span + CHUNK, span)
    bias_sample = rel_bias(rel_bias_table, D, Wc + D, Wc)
    zero_conv = jnp.zeros((B, CONV_W - 1, CONV_CH), state_conv.dtype)
    zero_S = jnp.zeros((B, DN_HEADS, DN_DK, DN_DV), state_delta.dtype)
    p_S, p_conv, p_k, p_v, s_S, s_conv, s_k, s_v = [], [], [], [], [], [], [], []
    for l in range(DEPTH):
        proj = hp @ w_in[l]
        dn_o, conv_new, S_new = deltanet_mixer(proj, zero_conv, zero_S, w_conv[l], dn_a_log[l],
                                               dn_dt_bias[l], dn_norm[l], META_PAD, CHUNK)
        sw_o, k_rows, v_rows = swa_prompt(proj, bias_prompt, prompt_key_valid, swa_sinks[l])
        hp = post_sublayers(hp, jnp.concatenate([dn_o, sw_o], axis=-1), w_o[l],
                            ln1_g[l], ln1_b[l], w_ff1[l], w_ff2[l], ln2_g[l], ln2_b[l])
        p_S.append(S_new); p_conv.append(conv_new); p_k.append(k_rows); p_v.append(v_rows)
        proj = hs @ w_in[l]
        dn_o, conv_new, S_new = deltanet_mixer(proj, state_conv[l], state_delta[l], w_conv[l], dn_a_log[l],
                                               dn_dt_bias[l], dn_norm[l], 0, D)
        sw_o, k_rows, v_rows = swa_sample(proj, cache_swa_k[l], cache_swa_v[l], bias_sample, swa_sinks[l])
        hs = post_sublayers(hs, jnp.concatenate([dn_o, sw_o], axis=-1), w_o[l],
                            ln1_g[l], ln1_b[l], w_ff1[l], w_ff2[l], ln2_g[l], ln2_b[l])
        s_S.append(S_new); s_conv.append(conv_new); s_k.append(k_rows); s_v.append(v_rows)
    y_prompt = hp[:, N_META:]
    y_sample = hs
    return (y_prompt, y_sample, jnp.stack(p_S), jnp.stack(p_conv), jnp.stack(p_k), jnp.stack(p_v),
            jnp.stack(s_S), jnp.stack(s_conv), jnp.stack(s_k), jnp.stack(s_v))
```

```python
import functools
import math

import jax
import jax.numpy as jnp
from jax import lax
from jax.experimental import pallas as pl
from jax.experimental.pallas import tpu as pltpu

F32 = jnp.float32
BF16 = jnp.bfloat16

D_MODEL = 1024
N_META = 16
HEADS = 8
HEAD_DIM = 64
DN_W = HEADS * HEAD_DIM
CONV_W = 4
CONV_CH = 3 * DN_W
SWA_HEADS = 8
SWA_KV = 2
SWA_G = SWA_HEADS // SWA_KV
KV_W = SWA_KV * HEAD_DIM
WINDOW = 128
N_BUCKETS = 32
MAX_DIST = 128
D_FF = 4 * D_MODEL
DEPTH = 1
DEEP_ALPHA = (2 * DEPTH) ** 0.25
LN_EPS = 1e-5
NORM_EPS = 1e-6
LANES = 128
VMEM_LIMIT = 56 * 1024 * 1024

OFF_Z = CONV_CH
OFF_A = OFF_Z + DN_W
OFF_B = OFF_A + HEADS
OFF_SQ = OFF_B + HEADS
OFF_SK = OFF_SQ + SWA_HEADS * HEAD_DIM
OFF_SV = OFF_SK + KV_W
PROJ_WIDTH = OFF_SV + KV_W
PROJ_SPLITS = (CONV_CH, DN_W, SWA_HEADS * HEAD_DIM, KV_W, KV_W, LANES)
PROJ_PAD_W = sum(PROJ_SPLITS)


def _layer_norm(x, g, b):
    mu = jnp.mean(x, axis=-1, keepdims=True)
    xc = x - mu
    var = jnp.mean(xc * xc, axis=-1, keepdims=True)
    return xc * lax.rsqrt(var + LN_EPS) * g + b


def _dot(a, b):
    return jnp.dot(a.astype(BF16), b.astype(BF16), preferred_element_type=F32)


def _dot_nt(a, b):
    return lax.dot_general(a.astype(BF16), b.astype(BF16), (((1,), (1,)), ((), ())),
                           preferred_element_type=F32)


def _split3(x):
    x1 = x.astype(BF16)
    r1 = x - x1.astype(F32)
    x2 = r1.astype(BF16)
    x3 = (r1 - x2.astype(F32)).astype(BF16)
    return x1, x2, x3


def _dot_exact_lhs(x, sel):
    m = x.shape[0]
    stacked = jnp.concatenate(_split3(x), axis=0)
    r = jnp.dot(stacked, sel, preferred_element_type=F32)
    return r[:m] + r[m:2 * m] + r[2 * m:]


def _proj_kernel(x_ref, g_ref, b_ref, w_ref, *out_refs):
    h = _layer_norm(x_ref[...], g_ref[...], b_ref[...]).astype(BF16)
    off = 0
    for o_ref, width in zip(out_refs, PROJ_SPLITS):
        o_ref[...] = jnp.dot(h, w_ref[:, off:off + width], preferred_element_type=F32)
        off += width


def _proj(x, g, b, w_r, tm):
    rows = x.shape[0]
    const = lambda i: (0, 0)
    row_blk = lambda width: pl.BlockSpec((tm, width), lambda i: (i, 0))
    return pl.pallas_call(
        _proj_kernel,
        out_shape=[jax.ShapeDtypeStruct((rows, width), F32) for width in PROJ_SPLITS],
        grid=(rows // tm,),
        in_specs=[row_blk(D_MODEL), pl.BlockSpec((1, D_MODEL), const),
                  pl.BlockSpec((1, D_MODEL), const), pl.BlockSpec((D_MODEL, PROJ_PAD_W), const)],
        out_specs=[row_blk(width) for width in PROJ_SPLITS],
        compiler_params=pltpu.CompilerParams(dimension_semantics=("arbitrary",),
                                             vmem_limit_bytes=VMEM_LIMIT),
        name="ln_in_proj",
    )(x, g, b, w_r)


def _sigmoid(x):
    return 1.0 / (1.0 + jnp.exp(-x))


def _softplus(x):
    return jnp.maximum(x, 0.0) + jnp.log1p(jnp.exp(-jnp.abs(x)))


def _delta_kernel(qkv_ref, z_ref, ab_ref, conv0_ref, s0_ref, wconv_ref, alog_ref, dtb_ref,
                  normw_ref, seg_ref, expand_ref, o_ref, s_ref, xbuf, obuf, *, chunk):
    C = chunk
    c = pl.program_id(1)

    @pl.when(c == 0)
    def _():
        xbuf[0:8, :] = conv0_ref[...]
        s_ref[...] = s0_ref[...]

    xbuf[8:8 + C, :] = qkv_ref[...]
    y = xbuf[8:8 + C, :] * wconv_ref[CONV_W - 1:CONV_W, :]
    for j in range(CONV_W - 1):
        y = y + xbuf[5 + j:5 + j + C, :] * wconv_ref[j:j + 1, :]
    xbuf[0:8, :] = xbuf[C:C + 8, :]
    y = y * _sigmoid(y)
    q = y[:, :DN_W]
    k = y[:, DN_W:2 * DN_W]
    v = y[:, 2 * DN_W:]

    seg = seg_ref[...]
    expand = expand_ref[...]

    def seg_sum(t):
        hi = t.astype(BF16)
        lo = (t - hi.astype(F32)).astype(BF16)
        r = jnp.dot(jnp.concatenate([hi, lo], axis=0), seg, preferred_element_type=F32)
        return r[:C] + r[C:]

    qn = q * (lax.rsqrt(seg_sum(q * q) + NORM_EPS) * HEAD_DIM ** -0.5)
    kn = k * lax.rsqrt(seg_sum(k * k) + NORM_EPS)

    ab = ab_ref[...]
    g = -jnp.exp(alog_ref[...]) * _softplus(ab + dtb_ref[...])
    beta = _sigmoid(pltpu.roll(ab, LANES - HEADS, axis=1))
    row = lax.broadcasted_iota(jnp.int32, (C, C), 0)
    col = lax.broadcasted_iota(jnp.int32, (C, C), 1)
    causal = row >= col
    strict = row > col
    eye = (row == col).astype(F32)
    tril = causal.astype(BF16)
    g1, g2, g3 = _split3(g)
    gs = jnp.dot(tril, jnp.concatenate([g1, g2, g3], axis=1), preferred_element_type=F32)
    G = gs[:, :LANES] + gs[:, LANES:2 * LANES] + gs[:, 2 * LANES:]
    GT = G.T
    g_last = G[C - 1:C, :]
    e_g = jnp.exp(G)
    e_rem = jnp.exp(g_last - G)
    G_x = _dot_exact_lhs(G, expand)
    beta_x = _dot_exact_lhs(beta, expand)
    eg_x = _dot_exact_lhs(e_g, expand)
    erem_x = _dot_exact_lhs(e_rem, expand)

    kbeta = kn * beta_x
    bk = kbeta * eg_x
    bv = v * beta_x
    qd = qn * eg_x
    kd = kn * erem_x

    n_sq = int(math.log2(C)) - 1
    for h in range(HEADS):
        sl = slice(h * HEAD_DIM, (h + 1) * HEAD_DIM)
        k_h = kn[:, sl]
        diff = G_x[:, h * HEAD_DIM:h * HEAD_DIM + C] - GT[h:h + 1, :]
        decay = jnp.exp(jnp.where(causal, diff, -jnp.inf))
        A = jnp.where(strict, _dot_nt(kbeta[:, sl], k_h) * decay, 0.0)
        P = eye - A
        Ap = A
        for _ in range(n_sq):
            Ap = _dot(Ap, Ap)
            P = P + _dot(P, Ap)
        W = _dot(P, bk[:, sl])
        U = _dot(P, bv[:, sl])
        qk = _dot_nt(qn[:, sl], k_h) * decay
        S = s_ref[h]
        u = U - _dot(W, S)
        obuf[:, sl] = _dot(qd[:, sl], S) + _dot(qk, u)
        s_ref[h] = eg_x[C - 1:C, sl] * S + _dot(kd[:, sl].T, u)

    o = obuf[...]
    ms = seg_sum(o * o) * (1.0 / HEAD_DIM)
    z = z_ref[...]
    o_ref[...] = (o * lax.rsqrt(ms + NORM_EPS) * normw_ref[...] * (z * _sigmoid(z))).astype(BF16)


def _delta(qkv, z, ab, conv0p, s0, consts, n_seq, n_chunk, chunk):
    wconv, alog, dtb, normw, seg, expand = consts
    rows = n_seq * n_chunk * chunk
    tok = lambda width: pl.BlockSpec((chunk, width), lambda s, c: (s * n_chunk + c, 0))
    const2 = lambda shape: pl.BlockSpec(shape, lambda s, c: (0, 0))
    state = pl.BlockSpec((None, HEADS, HEAD_DIM, HEAD_DIM), lambda s, c: (s, 0, 0, 0))
    return pl.pallas_call(
        functools.partial(_delta_kernel, chunk=chunk),
        out_shape=[jax.ShapeDtypeStruct((rows, DN_W), BF16),
                   jax.ShapeDtypeStruct((n_seq, HEADS, HEAD_DIM, HEAD_DIM), F32)],
        grid=(n_seq, n_chunk),
        in_specs=[tok(CONV_CH), tok(DN_W), tok(LANES),
                  pl.BlockSpec((None, 8, CONV_CH), lambda s, c: (s, 0, 0)), state,
                  const2((8, CONV_CH)), const2((1, LANES)), const2((1, LANES)),
                  const2((1, DN_W)), const2((DN_W, DN_W)), const2((LANES, DN_W))],
        out_specs=[tok(DN_W), state],
        scratch_shapes=[pltpu.VMEM((8 + chunk, CONV_CH), F32), pltpu.VMEM((chunk, DN_W), F32)],
        compiler_params=pltpu.CompilerParams(dimension_semantics=("arbitrary", "arbitrary"),
                                             vmem_limit_bytes=VMEM_LIMIT),
        name=f"delta_c{chunk}",
    )(qkv, z, ab, conv0p, s0, wconv, alog, dtb, normw, seg, expand)


def _swa_kernel(*refs, n_q, key_rows, first_valid):
    n_parts = len(key_rows)
    table_ref, sinks_ref, bucket_ref, q_ref = refs[:4]
    k_refs = refs[4:4 + n_parts]
    v_refs = refs[4 + n_parts:4 + 2 * n_parts]
    o_ref, bias_ref, obuf = refs[4 + 2 * n_parts:]
    n_k = sum(key_rows)
    step = pl.program_id(0)

    @pl.when(step == 0)
    def _():
        bucket = bucket_ref[...]
        for h in range(SWA_HEADS):
            acc = jnp.zeros((n_q, n_k), F32)
            for b in range(N_BUCKETS):
                acc = jnp.where(bucket == b, table_ref[b * SWA_HEADS + h], acc)
            kv, gi = divmod(h, SWA_G)
            bias_ref[kv, gi * n_q:(gi + 1) * n_q, :] = acc

    q = q_ref[...] * HEAD_DIM ** -0.5
    kcat = jnp.concatenate([r[...] for r in k_refs], axis=0)
    vcat = jnp.concatenate([r[...] for r in v_refs], axis=0)
    if first_valid is not None:
        key_pos = step * n_q + lax.broadcasted_iota(jnp.int32, (1, n_k), 1)
        valid = key_pos >= first_valid
    for kv in range(SWA_KV):
        heads = range(kv * SWA_G, (kv + 1) * SWA_G)
        q4 = jnp.concatenate([q[:, h * HEAD_DIM:(h + 1) * HEAD_DIM] for h in heads], axis=0)
        sink = jnp.concatenate([jnp.full((n_q, 1), sinks_ref[h], F32) for h in heads], axis=0)
        s = _dot_nt(q4, kcat[:, kv * HEAD_DIM:(kv + 1) * HEAD_DIM]) + bias_ref[kv]
        if first_valid is not None:
            s = jnp.where(valid, s, -jnp.inf)
        m = jnp.maximum(jnp.max(s, axis=-1, keepdims=True), sink)
        p = jnp.exp(s - m)
        denom = jnp.sum(p, axis=-1, keepdims=True) + jnp.exp(sink - m)
        o = _dot(p, vcat[:, kv * HEAD_DIM:(kv + 1) * HEAD_DIM]) / denom
        for gi, h in enumerate(heads):
            obuf[:, h * HEAD_DIM:(h + 1) * HEAD_DIM] = o[gi * n_q:(gi + 1) * n_q, :]
    o_ref[...] = obuf[...].astype(BF16)


def _swa(table, sinks, bucket, q, k_parts, v_parts, key_specs, key_rows, n_steps, n_q,
         first_valid, name):
    n_k = sum(key_rows)
    smem = pl.BlockSpec(memory_space=pltpu.SMEM)
    return pl.pallas_call(
        functools.partial(_swa_kernel, n_q=n_q, key_rows=key_rows, first_valid=first_valid),
        out_shape=jax.ShapeDtypeStruct((n_steps * n_q, SWA_HEADS * HEAD_DIM), BF16),
        grid=(n_steps,),
        in_specs=[smem, smem, pl.BlockSpec((n_q, n_k), lambda i: (0, 0)),
                  pl.BlockSpec((n_q, SWA_HEADS * HEAD_DIM), lambda i: (i, 0))]
                 + list(key_specs) + list(key_specs),
        out_specs=pl.BlockSpec((n_q, SWA_HEADS * HEAD_DIM), lambda i: (i, 0)),
        scratch_shapes=[pltpu.VMEM((SWA_KV, SWA_G * n_q, n_k), F32),
                        pltpu.VMEM((n_q, SWA_HEADS * HEAD_DIM), F32)],
        compiler_params=pltpu.CompilerParams(dimension_semantics=("arbitrary",),
                                             vmem_limit_bytes=VMEM_LIMIT),
        name=name,
    )(table, sinks, bucket, q, *k_parts, *v_parts)


def _t5_bucket(rel):
    half = N_BUCKETS // 2
    max_exact = half // 2
    n = jnp.abs(rel)
    large = max_exact + (jnp.log(jnp.maximum(n, 1).astype(F32) / max_exact)
                         / math.log(MAX_DIST / max_exact) * (half - max_exact)).astype(jnp.int32)
    large = jnp.minimum(large, half - 1)
    return jnp.where(rel > 0, half, 0) + jnp.where(n < max_exact, n, large)


def _bucket_map(n_q, n_k, key_offset):
    rel = (jnp.arange(n_k)[None, :] - key_offset) - jnp.arange(n_q)[:, None]
    return _t5_bucket(rel).astype(jnp.int32)


FF_BLOCK = 1024


def _post_kernel(x_ref, dn_ref, sw_ref, gin_ref, bin_ref, wo_ref, g1_ref, b1_ref,
                 w1_ref, w2_ref, g2_ref, b2_ref, y_ref):
    h = _layer_norm(x_ref[...], gin_ref[...], bin_ref[...])
    mix = (jnp.dot(dn_ref[...], wo_ref[:DN_W, :], preferred_element_type=F32)
           + jnp.dot(sw_ref[...], wo_ref[DN_W:, :], preferred_element_type=F32))
    h1 = _layer_norm(DEEP_ALPHA * h + mix, g1_ref[...], b1_ref[...])
    h1b = h1.astype(BF16)
    f = jnp.zeros_like(h1)
    for j in range(D_FF // FF_BLOCK):
        a = jnp.dot(h1b, w1_ref[:, j * FF_BLOCK:(j + 1) * FF_BLOCK], preferred_element_type=F32)
        a = jnp.square(jnp.maximum(a, 0.0)).astype(BF16)
        f = f + jnp.dot(a, w2_ref[j * FF_BLOCK:(j + 1) * FF_BLOCK, :], preferred_element_type=F32)
    y_ref[...] = _layer_norm(DEEP_ALPHA * h1 + f, g2_ref[...], b2_ref[...])


def _post(x, dn, sw, vecs, wo, w1, w2, tm):
    gin, bin_, g1, b1, g2, b2 = vecs
    rows = x.shape[0]
    const = lambda i: (0, 0)
    vec = pl.BlockSpec((1, D_MODEL), const)
    weight = lambda shape: pl.BlockSpec(shape, const, pipeline_mode=pl.Buffered(1))
    row_blk = lambda width: pl.BlockSpec((tm, width), lambda i: (i, 0))
    return pl.pallas_call(
        _post_kernel,
        out_shape=jax.ShapeDtypeStruct((rows, D_MODEL), F32),
        grid=(rows // tm,),
        in_specs=[row_blk(D_MODEL), row_blk(DN_W), row_blk(DN_W), vec, vec,
                  weight((2 * DN_W, D_MODEL)), vec, vec,
                  weight((D_MODEL, D_FF)), weight((D_FF, D_MODEL)), vec, vec],
        out_specs=row_blk(D_MODEL),
        compiler_params=pltpu.CompilerParams(dimension_semantics=("arbitrary",),
                                             vmem_limit_bytes=VMEM_LIMIT),
        name="post_mlp",
    )(x, dn, sw, gin, bin_, wo, g1, b1, w1, w2, g2, b2)


def kernel(x_prompt, x_sample, state_delta, state_conv, cache_swa_k, cache_swa_v, meta_tokens, ln_in_g, ln_in_b, w_in, w_conv, dn_a_log, dn_dt_bias, dn_norm, swa_sinks, rel_bias_table, w_o, ln1_g, ln1_b, w_ff1, w_ff2, ln2_g, ln2_b):
    assert w_in.shape[0] == DEPTH == 1
    n_seq_s, len_s = x_sample.shape[0], x_sample.shape[1]
    len_p = x_prompt.shape[1]
    cache_len = cache_swa_k.shape[2]
    chunk_p = 64
    row = lambda t: t.reshape(1, -1).astype(F32)

    w = w_in[0]
    w_r = jnp.concatenate(
        [w[:, :OFF_A], w[:, OFF_SQ:], w[:, OFF_A:OFF_SQ],
         jnp.zeros((D_MODEL, LANES - 2 * HEADS), w.dtype)], axis=1).astype(BF16)
    gin, bin_ = row(ln_in_g), row(ln_in_b)
    pad_lanes = lambda t: jnp.pad(t.reshape(1, -1).astype(F32), ((0, 0), (0, LANES - HEADS)))
    lane_head = jnp.arange(DN_W) // HEAD_DIM
    seg = (lane_head[:, None] == lane_head[None, :]).astype(BF16)
    expand = (jnp.arange(LANES)[:, None] == lane_head[None, :]).astype(BF16)
    delta_consts = (jnp.pad(w_conv[0].astype(F32), ((0, 8 - CONV_W), (0, 0))),
                    pad_lanes(dn_a_log[0]), pad_lanes(dn_dt_bias[0]),
                    jnp.tile(dn_norm[0].astype(F32), HEADS).reshape(1, DN_W), seg, expand)
    post_vecs = (gin, bin_, row(ln1_g[0]), row(ln1_b[0]), row(ln2_g[0]), row(ln2_b[0]))
    wo_b, w1_b, w2_b = w_o[0].astype(BF16), w_ff1[0].astype(BF16), w_ff2[0].astype(BF16)
    table = rel_bias_table.astype(F32).reshape(-1)
    sinks = swa_sinks[0].astype(F32)

    xp = x_prompt[0]
    xs = x_sample.reshape(n_seq_s * len_s, D_MODEL)

    p_qkv, p_z, p_sq, p_sk, p_sv, p_ab = _proj(xp, gin, bin_, w_r, 512)
    s_qkv, s_z, s_sq, s_sk, s_sv, s_ab = _proj(xs, gin, bin_, w_r, 512)
    m_qkv, m_z, m_sq, m_sk, m_sv, m_ab = _proj(meta_tokens.astype(F32), gin, bin_, w_r, N_META)

    zero_conv = jnp.zeros((1, 8, CONV_CH), F32)
    zero_s = jnp.zeros((1, HEADS, HEAD_DIM, HEAD_DIM), F32)
    _, s_meta = _delta(m_qkv, m_z, m_ab, zero_conv, zero_s, delta_consts, 1, 1, N_META)
    conv_meta = jnp.pad(m_qkv[-(CONV_W - 1):], ((8 - (CONV_W - 1), 0), (0, 0)))[None]
    p_dn, p_s = _delta(p_qkv, p_z, p_ab, conv_meta, s_meta, delta_consts, 1, len_p // chunk_p, chunk_p)
    conv_s = jnp.pad(state_conv[0].astype(F32), ((0, 0), (8 - (CONV_W - 1), 0), (0, 0)))
    s_dn, s_s = _delta(s_qkv, s_z, s_ab, conv_s, state_delta[0].astype(F32), delta_consts,
                       n_seq_s, 1, len_s)

    span = WINDOW
    lead = jnp.zeros((span - N_META, KV_W), F32)
    kfr = jnp.concatenate([lead, m_sk, p_sk], axis=0)
    vfr = jnp.concatenate([lead, m_sv, p_sv], axis=0)
    n_chunk_p = len_p // chunk_p
    key_specs_p = [pl.BlockSpec((chunk_p, KV_W), functools.partial(lambda i, j: (i + j, 0), j=j))
                   for j in range(3)]
    p_sw = _swa(table, sinks, _bucket_map(chunk_p, span + chunk_p, span), p_sq,
                [kfr] * 3, [vfr] * 3, key_specs_p, (chunk_p,) * 3, n_chunk_p, chunk_p,
                span - N_META, "swa_prompt")
    ck = cache_swa_k[0].astype(F32).reshape(n_seq_s * cache_len, KV_W)
    cv = cache_swa_v[0].astype(F32).reshape(n_seq_s * cache_len, KV_W)
    key_specs_s = [pl.BlockSpec((cache_len, KV_W), lambda i: (i, 0)),
                   pl.BlockSpec((len_s, KV_W), lambda i: (i, 0))]
    s_sw = _swa(table, sinks, _bucket_map(len_s, cache_len + len_s, cache_len), s_sq,
                [ck, s_sk], [cv, s_sv], key_specs_s, (cache_len, len_s), n_seq_s, len_s,
                None, "swa_sample")

    y_p = _post(xp, p_dn, p_sw, post_vecs, wo_b, w1_b, w2_b, 512)
    y_s = _post(xs, s_dn, s_sw, post_vecs, wo_b, w1_b, w2_b, 512)

    kv_shape = lambda n, length: (1, n, length, SWA_KV, HEAD_DIM)
    return (y_p[None], y_s.reshape(x_sample.shape),
            p_s[None], p_qkv[-(CONV_W - 1):][None, None],
            p_sk[-WINDOW:].reshape(kv_shape(1, WINDOW)), p_sv[-WINDOW:].reshape(kv_shape(1, WINDOW)),
            s_s[None], s_qkv.reshape(n_seq_s, len_s, CONV_CH)[:, -(CONV_W - 1):][None],
            s_sk.reshape(kv_shape(n_seq_s, len_s)), s_sv.reshape(kv_shape(n_seq_s, len_s)))
```

```python
import functools
import math

import jax
import jax.numpy as jnp
from jax import lax
from jax.experimental import pallas as pl
from jax.experimental.pallas import tpu as pltpu

F32 = jnp.float32
BF16 = jnp.bfloat16

D_MODEL = 1024
N_META = 16
HEADS = 8
HEAD_DIM = 64
DN_W = HEADS * HEAD_DIM
CONV_W = 4
CONV_CH = 3 * DN_W
SWA_HEADS = 8
SWA_KV = 2
SWA_G = SWA_HEADS // SWA_KV
KV_W = SWA_KV * HEAD_DIM
WINDOW = 128
N_BUCKETS = 32
MAX_DIST = 128
D_FF = 4 * D_MODEL
DEPTH = 1
DEEP_ALPHA = (2 * DEPTH) ** 0.25
LN_EPS = 1e-5
NORM_EPS = 1e-6
LANES = 128
VMEM_LIMIT = 56 * 1024 * 1024

OFF_Z = CONV_CH
OFF_A = OFF_Z + DN_W
OFF_B = OFF_A + HEADS
OFF_SQ = OFF_B + HEADS
OFF_SK = OFF_SQ + SWA_HEADS * HEAD_DIM
OFF_SV = OFF_SK + KV_W
PROJ_WIDTH = OFF_SV + KV_W
PROJ_SPLITS = (CONV_CH, DN_W, SWA_HEADS * HEAD_DIM, KV_W, KV_W, LANES)
PROJ_PAD_W = sum(PROJ_SPLITS)


def _layer_norm(x, g, b):
    mu = jnp.mean(x, axis=-1, keepdims=True)
    xc = x - mu
    var = jnp.mean(xc * xc, axis=-1, keepdims=True)
    return xc * lax.rsqrt(var + LN_EPS) * g + b


def _dot(a, b):
    return jnp.dot(a.astype(BF16), b.astype(BF16), preferred_element_type=F32)


def _dot_nt(a, b):
    return lax.dot_general(a.astype(BF16), b.astype(BF16), (((1,), (1,)), ((), ())),
                           preferred_element_type=F32)


def _split3(x):
    x1 = x.astype(BF16)
    r1 = x - x1.astype(F32)
    x2 = r1.astype(BF16)
    x3 = (r1 - x2.astype(F32)).astype(BF16)
    return x1, x2, x3


def _dot_exact_lhs(x, sel):
    m = x.shape[0]
    stacked = jnp.concatenate(_split3(x), axis=0)
    r = jnp.dot(stacked, sel, preferred_element_type=F32)
    return r[:m] + r[m:2 * m] + r[2 * m:]


def _proj_kernel(x_ref, g_ref, b_ref, w_ref, *out_refs):
    h = _layer_norm(x_ref[...], g_ref[...], b_ref[...]).astype(BF16)
    off = 0
    for o_ref, width in zip(out_refs, PROJ_SPLITS):
        o_ref[...] = jnp.dot(h, w_ref[:, off:off + width], preferred_element_type=F32)
        off += width


def _proj(x, g, b, w_r, tm):
    rows = x.shape[0]
    const = lambda i: (0, 0)
    row_blk = lambda width: pl.BlockSpec((tm, width), lambda i: (i, 0))
    return pl.pallas_call(
        _proj_kernel,
        out_shape=[jax.ShapeDtypeStruct((rows, width), F32) for width in PROJ_SPLITS],
        grid=(rows // tm,),
        in_specs=[row_blk(D_MODEL), pl.BlockSpec((1, D_MODEL), const),
                  pl.BlockSpec((1, D_MODEL), const), pl.BlockSpec((D_MODEL, PROJ_PAD_W), const)],
        out_specs=[row_blk(width) for width in PROJ_SPLITS],
        compiler_params=pltpu.CompilerParams(dimension_semantics=("arbitrary",),
                                             vmem_limit_bytes=VMEM_LIMIT),
        name="ln_in_proj",
    )(x, g, b, w_r)


def _sigmoid(x):
    return 1.0 / (1.0 + jnp.exp(-x))


def _softplus(x):
    return jnp.maximum(x, 0.0) + jnp.log1p(jnp.exp(-jnp.abs(x)))


def _delta_kernel(qkv_ref, z_ref, ab_ref, conv0_ref, s0_ref, wconv_ref, alog_ref, dtb_ref,
                  normw_ref, seg_ref, expand_ref, o_ref, s_ref, xbuf, obuf, sbd,
                  *, chunk, units, sequential):
    C, U = chunk, units
    R = U * C
    PAIRS = HEADS // 2
    PW = 2 * HEAD_DIM
    step = pl.program_id(0)
    last_step = pl.num_programs(0) - 1

    w_last = wconv_ref[CONV_W - 1:CONV_W, :]
    if sequential:
        @pl.when(step == 0)
        def _():
            xbuf[0:8, :] = conv0_ref[...]
        xbuf[8:8 + R, :] = qkv_ref[...]
        y = xbuf[8:8 + R, :] * w_last
        for j in range(CONV_W - 1):
            y = y + xbuf[5 + j:5 + j + R, :] * wconv_ref[j:j + 1, :]
        xbuf[0:8, :] = xbuf[R:R + 8, :]
    else:
        ys = []
        for u in range(U):
            xbuf[u, 0:8, :] = conv0_ref[u]
            xbuf[u, 8:8 + C, :] = qkv_ref[u * C:(u + 1) * C, :]
            yu = xbuf[u, 8:8 + C, :] * w_last
            for j in range(CONV_W - 1):
                yu = yu + xbuf[u, 5 + j:5 + j + C, :] * wconv_ref[j:j + 1, :]
            ys.append(yu)
        y = jnp.concatenate(ys, axis=0)
    y = y * _sigmoid(y)
    q = y[:, :DN_W]
    k = y[:, DN_W:2 * DN_W]
    v = y[:, 2 * DN_W:]

    seg = seg_ref[...]
    expand = expand_ref[...]

    def seg_sum(t):
        hi = t.astype(BF16)
        lo = (t - hi.astype(F32)).astype(BF16)
        r = jnp.dot(jnp.concatenate([hi, lo], axis=0), seg, preferred_element_type=F32)
        return r[:R] + r[R:]

    qn = q * (lax.rsqrt(seg_sum(q * q) + NORM_EPS) * HEAD_DIM ** -0.5)
    kn = k * lax.rsqrt(seg_sum(k * k) + NORM_EPS)

    ab = ab_ref[...]
    g = -jnp.exp(alog_ref[...]) * _softplus(ab + dtb_ref[...])
    beta = _sigmoid(pltpu.roll(ab, LANES - HEADS, axis=1))
    row = lax.broadcasted_iota(jnp.int32, (R, R), 0)
    col = lax.broadcasted_iota(jnp.int32, (R, R), 1)
    tril = ((row // C == col // C) & (row >= col)).astype(BF16)
    gs = jnp.dot(tril, jnp.concatenate(_split3(g), axis=1), preferred_element_type=F32)
    G = gs[:, :LANES] + gs[:, LANES:2 * LANES] + gs[:, 2 * LANES:]
    GT = G.T
    g_last = jnp.concatenate(
        [jnp.broadcast_to(G[(u + 1) * C - 1:(u + 1) * C, :], (C, LANES)) for u in range(U)], axis=0)
    e_g = jnp.exp(G)
    G_x = _dot_exact_lhs(G, expand)
    beta_x = _dot_exact_lhs(beta, expand)
    eg_x = _dot_exact_lhs(e_g, expand)
    erem_x = _dot_exact_lhs(jnp.exp(g_last - G), expand)

    kbeta = kn * beta_x
    bk = kbeta * eg_x
    bv = v * beta_x
    qd = qn * eg_x
    kd = kn * erem_x

    r2 = lax.broadcasted_iota(jnp.int32, (2 * C, 2 * C), 0)
    c2 = lax.broadcasted_iota(jnp.int32, (2 * C, 2 * C), 1)
    same = (r2 // C) == (c2 // C)
    causal = same & (r2 >= c2)
    strict = same & (r2 > c2)
    eye = (r2 == c2).astype(F32)
    lane = lax.broadcasted_iota(jnp.int32, (1, PW), 1)
    m_a = (lane < HEAD_DIM).astype(F32)
    m_b = 1.0 - m_a
    rs = lax.broadcasted_iota(jnp.int32, (PW, PW), 0)
    cs = lax.broadcasted_iota(jnp.int32, (PW, PW), 1)
    bd_mask = ((rs // HEAD_DIM) == (cs // HEAD_DIM)).astype(F32)

    def stack(x):
        return jnp.concatenate([x * m_a, x * m_b], axis=0)

    def fold(x):
        return x[:C] + x[C:]

    items = [(u, p) for u in range(U) for p in range(PAIRS)]
    A, QK = {}, {}
    for (u, p) in items:
        ru = slice(u * C, (u + 1) * C)
        lp = slice(p * PW, (p + 1) * PW)
        ks = stack(kn[ru, lp])
        gx = G_x[ru, lp]
        gc_a = gx[:, :2 * C]
        gc_b = gx if C == HEAD_DIM else gx[:, HEAD_DIM:HEAD_DIM + 2 * C]
        gr = jnp.concatenate([GT[2 * p:2 * p + 1, ru], GT[2 * p + 1:2 * p + 2, ru]], axis=1)
        diff = jnp.concatenate([gc_a, gc_b], axis=0) - gr
        decay = jnp.exp(jnp.where(causal, diff, -jnp.inf))
        A[u, p] = jnp.where(strict, _dot_nt(stack(kbeta[ru, lp]), ks) * decay, 0.0)
        QK[u, p] = _dot_nt(stack(qn[ru, lp]), ks) * decay

    P = {it: eye - A[it] for it in items}
    Ap = A
    for _ in range(int(math.log2(C)) - 1):
        Ap = {it: _dot(Ap[it], Ap[it]) for it in items}
        P = {it: P[it] + _dot(P[it], Ap[it]) for it in items}

    W2, U2, M, N = {}, {}, {}, {}
    for (u, p) in items:
        ru = slice(u * C, (u + 1) * C)
        lp = slice(p * PW, (p + 1) * PW)
        wu = _dot(P[u, p], jnp.concatenate([stack(bk[ru, lp]), stack(bv[ru, lp])], axis=1))
        wu = jnp.concatenate([fold(wu[:, :PW]), fold(wu[:, PW:])], axis=1)
        W2[u, p] = wu[:, :PW]
        U2[u, p] = wu[:, PW:]
        mn = _dot(kd[ru, lp].T, wu)
        M[u, p] = bd_mask * mn[:, :PW]
        N[u, p] = bd_mask * mn[:, PW:]

    def to_bd(sa, sb):
        zero = jnp.zeros_like(sa)
        return jnp.concatenate([jnp.concatenate([sa, zero], axis=1),
                                jnp.concatenate([zero, sb], axis=1)], axis=0)

    if sequential:
        @pl.when(step == 0)
        def _():
            for p in range(PAIRS):
                sbd[p] = to_bd(s0_ref[2 * p], s0_ref[2 * p + 1])
        S = [sbd[p] for p in range(PAIRS)]
    S_in = {}
    for u in range(U):
        for p in range(PAIRS):
            lp = slice(p * PW, (p + 1) * PW)
            s_cur = S[p] if sequential else to_bd(s0_ref[u, 2 * p], s0_ref[u, 2 * p + 1])
            S_in[u, p] = s_cur.astype(BF16)
            s_new = (s_cur * eg_x[(u + 1) * C - 1:(u + 1) * C, lp]
                     - _dot(M[u, p], S_in[u, p]) + N[u, p])
            if sequential:
                S[p] = s_new
            else:
                s_ref[u, 2 * p] = s_new[:HEAD_DIM, :HEAD_DIM]
                s_ref[u, 2 * p + 1] = s_new[HEAD_DIM:, HEAD_DIM:]
    if sequential:
        for p in range(PAIRS):
            sbd[p] = S[p]

        @pl.when(step == last_step)
        def _():
            for p in range(PAIRS):
                s_ref[2 * p] = S[p][:HEAD_DIM, :HEAD_DIM]
                s_ref[2 * p + 1] = S[p][HEAD_DIM:, HEAD_DIM:]

    u2 = {it: U2[it] - _dot(W2[it], S_in[it]) for it in items}
    for (u, p) in items:
        ru = slice(u * C, (u + 1) * C)
        lp = slice(p * PW, (p + 1) * PW)
        obuf[ru, lp] = _dot(qd[ru, lp], S_in[u, p]) + fold(_dot(QK[u, p], stack(u2[u, p])))

    o = obuf[...]
    ms = seg_sum(o * o) * (1.0 / HEAD_DIM)
    z = z_ref[...]
    o_ref[...] = (o * lax.rsqrt(ms + NORM_EPS) * normw_ref[...] * (z * _sigmoid(z))).astype(BF16)


def _delta(qkv, z, ab, conv0p, s0, consts, n_seq, seq_len, chunk, units):
    wconv, alog, dtb, normw, seg, expand = consts
    sequential = n_seq == 1
    rows = n_seq * seq_len
    blk_rows = units * chunk
    assert rows % blk_rows == 0 and (sequential or seq_len == chunk)
    tok = lambda width: pl.BlockSpec((blk_rows, width), lambda i: (i, 0))
    const2 = lambda shape: pl.BlockSpec(shape, lambda i: (0, 0))
    if sequential:
        state = pl.BlockSpec((None, HEADS, HEAD_DIM, HEAD_DIM), lambda i: (0, 0, 0, 0))
        conv0 = pl.BlockSpec((None, 8, CONV_CH), lambda i: (0, 0, 0))
        xbuf = pltpu.VMEM((8 + blk_rows, CONV_CH), F32)
    else:
        state = pl.BlockSpec((units, HEADS, HEAD_DIM, HEAD_DIM), lambda i: (i, 0, 0, 0))
        conv0 = pl.BlockSpec((units, 8, CONV_CH), lambda i: (i, 0, 0))
        xbuf = pltpu.VMEM((units, 8 + chunk, CONV_CH), F32)
    return pl.pallas_call(
        functools.partial(_delta_kernel, chunk=chunk, units=units, sequential=sequential),
        out_shape=[jax.ShapeDtypeStruct((rows, DN_W), BF16),
                   jax.ShapeDtypeStruct((n_seq, HEADS, HEAD_DIM, HEAD_DIM), F32)],
        grid=(rows // blk_rows,),
        in_specs=[tok(CONV_CH), tok(DN_W), tok(LANES), conv0, state,
                  const2((8, CONV_CH)), const2((1, LANES)), const2((1, LANES)),
                  const2((1, DN_W)), const2((DN_W, DN_W)), const2((LANES, DN_W))],
        out_specs=[tok(DN_W), state],
        scratch_shapes=[xbuf, pltpu.VMEM((blk_rows, DN_W), F32),
                        pltpu.VMEM((HEADS // 2, 2 * HEAD_DIM, 2 * HEAD_DIM), F32)],
        compiler_params=pltpu.CompilerParams(dimension_semantics=("arbitrary",),
                                             vmem_limit_bytes=VMEM_LIMIT),
        name=f"delta_c{chunk}",
    )(qkv, z, ab, conv0p, s0, wconv, alog, dtb, normw, seg, expand)


def _swa_kernel(*refs, n_q, key_rows, first_valid):
    n_parts = len(key_rows)
    table_ref, sinks_ref, bucket_ref, q_ref = refs[:4]
    k_refs = refs[4:4 + n_parts]
    v_refs = refs[4 + n_parts:4 + 2 * n_parts]
    o_ref, bias_ref, obuf = refs[4 + 2 * n_parts:]
    n_k = sum(key_rows)
    step = pl.program_id(0)

    @pl.when(step == 0)
    def _():
        bucket = bucket_ref[...]
        for h in range(SWA_HEADS):
            acc = jnp.zeros((n_q, n_k), F32)
            for b in range(N_BUCKETS):
                acc = jnp.where(bucket == b, table_ref[b * SWA_HEADS + h], acc)
            kv, gi = divmod(h, SWA_G)
            bias_ref[kv, gi * n_q:(gi + 1) * n_q, :] = acc

    q = q_ref[...] * HEAD_DIM ** -0.5
    kcat = jnp.concatenate([r[...] for r in k_refs], axis=0)
    vcat = jnp.concatenate([r[...] for r in v_refs], axis=0)
    if first_valid is not None:
        key_pos = step * n_q + lax.broadcasted_iota(jnp.int32, (1, n_k), 1)
        valid = key_pos >= first_valid
    for kv in range(SWA_KV):
        heads = range(kv * SWA_G, (kv + 1) * SWA_G)
        q4 = jnp.concatenate([q[:, h * HEAD_DIM:(h + 1) * HEAD_DIM] for h in heads], axis=0)
        sink = jnp.concatenate([jnp.full((n_q, 1), sinks_ref[h], F32) for h in heads], axis=0)
        s = _dot_nt(q4, kcat[:, kv * HEAD_DIM:(kv + 1) * HEAD_DIM]) + bias_ref[kv]
        if first_valid is not None:
            s = jnp.where(valid, s, -jnp.inf)
        m = jnp.maximum(jnp.max(s, axis=-1, keepdims=True), sink)
        p = jnp.exp(s - m)
        denom = jnp.sum(p, axis=-1, keepdims=True) + jnp.exp(sink - m)
        o = _dot(p, vcat[:, kv * HEAD_DIM:(kv + 1) * HEAD_DIM]) / denom
        for gi, h in enumerate(heads):
            obuf[:, h * HEAD_DIM:(h + 1) * HEAD_DIM] = o[gi * n_q:(gi + 1) * n_q, :]
    o_ref[...] = obuf[...].astype(BF16)


def _swa(table, sinks, bucket, q, k_parts, v_parts, key_specs, key_rows, n_steps, n_q,
         first_valid, name):
    n_k = sum(key_rows)
    smem = pl.BlockSpec(memory_space=pltpu.SMEM)
    return pl.pallas_call(
        functools.partial(_swa_kernel, n_q=n_q, key_rows=key_rows, first_valid=first_valid),
        out_shape=jax.ShapeDtypeStruct((n_steps * n_q, SWA_HEADS * HEAD_DIM), BF16),
        grid=(n_steps,),
        in_specs=[smem, smem, pl.BlockSpec((n_q, n_k), lambda i: (0, 0)),
                  pl.BlockSpec((n_q, SWA_HEADS * HEAD_DIM), lambda i: (i, 0))]
                 + list(key_specs) + list(key_specs),
        out_specs=pl.BlockSpec((n_q, SWA_HEADS * HEAD_DIM), lambda i: (i, 0)),
        scratch_shapes=[pltpu.VMEM((SWA_KV, SWA_G * n_q, n_k), F32),
                        pltpu.VMEM((n_q, SWA_HEADS * HEAD_DIM), F32)],
        compiler_params=pltpu.CompilerParams(dimension_semantics=("arbitrary",),
                                             vmem_limit_bytes=VMEM_LIMIT),
        name=name,
    )(table, sinks, bucket, q, *k_parts, *v_parts)


def _t5_bucket(rel):
    half = N_BUCKETS // 2
    max_exact = half // 2
    n = jnp.abs(rel)
    large = max_exact + (jnp.log(jnp.maximum(n, 1).astype(F32) / max_exact)
                         / math.log(MAX_DIST / max_exact) * (half - max_exact)).astype(jnp.int32)
    large = jnp.minimum(large, half - 1)
    return jnp.where(rel > 0, half, 0) + jnp.where(n < max_exact, n, large)


def _bucket_map(n_q, n_k, key_offset):
    rel = (jnp.arange(n_k)[None, :] - key_offset) - jnp.arange(n_q)[:, None]
    return _t5_bucket(rel).astype(jnp.int32)


FF_BLOCK = 1024


def _post_kernel(x_ref, dn_ref, sw_ref, gin_ref, bin_ref, wo_ref, g1_ref, b1_ref,
                 w1_ref, w2_ref, g2_ref, b2_ref, y_ref):
    h = _layer_norm(x_ref[...], gin_ref[...], bin_ref[...])
    mix = (jnp.dot(dn_ref[...], wo_ref[:DN_W, :], preferred_element_type=F32)
           + jnp.dot(sw_ref[...], wo_ref[DN_W:, :], preferred_element_type=F32))
    h1 = _layer_norm(DEEP_ALPHA * h + mix, g1_ref[...], b1_ref[...])
    h1b = h1.astype(BF16)
    f = jnp.zeros_like(h1)
    for j in range(D_FF // FF_BLOCK):
        a = jnp.dot(h1b, w1_ref[:, j * FF_BLOCK:(j + 1) * FF_BLOCK], preferred_element_type=F32)
        a = jnp.square(jnp.maximum(a, 0.0)).astype(BF16)
        f = f + jnp.dot(a, w2_ref[j * FF_BLOCK:(j + 1) * FF_BLOCK, :], preferred_element_type=F32)
    y_ref[...] = _layer_norm(DEEP_ALPHA * h1 + f, g2_ref[...], b2_ref[...])


def _post(x, dn, sw, vecs, wo, w1, w2, tm):
    gin, bin_, g1, b1, g2, b2 = vecs
    rows = x.shape[0]
    const = lambda i: (0, 0)
    vec = pl.BlockSpec((1, D_MODEL), const)
    weight = lambda shape: pl.BlockSpec(shape, const, pipeline_mode=pl.Buffered(1))
    row_blk = lambda width: pl.BlockSpec((tm, width), lambda i: (i, 0))
    return pl.pallas_call(
        _post_kernel,
        out_shape=jax.ShapeDtypeStruct((rows, D_MODEL), F32),
        grid=(rows // tm,),
        in_specs=[row_blk(D_MODEL), row_blk(DN_W), row_blk(DN_W), vec, vec,
                  weight((2 * DN_W, D_MODEL)), vec, vec,
                  weight((D_MODEL, D_FF)), weight((D_FF, D_MODEL)), vec, vec],
        out_specs=row_blk(D_MODEL),
        compiler_params=pltpu.CompilerParams(dimension_semantics=("arbitrary",),
                                             vmem_limit_bytes=VMEM_LIMIT),
        name="post_mlp",
    )(x, dn, sw, gin, bin_, wo, g1, b1, w1, w2, g2, b2)


def kernel(x_prompt, x_sample, state_delta, state_conv, cache_swa_k, cache_swa_v, meta_tokens, ln_in_g, ln_in_b, w_in, w_conv, dn_a_log, dn_dt_bias, dn_norm, swa_sinks, rel_bias_table, w_o, ln1_g, ln1_b, w_ff1, w_ff2, ln2_g, ln2_b):
    assert w_in.shape[0] == DEPTH == 1
    n_seq_s, len_s = x_sample.shape[0], x_sample.shape[1]
    len_p = x_prompt.shape[1]
    cache_len = cache_swa_k.shape[2]
    chunk_p = 64
    row = lambda t: t.reshape(1, -1).astype(F32)

    w = w_in[0]
    w_r = jnp.concatenate(
        [w[:, :OFF_A], w[:, OFF_SQ:], w[:, OFF_A:OFF_SQ],
         jnp.zeros((D_MODEL, LANES - 2 * HEADS), w.dtype)], axis=1).astype(BF16)
    gin, bin_ = row(ln_in_g), row(ln_in_b)
    pad_lanes = lambda t: jnp.pad(t.reshape(1, -1).astype(F32), ((0, 0), (0, LANES - HEADS)))
    lane_head = jnp.arange(DN_W) // HEAD_DIM
    seg = (lane_head[:, None] == lane_head[None, :]).astype(BF16)
    expand = (jnp.arange(LANES)[:, None] == lane_head[None, :]).astype(BF16)
    delta_consts = (jnp.pad(w_conv[0].astype(F32), ((0, 8 - CONV_W), (0, 0))),
                    pad_lanes(dn_a_log[0]), pad_lanes(dn_dt_bias[0]),
                    jnp.tile(dn_norm[0].astype(F32), HEADS).reshape(1, DN_W), seg, expand)
    post_vecs = (gin, bin_, row(ln1_g[0]), row(ln1_b[0]), row(ln2_g[0]), row(ln2_b[0]))
    wo_b, w1_b, w2_b = w_o[0].astype(BF16), w_ff1[0].astype(BF16), w_ff2[0].astype(BF16)
    table = rel_bias_table.astype(F32).reshape(-1)
    sinks = swa_sinks[0].astype(F32)

    xp = x_prompt[0]
    xs = x_sample.reshape(n_seq_s * len_s, D_MODEL)

    p_qkv, p_z, p_sq, p_sk, p_sv, p_ab = _proj(xp, gin, bin_, w_r, 512)
    s_qkv, s_z, s_sq, s_sk, s_sv, s_ab = _proj(xs, gin, bin_, w_r, 512)
    m_qkv, m_z, m_sq, m_sk, m_sv, m_ab = _proj(meta_tokens.astype(F32), gin, bin_, w_r, N_META)

    zero_conv = jnp.zeros((1, 8, CONV_CH), F32)
    zero_s = jnp.zeros((1, HEADS, HEAD_DIM, HEAD_DIM), F32)
    _, s_meta = _delta(m_qkv, m_z, m_ab, zero_conv, zero_s, delta_consts, 1, N_META, N_META, 1)
    conv_meta = jnp.pad(m_qkv[-(CONV_W - 1):], ((8 - (CONV_W - 1), 0), (0, 0)))[None]
    p_dn, p_s = _delta(p_qkv, p_z, p_ab, conv_meta, s_meta, delta_consts, 1, len_p, chunk_p, 4)
    conv_s = jnp.pad(state_conv[0].astype(F32), ((0, 0), (8 - (CONV_W - 1), 0), (0, 0)))
    s_dn, s_s = _delta(s_qkv, s_z, s_ab, conv_s, state_delta[0].astype(F32), delta_consts,
                       n_seq_s, len_s, len_s, 8)

    span = WINDOW
    lead = jnp.zeros((span - N_META, KV_W), F32)
    kfr = jnp.concatenate([lead, m_sk, p_sk], axis=0)
    vfr = jnp.concatenate([lead, m_sv, p_sv], axis=0)
    n_chunk_p = len_p // chunk_p
    key_specs_p = [pl.BlockSpec((chunk_p, KV_W), functools.partial(lambda i, j: (i + j, 0), j=j))
                   for j in range(3)]
    p_sw = _swa(table, sinks, _bucket_map(chunk_p, span + chunk_p, span), p_sq,
                [kfr] * 3, [vfr] * 3, key_specs_p, (chunk_p,) * 3, n_chunk_p, chunk_p,
                span - N_META, "swa_prompt")
    ck = cache_swa_k[0].astype(F32).reshape(n_seq_s * cache_len, KV_W)
    cv = cache_swa_v[0].astype(F32).reshape(n_seq_s * cache_len, KV_W)
    key_specs_s = [pl.BlockSpec((cache_len, KV_W), lambda i: (i, 0)),
                   pl.BlockSpec((len_s, KV_W), lambda i: (i, 0))]
    s_sw = _swa(table, sinks, _bucket_map(len_s, cache_len + len_s, cache_len), s_sq,
                [ck, s_sk], [cv, s_sv], key_specs_s, (cache_len, len_s), n_seq_s, len_s,
                None, "swa_sample")

    y_p = _post(xp, p_dn, p_sw, post_vecs, wo_b, w1_b, w2_b, 512)
    y_s = _post(xs, s_dn, s_sw, post_vecs, wo_b, w1_b, w2_b, 512)

    kv_shape = lambda n, length: (1, n, length, SWA_KV, HEAD_DIM)
    return (y_p[None], y_s.reshape(x_sample.shape),
            p_s[None], p_qkv[-(CONV_W - 1):][None, None],
            p_sk[-WINDOW:].reshape(kv_shape(1, WINDOW)), p_sv[-WINDOW:].reshape(kv_shape(1, WINDOW)),
            s_s[None], s_qkv.reshape(n_seq_s, len_s, CONV_CH)[:, -(CONV_W - 1):][None],
            s_sk.reshape(kv_shape(n_seq_s, len_s)), s_sv.reshape(kv_shape(n_seq_s, len_s)))
```

```python
import functools
import math

import jax
import jax.numpy as jnp
from jax import lax
from jax.experimental import pallas as pl
from jax.experimental.pallas import tpu as pltpu

F32 = jnp.float32
BF16 = jnp.bfloat16

D_MODEL = 1024
N_META = 16
HEADS = 8
HEAD_DIM = 64
DN_W = HEADS * HEAD_DIM
CONV_W = 4
CONV_CH = 3 * DN_W
SWA_HEADS = 8
SWA_KV = 2
SWA_G = SWA_HEADS // SWA_KV
KV_W = SWA_KV * HEAD_DIM
WINDOW = 128
N_BUCKETS = 32
MAX_DIST = 128
D_FF = 4 * D_MODEL
DEPTH = 1
DEEP_ALPHA = (2 * DEPTH) ** 0.25
LN_EPS = 1e-5
NORM_EPS = 1e-6
LANES = 128
VMEM_LIMIT = 56 * 1024 * 1024

OFF_Z = CONV_CH
OFF_A = OFF_Z + DN_W
OFF_B = OFF_A + HEADS
OFF_SQ = OFF_B + HEADS
OFF_SK = OFF_SQ + SWA_HEADS * HEAD_DIM
OFF_SV = OFF_SK + KV_W
PROJ_WIDTH = OFF_SV + KV_W
PROJ_SPLITS = (CONV_CH, DN_W, SWA_HEADS * HEAD_DIM, KV_W, KV_W, LANES)
PROJ_PAD_W = sum(PROJ_SPLITS)


def _layer_norm(x, g, b):
    mu = jnp.mean(x, axis=-1, keepdims=True)
    xc = x - mu
    var = jnp.mean(xc * xc, axis=-1, keepdims=True)
    return xc * lax.rsqrt(var + LN_EPS) * g + b


def _dot(a, b):
    return jnp.dot(a.astype(BF16), b.astype(BF16), preferred_element_type=F32)


def _dot_nt(a, b):
    return lax.dot_general(a.astype(BF16), b.astype(BF16), (((1,), (1,)), ((), ())),
                           preferred_element_type=F32)


def _split3(x):
    x1 = x.astype(BF16)
    r1 = x - x1.astype(F32)
    x2 = r1.astype(BF16)
    x3 = (r1 - x2.astype(F32)).astype(BF16)
    return x1, x2, x3


def _dot_exact_lhs(x, sel):
    m = x.shape[0]
    stacked = jnp.concatenate(_split3(x), axis=0)
    r = jnp.dot(stacked, sel, preferred_element_type=F32)
    return r[:m] + r[m:2 * m] + r[2 * m:]


def _proj_kernel(x_ref, g_ref, b_ref, w_ref, *out_refs):
    h = _layer_norm(x_ref[...], g_ref[...], b_ref[...]).astype(BF16)
    off = 0
    for o_ref, width in zip(out_refs, PROJ_SPLITS):
        o_ref[...] = jnp.dot(h, w_ref[:, off:off + width], preferred_element_type=F32)
        off += width


def _proj(x, g, b, w_r, tm):
    rows = x.shape[0]
    const = lambda i: (0, 0)
    row_blk = lambda width: pl.BlockSpec((tm, width), lambda i: (i, 0))
    return pl.pallas_call(
        _proj_kernel,
        out_shape=[jax.ShapeDtypeStruct((rows, width), F32) for width in PROJ_SPLITS],
        grid=(rows // tm,),
        in_specs=[row_blk(D_MODEL), pl.BlockSpec((1, D_MODEL), const),
                  pl.BlockSpec((1, D_MODEL), const), pl.BlockSpec((D_MODEL, PROJ_PAD_W), const)],
        out_specs=[row_blk(width) for width in PROJ_SPLITS],
        compiler_params=pltpu.CompilerParams(dimension_semantics=("arbitrary",),
                                             vmem_limit_bytes=VMEM_LIMIT),
        name="ln_in_proj",
    )(x, g, b, w_r)


def _sigmoid(x):
    return 1.0 / (1.0 + jnp.exp(-x))


def _softplus(x):
    return jnp.maximum(x, 0.0) + jnp.log1p(jnp.exp(-jnp.abs(x)))


def _delta_kernel(qkv_ref, z_ref, ab_ref, conv0_ref, s0_ref, wconv_ref, alog_ref, dtb_ref,
                  normw_ref, seg_ref, expand_ref, o_ref, s_ref, xbuf, obuf, sbd,
                  *, chunk, units, sequential):
    C, U = chunk, units
    R = U * C
    PAIRS = HEADS // 2
    PW = 2 * HEAD_DIM
    step = pl.program_id(0)
    last_step = pl.num_programs(0) - 1

    w_last = wconv_ref[CONV_W - 1:CONV_W, :]
    if sequential:
        @pl.when(step == 0)
        def _():
            xbuf[0:8, :] = conv0_ref[...]
        xbuf[8:8 + R, :] = qkv_ref[...]
        y = xbuf[8:8 + R, :] * w_last
        for j in range(CONV_W - 1):
            y = y + xbuf[5 + j:5 + j + R, :] * wconv_ref[j:j + 1, :]
        xbuf[0:8, :] = xbuf[R:R + 8, :]
    else:
        ys = []
        for u in range(U):
            xbuf[u, 0:8, :] = conv0_ref[u]
            xbuf[u, 8:8 + C, :] = qkv_ref[u * C:(u + 1) * C, :]
            yu = xbuf[u, 8:8 + C, :] * w_last
            for j in range(CONV_W - 1):
                yu = yu + xbuf[u, 5 + j:5 + j + C, :] * wconv_ref[j:j + 1, :]
            ys.append(yu)
        y = jnp.concatenate(ys, axis=0)
    y = y * _sigmoid(y)
    q = y[:, :DN_W]
    k = y[:, DN_W:2 * DN_W]
    v = y[:, 2 * DN_W:]

    seg = seg_ref[...]
    expand = expand_ref[...]

    def seg_sum(t):
        hi = t.astype(BF16)
        lo = (t - hi.astype(F32)).astype(BF16)
        r = jnp.dot(jnp.concatenate([hi, lo], axis=0), seg, preferred_element_type=F32)
        return r[:R] + r[R:]

    qn = q * (lax.rsqrt(seg_sum(q * q) + NORM_EPS) * HEAD_DIM ** -0.5)
    kn = k * lax.rsqrt(seg_sum(k * k) + NORM_EPS)

    ab = ab_ref[...]
    g = -jnp.exp(alog_ref[...]) * _softplus(ab + dtb_ref[...])
    beta = _sigmoid(pltpu.roll(ab, LANES - HEADS, axis=1))
    row = lax.broadcasted_iota(jnp.int32, (R, R), 0)
    col = lax.broadcasted_iota(jnp.int32, (R, R), 1)
    tril = ((row // C == col // C) & (row >= col)).astype(BF16)
    gs = jnp.dot(tril, jnp.concatenate(_split3(g), axis=1), preferred_element_type=F32)
    G = gs[:, :LANES] + gs[:, LANES:2 * LANES] + gs[:, 2 * LANES:]
    GT = G.T
    g_last = jnp.concatenate(
        [jnp.broadcast_to(G[(u + 1) * C - 1:(u + 1) * C, :], (C, LANES)) for u in range(U)], axis=0)
    e_g = jnp.exp(G)
    G_x = _dot_exact_lhs(G, expand)
    beta_x = _dot_exact_lhs(beta, expand)
    eg_x = _dot_exact_lhs(e_g, expand)
    erem_x = _dot_exact_lhs(jnp.exp(g_last - G), expand)

    kbeta = kn * beta_x
    bk = kbeta * eg_x
    bv = v * beta_x
    qd = qn * eg_x
    kd = kn * erem_x

    r2 = lax.broadcasted_iota(jnp.int32, (2 * C, 2 * C), 0)
    c2 = lax.broadcasted_iota(jnp.int32, (2 * C, 2 * C), 1)
    same = (r2 // C) == (c2 // C)
    causal = same & (r2 >= c2)
    strict = same & (r2 > c2)
    eye = (r2 == c2).astype(F32)
    lane = lax.broadcasted_iota(jnp.int32, (1, PW), 1)
    m_a = (lane < HEAD_DIM).astype(F32)
    m_b = 1.0 - m_a
    rs = lax.broadcasted_iota(jnp.int32, (PW, PW), 0)
    cs = lax.broadcasted_iota(jnp.int32, (PW, PW), 1)
    bd_mask = ((rs // HEAD_DIM) == (cs // HEAD_DIM)).astype(F32)

    def stack(x):
        return jnp.concatenate([x * m_a, x * m_b], axis=0)

    def fold(x):
        return x[:C] + x[C:]

    items = [(u, p) for u in range(U) for p in range(PAIRS)]
    A, QK = {}, {}
    for (u, p) in items:
        ru = slice(u * C, (u + 1) * C)
        lp = slice(p * PW, (p + 1) * PW)
        ks = stack(kn[ru, lp])
        gx = G_x[ru, lp]
        gc_a = gx[:, :2 * C]
        gc_b = gx if C == HEAD_DIM else gx[:, HEAD_DIM:HEAD_DIM + 2 * C]
        gr = jnp.concatenate([GT[2 * p:2 * p + 1, ru], GT[2 * p + 1:2 * p + 2, ru]], axis=1)
        diff = jnp.concatenate([gc_a, gc_b], axis=0) - gr
        decay = jnp.exp(jnp.where(causal, diff, -jnp.inf))
        A[u, p] = jnp.where(strict, _dot_nt(stack(kbeta[ru, lp]), ks) * decay, 0.0)
        QK[u, p] = _dot_nt(stack(qn[ru, lp]), ks) * decay

    P = {it: eye - A[it] for it in items}
    Ap = A
    for _ in range(int(math.log2(C)) - 1):
        Ap = {it: _dot(Ap[it], Ap[it]) for it in items}
        P = {it: P[it] + _dot(P[it], Ap[it]) for it in items}

    W2, U2, M, N = {}, {}, {}, {}
    for (u, p) in items:
        ru = slice(u * C, (u + 1) * C)
        lp = slice(p * PW, (p + 1) * PW)
        wu = _dot(P[u, p], jnp.concatenate([stack(bk[ru, lp]), stack(bv[ru, lp])], axis=1))
        wu = jnp.concatenate([fold(wu[:, :PW]), fold(wu[:, PW:])], axis=1)
        W2[u, p] = wu[:, :PW]
        U2[u, p] = wu[:, PW:]
        mn = _dot(kd[ru, lp].T, wu)
        M[u, p] = bd_mask * mn[:, :PW]
        N[u, p] = bd_mask * mn[:, PW:]

    def to_bd(sa, sb):
        zero = jnp.zeros_like(sa)
        return jnp.concatenate([jnp.concatenate([sa, zero], axis=1),
                                jnp.concatenate([zero, sb], axis=1)], axis=0)

    if sequential:
        @pl.when(step == 0)
        def _():
            for p in range(PAIRS):
                sbd[p] = to_bd(s0_ref[2 * p], s0_ref[2 * p + 1])
        S = [sbd[p] for p in range(PAIRS)]
    S_in = {}
    for u in range(U):
        for p in range(PAIRS):
            lp = slice(p * PW, (p + 1) * PW)
            s_cur = S[p] if sequential else to_bd(s0_ref[u, 2 * p], s0_ref[u, 2 * p + 1])
            S_in[u, p] = s_cur.astype(BF16)
            s_new = (s_cur * eg_x[(u + 1) * C - 1:(u + 1) * C, lp]
                     - _dot(M[u, p], S_in[u, p]) + N[u, p])
            if sequential:
                S[p] = s_new
            else:
                s_ref[u, 2 * p] = s_new[:HEAD_DIM, :HEAD_DIM]
                s_ref[u, 2 * p + 1] = s_new[HEAD_DIM:, HEAD_DIM:]
    if sequential:
        for p in range(PAIRS):
            sbd[p] = S[p]

        @pl.when(step == last_step)
        def _():
            for p in range(PAIRS):
                s_ref[2 * p] = S[p][:HEAD_DIM, :HEAD_DIM]
                s_ref[2 * p + 1] = S[p][HEAD_DIM:, HEAD_DIM:]

    u2 = {it: U2[it] - _dot(W2[it], S_in[it]) for it in items}
    for (u, p) in items:
        ru = slice(u * C, (u + 1) * C)
        lp = slice(p * PW, (p + 1) * PW)
        obuf[ru, lp] = _dot(qd[ru, lp], S_in[u, p]) + fold(_dot(QK[u, p], stack(u2[u, p])))

    o = obuf[...]
    ms = seg_sum(o * o) * (1.0 / HEAD_DIM)
    z = z_ref[...]
    o_ref[...] = (o * lax.rsqrt(ms + NORM_EPS) * normw_ref[...] * (z * _sigmoid(z))).astype(BF16)


def _delta(qkv, z, ab, conv0p, s0, consts, n_seq, seq_len, chunk, units):
    wconv, alog, dtb, normw, seg, expand = consts
    sequential = n_seq == 1
    rows = n_seq * seq_len
    blk_rows = units * chunk
    assert rows % blk_rows == 0 and (sequential or seq_len == chunk)
    tok = lambda width: pl.BlockSpec((blk_rows, width), lambda i: (i, 0))
    const2 = lambda shape: pl.BlockSpec(shape, lambda i: (0, 0))
    if sequential:
        state = pl.BlockSpec((None, HEADS, HEAD_DIM, HEAD_DIM), lambda i: (0, 0, 0, 0))
        conv0 = pl.BlockSpec((None, 8, CONV_CH), lambda i: (0, 0, 0))
        xbuf = pltpu.VMEM((8 + blk_rows, CONV_CH), F32)
    else:
        state = pl.BlockSpec((units, HEADS, HEAD_DIM, HEAD_DIM), lambda i: (i, 0, 0, 0))
        conv0 = pl.BlockSpec((units, 8, CONV_CH), lambda i: (i, 0, 0))
        xbuf = pltpu.VMEM((units, 8 + chunk, CONV_CH), F32)
    return pl.pallas_call(
        functools.partial(_delta_kernel, chunk=chunk, units=units, sequential=sequential),
        out_shape=[jax.ShapeDtypeStruct((rows, DN_W), BF16),
                   jax.ShapeDtypeStruct((n_seq, HEADS, HEAD_DIM, HEAD_DIM), F32)],
        grid=(rows // blk_rows,),
        in_specs=[tok(CONV_CH), tok(DN_W), tok(LANES), conv0, state,
                  const2((8, CONV_CH)), const2((1, LANES)), const2((1, LANES)),
                  const2((1, DN_W)), const2((DN_W, DN_W)), const2((LANES, DN_W))],
        out_specs=[tok(DN_W), state],
        scratch_shapes=[xbuf, pltpu.VMEM((blk_rows, DN_W), F32),
                        pltpu.VMEM((HEADS // 2, 2 * HEAD_DIM, 2 * HEAD_DIM), F32)],
        compiler_params=pltpu.CompilerParams(dimension_semantics=("arbitrary",),
                                             vmem_limit_bytes=VMEM_LIMIT),
        name=f"delta_c{chunk}",
    )(qkv, z, ab, conv0p, s0, wconv, alog, dtb, normw, seg, expand)


SWA_W = SWA_HEADS * HEAD_DIM


def _swa_kernel(*refs, n_q, units, prompt):
    if prompt:
        (table_ref, sinks_ref, bucket_ref, q_ref, km_ref, kp_ref, kc_ref, vm_ref, vp_ref, vc_ref,
         o_ref, bias_ref) = refs
    else:
        (table_ref, sinks_ref, bucket_ref, q_ref, kp_ref, kc_ref, vp_ref, vc_ref,
         o_ref, bias_ref) = refs
    n_k = bias_ref.shape[-1]
    n_grp = SWA_W // KV_W
    step = pl.program_id(0)

    @pl.when(step == 0)
    def _():
        bucket = bucket_ref[...]
        for h in range(SWA_HEADS):
            acc = jnp.zeros((n_q, n_k), F32)
            for b in range(N_BUCKETS):
                acc = jnp.where(bucket == b, table_ref[b * SWA_HEADS + h], acc)
            kv, gi = divmod(h, SWA_G)
            bias_ref[kv, gi * n_q:(gi + 1) * n_q, :] = acc

    lane = lax.broadcasted_iota(jnp.int32, (1, KV_W), 1)
    kv_mask = [(lane < HEAD_DIM).astype(F32), (lane >= HEAD_DIM).astype(F32)]
    if prompt:
        first = step == 0
        k_all = jnp.concatenate([jnp.where(first, km_ref[...], kp_ref[...]), kc_ref[...]], axis=0)
        v_all = jnp.concatenate([jnp.where(first, vm_ref[...], vp_ref[...]), vc_ref[...]], axis=0)
        k_kv = [(k_all * m).astype(BF16) for m in kv_mask]
        v_kv = [(v_all * m).astype(BF16) for m in kv_mask]
        keys = lambda j, kv: k_kv[kv][j * n_q:j * n_q + n_k]
        vals = lambda j, kv: v_kv[kv][j * n_q:j * n_q + n_k]
    else:
        n_c = n_k - n_q
        cat = lambda a, b, j: jnp.concatenate([a[j * n_c:(j + 1) * n_c, :],
                                               b[j * n_q:(j + 1) * n_q, :]], axis=0)
        keys = lambda j, kv: (cat(kp_ref, kc_ref, j) * kv_mask[kv]).astype(BF16)
        vals = lambda j, kv: (cat(vp_ref, vc_ref, j) * kv_mask[kv]).astype(BF16)

    q = (q_ref[...] * HEAD_DIM ** -0.5).astype(BF16)
    sink = [jnp.concatenate([jnp.full((n_q, 1), sinks_ref[kv * SWA_G + gi], F32)
                             for gi in range(SWA_G)], axis=0) for kv in range(SWA_KV)]
    items = [(j, kv) for j in range(units) for kv in range(SWA_KV)]
    s = {}
    for j in range(units):
        rows = slice(j * n_q, (j + 1) * n_q)
        q4 = jnp.concatenate([q[rows, g * KV_W:(g + 1) * KV_W] for g in range(n_grp)], axis=0)
        for kv in range(SWA_KV):
            sj = lax.dot_general(q4, keys(j, kv), (((1,), (1,)), ((), ())),
                                 preferred_element_type=F32) + bias_ref[kv]
            if prompt and j * n_q < WINDOW - N_META:
                key_pos = j * n_q + lax.broadcasted_iota(jnp.int32, (1, n_k), 1)
                n_invalid = jnp.where(first, WINDOW - N_META, 0)
                sj = jnp.where(key_pos < n_invalid, -jnp.inf, sj)
            s[j, kv] = sj
    m = {it: jnp.maximum(jnp.max(s[it], axis=-1, keepdims=True), sink[it[1]]) for it in items}
    p = {it: jnp.exp(s[it] - m[it]) for it in items}
    den = {it: jnp.sum(p[it], axis=-1, keepdims=True) + jnp.exp(sink[it[1]] - m[it]) for it in items}
    o = {it: jnp.dot(p[it].astype(BF16), vals(*it), preferred_element_type=F32) * (1.0 / den[it])
         for it in items}
    for j in range(units):
        o2 = o[j, 0] + o[j, 1]
        for g in range(n_grp):
            o_ref[j * n_q:(j + 1) * n_q, g * KV_W:(g + 1) * KV_W] = (
                o2[g * n_q:(g + 1) * n_q, :].astype(BF16))


def _swa(table, sinks, bucket, q, kv_args, kv_specs, n_steps, n_q, units, prompt, name):
    n_k = bucket.shape[1]
    smem = pl.BlockSpec(memory_space=pltpu.SMEM)
    rows = units * n_q
    return pl.pallas_call(
        functools.partial(_swa_kernel, n_q=n_q, units=units, prompt=prompt),
        out_shape=jax.ShapeDtypeStruct((n_steps * rows, SWA_W), BF16),
        grid=(n_steps,),
        in_specs=[smem, smem, pl.BlockSpec((n_q, n_k), lambda i: (0, 0)),
                  pl.BlockSpec((rows, SWA_W), lambda i: (i, 0))] + list(kv_specs),
        out_specs=pl.BlockSpec((rows, SWA_W), lambda i: (i, 0)),
        scratch_shapes=[pltpu.VMEM((SWA_KV, SWA_G * n_q, n_k), F32)],
        compiler_params=pltpu.CompilerParams(dimension_semantics=("arbitrary",),
                                             vmem_limit_bytes=VMEM_LIMIT),
        name=name,
    )(table, sinks, bucket, q, *kv_args)


def _t5_bucket(rel):
    half = N_BUCKETS // 2
    max_exact = half // 2
    n = jnp.abs(rel)
    large = max_exact + (jnp.log(jnp.maximum(n, 1).astype(F32) / max_exact)
                         / math.log(MAX_DIST / max_exact) * (half - max_exact)).astype(jnp.int32)
    large = jnp.minimum(large, half - 1)
    return jnp.where(rel > 0, half, 0) + jnp.where(n < max_exact, n, large)


def _bucket_map(n_q, n_k, key_offset):
    rel = (jnp.arange(n_k)[None, :] - key_offset) - jnp.arange(n_q)[:, None]
    return _t5_bucket(rel).astype(jnp.int32)


FF_BLOCK = 1024


def _post_kernel(x_ref, dn_ref, sw_ref, gin_ref, bin_ref, wo_ref, g1_ref, b1_ref,
                 w1_ref, w2_ref, g2_ref, b2_ref, y_ref):
    h = _layer_norm(x_ref[...], gin_ref[...], bin_ref[...])
    mix = (jnp.dot(dn_ref[...], wo_ref[:DN_W, :], preferred_element_type=F32)
           + jnp.dot(sw_ref[...], wo_ref[DN_W:, :], preferred_element_type=F32))
    h1 = _layer_norm(DEEP_ALPHA * h + mix, g1_ref[...], b1_ref[...])
    h1b = h1.astype(BF16)
    f = jnp.zeros_like(h1)
    for j in range(D_FF // FF_BLOCK):
        a = jnp.dot(h1b, w1_ref[:, j * FF_BLOCK:(j + 1) * FF_BLOCK], preferred_element_type=F32)
        a = jnp.square(jnp.maximum(a, 0.0)).astype(BF16)
        f = f + jnp.dot(a, w2_ref[j * FF_BLOCK:(j + 1) * FF_BLOCK, :], preferred_element_type=F32)
    y_ref[...] = _layer_norm(DEEP_ALPHA * h1 + f, g2_ref[...], b2_ref[...])


def _post(x, dn, sw, vecs, wo, w1, w2, tm):
    gin, bin_, g1, b1, g2, b2 = vecs
    rows = x.shape[0]
    const = lambda i: (0, 0)
    vec = pl.BlockSpec((1, D_MODEL), const)
    weight = lambda shape: pl.BlockSpec(shape, const, pipeline_mode=pl.Buffered(1))
    row_blk = lambda width: pl.BlockSpec((tm, width), lambda i: (i, 0))
    return pl.pallas_call(
        _post_kernel,
        out_shape=jax.ShapeDtypeStruct((rows, D_MODEL), F32),
        grid=(rows // tm,),
        in_specs=[row_blk(D_MODEL), row_blk(DN_W), row_blk(DN_W), vec, vec,
                  weight((2 * DN_W, D_MODEL)), vec, vec,
                  weight((D_MODEL, D_FF)), weight((D_FF, D_MODEL)), vec, vec],
        out_specs=row_blk(D_MODEL),
        compiler_params=pltpu.CompilerParams(dimension_semantics=("arbitrary",),
                                             vmem_limit_bytes=VMEM_LIMIT),
        name="post_mlp",
    )(x, dn, sw, gin, bin_, wo, g1, b1, w1, w2, g2, b2)


def kernel(x_prompt, x_sample, state_delta, state_conv, cache_swa_k, cache_swa_v, meta_tokens, ln_in_g, ln_in_b, w_in, w_conv, dn_a_log, dn_dt_bias, dn_norm, swa_sinks, rel_bias_table, w_o, ln1_g, ln1_b, w_ff1, w_ff2, ln2_g, ln2_b):
    assert w_in.shape[0] == DEPTH == 1
    n_seq_s, len_s = x_sample.shape[0], x_sample.shape[1]
    len_p = x_prompt.shape[1]
    cache_len = cache_swa_k.shape[2]
    chunk_p = 64
    row = lambda t: t.reshape(1, -1).astype(F32)

    w = w_in[0]
    w_sq = (w[:, OFF_SQ:OFF_SK].reshape(D_MODEL, SWA_KV, SWA_G, HEAD_DIM)
            .transpose(0, 2, 1, 3).reshape(D_MODEL, SWA_W))
    w_r = jnp.concatenate(
        [w[:, :OFF_A], w_sq, w[:, OFF_SK:], w[:, OFF_A:OFF_SQ],
         jnp.zeros((D_MODEL, LANES - 2 * HEADS), w.dtype)], axis=1).astype(BF16)
    gin, bin_ = row(ln_in_g), row(ln_in_b)
    pad_lanes = lambda t: jnp.pad(t.reshape(1, -1).astype(F32), ((0, 0), (0, LANES - HEADS)))
    lane_head = jnp.arange(DN_W) // HEAD_DIM
    seg = (lane_head[:, None] == lane_head[None, :]).astype(BF16)
    expand = (jnp.arange(LANES)[:, None] == lane_head[None, :]).astype(BF16)
    delta_consts = (jnp.pad(w_conv[0].astype(F32), ((0, 8 - CONV_W), (0, 0))),
                    pad_lanes(dn_a_log[0]), pad_lanes(dn_dt_bias[0]),
                    jnp.tile(dn_norm[0].astype(F32), HEADS).reshape(1, DN_W), seg, expand)
    post_vecs = (gin, bin_, row(ln1_g[0]), row(ln1_b[0]), row(ln2_g[0]), row(ln2_b[0]))
    wo_sw = (w_o[0][DN_W:].reshape(SWA_KV, SWA_G, HEAD_DIM, D_MODEL)
             .transpose(1, 0, 2, 3).reshape(SWA_W, D_MODEL))
    wo_b = jnp.concatenate([w_o[0][:DN_W], wo_sw], axis=0).astype(BF16)
    w1_b, w2_b = w_ff1[0].astype(BF16), w_ff2[0].astype(BF16)
    table = rel_bias_table.astype(F32).reshape(-1)
    sinks = swa_sinks[0].astype(F32)

    xp = x_prompt[0]
    xs = x_sample.reshape(n_seq_s * len_s, D_MODEL)

    p_qkv, p_z, p_sq, p_sk, p_sv, p_ab = _proj(xp, gin, bin_, w_r, 512)
    s_qkv, s_z, s_sq, s_sk, s_sv, s_ab = _proj(xs, gin, bin_, w_r, 512)
    m_qkv, m_z, m_sq, m_sk, m_sv, m_ab = _proj(meta_tokens.astype(F32), gin, bin_, w_r, N_META)

    zero_conv = jnp.zeros((1, 8, CONV_CH), F32)
    zero_s = jnp.zeros((1, HEADS, HEAD_DIM, HEAD_DIM), F32)
    _, s_meta = _delta(m_qkv, m_z, m_ab, zero_conv, zero_s, delta_consts, 1, N_META, N_META, 1)
    conv_meta = jnp.pad(m_qkv[-(CONV_W - 1):], ((8 - (CONV_W - 1), 0), (0, 0)))[None]
    p_dn, p_s = _delta(p_qkv, p_z, p_ab, conv_meta, s_meta, delta_consts, 1, len_p, chunk_p, 4)
    conv_s = jnp.pad(state_conv[0].astype(F32), ((0, 0), (8 - (CONV_W - 1), 0), (0, 0)))
    s_dn, s_s = _delta(s_qkv, s_z, s_ab, conv_s, state_delta[0].astype(F32), delta_consts,
                       n_seq_s, len_s, len_s, 8)

    span = WINDOW
    lead = jnp.zeros((span - N_META, KV_W), F32)
    units_p = 8
    rows_p = units_p * chunk_p
    prev_blocks = rows_p // span
    kv_specs_p = [pl.BlockSpec((span, KV_W), lambda i: (0, 0)),
                  pl.BlockSpec((span, KV_W), lambda i: (jnp.maximum(i * prev_blocks - 1, 0), 0)),
                  pl.BlockSpec((rows_p, KV_W), lambda i: (i, 0))] * 2
    p_sw = _swa(table, sinks, _bucket_map(chunk_p, span + chunk_p, span), p_sq,
                [jnp.concatenate([lead, m_sk], axis=0), p_sk, p_sk,
                 jnp.concatenate([lead, m_sv], axis=0), p_sv, p_sv],
                kv_specs_p, len_p // rows_p, chunk_p, units_p, True, "swa_prompt")
    ck = cache_swa_k[0].astype(F32).reshape(n_seq_s * cache_len, KV_W)
    cv = cache_swa_v[0].astype(F32).reshape(n_seq_s * cache_len, KV_W)
    units_s = 8
    kv_specs_s = [pl.BlockSpec((units_s * cache_len, KV_W), lambda i: (i, 0)),
                  pl.BlockSpec((units_s * len_s, KV_W), lambda i: (i, 0))] * 2
    s_sw = _swa(table, sinks, _bucket_map(len_s, cache_len + len_s, cache_len), s_sq,
                [ck, s_sk, cv, s_sv], kv_specs_s, n_seq_s // units_s, len_s, units_s, False,
                "swa_sample")

    y_p = _post(xp, p_dn, p_sw, post_vecs, wo_b, w1_b, w2_b, 512)
    y_s = _post(xs, s_dn, s_sw, post_vecs, wo_b, w1_b, w2_b, 512)

    kv_shape = lambda n, length: (1, n, length, SWA_KV, HEAD_DIM)
    return (y_p[None], y_s.reshape(x_sample.shape),
            p_s[None], p_qkv[-(CONV_W - 1):][None, None],
            p_sk[-WINDOW:].reshape(kv_shape(1, WINDOW)), p_sv[-WINDOW:].reshape(kv_shape(1, WINDOW)),
            s_s[None], s_qkv.reshape(n_seq_s, len_s, CONV_CH)[:, -(CONV_W - 1):][None],
            s_sk.reshape(kv_shape(n_seq_s, len_s)), s_sv.reshape(kv_shape(n_seq_s, len_s)))
```

```python
import functools
import math

import jax
import jax.numpy as jnp
from jax import lax
from jax.experimental import pallas as pl
from jax.experimental.pallas import tpu as pltpu

F32 = jnp.float32
BF16 = jnp.bfloat16

D_MODEL = 1024
N_META = 16
HEADS = 8
HEAD_DIM = 64
DN_W = HEADS * HEAD_DIM
CONV_W = 4
CONV_CH = 3 * DN_W
SWA_HEADS = 8
SWA_KV = 2
SWA_G = SWA_HEADS // SWA_KV
KV_W = SWA_KV * HEAD_DIM
WINDOW = 128
N_BUCKETS = 32
MAX_DIST = 128
D_FF = 4 * D_MODEL
DEPTH = 1
DEEP_ALPHA = (2 * DEPTH) ** 0.25
LN_EPS = 1e-5
NORM_EPS = 1e-6
LANES = 128
SUBLANES = 8
VMEM_LIMIT = 56 * 1024 * 1024

OFF_Z = CONV_CH
OFF_A = OFF_Z + DN_W
OFF_B = OFF_A + HEADS
OFF_SQ = OFF_B + HEADS
OFF_SK = OFF_SQ + SWA_HEADS * HEAD_DIM
OFF_SV = OFF_SK + KV_W
PROJ_WIDTH = OFF_SV + KV_W
PROJ_SPLITS = (CONV_CH, DN_W, SWA_HEADS * HEAD_DIM, KV_W, KV_W, LANES)
PROJ_PAD_W = sum(PROJ_SPLITS)


def _layer_norm(x, g, b):
    mu = jnp.mean(x, axis=-1, keepdims=True)
    xc = x - mu
    var = jnp.mean(xc * xc, axis=-1, keepdims=True)
    return xc * lax.rsqrt(var + LN_EPS) * g + b


def _mm(a, b):
    return jnp.dot(a.astype(BF16), b.astype(BF16), preferred_element_type=F32)


def _mm_nt(a, b):
    return lax.dot_general(a.astype(BF16), b.astype(BF16), (((1,), (1,)), ((), ())),
                           preferred_element_type=F32)


def _split3(x):
    x1 = x.astype(BF16)
    r1 = x - x1.astype(F32)
    x2 = r1.astype(BF16)
    x3 = (r1 - x2.astype(F32)).astype(BF16)
    return x1, x2, x3


def _dot_exact_lhs(x, sel):
    m = x.shape[0]
    stacked = jnp.concatenate(_split3(x), axis=0)
    r = jnp.dot(stacked, sel, preferred_element_type=F32)
    return r[:m] + r[m:2 * m] + r[2 * m:]


CONV_BLOCK = 256


def _sigmoid(x):
    return 1.0 / (1.0 + jnp.exp(-x))


def _proj_kernel(x_ref, g_ref, b_ref, w_ref, conv0_ref, wconv_ref,
                 y_ref, z_ref, sq_ref, sk_ref, sv_ref, ab_ref, tail_ref, xbuf, *, seq_len):
    tm = x_ref.shape[0]
    h = _layer_norm(x_ref[...], g_ref[...], b_ref[...]).astype(BF16)
    off = CONV_CH
    for o_ref, width in zip((z_ref, sq_ref, sk_ref, sv_ref, ab_ref), PROJ_SPLITS[1:]):
        o_ref[...] = jnp.dot(h, w_ref[:, off:off + width], preferred_element_type=F32)
        off += width

    chained = seq_len is None
    n_seq = 1 if chained else tm // seq_len
    L = tm if chained else seq_len
    if chained:
        @pl.when(pl.program_id(0) == 0)
        def _():
            xbuf[0, 0:SUBLANES, :] = conv0_ref[0]
    else:
        for s in range(n_seq):
            xbuf[s, 0:SUBLANES, :] = conv0_ref[s]
    for cb in range(0, CONV_CH, CONV_BLOCK):
        cols = slice(cb, cb + CONV_BLOCK)
        raw = jnp.dot(h, w_ref[:, cols], preferred_element_type=F32)
        for s in range(n_seq):
            raw_s = raw[s * L:(s + 1) * L, :]
            ext = jnp.concatenate([xbuf[s, :, cols], raw_s], axis=0)
            y = raw_s * wconv_ref[CONV_W - 1:CONV_W, cols]
            for j in range(CONV_W - 1):
                y = y + (ext[SUBLANES - 3 + j:SUBLANES - 3 + j + L, :]
                         * wconv_ref[j:j + 1, cols])
            y_ref[s * L:(s + 1) * L, cols] = y * _sigmoid(y)
            tail_ref[s, :, cols] = raw_s[L - SUBLANES:, :]
            if chained:
                xbuf[s, :, cols] = raw_s[L - SUBLANES:, :]


def _proj(x, g, b, w_r, conv0, wconv, tm, seq_len):
    rows = x.shape[0]
    chained = seq_len is None
    n_tile_seq = 1 if chained else tm // seq_len
    n_seq = 1 if chained else rows // seq_len
    const = lambda i: (0, 0)
    row_blk = lambda width: pl.BlockSpec((tm, width), lambda i: (i, 0))
    seq_blk = pl.BlockSpec((n_tile_seq, SUBLANES, CONV_CH),
                           (lambda i: (0, 0, 0)) if chained else (lambda i: (i, 0, 0)))
    return pl.pallas_call(
        functools.partial(_proj_kernel, seq_len=seq_len),
        out_shape=[jax.ShapeDtypeStruct((rows, width), F32) for width in PROJ_SPLITS]
                  + [jax.ShapeDtypeStruct((n_seq, SUBLANES, CONV_CH), F32)],
        grid=(rows // tm,),
        in_specs=[row_blk(D_MODEL), pl.BlockSpec((1, D_MODEL), const),
                  pl.BlockSpec((1, D_MODEL), const), pl.BlockSpec((D_MODEL, PROJ_PAD_W), const),
                  seq_blk, pl.BlockSpec((SUBLANES, CONV_CH), const)],
        out_specs=[row_blk(width) for width in PROJ_SPLITS] + [seq_blk],
        scratch_shapes=[pltpu.VMEM((n_tile_seq, SUBLANES, CONV_CH), F32)],
        compiler_params=pltpu.CompilerParams(dimension_semantics=("arbitrary",),
                                             vmem_limit_bytes=VMEM_LIMIT),
        name="ln_in_proj",
    )(x, g, b, w_r, conv0, wconv)


GROUP_HEADS = 4


def _softplus(x):
    return jnp.maximum(x, 0.0) + jnp.log1p(jnp.exp(-jnp.abs(x)))


def _delta_kernel(qkv_ref, z_ref, ab_ref, s0_ref, alog_ref, dtb_ref,
                  normw_ref, seg_ref, expand_ref, o_ref, s_ref, sbd,
                  *, chunk, units, groups, sequential):
    C, U = chunk, units
    RG = U * C
    PAIRS = HEADS // GROUP_HEADS
    PW = GROUP_HEADS * HEAD_DIM
    step = pl.program_id(0)
    last_step = pl.num_programs(0) - 1
    expand = expand_ref[...]

    def seg_sum(t):
        n = t.shape[0]
        hi = t.astype(BF16)
        lo = (t - hi.astype(F32)).astype(BF16)
        both = jnp.concatenate([hi, lo], axis=0)
        r = jnp.concatenate(
            [jnp.dot(both[:, g * PW:(g + 1) * PW], seg_ref[g * PW:(g + 1) * PW, g * PW:(g + 1) * PW],
                     preferred_element_type=F32) for g in range(PAIRS)], axis=1)
        return r[:n] + r[n:]

    ri = lax.broadcasted_iota(jnp.int32, (C, PW), 0)
    ci = lax.broadcasted_iota(jnp.int32, (C, PW), 1) % HEAD_DIM
    causal = (ri >= ci) & (ci < C)
    strict = (ri > ci) & (ci < C)
    eye = (ri == ci).astype(F32)
    lane_head = lax.broadcasted_iota(jnp.int32, (1, PW), 1) // HEAD_DIM
    rs = lax.broadcasted_iota(jnp.int32, (PW, PW), 0)
    cs = lax.broadcasted_iota(jnp.int32, (PW, PW), 1)
    bd_mask = ((rs // HEAD_DIM) == (cs // HEAD_DIM)).astype(F32)
    row_pad = [jnp.zeros((HEAD_DIM - C, PW), BF16)] if C < HEAD_DIM else []
    lane_pad = [jnp.zeros((1, HEAD_DIM - C), F32)] if C < HEAD_DIM else []
    rr = lax.broadcasted_iota(jnp.int32, (RG, RG), 0)
    cc = lax.broadcasted_iota(jnp.int32, (RG, RG), 1)
    tril = ((rr // C == cc // C) & (rr >= cc)).astype(BF16)

    def stack(y):
        yb = y.astype(BF16)
        zero = jnp.zeros_like(yb)
        pad = row_pad if y.shape[0] < HEAD_DIM else []
        blocks = []
        for h in range(GROUP_HEADS):
            blocks += [jnp.where(lane_head == h, yb, zero)] + pad
        return jnp.concatenate(blocks, axis=0)

    items = [(u, p) for u in range(U) for p in range(PAIRS)]
    sl = lambda u, p: (slice(u * C, (u + 1) * C), slice(p * PW, (p + 1) * PW))

    def front(gi):
        rg = slice(gi * RG, (gi + 1) * RG)
        q, k, v = (qkv_ref[rg, j * DN_W:(j + 1) * DN_W] for j in range(3))
        qn = q * (lax.rsqrt(seg_sum(q * q) + NORM_EPS) * HEAD_DIM ** -0.5)
        kn = k * lax.rsqrt(seg_sum(k * k) + NORM_EPS)
        yield

        ab = ab_ref[rg, :]
        g = -jnp.exp(alog_ref[...]) * _softplus(ab + dtb_ref[...])
        gs = jnp.dot(tril, jnp.concatenate(_split3(g), axis=1), preferred_element_type=F32)
        G = gs[:, :LANES] + gs[:, LANES:2 * LANES] + gs[:, 2 * LANES:]
        GT = G.T
        G_x = _dot_exact_lhs(G, expand)
        b_x = _dot_exact_lhs(pltpu.roll(ab, LANES - HEADS, axis=1), expand)
        glast_x = jnp.concatenate(
            [jnp.broadcast_to(G_x[(u + 1) * C - 1:(u + 1) * C, :], (C, DN_W)) for u in range(U)],
            axis=0)
        yield
        beta_x = _sigmoid(b_x)
        eg_x = jnp.exp(G_x)
        kbeta = kn * beta_x
        return dict(qn=qn, kn=kn, kbeta=kbeta, bk=kbeta * eg_x, bv=v * beta_x, qd=qn * eg_x,
                    kd=kn * jnp.exp(glast_x - G_x), G_x=G_x, GT=GT, eg_x=eg_x)

    def middle(f):
        A, QK = {}, {}
        for (u, p) in items:
            ru, lp = sl(u, p)
            gr = []
            for h in range(p * GROUP_HEADS, (p + 1) * GROUP_HEADS):
                gr += [f["GT"][h:h + 1, ru]] + lane_pad
            gr = jnp.concatenate(gr, axis=1)
            decay = jnp.exp(jnp.where(causal, f["G_x"][ru, lp] - gr, -jnp.inf))
            r = _mm_nt(jnp.concatenate([f["kbeta"][ru, lp], f["qn"][ru, lp]], axis=0),
                       stack(f["kn"][ru, lp]))
            A[u, p] = jnp.where(strict, r[:C] * decay, 0.0)
            QK[u, p] = r[C:] * decay
        yield

        n_pow = int(math.log2(C)) - 1
        P = {it: eye - A[it] for it in items}
        Ap = {it: _mm(A[it], stack(A[it])) for it in items}
        yield
        for i in range(n_pow):
            if i < n_pow - 1:
                r = {it: _mm(jnp.concatenate([Ap[it], P[it]], axis=0), stack(Ap[it])) for it in items}
                Ap = {it: r[it][:C] for it in items}
                P = {it: P[it] + r[it][C:] for it in items}
            else:
                P = {it: P[it] + _mm(P[it], stack(Ap[it])) for it in items}
            yield

        out = {}
        for (u, p) in items:
            ru, lp = sl(u, p)
            wu = _mm(P[u, p], jnp.concatenate([stack(f["bk"][ru, lp]), stack(f["bv"][ru, lp])],
                                              axis=1))
            t = _mm(wu.T, f["kd"][ru, lp])
            nt = bd_mask * t[PW:]
            out[u, p] = dict(W=wu[:, :PW], U=wu[:, PW:], MT=(bd_mask * t[:PW]).astype(BF16),
                             NT=sum(nt[h * HEAD_DIM:(h + 1) * HEAD_DIM] for h in range(GROUP_HEADS)),
                             QK=QK[u, p], qd=f["qd"][ru, lp],
                             eg_last=f["eg_x"][(u + 1) * C - 1:(u + 1) * C, lp])
        return out

    def load_state(ref, prefix, p):
        return jnp.concatenate([ref[prefix + (p * GROUP_HEADS + h,)].T
                                for h in range(GROUP_HEADS)], axis=1)

    def store_state(ref, prefix, p, st):
        for h in range(GROUP_HEADS):
            ref[prefix + (p * GROUP_HEADS + h,)] = st[:, h * HEAD_DIM:(h + 1) * HEAD_DIM].T

    chunks = [(gi, u) for gi in range(groups) for u in range(U)]
    if sequential:
        @pl.when(step == 0)
        def _():
            for p in range(PAIRS):
                sbd[p] = load_state(s0_ref, (), p)
        s_start = {p: sbd[p] for p in range(PAIRS)}
    else:
        s_start = {(idx, p): load_state(s0_ref, (idx,), p)
                   for idx in range(len(chunks)) for p in range(PAIRS)}

    def chain(gi, out, s_in):
        if sequential:
            states = {p: [s_in[p]] for p in range(PAIRS)}
            for u in range(U):
                for p in range(PAIRS):
                    d, s_cur = out[u, p], states[p][-1]
                    states[p].append(s_cur * d["eg_last"] - _mm(s_cur, d["MT"]) + d["NT"])
            return ({(u, p): states[p][u] for (u, p) in items},
                    {p: states[p][-1] for p in range(PAIRS)})
        entering = {(u, p): s_in[gi * U + u, p] for (u, p) in items}
        s_out = {}
        for (u, p) in items:
            d, s_cur = out[u, p], entering[u, p]
            s_out[gi * U + u, p] = s_cur * d["eg_last"] - _mm(s_cur, d["MT"]) + d["NT"]
        return entering, s_out

    def finish(gi, out, entering):
        r = {it: _mm_nt(jnp.concatenate([out[it]["W"], out[it]["qd"]], axis=0),
                        stack(entering[it])) for it in items}
        yield
        tiles = {it: r[it][C:] + _mm(out[it]["QK"], stack(out[it]["U"] - r[it][:C]))
                 for it in items}
        yield
        rg = slice(gi * RG, (gi + 1) * RG)
        o = jnp.concatenate([jnp.concatenate([tiles[u, p] for p in range(PAIRS)], axis=1)
                             for u in range(U)], axis=0)
        ms = seg_sum(o * o) * (1.0 / HEAD_DIM)
        z = z_ref[rg, :]
        o_ref[rg, :] = (o * lax.rsqrt(ms + NORM_EPS) * normw_ref[...]
                        * (z * _sigmoid(z))).astype(BF16)

    def drive(gens):
        results = [None] * len(gens)
        active = list(enumerate(gens))
        while active:
            still = []
            for i, gen in active:
                try:
                    next(gen)
                    still.append((i, gen))
                except StopIteration as stop:
                    results[i] = stop.value
            active = still
        return results

    state = s_start
    s_final = {}
    f_cur = drive([front(0)])[0]
    pending = None
    for gi in range(groups + 1):
        gens = []
        if gi < groups:
            gens.append(middle(f_cur))
        if gi + 1 < groups:
            gens.append(front(gi + 1))
        if pending is not None:
            entering, s_out = chain(gi - 1, pending, state)
            s_final.update(s_out)
            state = s_out if sequential else state
            gens.append(finish(gi - 1, pending, entering))
        res = drive(gens)
        pending = res[0] if gi < groups else None
        f_cur = res[1] if gi + 1 < groups else None

    if sequential:
        for p in range(PAIRS):
            sbd[p] = s_final[p]

        @pl.when(step == last_step)
        def _():
            for p in range(PAIRS):
                store_state(s_ref, (), p, s_final[p])
    else:
        for (idx, p), s_new in s_final.items():
            store_state(s_ref, (idx,), p, s_new)


def _delta(qkv, z, ab, s0, consts, n_seq, seq_len, chunk, units, groups):
    alog, dtb, normw, seg, expand = consts
    sequential = n_seq == 1
    rows = n_seq * seq_len
    n_chunks = units * groups
    blk_rows = n_chunks * chunk
    assert rows % blk_rows == 0 and (sequential or seq_len == chunk)
    const2 = lambda shape: pl.BlockSpec(shape, lambda i: (0, 0))
    tok = lambda width: pl.BlockSpec((blk_rows, width), lambda i: (i, 0))
    group_w = GROUP_HEADS * HEAD_DIM
    if sequential:
        state = pl.BlockSpec((None, HEADS, HEAD_DIM, HEAD_DIM), lambda i: (0, 0, 0, 0))
    else:
        state = pl.BlockSpec((n_chunks, HEADS, HEAD_DIM, HEAD_DIM), lambda i: (i, 0, 0, 0))
    return pl.pallas_call(
        functools.partial(_delta_kernel, chunk=chunk, units=units, groups=groups,
                          sequential=sequential),
        out_shape=[jax.ShapeDtypeStruct((rows, DN_W), BF16),
                   jax.ShapeDtypeStruct((n_seq, HEADS, HEAD_DIM, HEAD_DIM), F32)],
        grid=(rows // blk_rows,),
        in_specs=[tok(CONV_CH), tok(DN_W), tok(LANES), state,
                  const2((1, LANES)), const2((1, LANES)),
                  const2((1, DN_W)), const2((DN_W, DN_W)), const2((LANES, DN_W))],
        out_specs=[tok(DN_W), state],
        scratch_shapes=[pltpu.VMEM((HEADS // GROUP_HEADS, HEAD_DIM, group_w), F32)],
        compiler_params=pltpu.CompilerParams(
            dimension_semantics=("arbitrary",), vmem_limit_bytes=VMEM_LIMIT,
        ),
        name=f"delta_c{chunk}",
    )(qkv, z, ab, s0, alog, dtb, normw, seg, expand)


SWA_W = SWA_HEADS * HEAD_DIM


def _swa_kernel(*refs, n_q, units, prompt):
    if prompt:
        (table_ref, sinks_ref, bucket_ref, q_ref, km_ref, kp_ref, kc_ref, vm_ref, vp_ref, vc_ref,
         o_ref, bias_ref) = refs
    else:
        (table_ref, sinks_ref, bucket_ref, q_ref, kp_ref, kc_ref, vp_ref, vc_ref,
         o_ref, bias_ref) = refs
    n_k = bias_ref.shape[-1]
    n_grp = SWA_W // KV_W
    step = pl.program_id(0)

    @pl.when(step == 0)
    def _():
        bucket = bucket_ref[...]
        for h in range(SWA_HEADS):
            acc = jnp.zeros((n_q, n_k), F32)
            for b in range(N_BUCKETS):
                acc = jnp.where(bucket == b, table_ref[b * SWA_HEADS + h], acc)
            kv, gi = divmod(h, SWA_G)
            bias_ref[kv, gi * n_q:(gi + 1) * n_q, :] = acc

    lane = lax.broadcasted_iota(jnp.int32, (1, KV_W), 1)
    kv_mask = [(lane < HEAD_DIM).astype(F32), (lane >= HEAD_DIM).astype(F32)]
    if prompt:
        first = step == 0
        k_all = jnp.concatenate([jnp.where(first, km_ref[...], kp_ref[...]), kc_ref[...]], axis=0)
        v_all = jnp.concatenate([jnp.where(first, vm_ref[...], vp_ref[...]), vc_ref[...]], axis=0)
        k_kv = [(k_all * m).astype(BF16) for m in kv_mask]
        v_kv = [(v_all * m).astype(BF16) for m in kv_mask]
        keys = lambda j, kv: k_kv[kv][j * n_q:j * n_q + n_k]
        vals = lambda j, kv: v_kv[kv][j * n_q:j * n_q + n_k]
    else:
        n_c = n_k - n_q
        cat = lambda a, b, j: jnp.concatenate([a[j * n_c:(j + 1) * n_c, :],
                                               b[j * n_q:(j + 1) * n_q, :]], axis=0)
        keys = lambda j, kv: (cat(kp_ref, kc_ref, j) * kv_mask[kv]).astype(BF16)
        vals = lambda j, kv: (cat(vp_ref, vc_ref, j) * kv_mask[kv]).astype(BF16)

    q = (q_ref[...] * HEAD_DIM ** -0.5).astype(BF16)
    sink = [jnp.concatenate([jnp.full((n_q, 1), sinks_ref[kv * SWA_G + gi], F32)
                             for gi in range(SWA_G)], axis=0) for kv in range(SWA_KV)]
    items = [(j, kv) for j in range(units) for kv in range(SWA_KV)]
    s = {}
    for j in range(units):
        rows = slice(j * n_q, (j + 1) * n_q)
        q4 = jnp.concatenate([q[rows, g * KV_W:(g + 1) * KV_W] for g in range(n_grp)], axis=0)
        for kv in range(SWA_KV):
            sj = lax.dot_general(q4, keys(j, kv), (((1,), (1,)), ((), ())),
                                 preferred_element_type=F32) + bias_ref[kv]
            if prompt and j * n_q < WINDOW - N_META:
                key_pos = j * n_q + lax.broadcasted_iota(jnp.int32, (1, n_k), 1)
                n_invalid = jnp.where(first, WINDOW - N_META, 0)
                sj = jnp.where(key_pos < n_invalid, -jnp.inf, sj)
            s[j, kv] = sj
    m = {it: jnp.maximum(jnp.max(s[it], axis=-1, keepdims=True), sink[it[1]]) for it in items}
    p = {it: jnp.exp(s[it] - m[it]) for it in items}
    den = {it: jnp.sum(p[it], axis=-1, keepdims=True) + jnp.exp(sink[it[1]] - m[it]) for it in items}
    o = {it: jnp.dot(p[it].astype(BF16), vals(*it), preferred_element_type=F32) * (1.0 / den[it])
         for it in items}
    for j in range(units):
        o2 = o[j, 0] + o[j, 1]
        for g in range(n_grp):
            o_ref[j * n_q:(j + 1) * n_q, g * KV_W:(g + 1) * KV_W] = (
                o2[g * n_q:(g + 1) * n_q, :].astype(BF16))


def _swa(table, sinks, bucket, q, kv_args, kv_specs, n_steps, n_q, units, prompt, name):
    n_k = bucket.shape[1]
    smem = pl.BlockSpec(memory_space=pltpu.SMEM)
    rows = units * n_q
    return pl.pallas_call(
        functools.partial(_swa_kernel, n_q=n_q, units=units, prompt=prompt),
        out_shape=jax.ShapeDtypeStruct((n_steps * rows, SWA_W), BF16),
        grid=(n_steps,),
        in_specs=[smem, smem, pl.BlockSpec((n_q, n_k), lambda i: (0, 0)),
                  pl.BlockSpec((rows, SWA_W), lambda i: (i, 0))] + list(kv_specs),
        out_specs=pl.BlockSpec((rows, SWA_W), lambda i: (i, 0)),
        scratch_shapes=[pltpu.VMEM((SWA_KV, SWA_G * n_q, n_k), F32)],
        compiler_params=pltpu.CompilerParams(dimension_semantics=("arbitrary",),
                                             vmem_limit_bytes=VMEM_LIMIT),
        name=name,
    )(table, sinks, bucket, q, *kv_args)


def _t5_bucket(rel):
    half = N_BUCKETS // 2
    max_exact = half // 2
    n = jnp.abs(rel)
    large = max_exact + (jnp.log(jnp.maximum(n, 1).astype(F32) / max_exact)
                         / math.log(MAX_DIST / max_exact) * (half - max_exact)).astype(jnp.int32)
    large = jnp.minimum(large, half - 1)
    return jnp.where(rel > 0, half, 0) + jnp.where(n < max_exact, n, large)


def _bucket_map(n_q, n_k, key_offset):
    rel = (jnp.arange(n_k)[None, :] - key_offset) - jnp.arange(n_q)[:, None]
    return _t5_bucket(rel).astype(jnp.int32)


FF_BLOCK = 1024


def _post_kernel(x_ref, dn_ref, sw_ref, gin_ref, bin_ref, wo_ref, g1_ref, b1_ref,
                 w1_ref, w2_ref, g2_ref, b2_ref, y_ref):
    h = _layer_norm(x_ref[...], gin_ref[...], bin_ref[...])
    mix = (jnp.dot(dn_ref[...], wo_ref[:DN_W, :], preferred_element_type=F32)
           + jnp.dot(sw_ref[...], wo_ref[DN_W:, :], preferred_element_type=F32))
    h1 = _layer_norm(DEEP_ALPHA * h + mix, g1_ref[...], b1_ref[...])
    h1b = h1.astype(BF16)
    f = jnp.zeros_like(h1)
    for j in range(D_FF // FF_BLOCK):
        a = jnp.dot(h1b, w1_ref[:, j * FF_BLOCK:(j + 1) * FF_BLOCK], preferred_element_type=F32)
        a = jnp.square(jnp.maximum(a, 0.0)).astype(BF16)
        f = f + jnp.dot(a, w2_ref[j * FF_BLOCK:(j + 1) * FF_BLOCK, :], preferred_element_type=F32)
    y_ref[...] = _layer_norm(DEEP_ALPHA * h1 + f, g2_ref[...], b2_ref[...])


def _post(x, dn, sw, vecs, wo, w1, w2, tm):
    gin, bin_, g1, b1, g2, b2 = vecs
    rows = x.shape[0]
    const = lambda i: (0, 0)
    vec = pl.BlockSpec((1, D_MODEL), const)
    weight = lambda shape: pl.BlockSpec(shape, const, pipeline_mode=pl.Buffered(1))
    row_blk = lambda width: pl.BlockSpec((tm, width), lambda i: (i, 0))
    return pl.pallas_call(
        _post_kernel,
        out_shape=jax.ShapeDtypeStruct((rows, D_MODEL), F32),
        grid=(rows // tm,),
        in_specs=[row_blk(D_MODEL), row_blk(DN_W), row_blk(DN_W), vec, vec,
                  weight((2 * DN_W, D_MODEL)), vec, vec,
                  weight((D_MODEL, D_FF)), weight((D_FF, D_MODEL)), vec, vec],
        out_specs=row_blk(D_MODEL),
        compiler_params=pltpu.CompilerParams(dimension_semantics=("arbitrary",),
                                             vmem_limit_bytes=VMEM_LIMIT),
        name="post_mlp",
    )(x, dn, sw, gin, bin_, wo, g1, b1, w1, w2, g2, b2)


def kernel(x_prompt, x_sample, state_delta, state_conv, cache_swa_k, cache_swa_v, meta_tokens, ln_in_g, ln_in_b, w_in, w_conv, dn_a_log, dn_dt_bias, dn_norm, swa_sinks, rel_bias_table, w_o, ln1_g, ln1_b, w_ff1, w_ff2, ln2_g, ln2_b):
    assert w_in.shape[0] == DEPTH == 1
    n_seq_s, len_s = x_sample.shape[0], x_sample.shape[1]
    len_p = x_prompt.shape[1]
    cache_len = cache_swa_k.shape[2]
    chunk_p = 64
    row = lambda t: t.reshape(1, -1).astype(F32)

    w = w_in[0]
    w_sq = (w[:, OFF_SQ:OFF_SK].reshape(D_MODEL, SWA_KV, SWA_G, HEAD_DIM)
            .transpose(0, 2, 1, 3).reshape(D_MODEL, SWA_W))
    w_r = jnp.concatenate(
        [w[:, :OFF_A], w_sq, w[:, OFF_SK:], w[:, OFF_A:OFF_SQ],
         jnp.zeros((D_MODEL, LANES - 2 * HEADS), w.dtype)], axis=1).astype(BF16)
    gin, bin_ = row(ln_in_g), row(ln_in_b)
    pad_lanes = lambda t: jnp.pad(t.reshape(1, -1).astype(F32), ((0, 0), (0, LANES - HEADS)))
    lane_head = jnp.arange(DN_W) // HEAD_DIM
    seg = (lane_head[:, None] == lane_head[None, :]).astype(BF16)
    expand = (jnp.arange(LANES)[:, None] == lane_head[None, :]).astype(BF16)
    wconv = jnp.pad(w_conv[0].astype(F32), ((0, SUBLANES - CONV_W), (0, 0)))
    delta_consts = (pad_lanes(dn_a_log[0]), pad_lanes(dn_dt_bias[0]),
                    jnp.tile(dn_norm[0].astype(F32), HEADS).reshape(1, DN_W), seg, expand)
    post_vecs = (gin, bin_, row(ln1_g[0]), row(ln1_b[0]), row(ln2_g[0]), row(ln2_b[0]))
    wo_sw = (w_o[0][DN_W:].reshape(SWA_KV, SWA_G, HEAD_DIM, D_MODEL)
             .transpose(1, 0, 2, 3).reshape(SWA_W, D_MODEL))
    wo_b = jnp.concatenate([w_o[0][:DN_W], wo_sw], axis=0).astype(BF16)
    w1_b, w2_b = w_ff1[0].astype(BF16), w_ff2[0].astype(BF16)
    table = rel_bias_table.astype(F32).reshape(-1)
    sinks = swa_sinks[0].astype(F32)

    xp = x_prompt[0]
    xs = x_sample.reshape(n_seq_s * len_s, D_MODEL)

    conv_pad = SUBLANES - (CONV_W - 1)
    zero_conv = jnp.zeros((1, SUBLANES, CONV_CH), F32)
    conv_s = jnp.pad(state_conv[0].astype(F32), ((0, 0), (conv_pad, 0), (0, 0)))
    m_qkv, m_z, m_sq, m_sk, m_sv, m_ab, m_tail = _proj(
        meta_tokens.astype(F32), gin, bin_, w_r, zero_conv, wconv, N_META, N_META)
    p_qkv, p_z, p_sq, p_sk, p_sv, p_ab, p_tail = _proj(xp, gin, bin_, w_r, m_tail, wconv, 512, None)
    s_qkv, s_z, s_sq, s_sk, s_sv, s_ab, s_tail = _proj(xs, gin, bin_, w_r, conv_s, wconv, 512, len_s)

    zero_s = jnp.zeros((1, HEADS, HEAD_DIM, HEAD_DIM), F32)
    _, s_meta = _delta(m_qkv, m_z, m_ab, zero_s, delta_consts, 1, N_META, N_META, 1, 1)
    p_dn, p_s = _delta(p_qkv, p_z, p_ab, s_meta, delta_consts, 1, len_p, chunk_p, 8, 1)
    s_dn, s_s = _delta(s_qkv, s_z, s_ab, state_delta[0].astype(F32), delta_consts,
                       n_seq_s, len_s, len_s, 8, 2)

    span = WINDOW
    lead = jnp.zeros((span - N_META, KV_W), F32)
    units_p = 8
    rows_p = units_p * chunk_p
    prev_blocks = rows_p // span
    kv_specs_p = [pl.BlockSpec((span, KV_W), lambda i: (0, 0)),
                  pl.BlockSpec((span, KV_W), lambda i: (jnp.maximum(i * prev_blocks - 1, 0), 0)),
                  pl.BlockSpec((rows_p, KV_W), lambda i: (i, 0))] * 2
    p_sw = _swa(table, sinks, _bucket_map(chunk_p, span + chunk_p, span), p_sq,
                [jnp.concatenate([lead, m_sk], axis=0), p_sk, p_sk,
                 jnp.concatenate([lead, m_sv], axis=0), p_sv, p_sv],
                kv_specs_p, len_p // rows_p, chunk_p, units_p, True, "swa_prompt")
    ck = cache_swa_k[0].astype(F32).reshape(n_seq_s * cache_len, KV_W)
    cv = cache_swa_v[0].astype(F32).reshape(n_seq_s * cache_len, KV_W)
    units_s = 8
    kv_specs_s = [pl.BlockSpec((units_s * cache_len, KV_W), lambda i: (i, 0)),
                  pl.BlockSpec((units_s * len_s, KV_W), lambda i: (i, 0))] * 2
    s_sw = _swa(table, sinks, _bucket_map(len_s, cache_len + len_s, cache_len), s_sq,
                [ck, s_sk, cv, s_sv], kv_specs_s, n_seq_s // units_s, len_s, units_s, False,
                "swa_sample")

    y_p = _post(xp, p_dn, p_sw, post_vecs, wo_b, w1_b, w2_b, 512)
    y_s = _post(xs, s_dn, s_sw, post_vecs, wo_b, w1_b, w2_b, 512)

    kv_shape = lambda n, length: (1, n, length, SWA_KV, HEAD_DIM)
    return (y_p[None], y_s.reshape(x_sample.shape),
            p_s[None], p_tail[:, -(CONV_W - 1):][None],
            p_sk[-WINDOW:].reshape(kv_shape(1, WINDOW)), p_sv[-WINDOW:].reshape(kv_shape(1, WINDOW)),
            s_s[None], s_tail[:, -(CONV_W - 1):][None],
            s_sk.reshape(kv_shape(n_seq_s, len_s)), s_sv.reshape(kv_shape(n_seq_s, len_s)))
```

```python
import functools
import math

import jax
import jax.numpy as jnp
from jax import lax
from jax.experimental import pallas as pl
from jax.experimental.pallas import tpu as pltpu

F32 = jnp.float32
BF16 = jnp.bfloat16

D_MODEL = 1024
N_META = 16
HEADS = 8
HEAD_DIM = 64
DN_W = HEADS * HEAD_DIM
CONV_W = 4
CONV_CH = 3 * DN_W
SWA_HEADS = 8
SWA_KV = 2
SWA_G = SWA_HEADS // SWA_KV
KV_W = SWA_KV * HEAD_DIM
WINDOW = 128
N_BUCKETS = 32
MAX_DIST = 128
D_FF = 4 * D_MODEL
DEPTH = 1
DEEP_ALPHA = (2 * DEPTH) ** 0.25
LN_EPS = 1e-5
NORM_EPS = 1e-6
LANES = 128
SUBLANES = 8
VMEM_LIMIT = 56 * 1024 * 1024

OFF_Z = CONV_CH
OFF_A = OFF_Z + DN_W
OFF_B = OFF_A + HEADS
OFF_SQ = OFF_B + HEADS
OFF_SK = OFF_SQ + SWA_HEADS * HEAD_DIM
OFF_SV = OFF_SK + KV_W
PROJ_WIDTH = OFF_SV + KV_W
PROJ_SPLITS = (CONV_CH, DN_W, SWA_HEADS * HEAD_DIM, KV_W, KV_W, LANES)
PROJ_PAD_W = sum(PROJ_SPLITS)


def _layer_norm(x, g, b):
    mu = jnp.mean(x, axis=-1, keepdims=True)
    xc = x - mu
    var = jnp.mean(xc * xc, axis=-1, keepdims=True)
    return xc * lax.rsqrt(var + LN_EPS) * g + b


def _mm(a, b):
    return jnp.dot(a.astype(BF16), b.astype(BF16), preferred_element_type=F32)


def _mm_nt(a, b):
    return lax.dot_general(a.astype(BF16), b.astype(BF16), (((1,), (1,)), ((), ())),
                           preferred_element_type=F32)


def _split3(x):
    x1 = x.astype(BF16)
    r1 = x - x1.astype(F32)
    x2 = r1.astype(BF16)
    x3 = (r1 - x2.astype(F32)).astype(BF16)
    return x1, x2, x3


def _dot_exact_lhs(x, sel):
    m = x.shape[0]
    stacked = jnp.concatenate(_split3(x), axis=0)
    r = jnp.dot(stacked, sel, preferred_element_type=F32)
    return r[:m] + r[m:2 * m] + r[2 * m:]


CONV_BLOCK = 256


def _sigmoid(x):
    return 1.0 / (1.0 + jnp.exp(-x))


def _proj_kernel(x_ref, g_ref, b_ref, w_ref, conv0_ref, wconv_ref,
                 y_ref, z_ref, sq_ref, sk_ref, sv_ref, ab_ref, tail_ref, xbuf, *, seq_len):
    tm = x_ref.shape[0]
    h = _layer_norm(x_ref[...], g_ref[...], b_ref[...]).astype(BF16)
    off = CONV_CH
    for o_ref, width in zip((z_ref, sq_ref, sk_ref, sv_ref, ab_ref), PROJ_SPLITS[1:]):
        o_ref[...] = jnp.dot(h, w_ref[:, off:off + width], preferred_element_type=F32)
        off += width

    chained = seq_len is None
    n_seq = 1 if chained else tm // seq_len
    L = tm if chained else seq_len
    if chained:
        @pl.when(pl.program_id(0) == 0)
        def _():
            xbuf[0, 0:SUBLANES, :] = conv0_ref[0]
    else:
        for s in range(n_seq):
            xbuf[s, 0:SUBLANES, :] = conv0_ref[s]
    for cb in range(0, CONV_CH, CONV_BLOCK):
        cols = slice(cb, cb + CONV_BLOCK)
        raw = jnp.dot(h, w_ref[:, cols], preferred_element_type=F32)
        for s in range(n_seq):
            raw_s = raw[s * L:(s + 1) * L, :]
            ext = jnp.concatenate([xbuf[s, :, cols], raw_s], axis=0)
            y = raw_s * wconv_ref[CONV_W - 1:CONV_W, cols]
            for j in range(CONV_W - 1):
                y = y + (ext[SUBLANES - 3 + j:SUBLANES - 3 + j + L, :]
                         * wconv_ref[j:j + 1, cols])
            y_ref[s * L:(s + 1) * L, cols] = y * _sigmoid(y)
            tail_ref[s, :, cols] = raw_s[L - SUBLANES:, :]
            if chained:
                xbuf[s, :, cols] = raw_s[L - SUBLANES:, :]


def _proj(x, g, b, w_r, conv0, wconv, tm, seq_len):
    rows = x.shape[0]
    chained = seq_len is None
    n_tile_seq = 1 if chained else tm // seq_len
    n_seq = 1 if chained else rows // seq_len
    const = lambda i: (0, 0)
    row_blk = lambda width: pl.BlockSpec((tm, width), lambda i: (i, 0))
    seq_blk = pl.BlockSpec((n_tile_seq, SUBLANES, CONV_CH),
                           (lambda i: (0, 0, 0)) if chained else (lambda i: (i, 0, 0)))
    return pl.pallas_call(
        functools.partial(_proj_kernel, seq_len=seq_len),
        out_shape=[jax.ShapeDtypeStruct((rows, width), F32) for width in PROJ_SPLITS]
                  + [jax.ShapeDtypeStruct((n_seq, SUBLANES, CONV_CH), F32)],
        grid=(rows // tm,),
        in_specs=[row_blk(D_MODEL), pl.BlockSpec((1, D_MODEL), const),
                  pl.BlockSpec((1, D_MODEL), const), pl.BlockSpec((D_MODEL, PROJ_PAD_W), const),
                  seq_blk, pl.BlockSpec((SUBLANES, CONV_CH), const)],
        out_specs=[row_blk(width) for width in PROJ_SPLITS] + [seq_blk],
        scratch_shapes=[pltpu.VMEM((n_tile_seq, SUBLANES, CONV_CH), F32)],
        compiler_params=pltpu.CompilerParams(dimension_semantics=("arbitrary",),
                                             vmem_limit_bytes=VMEM_LIMIT),
        name="ln_in_proj",
    )(x, g, b, w_r, conv0, wconv)


GROUP_HEADS = 4


def _softplus(x):
    return jnp.maximum(x, 0.0) + jnp.log1p(jnp.exp(-jnp.abs(x)))


def _delta_kernel(qkv_ref, z_ref, ab_ref, s0_ref, alog_ref, dtb_ref,
                  normw_ref, seg_ref, expand_ref, o_ref, s_ref, sbd,
                  *, chunk, units, groups, sequential):
    C, U = chunk, units
    RG = U * C
    PAIRS = HEADS // GROUP_HEADS
    PW = GROUP_HEADS * HEAD_DIM
    step = pl.program_id(0)
    last_step = pl.num_programs(0) - 1
    expand = expand_ref[...]

    def seg_sum(t):
        tb = t.astype(BF16)
        return jnp.concatenate(
            [jnp.dot(tb[:, g * PW:(g + 1) * PW], seg_ref[g * PW:(g + 1) * PW, g * PW:(g + 1) * PW],
                     preferred_element_type=F32) for g in range(PAIRS)], axis=1)

    ri = lax.broadcasted_iota(jnp.int32, (C, PW), 0)
    ci = lax.broadcasted_iota(jnp.int32, (C, PW), 1) % HEAD_DIM
    causal = (ri >= ci) & (ci < C)
    strict = (ri > ci) & (ci < C)
    eye = (ri == ci).astype(F32)
    lane_head = lax.broadcasted_iota(jnp.int32, (1, PW), 1) // HEAD_DIM
    rs = lax.broadcasted_iota(jnp.int32, (LANES, LANES), 0)
    cs = lax.broadcasted_iota(jnp.int32, (LANES, LANES), 1)
    half_mask = ((rs // HEAD_DIM) == (cs // HEAD_DIM)).astype(F32)
    row_pad = [jnp.zeros((HEAD_DIM - C, PW), BF16)] if C < HEAD_DIM else []
    lane_pad = [jnp.zeros((1, HEAD_DIM - C), F32)] if C < HEAD_DIM else []
    rr = lax.broadcasted_iota(jnp.int32, (RG, RG), 0)
    cc = lax.broadcasted_iota(jnp.int32, (RG, RG), 1)
    tril = ((rr // C == cc // C) & (rr >= cc)).astype(BF16)

    def stack(y):
        yb = y.astype(BF16)
        zero = jnp.zeros_like(yb)
        pad = row_pad if y.shape[0] < HEAD_DIM else []
        blocks = []
        for h in range(GROUP_HEADS):
            blocks += [jnp.where(lane_head == h, yb, zero)] + pad
        return jnp.concatenate(blocks, axis=0)

    items = [(u, p) for u in range(U) for p in range(PAIRS)]
    sl = lambda u, p: (slice(u * C, (u + 1) * C), slice(p * PW, (p + 1) * PW))

    def front(gi):
        rg = slice(gi * RG, (gi + 1) * RG)
        q, k, v = (qkv_ref[rg, j * DN_W:(j + 1) * DN_W] for j in range(3))
        qn = q * (lax.rsqrt(seg_sum(q * q) + NORM_EPS) * HEAD_DIM ** -0.5)
        kn = k * lax.rsqrt(seg_sum(k * k) + NORM_EPS)
        yield

        ab = ab_ref[rg, :]
        g = -jnp.exp(alog_ref[...]) * _softplus(ab + dtb_ref[...])
        gs = jnp.dot(tril, jnp.concatenate(_split3(g), axis=1), preferred_element_type=F32)
        G = gs[:, :LANES] + gs[:, LANES:2 * LANES] + gs[:, 2 * LANES:]
        GT = G.T
        G_x = _dot_exact_lhs(G, expand)
        beta = _sigmoid(pltpu.roll(ab, LANES - HEADS, axis=1))
        b_hi = beta.astype(BF16)
        b_lo = (beta - b_hi.astype(F32)).astype(BF16)
        bx = jnp.dot(jnp.concatenate([b_hi, b_lo], axis=0), expand, preferred_element_type=F32)
        beta_x = bx[:RG] + bx[RG:]
        glast_x = jnp.concatenate(
            [jnp.broadcast_to(G_x[(u + 1) * C - 1:(u + 1) * C, :], (C, DN_W)) for u in range(U)],
            axis=0)
        yield
        eg_x = jnp.exp(G_x)
        kbeta = kn * beta_x
        return dict(qn=qn, kn=kn, kbeta=kbeta, bk=kbeta * eg_x, bv=v * beta_x, qd=qn * eg_x,
                    kd=kn * jnp.exp(glast_x - G_x), G_x=G_x, GT=GT, eg_x=eg_x)

    def middle(f):
        A, QK = {}, {}
        for (u, p) in items:
            ru, lp = sl(u, p)
            gr = []
            for h in range(p * GROUP_HEADS, (p + 1) * GROUP_HEADS):
                gr += [f["GT"][h:h + 1, ru]] + lane_pad
            gr = jnp.concatenate(gr, axis=1)
            decay = jnp.exp(jnp.where(causal, f["G_x"][ru, lp] - gr, -jnp.inf))
            r = _mm_nt(jnp.concatenate([f["kbeta"][ru, lp], f["qn"][ru, lp]], axis=0),
                       stack(f["kn"][ru, lp]))
            A[u, p] = jnp.where(strict, r[:C] * decay, 0.0)
            QK[u, p] = r[C:] * decay
        yield

        n_pow = int(math.log2(C)) - 1
        P = {it: eye - A[it] for it in items}
        Ap = {it: _mm(A[it], stack(A[it])) for it in items}
        yield
        for i in range(n_pow):
            if i < n_pow - 1:
                r = {it: _mm(jnp.concatenate([Ap[it], P[it]], axis=0), stack(Ap[it])) for it in items}
                Ap = {it: r[it][:C] for it in items}
                P = {it: P[it] + r[it][C:] for it in items}
            else:
                P = {it: P[it] + _mm(P[it], stack(Ap[it])) for it in items}
            yield

        out = {}
        for (u, p) in items:
            ru, lp = sl(u, p)
            wu = _mm(P[u, p], jnp.concatenate([stack(f["bk"][ru, lp]), stack(f["bv"][ru, lp])],
                                              axis=1))
            W, Uu, kd = wu[:, :PW], wu[:, PW:], f["kd"][ru, lp]
            Ms, NTs = [], []
            for hb in range(PW // LANES):
                cols = slice(hb * LANES, (hb + 1) * LANES)
                mn = _mm(kd[:, cols].T, jnp.concatenate([W[:, cols], Uu[:, cols]], axis=1))
                Ms.append(half_mask * mn[:, :LANES])
                n_t = (half_mask * mn[:, LANES:]).T
                NTs.append(n_t[:HEAD_DIM] + n_t[HEAD_DIM:])
            out[u, p] = dict(W=W, U=Uu, M=diag_blocks(Ms).astype(BF16),
                             NT=jnp.concatenate(NTs, axis=1), QK=QK[u, p], qd=f["qd"][ru, lp],
                             eg_last=f["eg_x"][(u + 1) * C - 1:(u + 1) * C, lp])
        return out

    def diag_blocks(blocks):
        n = len(blocks)
        rows = []
        for i, blk in enumerate(blocks):
            rows.append(jnp.concatenate(
                [blk if j == i else jnp.zeros_like(blk) for j in range(n)], axis=1))
        return jnp.concatenate(rows, axis=0)

    def load_state(ref, prefix, p):
        return jnp.concatenate([ref[prefix + (p * GROUP_HEADS + h,)].T
                                for h in range(GROUP_HEADS)], axis=1)

    def store_state(ref, prefix, p, st):
        for h in range(GROUP_HEADS):
            ref[prefix + (p * GROUP_HEADS + h,)] = st[:, h * HEAD_DIM:(h + 1) * HEAD_DIM].T

    chunks = [(gi, u) for gi in range(groups) for u in range(U)]
    if sequential:
        @pl.when(step == 0)
        def _():
            for p in range(PAIRS):
                sbd[p] = load_state(s0_ref, (), p)
        s_start = {p: sbd[p] for p in range(PAIRS)}
    else:
        s_start = {(idx, p): load_state(s0_ref, (idx,), p)
                   for idx in range(len(chunks)) for p in range(PAIRS)}

    def chain(gi, out, s_in):
        if sequential:
            states = {p: [s_in[p]] for p in range(PAIRS)}
            for u in range(U):
                for p in range(PAIRS):
                    d, s_cur = out[u, p], states[p][-1]
                    states[p].append(s_cur * d["eg_last"] - _mm_nt(s_cur, d["M"]) + d["NT"])
            return ({(u, p): states[p][u] for (u, p) in items},
                    {p: states[p][-1] for p in range(PAIRS)})
        entering = {(u, p): s_in[gi * U + u, p] for (u, p) in items}
        s_out = {}
        for (u, p) in items:
            d, s_cur = out[u, p], entering[u, p]
            s_out[gi * U + u, p] = s_cur * d["eg_last"] - _mm_nt(s_cur, d["M"]) + d["NT"]
        return entering, s_out

    def finish(gi, out, entering):
        r = {it: _mm_nt(jnp.concatenate([out[it]["W"], out[it]["qd"]], axis=0),
                        stack(entering[it])) for it in items}
        yield
        tiles = {it: r[it][C:] + _mm(out[it]["QK"], stack(out[it]["U"] - r[it][:C]))
                 for it in items}
        yield
        rg = slice(gi * RG, (gi + 1) * RG)
        o = jnp.concatenate([jnp.concatenate([tiles[u, p] for p in range(PAIRS)], axis=1)
                             for u in range(U)], axis=0)
        ms = seg_sum(o * o) * (1.0 / HEAD_DIM)
        z = z_ref[rg, :]
        o_ref[rg, :] = (o * lax.rsqrt(ms + NORM_EPS) * normw_ref[...]
                        * (z * _sigmoid(z))).astype(BF16)

    def drive(gens):
        results = [None] * len(gens)
        active = list(enumerate(gens))
        while active:
            still = []
            for i, gen in active:
                try:
                    next(gen)
                    still.append((i, gen))
                except StopIteration as stop:
                    results[i] = stop.value
            active = still
        return results

    state = s_start
    s_final = {}
    f_cur = drive([front(0)])[0]
    pending = None
    for gi in range(groups + 1):
        gens = []
        if gi < groups:
            gens.append(middle(f_cur))
        if gi + 1 < groups:
            gens.append(front(gi + 1))
        if pending is not None:
            entering, s_out = chain(gi - 1, pending, state)
            s_final.update(s_out)
            state = s_out if sequential else state
            gens.append(finish(gi - 1, pending, entering))
        res = drive(gens)
        pending = res[0] if gi < groups else None
        f_cur = res[1] if gi + 1 < groups else None

    if sequential:
        for p in range(PAIRS):
            sbd[p] = s_final[p]

        @pl.when(step == last_step)
        def _():
            for p in range(PAIRS):
                store_state(s_ref, (), p, s_final[p])
    else:
        for (idx, p), s_new in s_final.items():
            store_state(s_ref, (idx,), p, s_new)


def _delta(qkv, z, ab, s0, consts, n_seq, seq_len, chunk, units, groups):
    alog, dtb, normw, seg, expand = consts
    sequential = n_seq == 1
    rows = n_seq * seq_len
    n_chunks = units * groups
    blk_rows = n_chunks * chunk
    assert rows % blk_rows == 0 and (sequential or seq_len == chunk)
    const2 = lambda shape: pl.BlockSpec(shape, lambda i: (0, 0))
    tok = lambda width: pl.BlockSpec((blk_rows, width), lambda i: (i, 0))
    group_w = GROUP_HEADS * HEAD_DIM
    if sequential:
        state = pl.BlockSpec((None, HEADS, HEAD_DIM, HEAD_DIM), lambda i: (0, 0, 0, 0))
    else:
        state = pl.BlockSpec((n_chunks, HEADS, HEAD_DIM, HEAD_DIM), lambda i: (i, 0, 0, 0))
    return pl.pallas_call(
        functools.partial(_delta_kernel, chunk=chunk, units=units, groups=groups,
                          sequential=sequential),
        out_shape=[jax.ShapeDtypeStruct((rows, DN_W), BF16),
                   jax.ShapeDtypeStruct((n_seq, HEADS, HEAD_DIM, HEAD_DIM), F32)],
        grid=(rows // blk_rows,),
        in_specs=[tok(CONV_CH), tok(DN_W), tok(LANES), state,
                  const2((1, LANES)), const2((1, LANES)),
                  const2((1, DN_W)), const2((DN_W, DN_W)), const2((LANES, DN_W))],
        out_specs=[tok(DN_W), state],
        scratch_shapes=[pltpu.VMEM((HEADS // GROUP_HEADS, HEAD_DIM, group_w), F32)],
        compiler_params=pltpu.CompilerParams(
            dimension_semantics=("arbitrary",), vmem_limit_bytes=VMEM_LIMIT,
        ),
        name=f"delta_c{chunk}",
    )(qkv, z, ab, s0, alog, dtb, normw, seg, expand)


SWA_W = SWA_HEADS * HEAD_DIM


def _swa_kernel(*refs, n_q, units, prompt):
    if prompt:
        (table_ref, sinks_ref, bucket_ref, q_ref, km_ref, kp_ref, kc_ref, vm_ref, vp_ref, vc_ref,
         o_ref, bias_ref) = refs
    else:
        (table_ref, sinks_ref, bucket_ref, q_ref, kp_ref, kc_ref, vp_ref, vc_ref,
         o_ref, bias_ref) = refs
    n_k = bias_ref.shape[-1]
    n_grp = SWA_W // KV_W
    step = pl.program_id(0)

    @pl.when(step == 0)
    def _():
        bucket = bucket_ref[...]
        for h in range(SWA_HEADS):
            acc = jnp.zeros((n_q, n_k), F32)
            for b in range(N_BUCKETS):
                acc = jnp.where(bucket == b, table_ref[b * SWA_HEADS + h], acc)
            kv, gi = divmod(h, SWA_G)
            bias_ref[kv, gi * n_q:(gi + 1) * n_q, :] = acc

    lane = lax.broadcasted_iota(jnp.int32, (1, KV_W), 1)
    kv_mask = [(lane < HEAD_DIM).astype(F32), (lane >= HEAD_DIM).astype(F32)]
    if prompt:
        first = step == 0
        k_all = jnp.concatenate([jnp.where(first, km_ref[...], kp_ref[...]), kc_ref[...]], axis=0)
        v_all = jnp.concatenate([jnp.where(first, vm_ref[...], vp_ref[...]), vc_ref[...]], axis=0)
        k_kv = [(k_all * m).astype(BF16) for m in kv_mask]
        v_kv = [(v_all * m).astype(BF16) for m in kv_mask]
        keys = lambda j, kv: k_kv[kv][j * n_q:j * n_q + n_k]
        vals = lambda j, kv: v_kv[kv][j * n_q:j * n_q + n_k]
    else:
        n_c = n_k - n_q
        cat = lambda a, b, j: jnp.concatenate([a[j * n_c:(j + 1) * n_c, :],
                                               b[j * n_q:(j + 1) * n_q, :]], axis=0)
        keys = lambda j, kv: (cat(kp_ref, kc_ref, j) * kv_mask[kv]).astype(BF16)
        vals = lambda j, kv: (cat(vp_ref, vc_ref, j) * kv_mask[kv]).astype(BF16)

    q = (q_ref[...] * HEAD_DIM ** -0.5).astype(BF16)
    sink = [jnp.concatenate([jnp.full((n_q, 1), sinks_ref[kv * SWA_G + gi], F32)
                             for gi in range(SWA_G)], axis=0) for kv in range(SWA_KV)]
    items = [(j, kv) for j in range(units) for kv in range(SWA_KV)]
    s = {}
    for j in range(units):
        rows = slice(j * n_q, (j + 1) * n_q)
        q4 = jnp.concatenate([q[rows, g * KV_W:(g + 1) * KV_W] for g in range(n_grp)], axis=0)
        for kv in range(SWA_KV):
            sj = lax.dot_general(q4, keys(j, kv), (((1,), (1,)), ((), ())),
                                 preferred_element_type=F32) + bias_ref[kv]
            if prompt and j * n_q < WINDOW - N_META:
                key_pos = j * n_q + lax.broadcasted_iota(jnp.int32, (1, n_k), 1)
                n_invalid = jnp.where(first, WINDOW - N_META, 0)
                sj = jnp.where(key_pos < n_invalid, -jnp.inf, sj)
            s[j, kv] = sj
    m = {it: jnp.maximum(jnp.max(s[it], axis=-1, keepdims=True), sink[it[1]]) for it in items}
    p = {it: jnp.exp(s[it] - m[it]) for it in items}
    den = {it: jnp.sum(p[it], axis=-1, keepdims=True) + jnp.exp(sink[it[1]] - m[it]) for it in items}
    o = {it: jnp.dot(p[it].astype(BF16), vals(*it), preferred_element_type=F32) * (1.0 / den[it])
         for it in items}
    for j in range(units):
        o2 = o[j, 0] + o[j, 1]
        for g in range(n_grp):
            o_ref[j * n_q:(j + 1) * n_q, g * KV_W:(g + 1) * KV_W] = (
                o2[g * n_q:(g + 1) * n_q, :].astype(BF16))


def _swa(table, sinks, bucket, q, kv_args, kv_specs, n_steps, n_q, units, prompt, name):
    n_k = bucket.shape[1]
    smem = pl.BlockSpec(memory_space=pltpu.SMEM)
    rows = units * n_q
    return pl.pallas_call(
        functools.partial(_swa_kernel, n_q=n_q, units=units, prompt=prompt),
        out_shape=jax.ShapeDtypeStruct((n_steps * rows, SWA_W), BF16),
        grid=(n_steps,),
        in_specs=[smem, smem, pl.BlockSpec((n_q, n_k), lambda i: (0, 0)),
                  pl.BlockSpec((rows, SWA_W), lambda i: (i, 0))] + list(kv_specs),
        out_specs=pl.BlockSpec((rows, SWA_W), lambda i: (i, 0)),
        scratch_shapes=[pltpu.VMEM((SWA_KV, SWA_G * n_q, n_k), F32)],
        compiler_params=pltpu.CompilerParams(dimension_semantics=("arbitrary",),
                                             vmem_limit_bytes=VMEM_LIMIT),
        name=name,
    )(table, sinks, bucket, q, *kv_args)


def _t5_bucket(rel):
    half = N_BUCKETS // 2
    max_exact = half // 2
    n = jnp.abs(rel)
    large = max_exact + (jnp.log(jnp.maximum(n, 1).astype(F32) / max_exact)
                         / math.log(MAX_DIST / max_exact) * (half - max_exact)).astype(jnp.int32)
    large = jnp.minimum(large, half - 1)
    return jnp.where(rel > 0, half, 0) + jnp.where(n < max_exact, n, large)


def _bucket_map(n_q, n_k, key_offset):
    rel = (jnp.arange(n_k)[None, :] - key_offset) - jnp.arange(n_q)[:, None]
    return _t5_bucket(rel).astype(jnp.int32)


FF_BLOCK = 1024


def _post_kernel(x_ref, dn_ref, sw_ref, gin_ref, bin_ref, wo_ref, g1_ref, b1_ref,
                 w1_ref, w2_ref, g2_ref, b2_ref, y_ref):
    h = _layer_norm(x_ref[...], gin_ref[...], bin_ref[...])
    mix = (jnp.dot(dn_ref[...], wo_ref[:DN_W, :], preferred_element_type=F32)
           + jnp.dot(sw_ref[...], wo_ref[DN_W:, :], preferred_element_type=F32))
    h1 = _layer_norm(DEEP_ALPHA * h + mix, g1_ref[...], b1_ref[...])
    h1b = h1.astype(BF16)
    f = jnp.zeros_like(h1)
    for j in range(D_FF // FF_BLOCK):
        a = jnp.dot(h1b, w1_ref[:, j * FF_BLOCK:(j + 1) * FF_BLOCK], preferred_element_type=F32)
        a = jnp.square(jnp.maximum(a, 0.0)).astype(BF16)
        f = f + jnp.dot(a, w2_ref[j * FF_BLOCK:(j + 1) * FF_BLOCK, :], preferred_element_type=F32)
    y_ref[...] = _layer_norm(DEEP_ALPHA * h1 + f, g2_ref[...], b2_ref[...])


def _post(x, dn, sw, vecs, wo, w1, w2, tm):
    gin, bin_, g1, b1, g2, b2 = vecs
    rows = x.shape[0]
    const = lambda i: (0, 0)
    vec = pl.BlockSpec((1, D_MODEL), const)
    weight = lambda shape: pl.BlockSpec(shape, const, pipeline_mode=pl.Buffered(1))
    row_blk = lambda width: pl.BlockSpec((tm, width), lambda i: (i, 0))
    return pl.pallas_call(
        _post_kernel,
        out_shape=jax.ShapeDtypeStruct((rows, D_MODEL), F32),
        grid=(rows // tm,),
        in_specs=[row_blk(D_MODEL), row_blk(DN_W), row_blk(DN_W), vec, vec,
                  weight((2 * DN_W, D_MODEL)), vec, vec,
                  weight((D_MODEL, D_FF)), weight((D_FF, D_MODEL)), vec, vec],
        out_specs=row_blk(D_MODEL),
        compiler_params=pltpu.CompilerParams(dimension_semantics=("arbitrary",),
                                             vmem_limit_bytes=VMEM_LIMIT),
        name="post_mlp",
    )(x, dn, sw, gin, bin_, wo, g1, b1, w1, w2, g2, b2)


def kernel(x_prompt, x_sample, state_delta, state_conv, cache_swa_k, cache_swa_v, meta_tokens, ln_in_g, ln_in_b, w_in, w_conv, dn_a_log, dn_dt_bias, dn_norm, swa_sinks, rel_bias_table, w_o, ln1_g, ln1_b, w_ff1, w_ff2, ln2_g, ln2_b):
    assert w_in.shape[0] == DEPTH == 1
    n_seq_s, len_s = x_sample.shape[0], x_sample.shape[1]
    len_p = x_prompt.shape[1]
    cache_len = cache_swa_k.shape[2]
    chunk_p = 64
    row = lambda t: t.reshape(1, -1).astype(F32)

    w = w_in[0]
    w_sq = (w[:, OFF_SQ:OFF_SK].reshape(D_MODEL, SWA_KV, SWA_G, HEAD_DIM)
            .transpose(0, 2, 1, 3).reshape(D_MODEL, SWA_W))
    w_r = jnp.concatenate(
        [w[:, :OFF_A], w_sq, w[:, OFF_SK:], w[:, OFF_A:OFF_SQ],
         jnp.zeros((D_MODEL, LANES - 2 * HEADS), w.dtype)], axis=1).astype(BF16)
    gin, bin_ = row(ln_in_g), row(ln_in_b)
    pad_lanes = lambda t: jnp.pad(t.reshape(1, -1).astype(F32), ((0, 0), (0, LANES - HEADS)))
    lane_head = jnp.arange(DN_W) // HEAD_DIM
    seg = (lane_head[:, None] == lane_head[None, :]).astype(BF16)
    expand = (jnp.arange(LANES)[:, None] == lane_head[None, :]).astype(BF16)
    wconv = jnp.pad(w_conv[0].astype(F32), ((0, SUBLANES - CONV_W), (0, 0)))
    delta_consts = (pad_lanes(dn_a_log[0]), pad_lanes(dn_dt_bias[0]),
                    jnp.tile(dn_norm[0].astype(F32), HEADS).reshape(1, DN_W), seg, expand)
    post_vecs = (gin, bin_, row(ln1_g[0]), row(ln1_b[0]), row(ln2_g[0]), row(ln2_b[0]))
    wo_sw = (w_o[0][DN_W:].reshape(SWA_KV, SWA_G, HEAD_DIM, D_MODEL)
             .transpose(1, 0, 2, 3).reshape(SWA_W, D_MODEL))
    wo_b = jnp.concatenate([w_o[0][:DN_W], wo_sw], axis=0).astype(BF16)
    w1_b, w2_b = w_ff1[0].astype(BF16), w_ff2[0].astype(BF16)
    table = rel_bias_table.astype(F32).reshape(-1)
    sinks = swa_sinks[0].astype(F32)

    xp = x_prompt[0]
    xs = x_sample.reshape(n_seq_s * len_s, D_MODEL)

    conv_pad = SUBLANES - (CONV_W - 1)
    zero_conv = jnp.zeros((1, SUBLANES, CONV_CH), F32)
    conv_s = jnp.pad(state_conv[0].astype(F32), ((0, 0), (conv_pad, 0), (0, 0)))
    m_qkv, m_z, m_sq, m_sk, m_sv, m_ab, m_tail = _proj(
        meta_tokens.astype(F32), gin, bin_, w_r, zero_conv, wconv, N_META, N_META)
    p_qkv, p_z, p_sq, p_sk, p_sv, p_ab, p_tail = _proj(xp, gin, bin_, w_r, m_tail, wconv, 512, None)
    s_qkv, s_z, s_sq, s_sk, s_sv, s_ab, s_tail = _proj(xs, gin, bin_, w_r, conv_s, wconv, 512, len_s)

    zero_s = jnp.zeros((1, HEADS, HEAD_DIM, HEAD_DIM), F32)
    _, s_meta = _delta(m_qkv, m_z, m_ab, zero_s, delta_consts, 1, N_META, N_META, 1, 1)
    p_dn, p_s = _delta(p_qkv, p_z, p_ab, s_meta, delta_consts, 1, len_p, chunk_p, 8, 1)
    s_dn, s_s = _delta(s_qkv, s_z, s_ab, state_delta[0].astype(F32), delta_consts,
                       n_seq_s, len_s, len_s, 8, 2)

    span = WINDOW
    lead = jnp.zeros((span - N_META, KV_W), F32)
    units_p = 8
    rows_p = units_p * chunk_p
    prev_blocks = rows_p // span
    kv_specs_p = [pl.BlockSpec((span, KV_W), lambda i: (0, 0)),
                  pl.BlockSpec((span, KV_W), lambda i: (jnp.maximum(i * prev_blocks - 1, 0), 0)),
                  pl.BlockSpec((rows_p, KV_W), lambda i: (i, 0))] * 2
    p_sw = _swa(table, sinks, _bucket_map(chunk_p, span + chunk_p, span), p_sq,
                [jnp.concatenate([lead, m_sk], axis=0), p_sk, p_sk,
                 jnp.concatenate([lead, m_sv], axis=0), p_sv, p_sv],
                kv_specs_p, len_p // rows_p, chunk_p, units_p, True, "swa_prompt")
    ck = cache_swa_k[0].astype(F32).reshape(n_seq_s * cache_len, KV_W)
    cv = cache_swa_v[0].astype(F32).reshape(n_seq_s * cache_len, KV_W)
    units_s = 8
    kv_specs_s = [pl.BlockSpec((units_s * cache_len, KV_W), lambda i: (i, 0)),
                  pl.BlockSpec((units_s * len_s, KV_W), lambda i: (i, 0))] * 2
    s_sw = _swa(table, sinks, _bucket_map(len_s, cache_len + len_s, cache_len), s_sq,
                [ck, s_sk, cv, s_sv], kv_specs_s, n_seq_s // units_s, len_s, units_s, False,
                "swa_sample")

    y_p = _post(xp, p_dn, p_sw, post_vecs, wo_b, w1_b, w2_b, 512)
    y_s = _post(xs, s_dn, s_sw, post_vecs, wo_b, w1_b, w2_b, 512)

    kv_shape = lambda n, length: (1, n, length, SWA_KV, HEAD_DIM)
    return (y_p[None], y_s.reshape(x_sample.shape),
            p_s[None], p_tail[:, -(CONV_W - 1):][None],
            p_sk[-WINDOW:].reshape(kv_shape(1, WINDOW)), p_sv[-WINDOW:].reshape(kv_shape(1, WINDOW)),
            s_s[None], s_tail[:, -(CONV_W - 1):][None],
            s_sk.reshape(kv_shape(n_seq_s, len_s)), s_sv.reshape(kv_shape(n_seq_s, len_s)))
```

```python
import functools
import math

import jax
import jax.numpy as jnp
from jax import lax
from jax.experimental import pallas as pl
from jax.experimental.pallas import tpu as pltpu

F32 = jnp.float32
BF16 = jnp.bfloat16

D_MODEL = 1024
N_META = 16
HEADS = 8
HEAD_DIM = 64
DN_W = HEADS * HEAD_DIM
CONV_W = 4
CONV_CH = 3 * DN_W
SWA_HEADS = 8
SWA_KV = 2
SWA_G = SWA_HEADS // SWA_KV
KV_W = SWA_KV * HEAD_DIM
WINDOW = 128
N_BUCKETS = 32
MAX_DIST = 128
D_FF = 4 * D_MODEL
DEPTH = 1
DEEP_ALPHA = (2 * DEPTH) ** 0.25
LN_EPS = 1e-5
NORM_EPS = 1e-6
LANES = 128
SUBLANES = 8
VMEM_LIMIT = 56 * 1024 * 1024

OFF_Z = CONV_CH
OFF_A = OFF_Z + DN_W
OFF_B = OFF_A + HEADS
OFF_SQ = OFF_B + HEADS
OFF_SK = OFF_SQ + SWA_HEADS * HEAD_DIM
OFF_SV = OFF_SK + KV_W
PROJ_WIDTH = OFF_SV + KV_W
PROJ_SPLITS = (CONV_CH, DN_W, SWA_HEADS * HEAD_DIM, KV_W, KV_W, LANES)
PROJ_PAD_W = sum(PROJ_SPLITS)


def _layer_norm(x, g, b):
    mu = jnp.mean(x, axis=-1, keepdims=True)
    xc = x - mu
    var = jnp.mean(xc * xc, axis=-1, keepdims=True)
    return xc * lax.rsqrt(var + LN_EPS) * g + b


def _mm(a, b):
    return jnp.dot(a.astype(BF16), b.astype(BF16), preferred_element_type=F32)


def _mm_nt(a, b):
    return lax.dot_general(a.astype(BF16), b.astype(BF16), (((1,), (1,)), ((), ())),
                           preferred_element_type=F32)


def _split3(x):
    x1 = x.astype(BF16)
    r1 = x - x1.astype(F32)
    x2 = r1.astype(BF16)
    x3 = (r1 - x2.astype(F32)).astype(BF16)
    return x1, x2, x3


def _dot_exact_lhs(x, sel):
    m = x.shape[0]
    stacked = jnp.concatenate(_split3(x), axis=0)
    r = jnp.dot(stacked, sel, preferred_element_type=F32)
    return r[:m] + r[m:2 * m] + r[2 * m:]


CONV_BLOCK = 256


def _sigmoid(x):
    return 1.0 / (1.0 + jnp.exp(-x))


def _proj_kernel(x_ref, g_ref, b_ref, w_ref, conv0_ref, wconv_ref,
                 y_ref, z_ref, sq_ref, sk_ref, sv_ref, ab_ref, tail_ref, xbuf, *, seq_len):
    tm = x_ref.shape[0]
    h = _layer_norm(x_ref[...], g_ref[...], b_ref[...]).astype(BF16)
    off = CONV_CH
    for o_ref, width in zip((z_ref, sq_ref, sk_ref, sv_ref, ab_ref), PROJ_SPLITS[1:]):
        o_ref[...] = jnp.dot(h, w_ref[:, off:off + width], preferred_element_type=F32)
        off += width

    chained = seq_len is None
    n_seq = 1 if chained else tm // seq_len
    L = tm if chained else seq_len
    if chained:
        @pl.when(pl.program_id(0) == 0)
        def _():
            xbuf[0, 0:SUBLANES, :] = conv0_ref[0]
    else:
        for s in range(n_seq):
            xbuf[s, 0:SUBLANES, :] = conv0_ref[s]
    for cb in range(0, CONV_CH, CONV_BLOCK):
        cols = slice(cb, cb + CONV_BLOCK)
        raw = jnp.dot(h, w_ref[:, cols], preferred_element_type=F32)
        for s in range(n_seq):
            raw_s = raw[s * L:(s + 1) * L, :]
            ext = jnp.concatenate([xbuf[s, :, cols], raw_s], axis=0)
            y = raw_s * wconv_ref[CONV_W - 1:CONV_W, cols]
            for j in range(CONV_W - 1):
                y = y + (ext[SUBLANES - 3 + j:SUBLANES - 3 + j + L, :]
                         * wconv_ref[j:j + 1, cols])
            y_ref[s * L:(s + 1) * L, cols] = y * _sigmoid(y)
            tail_ref[s, :, cols] = raw_s[L - SUBLANES:, :]
            if chained:
                xbuf[s, :, cols] = raw_s[L - SUBLANES:, :]


def _proj(x, g, b, w_r, conv0, wconv, tm, seq_len):
    rows = x.shape[0]
    chained = seq_len is None
    n_tile_seq = 1 if chained else tm // seq_len
    n_seq = 1 if chained else rows // seq_len
    const = lambda i: (0, 0)
    row_blk = lambda width: pl.BlockSpec((tm, width), lambda i: (i, 0))
    seq_blk = pl.BlockSpec((n_tile_seq, SUBLANES, CONV_CH),
                           (lambda i: (0, 0, 0)) if chained else (lambda i: (i, 0, 0)))
    return pl.pallas_call(
        functools.partial(_proj_kernel, seq_len=seq_len),
        out_shape=[jax.ShapeDtypeStruct((rows, width), F32) for width in PROJ_SPLITS]
                  + [jax.ShapeDtypeStruct((n_seq, SUBLANES, CONV_CH), F32)],
        grid=(rows // tm,),
        in_specs=[row_blk(D_MODEL), pl.BlockSpec((1, D_MODEL), const),
                  pl.BlockSpec((1, D_MODEL), const), pl.BlockSpec((D_MODEL, PROJ_PAD_W), const),
                  seq_blk, pl.BlockSpec((SUBLANES, CONV_CH), const)],
        out_specs=[row_blk(width) for width in PROJ_SPLITS] + [seq_blk],
        scratch_shapes=[pltpu.VMEM((n_tile_seq, SUBLANES, CONV_CH), F32)],
        compiler_params=pltpu.CompilerParams(dimension_semantics=("arbitrary",),
                                             vmem_limit_bytes=VMEM_LIMIT),
        name="ln_in_proj",
    )(x, g, b, w_r, conv0, wconv)


GROUP_HEADS = 4


def _softplus(x):
    return jnp.maximum(x, 0.0) + jnp.log1p(jnp.exp(-jnp.abs(x)))


def _delta_kernel(qkv_ref, z_ref, ab_ref, s0_ref, alog_ref, dtb_ref,
                  normw_ref, seg_ref, expand_ref, o_ref, s_ref, sbd,
                  *, chunk, units, groups, sequential):
    C, U = chunk, units
    RG = U * C
    PAIRS = HEADS // GROUP_HEADS
    PW = GROUP_HEADS * HEAD_DIM
    step = pl.program_id(0)
    last_step = pl.num_programs(0) - 1
    expand = expand_ref[...]

    def seg_sum(t):
        tb = t.astype(BF16)
        return jnp.concatenate(
            [jnp.dot(tb[:, g * PW:(g + 1) * PW], seg_ref[g * PW:(g + 1) * PW, g * PW:(g + 1) * PW],
                     preferred_element_type=F32) for g in range(PAIRS)], axis=1)

    ri = lax.broadcasted_iota(jnp.int32, (C, PW), 0)
    ci = lax.broadcasted_iota(jnp.int32, (C, PW), 1) % HEAD_DIM
    causal = (ri >= ci) & (ci < C)
    strict = (ri > ci) & (ci < C)
    eye = (ri == ci).astype(F32)
    lane_head = lax.broadcasted_iota(jnp.int32, (1, PW), 1) // HEAD_DIM
    rs = lax.broadcasted_iota(jnp.int32, (LANES, LANES), 0)
    cs = lax.broadcasted_iota(jnp.int32, (LANES, LANES), 1)
    half_mask = ((rs // HEAD_DIM) == (cs // HEAD_DIM)).astype(F32)
    row_pad = [jnp.zeros((HEAD_DIM - C, PW), BF16)] if C < HEAD_DIM else []
    lane_pad = [jnp.zeros((1, HEAD_DIM - C), F32)] if C < HEAD_DIM else []
    rr = lax.broadcasted_iota(jnp.int32, (RG, RG), 0)
    cc = lax.broadcasted_iota(jnp.int32, (RG, RG), 1)
    tril = ((rr // C == cc // C) & (rr >= cc)).astype(BF16)

    def stack(y):
        yb = y.astype(BF16)
        zero = jnp.zeros_like(yb)
        pad = row_pad if y.shape[0] < HEAD_DIM else []
        blocks = []
        for h in range(GROUP_HEADS):
            blocks += [jnp.where(lane_head == h, yb, zero)] + pad
        return jnp.concatenate(blocks, axis=0)

    items = [(u, p) for u in range(U) for p in range(PAIRS)]
    sl = lambda u, p: (slice(u * C, (u + 1) * C), slice(p * PW, (p + 1) * PW))

    def front(gi):
        rg = slice(gi * RG, (gi + 1) * RG)
        q, k, v = (qkv_ref[rg, j * DN_W:(j + 1) * DN_W] for j in range(3))
        qn = q * (lax.rsqrt(seg_sum(q * q) + NORM_EPS) * HEAD_DIM ** -0.5)
        kn = k * lax.rsqrt(seg_sum(k * k) + NORM_EPS)
        yield

        ab = ab_ref[rg, :]
        g = -jnp.exp(alog_ref[...]) * _softplus(ab + dtb_ref[...])
        gs = jnp.dot(tril, jnp.concatenate(_split3(g), axis=1), preferred_element_type=F32)
        G = gs[:, :LANES] + gs[:, LANES:2 * LANES] + gs[:, 2 * LANES:]
        GT = G.T
        G_x = _dot_exact_lhs(G, expand)
        beta = _sigmoid(pltpu.roll(ab, LANES - HEADS, axis=1))
        b_hi = beta.astype(BF16)
        b_lo = (beta - b_hi.astype(F32)).astype(BF16)
        bx = jnp.dot(jnp.concatenate([b_hi, b_lo], axis=0), expand, preferred_element_type=F32)
        beta_x = bx[:RG] + bx[RG:]
        glast_x = jnp.concatenate(
            [jnp.broadcast_to(G_x[(u + 1) * C - 1:(u + 1) * C, :], (C, DN_W)) for u in range(U)],
            axis=0)
        yield
        eg_x = jnp.exp(G_x)
        kbeta = kn * beta_x
        return dict(qn=qn, kn=kn, kbeta=kbeta, bk=kbeta * eg_x, bv=v * beta_x, qd=qn * eg_x,
                    kd=kn * jnp.exp(glast_x - G_x), G_x=G_x, GT=GT, eg_x=eg_x)

    def middle(f):
        A, QK = {}, {}
        for (u, p) in items:
            ru, lp = sl(u, p)
            gr = []
            for h in range(p * GROUP_HEADS, (p + 1) * GROUP_HEADS):
                gr += [f["GT"][h:h + 1, ru]] + lane_pad
            gr = jnp.concatenate(gr, axis=1)
            decay = jnp.exp(jnp.where(causal, f["G_x"][ru, lp] - gr, -jnp.inf))
            r = _mm_nt(jnp.concatenate([f["kbeta"][ru, lp], f["qn"][ru, lp]], axis=0),
                       stack(f["kn"][ru, lp]))
            A[u, p] = jnp.where(strict, r[:C] * decay, 0.0)
            QK[u, p] = r[C:] * decay
        yield

        n_pow = int(math.log2(C)) - 1
        P = {it: eye - A[it] for it in items}
        Ap = {it: _mm(A[it], stack(A[it])) for it in items}
        yield
        for i in range(n_pow):
            if i < n_pow - 1:
                r = {it: _mm(jnp.concatenate([Ap[it], P[it]], axis=0), stack(Ap[it])) for it in items}
                Ap = {it: r[it][:C] for it in items}
                P = {it: P[it] + r[it][C:] for it in items}
            else:
                P = {it: P[it] + _mm(P[it], stack(Ap[it])) for it in items}
            yield

        out = {}
        for (u, p) in items:
            ru, lp = sl(u, p)
            wu = _mm(P[u, p], jnp.concatenate([stack(f["bk"][ru, lp]), stack(f["bv"][ru, lp])],
                                              axis=1))
            W, Uu, kd = wu[:, :PW], wu[:, PW:], f["kd"][ru, lp]
            Ms, NTs = [], []
            for hb in range(PW // LANES):
                cols = slice(hb * LANES, (hb + 1) * LANES)
                mn = _mm(kd[:, cols].T, jnp.concatenate([W[:, cols], Uu[:, cols]], axis=1))
                Ms.append(half_mask * mn[:, :LANES])
                n_t = (half_mask * mn[:, LANES:]).T
                NTs.append(n_t[:HEAD_DIM] + n_t[HEAD_DIM:])
            out[u, p] = dict(W=W, U=Uu, M=diag_blocks(Ms).astype(BF16),
                             NT=jnp.concatenate(NTs, axis=1), QK=QK[u, p], qd=f["qd"][ru, lp],
                             eg_last=f["eg_x"][(u + 1) * C - 1:(u + 1) * C, lp])
        return out

    def diag_blocks(blocks):
        n = len(blocks)
        rows = []
        for i, blk in enumerate(blocks):
            rows.append(jnp.concatenate(
                [blk if j == i else jnp.zeros_like(blk) for j in range(n)], axis=1))
        return jnp.concatenate(rows, axis=0)

    def load_state(ref, prefix, p):
        return jnp.concatenate([ref[prefix + (p * GROUP_HEADS + h,)].T
                                for h in range(GROUP_HEADS)], axis=1)

    def store_state(ref, prefix, p, st):
        for h in range(GROUP_HEADS):
            ref[prefix + (p * GROUP_HEADS + h,)] = st[:, h * HEAD_DIM:(h + 1) * HEAD_DIM].T

    chunks = [(gi, u) for gi in range(groups) for u in range(U)]
    if sequential:
        @pl.when(step == 0)
        def _():
            for p in range(PAIRS):
                sbd[p] = load_state(s0_ref, (), p)
        s_start = {p: sbd[p] for p in range(PAIRS)}
    else:
        s_start = {(idx, p): load_state(s0_ref, (idx,), p)
                   for idx in range(len(chunks)) for p in range(PAIRS)}

    def chain(gi, out, s_in):
        if sequential:
            states = {p: [s_in[p]] for p in range(PAIRS)}
            for u in range(U):
                for p in range(PAIRS):
                    d, s_cur = out[u, p], states[p][-1]
                    states[p].append(s_cur * d["eg_last"] - _mm_nt(s_cur, d["M"]) + d["NT"])
            return ({(u, p): states[p][u] for (u, p) in items},
                    {p: states[p][-1] for p in range(PAIRS)})
        entering = {(u, p): s_in[gi * U + u, p] for (u, p) in items}
        s_out = {}
        for (u, p) in items:
            d, s_cur = out[u, p], entering[u, p]
            s_out[gi * U + u, p] = s_cur * d["eg_last"] - _mm_nt(s_cur, d["M"]) + d["NT"]
        return entering, s_out

    def finish(gi, out, entering):
        r = {it: _mm_nt(jnp.concatenate([out[it]["W"], out[it]["qd"]], axis=0),
                        stack(entering[it])) for it in items}
        yield
        tiles = {it: r[it][C:] + _mm(out[it]["QK"], stack(out[it]["U"] - r[it][:C]))
                 for it in items}
        yield
        rg = slice(gi * RG, (gi + 1) * RG)
        o = jnp.concatenate([jnp.concatenate([tiles[u, p] for p in range(PAIRS)], axis=1)
                             for u in range(U)], axis=0)
        ms = seg_sum(o * o) * (1.0 / HEAD_DIM)
        z = z_ref[rg, :]
        o_ref[rg, :] = (o * lax.rsqrt(ms + NORM_EPS) * normw_ref[...]
                        * (z * _sigmoid(z))).astype(BF16)

    def drive(gens):
        results = [None] * len(gens)
        active = list(enumerate(gens))
        while active:
            still = []
            for i, gen in active:
                try:
                    next(gen)
                    still.append((i, gen))
                except StopIteration as stop:
                    results[i] = stop.value
            active = still
        return results

    state = s_start
    s_final = {}
    f_cur = drive([front(0)])[0]
    pending = None
    for gi in range(groups + 1):
        gens = []
        if gi < groups:
            gens.append(middle(f_cur))
        if gi + 1 < groups:
            gens.append(front(gi + 1))
        if pending is not None:
            entering, s_out = chain(gi - 1, pending, state)
            s_final.update(s_out)
            state = s_out if sequential else state
            gens.append(finish(gi - 1, pending, entering))
        res = drive(gens)
        pending = res[0] if gi < groups else None
        f_cur = res[1] if gi + 1 < groups else None

    if sequential:
        for p in range(PAIRS):
            sbd[p] = s_final[p]

        @pl.when(step == last_step)
        def _():
            for p in range(PAIRS):
                store_state(s_ref, (), p, s_final[p])
    else:
        for (idx, p), s_new in s_final.items():
            store_state(s_ref, (idx,), p, s_new)


def _delta(qkv, z, ab, s0, consts, n_seq, seq_len, chunk, units, groups):
    alog, dtb, normw, seg, expand = consts
    sequential = n_seq == 1
    rows = n_seq * seq_len
    n_chunks = units * groups
    blk_rows = n_chunks * chunk
    assert rows % blk_rows == 0 and (sequential or seq_len == chunk)
    const2 = lambda shape: pl.BlockSpec(shape, lambda i: (0, 0))
    tok = lambda width: pl.BlockSpec((blk_rows, width), lambda i: (i, 0))
    group_w = GROUP_HEADS * HEAD_DIM
    if sequential:
        state = pl.BlockSpec((None, HEADS, HEAD_DIM, HEAD_DIM), lambda i: (0, 0, 0, 0))
    else:
        state = pl.BlockSpec((n_chunks, HEADS, HEAD_DIM, HEAD_DIM), lambda i: (i, 0, 0, 0))
    return pl.pallas_call(
        functools.partial(_delta_kernel, chunk=chunk, units=units, groups=groups,
                          sequential=sequential),
        out_shape=[jax.ShapeDtypeStruct((rows, DN_W), BF16),
                   jax.ShapeDtypeStruct((n_seq, HEADS, HEAD_DIM, HEAD_DIM), F32)],
        grid=(rows // blk_rows,),
        in_specs=[tok(CONV_CH), tok(DN_W), tok(LANES), state,
                  const2((1, LANES)), const2((1, LANES)),
                  const2((1, DN_W)), const2((DN_W, DN_W)), const2((LANES, DN_W))],
        out_specs=[tok(DN_W), state],
        scratch_shapes=[pltpu.VMEM((HEADS // GROUP_HEADS, HEAD_DIM, group_w), F32)],
        compiler_params=pltpu.CompilerParams(
            dimension_semantics=("arbitrary",), vmem_limit_bytes=VMEM_LIMIT,
        ),
        name=f"delta_c{chunk}",
    )(qkv, z, ab, s0, alog, dtb, normw, seg, expand)


SWA_W = SWA_HEADS * HEAD_DIM


def _swa_kernel(*refs, n_q, units, prompt, batch=4):
    if prompt:
        (table_ref, sinks_ref, bucket_ref, q_ref, km_ref, kp_ref, kc_ref, vm_ref, vp_ref, vc_ref,
         o_ref, bias_ref) = refs
    else:
        (table_ref, sinks_ref, bucket_ref, q_ref, kp_ref, kc_ref, vp_ref, vc_ref,
         o_ref, bias_ref) = refs
    n_k = bias_ref.shape[-1]
    n_grp = SWA_W // KV_W
    step = pl.program_id(0)

    @pl.when(step == 0)
    def _():
        bucket = bucket_ref[...]
        for h in range(SWA_HEADS):
            acc = jnp.zeros((n_q, n_k), F32)
            for b in range(N_BUCKETS):
                acc = jnp.where(bucket == b, table_ref[b * SWA_HEADS + h], acc)
            kv, gi = divmod(h, SWA_G)
            bias_ref[kv, gi * n_q:(gi + 1) * n_q, :] = acc

    lane = lax.broadcasted_iota(jnp.int32, (1, KV_W), 1)
    kv_mask = [(lane < HEAD_DIM).astype(F32), (lane >= HEAD_DIM).astype(F32)]
    if prompt:
        first = step == 0
        k_all = jnp.concatenate([jnp.where(first, km_ref[...], kp_ref[...]), kc_ref[...]], axis=0)
        v_all = jnp.concatenate([jnp.where(first, vm_ref[...], vp_ref[...]), vc_ref[...]], axis=0)
        k_kv = [(k_all * m).astype(BF16) for m in kv_mask]
        v_kv = [(v_all * m).astype(BF16) for m in kv_mask]
        keys = lambda j, kv: k_kv[kv][j * n_q:j * n_q + n_k]
        vals = lambda j, kv: v_kv[kv][j * n_q:j * n_q + n_k]
    else:
        n_c = n_k - n_q
        cat = lambda a, b, j: jnp.concatenate([a[j * n_c:(j + 1) * n_c, :],
                                               b[j * n_q:(j + 1) * n_q, :]], axis=0)
        keys = lambda j, kv: (cat(kp_ref, kc_ref, j) * kv_mask[kv]).astype(BF16)
        vals = lambda j, kv: (cat(vp_ref, vc_ref, j) * kv_mask[kv]).astype(BF16)

    q = (q_ref[...] * HEAD_DIM ** -0.5).astype(BF16)
    ones_k = jnp.ones((n_k, KV_W), BF16)
    sink = [jnp.concatenate([jnp.full((n_q, KV_W), sinks_ref[kv * SWA_G + gi], F32)
                             for gi in range(SWA_G)], axis=0) for kv in range(SWA_KV)]
    rows4 = SWA_G * n_q
    for j0 in range(0, units, batch):
        js = range(j0, min(j0 + batch, units))
        items = [(j, kv) for j in js for kv in range(SWA_KV)]
        s = {}
        for j in js:
            rows = slice(j * n_q, (j + 1) * n_q)
            q4 = jnp.concatenate([q[rows, g * KV_W:(g + 1) * KV_W] for g in range(n_grp)], axis=0)
            for kv in range(SWA_KV):
                sj = lax.dot_general(q4, keys(j, kv), (((1,), (1,)), ((), ())),
                                     preferred_element_type=F32) + bias_ref[kv]
                if prompt and j * n_q < WINDOW - N_META:
                    key_pos = j * n_q + lax.broadcasted_iota(jnp.int32, (1, n_k), 1)
                    n_invalid = jnp.where(first, WINDOW - N_META, 0)
                    sj = jnp.where(key_pos < n_invalid, -jnp.inf, sj)
                s[j, kv] = sj
        m = {it: jnp.maximum(jnp.broadcast_to(jnp.max(s[it], axis=-1, keepdims=True),
                                              (rows4, KV_W)), sink[it[1]]) for it in items}
        p = {it: jnp.exp(s[it] - jnp.concatenate([m[it], m[it][:, :n_k - KV_W]], axis=1))
             for it in items}
        pv = {it: jnp.dot(p[it].astype(BF16), jnp.concatenate([vals(*it), ones_k], axis=1),
                          preferred_element_type=F32) for it in items}
        o = {it: pv[it][:, :KV_W] * (1.0 / (pv[it][:, KV_W:] + jnp.exp(sink[it[1]] - m[it])))
             for it in items}
        for j in js:
            o2 = o[j, 0] + o[j, 1]
            for g in range(n_grp):
                o_ref[j * n_q:(j + 1) * n_q, g * KV_W:(g + 1) * KV_W] = (
                    o2[g * n_q:(g + 1) * n_q, :].astype(BF16))


def _swa(table, sinks, bucket, q, kv_args, kv_specs, n_steps, n_q, units, prompt, name):
    n_k = bucket.shape[1]
    smem = pl.BlockSpec(memory_space=pltpu.SMEM)
    rows = units * n_q
    return pl.pallas_call(
        functools.partial(_swa_kernel, n_q=n_q, units=units, prompt=prompt),
        out_shape=jax.ShapeDtypeStruct((n_steps * rows, SWA_W), BF16),
        grid=(n_steps,),
        in_specs=[smem, smem, pl.BlockSpec((n_q, n_k), lambda i: (0, 0)),
                  pl.BlockSpec((rows, SWA_W), lambda i: (i, 0))] + list(kv_specs),
        out_specs=pl.BlockSpec((rows, SWA_W), lambda i: (i, 0)),
        scratch_shapes=[pltpu.VMEM((SWA_KV, SWA_G * n_q, n_k), F32)],
        compiler_params=pltpu.CompilerParams(dimension_semantics=("arbitrary",),
                                             vmem_limit_bytes=VMEM_LIMIT),
        name=name,
    )(table, sinks, bucket, q, *kv_args)


def _t5_bucket(rel):
    half = N_BUCKETS // 2
    max_exact = half // 2
    n = jnp.abs(rel)
    large = max_exact + (jnp.log(jnp.maximum(n, 1).astype(F32) / max_exact)
                         / math.log(MAX_DIST / max_exact) * (half - max_exact)).astype(jnp.int32)
    large = jnp.minimum(large, half - 1)
    return jnp.where(rel > 0, half, 0) + jnp.where(n < max_exact, n, large)


def _bucket_map(n_q, n_k, key_offset):
    rel = (jnp.arange(n_k)[None, :] - key_offset) - jnp.arange(n_q)[:, None]
    return _t5_bucket(rel).astype(jnp.int32)


FF_BLOCK = 1024


def _post_kernel(x_ref, dn_ref, sw_ref, gin_ref, bin_ref, wo_ref, g1_ref, b1_ref,
                 w1_ref, w2_ref, g2_ref, b2_ref, y_ref):
    h = _layer_norm(x_ref[...], gin_ref[...], bin_ref[...])
    mix = (jnp.dot(dn_ref[...], wo_ref[:DN_W, :], preferred_element_type=F32)
           + jnp.dot(sw_ref[...], wo_ref[DN_W:, :], preferred_element_type=F32))
    h1 = _layer_norm(DEEP_ALPHA * h + mix, g1_ref[...], b1_ref[...])
    h1b = h1.astype(BF16)
    f = jnp.zeros_like(h1)
    for j in range(D_FF // FF_BLOCK):
        a = jnp.dot(h1b, w1_ref[:, j * FF_BLOCK:(j + 1) * FF_BLOCK], preferred_element_type=F32)
        a = jnp.square(jnp.maximum(a, 0.0)).astype(BF16)
        f = f + jnp.dot(a, w2_ref[j * FF_BLOCK:(j + 1) * FF_BLOCK, :], preferred_element_type=F32)
    y_ref[...] = _layer_norm(DEEP_ALPHA * h1 + f, g2_ref[...], b2_ref[...])


def _post(x, dn, sw, vecs, wo, w1, w2, tm):
    gin, bin_, g1, b1, g2, b2 = vecs
    rows = x.shape[0]
    const = lambda i: (0, 0)
    vec = pl.BlockSpec((1, D_MODEL), const)
    weight = lambda shape: pl.BlockSpec(shape, const, pipeline_mode=pl.Buffered(1))
    row_blk = lambda width: pl.BlockSpec((tm, width), lambda i: (i, 0))
    return pl.pallas_call(
        _post_kernel,
        out_shape=jax.ShapeDtypeStruct((rows, D_MODEL), F32),
        grid=(rows // tm,),
        in_specs=[row_blk(D_MODEL), row_blk(DN_W), row_blk(DN_W), vec, vec,
                  weight((2 * DN_W, D_MODEL)), vec, vec,
                  weight((D_MODEL, D_FF)), weight((D_FF, D_MODEL)), vec, vec],
        out_specs=row_blk(D_MODEL),
        compiler_params=pltpu.CompilerParams(dimension_semantics=("arbitrary",),
                                             vmem_limit_bytes=VMEM_LIMIT),
        name="post_mlp",
    )(x, dn, sw, gin, bin_, wo, g1, b1, w1, w2, g2, b2)


def kernel(x_prompt, x_sample, state_delta, state_conv, cache_swa_k, cache_swa_v, meta_tokens, ln_in_g, ln_in_b, w_in, w_conv, dn_a_log, dn_dt_bias, dn_norm, swa_sinks, rel_bias_table, w_o, ln1_g, ln1_b, w_ff1, w_ff2, ln2_g, ln2_b):
    assert w_in.shape[0] == DEPTH == 1
    n_seq_s, len_s = x_sample.shape[0], x_sample.shape[1]
    len_p = x_prompt.shape[1]
    cache_len = cache_swa_k.shape[2]
    chunk_p = 64
    row = lambda t: t.reshape(1, -1).astype(F32)

    w = w_in[0]
    w_sq = (w[:, OFF_SQ:OFF_SK].reshape(D_MODEL, SWA_KV, SWA_G, HEAD_DIM)
            .transpose(0, 2, 1, 3).reshape(D_MODEL, SWA_W))
    w_r = jnp.concatenate(
        [w[:, :OFF_A], w_sq, w[:, OFF_SK:], w[:, OFF_A:OFF_SQ],
         jnp.zeros((D_MODEL, LANES - 2 * HEADS), w.dtype)], axis=1).astype(BF16)
    gin, bin_ = row(ln_in_g), row(ln_in_b)
    pad_lanes = lambda t: jnp.pad(t.reshape(1, -1).astype(F32), ((0, 0), (0, LANES - HEADS)))
    lane_head = jnp.arange(DN_W) // HEAD_DIM
    seg = (lane_head[:, None] == lane_head[None, :]).astype(BF16)
    expand = (jnp.arange(LANES)[:, None] == lane_head[None, :]).astype(BF16)
    wconv = jnp.pad(w_conv[0].astype(F32), ((0, SUBLANES - CONV_W), (0, 0)))
    delta_consts = (pad_lanes(dn_a_log[0]), pad_lanes(dn_dt_bias[0]),
                    jnp.tile(dn_norm[0].astype(F32), HEADS).reshape(1, DN_W), seg, expand)
    post_vecs = (gin, bin_, row(ln1_g[0]), row(ln1_b[0]), row(ln2_g[0]), row(ln2_b[0]))
    wo_sw = (w_o[0][DN_W:].reshape(SWA_KV, SWA_G, HEAD_DIM, D_MODEL)
             .transpose(1, 0, 2, 3).reshape(SWA_W, D_MODEL))
    wo_b = jnp.concatenate([w_o[0][:DN_W], wo_sw], axis=0).astype(BF16)
    w1_b, w2_b = w_ff1[0].astype(BF16), w_ff2[0].astype(BF16)
    table = rel_bias_table.astype(F32).reshape(-1)
    sinks = swa_sinks[0].astype(F32)

    xp = x_prompt[0]
    xs = x_sample.reshape(n_seq_s * len_s, D_MODEL)

    conv_pad = SUBLANES - (CONV_W - 1)
    zero_conv = jnp.zeros((1, SUBLANES, CONV_CH), F32)
    conv_s = jnp.pad(state_conv[0].astype(F32), ((0, 0), (conv_pad, 0), (0, 0)))
    m_qkv, m_z, m_sq, m_sk, m_sv, m_ab, m_tail = _proj(
        meta_tokens.astype(F32), gin, bin_, w_r, zero_conv, wconv, N_META, N_META)
    p_qkv, p_z, p_sq, p_sk, p_sv, p_ab, p_tail = _proj(xp, gin, bin_, w_r, m_tail, wconv, 512, None)
    s_qkv, s_z, s_sq, s_sk, s_sv, s_ab, s_tail = _proj(xs, gin, bin_, w_r, conv_s, wconv, 512, len_s)

    zero_s = jnp.zeros((1, HEADS, HEAD_DIM, HEAD_DIM), F32)
    _, s_meta = _delta(m_qkv, m_z, m_ab, zero_s, delta_consts, 1, N_META, N_META, 1, 1)
    p_dn, p_s = _delta(p_qkv, p_z, p_ab, s_meta, delta_consts, 1, len_p, chunk_p, 8, 1)
    s_dn, s_s = _delta(s_qkv, s_z, s_ab, state_delta[0].astype(F32), delta_consts,
                       n_seq_s, len_s, len_s, 8, 2)

    span = WINDOW
    lead = jnp.zeros((span - N_META, KV_W), F32)
    units_p = 8
    rows_p = units_p * chunk_p
    prev_blocks = rows_p // span
    kv_specs_p = [pl.BlockSpec((span, KV_W), lambda i: (0, 0)),
                  pl.BlockSpec((span, KV_W), lambda i: (jnp.maximum(i * prev_blocks - 1, 0), 0)),
                  pl.BlockSpec((rows_p, KV_W), lambda i: (i, 0))] * 2
    p_sw = _swa(table, sinks, _bucket_map(chunk_p, span + chunk_p, span), p_sq,
                [jnp.concatenate([lead, m_sk], axis=0), p_sk, p_sk,
                 jnp.concatenate([lead, m_sv], axis=0), p_sv, p_sv],
                kv_specs_p, len_p // rows_p, chunk_p, units_p, True, "swa_prompt")
    ck = cache_swa_k[0].astype(F32).reshape(n_seq_s * cache_len, KV_W)
    cv = cache_swa_v[0].astype(F32).reshape(n_seq_s * cache_len, KV_W)
    units_s = 8
    kv_specs_s = [pl.BlockSpec((units_s * cache_len, KV_W), lambda i: (i, 0)),
                  pl.BlockSpec((units_s * len_s, KV_W), lambda i: (i, 0))] * 2
    s_sw = _swa(table, sinks, _bucket_map(len_s, cache_len + len_s, cache_len), s_sq,
                [ck, s_sk, cv, s_sv], kv_specs_s, n_seq_s // units_s, len_s, units_s, False,
                "swa_sample")

    y_p = _post(xp, p_dn, p_sw, post_vecs, wo_b, w1_b, w2_b, 512)
    y_s = _post(xs, s_dn, s_sw, post_vecs, wo_b, w1_b, w2_b, 512)

    kv_shape = lambda n, length: (1, n, length, SWA_KV, HEAD_DIM)
    return (y_p[None], y_s.reshape(x_sample.shape),
            p_s[None], p_tail[:, -(CONV_W - 1):][None],
            p_sk[-WINDOW:].reshape(kv_shape(1, WINDOW)), p_sv[-WINDOW:].reshape(kv_shape(1, WINDOW)),
            s_s[None], s_tail[:, -(CONV_W - 1):][None],
            s_sk.reshape(kv_shape(n_seq_s, len_s)), s_sv.reshape(kv_shape(n_seq_s, len_s)))
```

```python
import functools
import math

import jax
import jax.numpy as jnp
from jax import lax
from jax.experimental import pallas as pl
from jax.experimental.pallas import tpu as pltpu

F32 = jnp.float32
BF16 = jnp.bfloat16

D_MODEL = 1024
N_META = 16
HEADS = 8
HEAD_DIM = 64
DN_W = HEADS * HEAD_DIM
CONV_W = 4
CONV_CH = 3 * DN_W
SWA_HEADS = 8
SWA_KV = 2
SWA_G = SWA_HEADS // SWA_KV
KV_W = SWA_KV * HEAD_DIM
WINDOW = 128
N_BUCKETS = 32
MAX_DIST = 128
D_FF = 4 * D_MODEL
DEPTH = 1
DEEP_ALPHA = (2 * DEPTH) ** 0.25
LN_EPS = 1e-5
NORM_EPS = 1e-6
LANES = 128
SUBLANES = 8
VMEM_LIMIT = 56 * 1024 * 1024

OFF_Z = CONV_CH
OFF_A = OFF_Z + DN_W
OFF_B = OFF_A + HEADS
OFF_SQ = OFF_B + HEADS
OFF_SK = OFF_SQ + SWA_HEADS * HEAD_DIM
OFF_SV = OFF_SK + KV_W
PROJ_WIDTH = OFF_SV + KV_W
PROJ_SPLITS = (CONV_CH, DN_W, SWA_HEADS * HEAD_DIM, KV_W, KV_W, LANES)
PROJ_PAD_W = sum(PROJ_SPLITS)


def _layer_norm(x, g, b):
    mu = jnp.mean(x, axis=-1, keepdims=True)
    xc = x - mu
    var = jnp.mean(xc * xc, axis=-1, keepdims=True)
    return xc * lax.rsqrt(var + LN_EPS) * g + b


def _mm(a, b):
    return jnp.dot(a.astype(BF16), b.astype(BF16), preferred_element_type=F32)


def _mm_nt(a, b):
    return lax.dot_general(a.astype(BF16), b.astype(BF16), (((1,), (1,)), ((), ())),
                           preferred_element_type=F32)


def _split3(x):
    x1 = x.astype(BF16)
    r1 = x - x1.astype(F32)
    x2 = r1.astype(BF16)
    x3 = (r1 - x2.astype(F32)).astype(BF16)
    return x1, x2, x3


def _dot_exact_lhs(x, sel):
    m = x.shape[0]
    stacked = jnp.concatenate(_split3(x), axis=0)
    r = jnp.dot(stacked, sel, preferred_element_type=F32)
    return r[:m] + r[m:2 * m] + r[2 * m:]


CONV_BLOCK = 256


def _sigmoid(x):
    return 1.0 / (1.0 + jnp.exp(-x))


def _proj_kernel(x_ref, g_ref, b_ref, w_ref, conv0_ref, wconv_ref,
                 y_ref, z_ref, sq_ref, sk_ref, sv_ref, ab_ref, tail_ref, xbuf, *, seq_len):
    tm = x_ref.shape[0]
    h = _layer_norm(x_ref[...], g_ref[...], b_ref[...]).astype(BF16)
    off = CONV_CH
    for o_ref, width in zip((z_ref, sq_ref, sk_ref, sv_ref, ab_ref), PROJ_SPLITS[1:]):
        o_ref[...] = jnp.dot(h, w_ref[:, off:off + width], preferred_element_type=F32)
        off += width

    chained = seq_len is None
    n_seq = 1 if chained else tm // seq_len
    L = tm if chained else seq_len
    if chained:
        @pl.when(pl.program_id(0) == 0)
        def _():
            xbuf[0, 0:SUBLANES, :] = conv0_ref[0]
    else:
        for s in range(n_seq):
            xbuf[s, 0:SUBLANES, :] = conv0_ref[s]
    for cb in range(0, CONV_CH, CONV_BLOCK):
        cols = slice(cb, cb + CONV_BLOCK)
        raw = jnp.dot(h, w_ref[:, cols], preferred_element_type=F32)
        for s in range(n_seq):
            raw_s = raw[s * L:(s + 1) * L, :]
            ext = jnp.concatenate([xbuf[s, :, cols], raw_s], axis=0)
            y = raw_s * wconv_ref[CONV_W - 1:CONV_W, cols]
            for j in range(CONV_W - 1):
                y = y + (ext[SUBLANES - 3 + j:SUBLANES - 3 + j + L, :]
                         * wconv_ref[j:j + 1, cols])
            y_ref[s * L:(s + 1) * L, cols] = y * _sigmoid(y)
            tail_ref[s, :, cols] = raw_s[L - SUBLANES:, :]
            if chained:
                xbuf[s, :, cols] = raw_s[L - SUBLANES:, :]


def _proj(x, g, b, w_r, conv0, wconv, tm, seq_len):
    rows = x.shape[0]
    chained = seq_len is None
    n_tile_seq = 1 if chained else tm // seq_len
    n_seq = 1 if chained else rows // seq_len
    const = lambda i: (0, 0)
    row_blk = lambda width: pl.BlockSpec((tm, width), lambda i: (i, 0))
    seq_blk = pl.BlockSpec((n_tile_seq, SUBLANES, CONV_CH),
                           (lambda i: (0, 0, 0)) if chained else (lambda i: (i, 0, 0)))
    return pl.pallas_call(
        functools.partial(_proj_kernel, seq_len=seq_len),
        out_shape=[jax.ShapeDtypeStruct((rows, width), F32) for width in PROJ_SPLITS]
                  + [jax.ShapeDtypeStruct((n_seq, SUBLANES, CONV_CH), F32)],
        grid=(rows // tm,),
        in_specs=[row_blk(D_MODEL), pl.BlockSpec((1, D_MODEL), const),
                  pl.BlockSpec((1, D_MODEL), const), pl.BlockSpec((D_MODEL, PROJ_PAD_W), const),
                  seq_blk, pl.BlockSpec((SUBLANES, CONV_CH), const)],
        out_specs=[row_blk(width) for width in PROJ_SPLITS] + [seq_blk],
        scratch_shapes=[pltpu.VMEM((n_tile_seq, SUBLANES, CONV_CH), F32)],
        compiler_params=pltpu.CompilerParams(dimension_semantics=("arbitrary",),
                                             vmem_limit_bytes=VMEM_LIMIT),
        name="ln_in_proj",
    )(x, g, b, w_r, conv0, wconv)


GROUP_HEADS = 4


def _softplus(x):
    return jnp.maximum(x, 0.0) + jnp.log1p(jnp.exp(-jnp.abs(x)))


def _delta_kernel(qkv_ref, z_ref, ab_ref, s0_ref, alog_ref, dtb_ref,
                  normw_ref, seg_ref, expand_ref, o_ref, s_ref, sbd,
                  *, chunk, units, groups, sequential):
    C, U = chunk, units
    RG = U * C
    PAIRS = HEADS // GROUP_HEADS
    PW = GROUP_HEADS * HEAD_DIM
    step = pl.program_id(0)
    last_step = pl.num_programs(0) - 1
    expand = expand_ref[...]

    def seg_sum(t):
        tb = t.astype(BF16)
        return jnp.concatenate(
            [jnp.dot(tb[:, g * PW:(g + 1) * PW], seg_ref[g * PW:(g + 1) * PW, g * PW:(g + 1) * PW],
                     preferred_element_type=F32) for g in range(PAIRS)], axis=1)

    ri = lax.broadcasted_iota(jnp.int32, (C, PW), 0)
    ci = lax.broadcasted_iota(jnp.int32, (C, PW), 1) % HEAD_DIM
    causal = (ri >= ci) & (ci < C)
    strict = (ri > ci) & (ci < C)
    eye = (ri == ci).astype(F32)
    lane_head = lax.broadcasted_iota(jnp.int32, (1, PW), 1) // HEAD_DIM
    rs = lax.broadcasted_iota(jnp.int32, (LANES, LANES), 0)
    cs = lax.broadcasted_iota(jnp.int32, (LANES, LANES), 1)
    half_mask = ((rs // HEAD_DIM) == (cs // HEAD_DIM)).astype(F32)
    row_pad = [jnp.zeros((HEAD_DIM - C, PW), BF16)] if C < HEAD_DIM else []
    lane_pad = [jnp.zeros((1, HEAD_DIM - C), F32)] if C < HEAD_DIM else []
    rr = lax.broadcasted_iota(jnp.int32, (RG, RG), 0)
    cc = lax.broadcasted_iota(jnp.int32, (RG, RG), 1)
    tril = ((rr // C == cc // C) & (rr >= cc)).astype(BF16)

    def stack(y):
        yb = y.astype(BF16)
        zero = jnp.zeros_like(yb)
        pad = row_pad if y.shape[0] < HEAD_DIM else []
        blocks = []
        for h in range(GROUP_HEADS):
            blocks += [jnp.where(lane_head == h, yb, zero)] + pad
        return jnp.concatenate(blocks, axis=0)

    items = [(u, p) for u in range(U) for p in range(PAIRS)]
    sl = lambda u, p: (slice(u * C, (u + 1) * C), slice(p * PW, (p + 1) * PW))

    def front(gi):
        rg = slice(gi * RG, (gi + 1) * RG)
        q, k, v = (qkv_ref[rg, j * DN_W:(j + 1) * DN_W] for j in range(3))
        qn = q * (lax.rsqrt(seg_sum(q * q) + NORM_EPS) * HEAD_DIM ** -0.5)
        kn = k * lax.rsqrt(seg_sum(k * k) + NORM_EPS)
        yield

        ab = ab_ref[rg, :]
        g = -jnp.exp(alog_ref[...]) * _softplus(ab + dtb_ref[...])
        gs = jnp.dot(tril, jnp.concatenate(_split3(g), axis=1), preferred_element_type=F32)
        G = gs[:, :LANES] + gs[:, LANES:2 * LANES] + gs[:, 2 * LANES:]
        GT = G.T
        G_x = _dot_exact_lhs(G, expand)
        beta = _sigmoid(pltpu.roll(ab, LANES - HEADS, axis=1))
        b_hi = beta.astype(BF16)
        b_lo = (beta - b_hi.astype(F32)).astype(BF16)
        bx = jnp.dot(jnp.concatenate([b_hi, b_lo], axis=0), expand, preferred_element_type=F32)
        beta_x = bx[:RG] + bx[RG:]
        glast_x = jnp.concatenate(
            [jnp.broadcast_to(G_x[(u + 1) * C - 1:(u + 1) * C, :], (C, DN_W)) for u in range(U)],
            axis=0)
        yield
        eg_x = jnp.exp(G_x)
        kbeta = kn * beta_x
        return dict(qn=qn, kn=kn, kbeta=kbeta, bk=kbeta * eg_x, bv=v * beta_x, qd=qn * eg_x,
                    kd=kn * jnp.exp(glast_x - G_x), G_x=G_x, GT=GT, eg_x=eg_x)

    def middle(f):
        A, QK = {}, {}
        for (u, p) in items:
            ru, lp = sl(u, p)
            gr = []
            for h in range(p * GROUP_HEADS, (p + 1) * GROUP_HEADS):
                gr += [f["GT"][h:h + 1, ru]] + lane_pad
            gr = jnp.concatenate(gr, axis=1)
            decay = jnp.exp(jnp.where(causal, f["G_x"][ru, lp] - gr, -jnp.inf))
            r = _mm_nt(jnp.concatenate([f["kbeta"][ru, lp], f["qn"][ru, lp]], axis=0),
                       stack(f["kn"][ru, lp]))
            A[u, p] = jnp.where(strict, r[:C] * decay, 0.0)
            QK[u, p] = r[C:] * decay
        yield

        n_pow = int(math.log2(C)) - 1
        P = {it: eye - A[it] for it in items}
        Ap = {it: _mm(A[it], stack(A[it])) for it in items}
        yield
        for i in range(n_pow):
            if i < n_pow - 1:
                r = {it: _mm(jnp.concatenate([Ap[it], P[it]], axis=0), stack(Ap[it])) for it in items}
                Ap = {it: r[it][:C] for it in items}
                P = {it: P[it] + r[it][C:] for it in items}
            else:
                P = {it: P[it] + _mm(P[it], stack(Ap[it])) for it in items}
            yield

        halves = [slice(hb * LANES, (hb + 1) * LANES) for hb in range(PW // LANES)]
        wu = {it: _mm(P[it], jnp.concatenate([stack(f["bk"][sl(*it)]), stack(f["bv"][sl(*it)])],
                                             axis=1)) for it in items}
        kdT = {(it, hb): f["kd"][sl(*it)][:, cols].T for it in items
               for hb, cols in enumerate(halves)}
        yield
        mn = {(it, hb): _mm(kdT[it, hb], jnp.concatenate(
            [wu[it][:, cols], wu[it][:, PW:][:, cols]], axis=1))
              for it in items for hb, cols in enumerate(halves)}
        yield
        n_t = {key: (half_mask * mn[key][:, LANES:]).T for key in mn}
        out = {}
        for (u, p) in items:
            ru, lp = sl(u, p)
            it = (u, p)
            out[it] = dict(
                W=wu[it][:, :PW], U=wu[it][:, PW:],
                M=diag_blocks([half_mask * mn[it, hb][:, :LANES]
                               for hb in range(len(halves))]).astype(BF16),
                NT=jnp.concatenate([n_t[it, hb][:HEAD_DIM] + n_t[it, hb][HEAD_DIM:]
                                    for hb in range(len(halves))], axis=1),
                QK=QK[it], qd=f["qd"][ru, lp],
                eg_last=f["eg_x"][(u + 1) * C - 1:(u + 1) * C, lp])
        return out

    def diag_blocks(blocks):
        n = len(blocks)
        rows = []
        for i, blk in enumerate(blocks):
            rows.append(jnp.concatenate(
                [blk if j == i else jnp.zeros_like(blk) for j in range(n)], axis=1))
        return jnp.concatenate(rows, axis=0)

    def load_state(ref, prefix, p):
        return jnp.concatenate([ref[prefix + (p * GROUP_HEADS + h,)].T
                                for h in range(GROUP_HEADS)], axis=1)

    def store_state(ref, prefix, p, st):
        for h in range(GROUP_HEADS):
            ref[prefix + (p * GROUP_HEADS + h,)] = st[:, h * HEAD_DIM:(h + 1) * HEAD_DIM].T

    chunks = [(gi, u) for gi in range(groups) for u in range(U)]
    if sequential:
        @pl.when(step == 0)
        def _():
            for p in range(PAIRS):
                sbd[p] = load_state(s0_ref, (), p)
        s_start = {p: sbd[p] for p in range(PAIRS)}
    else:
        s_start = {(idx, p): load_state(s0_ref, (idx,), p)
                   for idx in range(len(chunks)) for p in range(PAIRS)}

    def chain(gi, out, s_in):
        if sequential:
            states = {p: [s_in[p]] for p in range(PAIRS)}
            for u in range(U):
                for p in range(PAIRS):
                    d, s_cur = out[u, p], states[p][-1]
                    states[p].append(s_cur * d["eg_last"] - _mm_nt(s_cur, d["M"]) + d["NT"])
            return ({(u, p): states[p][u] for (u, p) in items},
                    {p: states[p][-1] for p in range(PAIRS)})
        entering = {(u, p): s_in[gi * U + u, p] for (u, p) in items}
        s_out = {}
        for (u, p) in items:
            d, s_cur = out[u, p], entering[u, p]
            s_out[gi * U + u, p] = s_cur * d["eg_last"] - _mm_nt(s_cur, d["M"]) + d["NT"]
        return entering, s_out

    def finish(gi, out, entering):
        r = {it: _mm_nt(jnp.concatenate([out[it]["W"], out[it]["qd"]], axis=0),
                        stack(entering[it])) for it in items}
        yield
        tiles = {it: r[it][C:] + _mm(out[it]["QK"], stack(out[it]["U"] - r[it][:C]))
                 for it in items}
        yield
        rg = slice(gi * RG, (gi + 1) * RG)
        o = jnp.concatenate([jnp.concatenate([tiles[u, p] for p in range(PAIRS)], axis=1)
                             for u in range(U)], axis=0)
        ms = seg_sum(o * o) * (1.0 / HEAD_DIM)
        z = z_ref[rg, :]
        o_ref[rg, :] = (o * lax.rsqrt(ms + NORM_EPS) * normw_ref[...]
                        * (z * _sigmoid(z))).astype(BF16)

    def drive(gens):
        results = [None] * len(gens)
        active = list(enumerate(gens))
        while active:
            still = []
            for i, gen in active:
                try:
                    next(gen)
                    still.append((i, gen))
                except StopIteration as stop:
                    results[i] = stop.value
            active = still
        return results

    state = s_start
    s_final = {}
    f_cur = drive([front(0)])[0]
    pending = None
    for gi in range(groups + 1):
        gens = []
        if gi < groups:
            gens.append(middle(f_cur))
        if gi + 1 < groups:
            gens.append(front(gi + 1))
        if pending is not None:
            entering, s_out = chain(gi - 1, pending, state)
            s_final.update(s_out)
            state = s_out if sequential else state
            gens.append(finish(gi - 1, pending, entering))
        res = drive(gens)
        pending = res[0] if gi < groups else None
        f_cur = res[1] if gi + 1 < groups else None

    if sequential:
        for p in range(PAIRS):
            sbd[p] = s_final[p]

        @pl.when(step == last_step)
        def _():
            for p in range(PAIRS):
                store_state(s_ref, (), p, s_final[p])
    else:
        for (idx, p), s_new in s_final.items():
            store_state(s_ref, (idx,), p, s_new)


def _delta(qkv, z, ab, s0, consts, n_seq, seq_len, chunk, units, groups):
    alog, dtb, normw, seg, expand = consts
    sequential = n_seq == 1
    rows = n_seq * seq_len
    n_chunks = units * groups
    blk_rows = n_chunks * chunk
    assert rows % blk_rows == 0 and (sequential or seq_len == chunk)
    const2 = lambda shape: pl.BlockSpec(shape, lambda i: (0, 0))
    tok = lambda width: pl.BlockSpec((blk_rows, width), lambda i: (i, 0))
    group_w = GROUP_HEADS * HEAD_DIM
    if sequential:
        state = pl.BlockSpec((None, HEADS, HEAD_DIM, HEAD_DIM), lambda i: (0, 0, 0, 0))
    else:
        state = pl.BlockSpec((n_chunks, HEADS, HEAD_DIM, HEAD_DIM), lambda i: (i, 0, 0, 0))
    return pl.pallas_call(
        functools.partial(_delta_kernel, chunk=chunk, units=units, groups=groups,
                          sequential=sequential),
        out_shape=[jax.ShapeDtypeStruct((rows, DN_W), BF16),
                   jax.ShapeDtypeStruct((n_seq, HEADS, HEAD_DIM, HEAD_DIM), F32)],
        grid=(rows // blk_rows,),
        in_specs=[tok(CONV_CH), tok(DN_W), tok(LANES), state,
                  const2((1, LANES)), const2((1, LANES)),
                  const2((1, DN_W)), const2((DN_W, DN_W)), const2((LANES, DN_W))],
        out_specs=[tok(DN_W), state],
        scratch_shapes=[pltpu.VMEM((HEADS // GROUP_HEADS, HEAD_DIM, group_w), F32)],
        compiler_params=pltpu.CompilerParams(
            dimension_semantics=("arbitrary",), vmem_limit_bytes=VMEM_LIMIT,
        ),
        name=f"delta_c{chunk}",
    )(qkv, z, ab, s0, alog, dtb, normw, seg, expand)


SWA_W = SWA_HEADS * HEAD_DIM


def _swa_kernel(*refs, n_q, units, prompt, batch=4):
    if prompt:
        (table_ref, sinks_ref, bucket_ref, q_ref, km_ref, kp_ref, kc_ref, vm_ref, vp_ref, vc_ref,
         o_ref, bias_ref) = refs
    else:
        (table_ref, sinks_ref, bucket_ref, q_ref, kp_ref, kc_ref, vp_ref, vc_ref,
         o_ref, bias_ref) = refs
    n_k = bias_ref.shape[-1]
    n_grp = SWA_W // KV_W
    step = pl.program_id(0)

    @pl.when(step == 0)
    def _():
        bucket = bucket_ref[...]
        for h in range(SWA_HEADS):
            acc = jnp.zeros((n_q, n_k), F32)
            for b in range(N_BUCKETS):
                acc = jnp.where(bucket == b, table_ref[b * SWA_HEADS + h], acc)
            kv, gi = divmod(h, SWA_G)
            bias_ref[kv, gi * n_q:(gi + 1) * n_q, :] = acc

    lane = lax.broadcasted_iota(jnp.int32, (1, KV_W), 1)
    kv_mask = [(lane < HEAD_DIM).astype(F32), (lane >= HEAD_DIM).astype(F32)]
    if prompt:
        first = step == 0
        k_all = jnp.concatenate([jnp.where(first, km_ref[...], kp_ref[...]), kc_ref[...]], axis=0)
        v_all = jnp.concatenate([jnp.where(first, vm_ref[...], vp_ref[...]), vc_ref[...]], axis=0)
        k_kv = [(k_all * m).astype(BF16) for m in kv_mask]
        v_kv = [(v_all * m).astype(BF16) for m in kv_mask]
        keys = lambda j, kv: k_kv[kv][j * n_q:j * n_q + n_k]
        vals = lambda j, kv: v_kv[kv][j * n_q:j * n_q + n_k]
    else:
        n_c = n_k - n_q
        cat = lambda a, b, j: jnp.concatenate([a[j * n_c:(j + 1) * n_c, :],
                                               b[j * n_q:(j + 1) * n_q, :]], axis=0)
        keys = lambda j, kv: (cat(kp_ref, kc_ref, j) * kv_mask[kv]).astype(BF16)
        vals = lambda j, kv: (cat(vp_ref, vc_ref, j) * kv_mask[kv]).astype(BF16)

    q = (q_ref[...] * HEAD_DIM ** -0.5).astype(BF16)
    ones_k = jnp.ones((n_k, KV_W), BF16)
    sink = [jnp.concatenate([jnp.full((n_q, KV_W), sinks_ref[kv * SWA_G + gi], F32)
                             for gi in range(SWA_G)], axis=0) for kv in range(SWA_KV)]
    rows4 = SWA_G * n_q
    for j0 in range(0, units, batch):
        js = range(j0, min(j0 + batch, units))
        items = [(j, kv) for j in js for kv in range(SWA_KV)]
        s = {}
        for j in js:
            rows = slice(j * n_q, (j + 1) * n_q)
            q4 = jnp.concatenate([q[rows, g * KV_W:(g + 1) * KV_W] for g in range(n_grp)], axis=0)
            for kv in range(SWA_KV):
                sj = lax.dot_general(q4, keys(j, kv), (((1,), (1,)), ((), ())),
                                     preferred_element_type=F32) + bias_ref[kv]
                if prompt and j * n_q < WINDOW - N_META:
                    key_pos = j * n_q + lax.broadcasted_iota(jnp.int32, (1, n_k), 1)
                    n_invalid = jnp.where(first, WINDOW - N_META, 0)
                    sj = jnp.where(key_pos < n_invalid, -jnp.inf, sj)
                s[j, kv] = sj
        m = {it: jnp.maximum(jnp.broadcast_to(jnp.max(s[it], axis=-1, keepdims=True),
                                              (rows4, KV_W)), sink[it[1]]) for it in items}
        p = {it: jnp.exp(s[it] - jnp.concatenate([m[it], m[it][:, :n_k - KV_W]], axis=1))
             for it in items}
        pv = {it: jnp.dot(p[it].astype(BF16), jnp.concatenate([vals(*it), ones_k], axis=1),
                          preferred_element_type=F32) for it in items}
        o = {it: pv[it][:, :KV_W] * (1.0 / (pv[it][:, KV_W:] + jnp.exp(sink[it[1]] - m[it])))
             for it in items}
        for j in js:
            o2 = o[j, 0] + o[j, 1]
            for g in range(n_grp):
                o_ref[j * n_q:(j + 1) * n_q, g * KV_W:(g + 1) * KV_W] = (
                    o2[g * n_q:(g + 1) * n_q, :].astype(BF16))


def _swa(table, sinks, bucket, q, kv_args, kv_specs, n_steps, n_q, units, prompt, name):
    n_k = bucket.shape[1]
    smem = pl.BlockSpec(memory_space=pltpu.SMEM)
    rows = units * n_q
    return pl.pallas_call(
        functools.partial(_swa_kernel, n_q=n_q, units=units, prompt=prompt),
        out_shape=jax.ShapeDtypeStruct((n_steps * rows, SWA_W), BF16),
        grid=(n_steps,),
        in_specs=[smem, smem, pl.BlockSpec((n_q, n_k), lambda i: (0, 0)),
                  pl.BlockSpec((rows, SWA_W), lambda i: (i, 0))] + list(kv_specs),
        out_specs=pl.BlockSpec((rows, SWA_W), lambda i: (i, 0)),
        scratch_shapes=[pltpu.VMEM((SWA_KV, SWA_G * n_q, n_k), F32)],
        compiler_params=pltpu.CompilerParams(dimension_semantics=("arbitrary",),
                                             vmem_limit_bytes=VMEM_LIMIT),
        name=name,
    )(table, sinks, bucket, q, *kv_args)


def _t5_bucket(rel):
    half = N_BUCKETS // 2
    max_exact = half // 2
    n = jnp.abs(rel)
    large = max_exact + (jnp.log(jnp.maximum(n, 1).astype(F32) / max_exact)
                         / math.log(MAX_DIST / max_exact) * (half - max_exact)).astype(jnp.int32)
    large = jnp.minimum(large, half - 1)
    return jnp.where(rel > 0, half, 0) + jnp.where(n < max_exact, n, large)


def _bucket_map(n_q, n_k, key_offset):
    rel = (jnp.arange(n_k)[None, :] - key_offset) - jnp.arange(n_q)[:, None]
    return _t5_bucket(rel).astype(jnp.int32)


FF_BLOCK = 1024


def _post_kernel(x_ref, dn_ref, sw_ref, gin_ref, bin_ref, wo_ref, g1_ref, b1_ref,
                 w1_ref, w2_ref, g2_ref, b2_ref, y_ref):
    h = _layer_norm(x_ref[...], gin_ref[...], bin_ref[...])
    mix = (jnp.dot(dn_ref[...], wo_ref[:DN_W, :], preferred_element_type=F32)
           + jnp.dot(sw_ref[...], wo_ref[DN_W:, :], preferred_element_type=F32))
    h1 = _layer_norm(DEEP_ALPHA * h + mix, g1_ref[...], b1_ref[...])
    h1b = h1.astype(BF16)
    f = jnp.zeros_like(h1)
    for j in range(D_FF // FF_BLOCK):
        a = jnp.dot(h1b, w1_ref[:, j * FF_BLOCK:(j + 1) * FF_BLOCK], preferred_element_type=F32)
        a = jnp.square(jnp.maximum(a, 0.0)).astype(BF16)
        f = f + jnp.dot(a, w2_ref[j * FF_BLOCK:(j + 1) * FF_BLOCK, :], preferred_element_type=F32)
    y_ref[...] = _layer_norm(DEEP_ALPHA * h1 + f, g2_ref[...], b2_ref[...])


def _post(x, dn, sw, vecs, wo, w1, w2, tm):
    gin, bin_, g1, b1, g2, b2 = vecs
    rows = x.shape[0]
    const = lambda i: (0, 0)
    vec = pl.BlockSpec((1, D_MODEL), const)
    weight = lambda shape: pl.BlockSpec(shape, const, pipeline_mode=pl.Buffered(1))
    row_blk = lambda width: pl.BlockSpec((tm, width), lambda i: (i, 0))
    return pl.pallas_call(
        _post_kernel,
        out_shape=jax.ShapeDtypeStruct((rows, D_MODEL), F32),
        grid=(rows // tm,),
        in_specs=[row_blk(D_MODEL), row_blk(DN_W), row_blk(DN_W), vec, vec,
                  weight((2 * DN_W, D_MODEL)), vec, vec,
                  weight((D_MODEL, D_FF)), weight((D_FF, D_MODEL)), vec, vec],
        out_specs=row_blk(D_MODEL),
        compiler_params=pltpu.CompilerParams(dimension_semantics=("arbitrary",),
                                             vmem_limit_bytes=VMEM_LIMIT),
        name="post_mlp",
    )(x, dn, sw, gin, bin_, wo, g1, b1, w1, w2, g2, b2)


def kernel(x_prompt, x_sample, state_delta, state_conv, cache_swa_k, cache_swa_v, meta_tokens, ln_in_g, ln_in_b, w_in, w_conv, dn_a_log, dn_dt_bias, dn_norm, swa_sinks, rel_bias_table, w_o, ln1_g, ln1_b, w_ff1, w_ff2, ln2_g, ln2_b):
    assert w_in.shape[0] == DEPTH == 1
    n_seq_s, len_s = x_sample.shape[0], x_sample.shape[1]
    len_p = x_prompt.shape[1]
    cache_len = cache_swa_k.shape[2]
    chunk_p = 64
    row = lambda t: t.reshape(1, -1).astype(F32)

    w = w_in[0]
    w_sq = (w[:, OFF_SQ:OFF_SK].reshape(D_MODEL, SWA_KV, SWA_G, HEAD_DIM)
            .transpose(0, 2, 1, 3).reshape(D_MODEL, SWA_W))
    w_r = jnp.concatenate(
        [w[:, :OFF_A], w_sq, w[:, OFF_SK:], w[:, OFF_A:OFF_SQ],
         jnp.zeros((D_MODEL, LANES - 2 * HEADS), w.dtype)], axis=1).astype(BF16)
    gin, bin_ = row(ln_in_g), row(ln_in_b)
    pad_lanes = lambda t: jnp.pad(t.reshape(1, -1).astype(F32), ((0, 0), (0, LANES - HEADS)))
    lane_head = jnp.arange(DN_W) // HEAD_DIM
    seg = (lane_head[:, None] == lane_head[None, :]).astype(BF16)
    expand = (jnp.arange(LANES)[:, None] == lane_head[None, :]).astype(BF16)
    wconv = jnp.pad(w_conv[0].astype(F32), ((0, SUBLANES - CONV_W), (0, 0)))
    delta_consts = (pad_lanes(dn_a_log[0]), pad_lanes(dn_dt_bias[0]),
                    jnp.tile(dn_norm[0].astype(F32), HEADS).reshape(1, DN_W), seg, expand)
    post_vecs = (gin, bin_, row(ln1_g[0]), row(ln1_b[0]), row(ln2_g[0]), row(ln2_b[0]))
    wo_sw = (w_o[0][DN_W:].reshape(SWA_KV, SWA_G, HEAD_DIM, D_MODEL)
             .transpose(1, 0, 2, 3).reshape(SWA_W, D_MODEL))
    wo_b = jnp.concatenate([w_o[0][:DN_W], wo_sw], axis=0).astype(BF16)
    w1_b, w2_b = w_ff1[0].astype(BF16), w_ff2[0].astype(BF16)
    table = rel_bias_table.astype(F32).reshape(-1)
    sinks = swa_sinks[0].astype(F32)

    xp = x_prompt[0]
    xs = x_sample.reshape(n_seq_s * len_s, D_MODEL)

    conv_pad = SUBLANES - (CONV_W - 1)
    zero_conv = jnp.zeros((1, SUBLANES, CONV_CH), F32)
    conv_s = jnp.pad(state_conv[0].astype(F32), ((0, 0), (conv_pad, 0), (0, 0)))
    m_qkv, m_z, m_sq, m_sk, m_sv, m_ab, m_tail = _proj(
        meta_tokens.astype(F32), gin, bin_, w_r, zero_conv, wconv, N_META, N_META)
    p_qkv, p_z, p_sq, p_sk, p_sv, p_ab, p_tail = _proj(xp, gin, bin_, w_r, m_tail, wconv, 512, None)
    s_qkv, s_z, s_sq, s_sk, s_sv, s_ab, s_tail = _proj(xs, gin, bin_, w_r, conv_s, wconv, 512, len_s)

    zero_s = jnp.zeros((1, HEADS, HEAD_DIM, HEAD_DIM), F32)
    _, s_meta = _delta(m_qkv, m_z, m_ab, zero_s, delta_consts, 1, N_META, N_META, 1, 1)
    p_dn, p_s = _delta(p_qkv, p_z, p_ab, s_meta, delta_consts, 1, len_p, chunk_p, 8, 1)
    s_dn, s_s = _delta(s_qkv, s_z, s_ab, state_delta[0].astype(F32), delta_consts,
                       n_seq_s, len_s, len_s, 8, 2)

    span = WINDOW
    lead = jnp.zeros((span - N_META, KV_W), F32)
    units_p = 8
    rows_p = units_p * chunk_p
    prev_blocks = rows_p // span
    kv_specs_p = [pl.BlockSpec((span, KV_W), lambda i: (0, 0)),
                  pl.BlockSpec((span, KV_W), lambda i: (jnp.maximum(i * prev_blocks - 1, 0), 0)),
                  pl.BlockSpec((rows_p, KV_W), lambda i: (i, 0))] * 2
    p_sw = _swa(table, sinks, _bucket_map(chunk_p, span + chunk_p, span), p_sq,
                [jnp.concatenate([lead, m_sk], axis=0), p_sk, p_sk,
                 jnp.concatenate([lead, m_sv], axis=0), p_sv, p_sv],
                kv_specs_p, len_p // rows_p, chunk_p, units_p, True, "swa_prompt")
    ck = cache_swa_k[0].astype(F32).reshape(n_seq_s * cache_len, KV_W)
    cv = cache_swa_v[0].astype(F32).reshape(n_seq_s * cache_len, KV_W)
    units_s = 8
    kv_specs_s = [pl.BlockSpec((units_s * cache_len, KV_W), lambda i: (i, 0)),
                  pl.BlockSpec((units_s * len_s, KV_W), lambda i: (i, 0))] * 2
    s_sw = _swa(table, sinks, _bucket_map(len_s, cache_len + len_s, cache_len), s_sq,
                [ck, s_sk, cv, s_sv], kv_specs_s, n_seq_s // units_s, len_s, units_s, False,
                "swa_sample")

    y_p = _post(xp, p_dn, p_sw, post_vecs, wo_b, w1_b, w2_b, 512)
    y_s = _post(xs, s_dn, s_sw, post_vecs, wo_b, w1_b, w2_b, 512)

    kv_shape = lambda n, length: (1, n, length, SWA_KV, HEAD_DIM)
    return (y_p[None], y_s.reshape(x_sample.shape),
            p_s[None], p_tail[:, -(CONV_W - 1):][None],
            p_sk[-WINDOW:].reshape(kv_shape(1, WINDOW)), p_sv[-WINDOW:].reshape(kv_shape(1, WINDOW)),
            s_s[None], s_tail[:, -(CONV_W - 1):][None],
            s_sk.reshape(kv_shape(n_seq_s, len_s)), s_sv.reshape(kv_shape(n_seq_s, len_s)))
```

```python
import functools
import math

import jax
import jax.numpy as jnp
from jax import lax
from jax.experimental import pallas as pl
from jax.experimental.pallas import tpu as pltpu

F32 = jnp.float32
BF16 = jnp.bfloat16

D_MODEL = 1024
N_META = 16
HEADS = 8
HEAD_DIM = 64
DN_W = HEADS * HEAD_DIM
CONV_W = 4
CONV_CH = 3 * DN_W
SWA_HEADS = 8
SWA_KV = 2
SWA_G = SWA_HEADS // SWA_KV
KV_W = SWA_KV * HEAD_DIM
WINDOW = 128
N_BUCKETS = 32
MAX_DIST = 128
D_FF = 4 * D_MODEL
DEPTH = 1
DEEP_ALPHA = (2 * DEPTH) ** 0.25
LN_EPS = 1e-5
NORM_EPS = 1e-6
LANES = 128
SUBLANES = 8
VMEM_LIMIT = 56 * 1024 * 1024

OFF_Z = CONV_CH
OFF_A = OFF_Z + DN_W
OFF_B = OFF_A + HEADS
OFF_SQ = OFF_B + HEADS
OFF_SK = OFF_SQ + SWA_HEADS * HEAD_DIM
OFF_SV = OFF_SK + KV_W
PROJ_WIDTH = OFF_SV + KV_W
PROJ_SPLITS = (CONV_CH, DN_W, SWA_HEADS * HEAD_DIM, KV_W, KV_W, LANES)
PROJ_PAD_W = sum(PROJ_SPLITS)


def _layer_norm(x, g, b):
    mu = jnp.mean(x, axis=-1, keepdims=True)
    xc = x - mu
    var = jnp.mean(xc * xc, axis=-1, keepdims=True)
    return xc * lax.rsqrt(var + LN_EPS) * g + b


def _mm(a, b):
    return jnp.dot(a.astype(BF16), b.astype(BF16), preferred_element_type=F32)


def _mm_nt(a, b):
    return lax.dot_general(a.astype(BF16), b.astype(BF16), (((1,), (1,)), ((), ())),
                           preferred_element_type=F32)


def _split3(x):
    x1 = x.astype(BF16)
    r1 = x - x1.astype(F32)
    x2 = r1.astype(BF16)
    x3 = (r1 - x2.astype(F32)).astype(BF16)
    return x1, x2, x3


def _dot_exact_lhs(x, sel):
    m = x.shape[0]
    stacked = jnp.concatenate(_split3(x), axis=0)
    r = jnp.dot(stacked, sel, preferred_element_type=F32)
    return r[:m] + r[m:2 * m] + r[2 * m:]


CONV_BLOCK = 256


def _sigmoid(x):
    return 1.0 / (1.0 + jnp.exp(-x))


def _proj_kernel(x_ref, g_ref, b_ref, w_ref, conv0_ref, wconv_ref,
                 y_ref, z_ref, sq_ref, sk_ref, sv_ref, ab_ref, tail_ref, xbuf, *, seq_len):
    tm = x_ref.shape[0]
    h = _layer_norm(x_ref[...], g_ref[...], b_ref[...]).astype(BF16)
    off = CONV_CH
    for o_ref, width in zip((z_ref, sq_ref, sk_ref, sv_ref, ab_ref), PROJ_SPLITS[1:]):
        o_ref[...] = jnp.dot(h, w_ref[:, off:off + width], preferred_element_type=F32)
        off += width

    chained = seq_len is None
    n_seq = 1 if chained else tm // seq_len
    L = tm if chained else seq_len
    if chained:
        @pl.when(pl.program_id(0) == 0)
        def _():
            xbuf[0, 0:SUBLANES, :] = conv0_ref[0]
    else:
        for s in range(n_seq):
            xbuf[s, 0:SUBLANES, :] = conv0_ref[s]
    blocks = [slice(cb, cb + CONV_BLOCK) for cb in range(0, CONV_CH, CONV_BLOCK)]

    def project(cols):
        raw = jnp.dot(h, w_ref[:, cols], preferred_element_type=F32)
        for s in range(n_seq):
            xbuf[s, SUBLANES:SUBLANES + L, cols] = raw[s * L:(s + 1) * L, :]

    def conv(cols):
        for s in range(n_seq):
            y = xbuf[s, SUBLANES:SUBLANES + L, cols] * wconv_ref[CONV_W - 1:CONV_W, cols]
            for j in range(CONV_W - 1):
                y = y + (xbuf[s, SUBLANES - 3 + j:SUBLANES - 3 + j + L, cols]
                         * wconv_ref[j:j + 1, cols])
            y_ref[s * L:(s + 1) * L, cols] = y * _sigmoid(y)
            tail_ref[s, :, cols] = xbuf[s, L:L + SUBLANES, cols]
            if chained:
                xbuf[s, 0:SUBLANES, cols] = xbuf[s, L:L + SUBLANES, cols]

    project(blocks[0])
    for bi, cols in enumerate(blocks):
        if bi + 1 < len(blocks):
            project(blocks[bi + 1])
        conv(cols)


def _proj(x, g, b, w_r, conv0, wconv, tm, seq_len):
    rows = x.shape[0]
    chained = seq_len is None
    n_tile_seq = 1 if chained else tm // seq_len
    n_seq = 1 if chained else rows // seq_len
    const = lambda i: (0, 0)
    row_blk = lambda width: pl.BlockSpec((tm, width), lambda i: (i, 0))
    seq_blk = pl.BlockSpec((n_tile_seq, SUBLANES, CONV_CH),
                           (lambda i: (0, 0, 0)) if chained else (lambda i: (i, 0, 0)))
    return pl.pallas_call(
        functools.partial(_proj_kernel, seq_len=seq_len),
        out_shape=[jax.ShapeDtypeStruct((rows, width), F32) for width in PROJ_SPLITS]
                  + [jax.ShapeDtypeStruct((n_seq, SUBLANES, CONV_CH), F32)],
        grid=(rows // tm,),
        in_specs=[row_blk(D_MODEL), pl.BlockSpec((1, D_MODEL), const),
                  pl.BlockSpec((1, D_MODEL), const), pl.BlockSpec((D_MODEL, PROJ_PAD_W), const),
                  seq_blk, pl.BlockSpec((SUBLANES, CONV_CH), const)],
        out_specs=[row_blk(width) for width in PROJ_SPLITS] + [seq_blk],
        scratch_shapes=[pltpu.VMEM((n_tile_seq, SUBLANES + (tm if chained else seq_len), CONV_CH),
                                   F32)],
        compiler_params=pltpu.CompilerParams(dimension_semantics=("arbitrary",),
                                             vmem_limit_bytes=VMEM_LIMIT),
        name="ln_in_proj",
    )(x, g, b, w_r, conv0, wconv)


GROUP_HEADS = 4


def _softplus(x):
    return jnp.maximum(x, 0.0) + jnp.log1p(jnp.exp(-jnp.abs(x)))


def _delta_kernel(qkv_ref, z_ref, ab_ref, s0_ref, alog_ref, dtb_ref,
                  normw_ref, seg_ref, expand_ref, o_ref, s_ref, sbd,
                  *, chunk, units, groups, sequential):
    C, U = chunk, units
    RG = U * C
    PAIRS = HEADS // GROUP_HEADS
    PW = GROUP_HEADS * HEAD_DIM
    step = pl.program_id(0)
    last_step = pl.num_programs(0) - 1
    expand = expand_ref[...]

    def seg_sum(t):
        tb = t.astype(BF16)
        return jnp.concatenate(
            [jnp.dot(tb[:, g * PW:(g + 1) * PW], seg_ref[g * PW:(g + 1) * PW, g * PW:(g + 1) * PW],
                     preferred_element_type=F32) for g in range(PAIRS)], axis=1)

    ri = lax.broadcasted_iota(jnp.int32, (C, PW), 0)
    ci = lax.broadcasted_iota(jnp.int32, (C, PW), 1) % HEAD_DIM
    causal = (ri >= ci) & (ci < C)
    strict = (ri > ci) & (ci < C)
    eye = (ri == ci).astype(F32)
    lane_head = lax.broadcasted_iota(jnp.int32, (1, PW), 1) // HEAD_DIM
    rs = lax.broadcasted_iota(jnp.int32, (LANES, LANES), 0)
    cs = lax.broadcasted_iota(jnp.int32, (LANES, LANES), 1)
    half_mask = ((rs // HEAD_DIM) == (cs // HEAD_DIM)).astype(F32)
    row_pad = [jnp.zeros((HEAD_DIM - C, PW), BF16)] if C < HEAD_DIM else []
    lane_pad = [jnp.zeros((1, HEAD_DIM - C), F32)] if C < HEAD_DIM else []
    rr = lax.broadcasted_iota(jnp.int32, (RG, RG), 0)
    cc = lax.broadcasted_iota(jnp.int32, (RG, RG), 1)
    tril = ((rr // C == cc // C) & (rr >= cc)).astype(BF16)

    def stack(y):
        yb = y.astype(BF16)
        zero = jnp.zeros_like(yb)
        pad = row_pad if y.shape[0] < HEAD_DIM else []
        blocks = []
        for h in range(GROUP_HEADS):
            blocks += [jnp.where(lane_head == h, yb, zero)] + pad
        return jnp.concatenate(blocks, axis=0)

    items = [(u, p) for u in range(U) for p in range(PAIRS)]
    sl = lambda u, p: (slice(u * C, (u + 1) * C), slice(p * PW, (p + 1) * PW))

    def front(gi):
        rg = slice(gi * RG, (gi + 1) * RG)
        q, k, v = (qkv_ref[rg, j * DN_W:(j + 1) * DN_W] for j in range(3))
        qn = q * (lax.rsqrt(seg_sum(q * q) + NORM_EPS) * HEAD_DIM ** -0.5)
        kn = k * lax.rsqrt(seg_sum(k * k) + NORM_EPS)
        yield

        ab = ab_ref[rg, :]
        g = -jnp.exp(alog_ref[...]) * _softplus(ab + dtb_ref[...])
        gs = jnp.dot(tril, jnp.concatenate(_split3(g), axis=1), preferred_element_type=F32)
        G = gs[:, :LANES] + gs[:, LANES:2 * LANES] + gs[:, 2 * LANES:]
        GT = G.T
        G_x = _dot_exact_lhs(G, expand)
        beta = _sigmoid(pltpu.roll(ab, LANES - HEADS, axis=1))
        b_hi = beta.astype(BF16)
        b_lo = (beta - b_hi.astype(F32)).astype(BF16)
        bx = jnp.dot(jnp.concatenate([b_hi, b_lo], axis=0), expand, preferred_element_type=F32)
        beta_x = bx[:RG] + bx[RG:]
        glast_x = jnp.concatenate(
            [jnp.broadcast_to(G_x[(u + 1) * C - 1:(u + 1) * C, :], (C, DN_W)) for u in range(U)],
            axis=0)
        yield
        eg_x = jnp.exp(G_x)
        kbeta = kn * beta_x
        return dict(qn=qn, kn=kn, kbeta=kbeta, bk=kbeta * eg_x, bv=v * beta_x, qd=qn * eg_x,
                    kd=kn * jnp.exp(glast_x - G_x), G_x=G_x, GT=GT, eg_x=eg_x)

    def middle(f):
        A, QK = {}, {}
        for (u, p) in items:
            ru, lp = sl(u, p)
            gr = []
            for h in range(p * GROUP_HEADS, (p + 1) * GROUP_HEADS):
                gr += [f["GT"][h:h + 1, ru]] + lane_pad
            gr = jnp.concatenate(gr, axis=1)
            decay = jnp.exp(jnp.where(causal, f["G_x"][ru, lp] - gr, -jnp.inf))
            r = _mm_nt(jnp.concatenate([f["kbeta"][ru, lp], f["qn"][ru, lp]], axis=0),
                       stack(f["kn"][ru, lp]))
            A[u, p] = jnp.where(strict, r[:C] * decay, 0.0)
            QK[u, p] = r[C:] * decay
        yield

        n_pow = int(math.log2(C)) - 1
        P = {it: eye - A[it] for it in items}
        Ap = {it: _mm(A[it], stack(A[it])) for it in items}
        yield
        for i in range(n_pow):
            if i < n_pow - 1:
                r = {it: _mm(jnp.concatenate([Ap[it], P[it]], axis=0), stack(Ap[it])) for it in items}
                Ap = {it: r[it][:C] for it in items}
                P = {it: P[it] + r[it][C:] for it in items}
            else:
                P = {it: P[it] + _mm(P[it], stack(Ap[it])) for it in items}
            yield

        halves = [slice(hb * LANES, (hb + 1) * LANES) for hb in range(PW // LANES)]
        wu = {it: _mm(P[it], jnp.concatenate([stack(f["bk"][sl(*it)]), stack(f["bv"][sl(*it)])],
                                             axis=1)) for it in items}
        kdT = {(it, hb): f["kd"][sl(*it)][:, cols].T for it in items
               for hb, cols in enumerate(halves)}
        yield
        mn = {(it, hb): _mm(kdT[it, hb], jnp.concatenate(
            [wu[it][:, cols], wu[it][:, PW:][:, cols]], axis=1))
              for it in items for hb, cols in enumerate(halves)}
        yield
        n_t = {key: (half_mask * mn[key][:, LANES:]).T for key in mn}
        out = {}
        for (u, p) in items:
            ru, lp = sl(u, p)
            it = (u, p)
            out[it] = dict(
                W=wu[it][:, :PW], U=wu[it][:, PW:],
                M=diag_blocks([half_mask * mn[it, hb][:, :LANES]
                               for hb in range(len(halves))]).astype(BF16),
                NT=jnp.concatenate([n_t[it, hb][:HEAD_DIM] + n_t[it, hb][HEAD_DIM:]
                                    for hb in range(len(halves))], axis=1),
                QK=QK[it], qd=f["qd"][ru, lp],
                eg_last=f["eg_x"][(u + 1) * C - 1:(u + 1) * C, lp])
        return out

    def diag_blocks(blocks):
        n = len(blocks)
        rows = []
        for i, blk in enumerate(blocks):
            rows.append(jnp.concatenate(
                [blk if j == i else jnp.zeros_like(blk) for j in range(n)], axis=1))
        return jnp.concatenate(rows, axis=0)

    def load_state(ref, prefix, p):
        return jnp.concatenate([ref[prefix + (p * GROUP_HEADS + h,)].T
                                for h in range(GROUP_HEADS)], axis=1)

    def store_state(ref, prefix, p, st):
        for h in range(GROUP_HEADS):
            ref[prefix + (p * GROUP_HEADS + h,)] = st[:, h * HEAD_DIM:(h + 1) * HEAD_DIM].T

    chunks = [(gi, u) for gi in range(groups) for u in range(U)]
    if sequential:
        @pl.when(step == 0)
        def _():
            for p in range(PAIRS):
                sbd[p] = load_state(s0_ref, (), p)
        s_start = {p: sbd[p] for p in range(PAIRS)}
    else:
        s_start = {(idx, p): load_state(s0_ref, (idx,), p)
                   for idx in range(len(chunks)) for p in range(PAIRS)}

    def chain(gi, out, s_in):
        if sequential:
            states = {p: [s_in[p]] for p in range(PAIRS)}
            for u in range(U):
                for p in range(PAIRS):
                    d, s_cur = out[u, p], states[p][-1]
                    states[p].append(s_cur * d["eg_last"] - _mm_nt(s_cur, d["M"]) + d["NT"])
            return ({(u, p): states[p][u] for (u, p) in items},
                    {p: states[p][-1] for p in range(PAIRS)})
        entering = {(u, p): s_in[gi * U + u, p] for (u, p) in items}
        s_out = {}
        for (u, p) in items:
            d, s_cur = out[u, p], entering[u, p]
            s_out[gi * U + u, p] = s_cur * d["eg_last"] - _mm_nt(s_cur, d["M"]) + d["NT"]
        return entering, s_out

    def finish(gi, out, entering):
        r = {it: _mm_nt(jnp.concatenate([out[it]["W"], out[it]["qd"]], axis=0),
                        stack(entering[it])) for it in items}
        yield
        tiles = {it: r[it][C:] + _mm(out[it]["QK"], stack(out[it]["U"] - r[it][:C]))
                 for it in items}
        yield
        rg = slice(gi * RG, (gi + 1) * RG)
        o = jnp.concatenate([jnp.concatenate([tiles[u, p] for p in range(PAIRS)], axis=1)
                             for u in range(U)], axis=0)
        ms = seg_sum(o * o) * (1.0 / HEAD_DIM)
        z = z_ref[rg, :]
        o_ref[rg, :] = (o * lax.rsqrt(ms + NORM_EPS) * normw_ref[...]
                        * (z * _sigmoid(z))).astype(BF16)

    def drive(gens):
        results = [None] * len(gens)
        active = list(enumerate(gens))
        while active:
            still = []
            for i, gen in active:
                try:
                    next(gen)
                    still.append((i, gen))
                except StopIteration as stop:
                    results[i] = stop.value
            active = still
        return results

    state = s_start
    s_final = {}
    f_cur = drive([front(0)])[0]
    pending = None
    for gi in range(groups + 1):
        gens = []
        if gi < groups:
            gens.append(middle(f_cur))
        if gi + 1 < groups:
            gens.append(front(gi + 1))
        if pending is not None:
            entering, s_out = chain(gi - 1, pending, state)
            s_final.update(s_out)
            state = s_out if sequential else state
            gens.append(finish(gi - 1, pending, entering))
        res = drive(gens)
        pending = res[0] if gi < groups else None
        f_cur = res[1] if gi + 1 < groups else None

    if sequential:
        for p in range(PAIRS):
            sbd[p] = s_final[p]

        @pl.when(step == last_step)
        def _():
            for p in range(PAIRS):
                store_state(s_ref, (), p, s_final[p])
    else:
        for (idx, p), s_new in s_final.items():
            store_state(s_ref, (idx,), p, s_new)


def _delta(qkv, z, ab, s0, consts, n_seq, seq_len, chunk, units, groups):
    alog, dtb, normw, seg, expand = consts
    sequential = n_seq == 1
    rows = n_seq * seq_len
    n_chunks = units * groups
    blk_rows = n_chunks * chunk
    assert rows % blk_rows == 0 and (sequential or seq_len == chunk)
    const2 = lambda shape: pl.BlockSpec(shape, lambda i: (0, 0))
    tok = lambda width: pl.BlockSpec((blk_rows, width), lambda i: (i, 0))
    group_w = GROUP_HEADS * HEAD_DIM
    if sequential:
        state = pl.BlockSpec((None, HEADS, HEAD_DIM, HEAD_DIM), lambda i: (0, 0, 0, 0))
    else:
        state = pl.BlockSpec((n_chunks, HEADS, HEAD_DIM, HEAD_DIM), lambda i: (i, 0, 0, 0))
    return pl.pallas_call(
        functools.partial(_delta_kernel, chunk=chunk, units=units, groups=groups,
                          sequential=sequential),
        out_shape=[jax.ShapeDtypeStruct((rows, DN_W), BF16),
                   jax.ShapeDtypeStruct((n_seq, HEADS, HEAD_DIM, HEAD_DIM), F32)],
        grid=(rows // blk_rows,),
        in_specs=[tok(CONV_CH), tok(DN_W), tok(LANES), state,
                  const2((1, LANES)), const2((1, LANES)),
                  const2((1, DN_W)), const2((DN_W, DN_W)), const2((LANES, DN_W))],
        out_specs=[tok(DN_W), state],
        scratch_shapes=[pltpu.VMEM((HEADS // GROUP_HEADS, HEAD_DIM, group_w), F32)],
        compiler_params=pltpu.CompilerParams(
            dimension_semantics=("arbitrary",), vmem_limit_bytes=VMEM_LIMIT,
        ),
        name=f"delta_c{chunk}",
    )(qkv, z, ab, s0, alog, dtb, normw, seg, expand)


SWA_W = SWA_HEADS * HEAD_DIM


def _swa_kernel(*refs, n_q, units, prompt, batch=4):
    if prompt:
        (table_ref, sinks_ref, bucket_ref, q_ref, km_ref, kp_ref, kc_ref, vm_ref, vp_ref, vc_ref,
         o_ref, bias_ref) = refs
    else:
        (table_ref, sinks_ref, bucket_ref, q_ref, kp_ref, kc_ref, vp_ref, vc_ref,
         o_ref, bias_ref) = refs
    n_k = bias_ref.shape[-1]
    n_grp = SWA_W // KV_W
    step = pl.program_id(0)

    @pl.when(step == 0)
    def _():
        bucket = bucket_ref[...]
        for h in range(SWA_HEADS):
            acc = jnp.zeros((n_q, n_k), F32)
            for b in range(N_BUCKETS):
                acc = jnp.where(bucket == b, table_ref[b * SWA_HEADS + h], acc)
            kv, gi = divmod(h, SWA_G)
            bias_ref[kv, gi * n_q:(gi + 1) * n_q, :] = acc

    lane = lax.broadcasted_iota(jnp.int32, (1, KV_W), 1)
    kv_mask = [(lane < HEAD_DIM).astype(F32), (lane >= HEAD_DIM).astype(F32)]
    if prompt:
        first = step == 0
        k_all = jnp.concatenate([jnp.where(first, km_ref[...], kp_ref[...]), kc_ref[...]], axis=0)
        v_all = jnp.concatenate([jnp.where(first, vm_ref[...], vp_ref[...]), vc_ref[...]], axis=0)
        k_kv = [(k_all * m).astype(BF16) for m in kv_mask]
        v_kv = [(v_all * m).astype(BF16) for m in kv_mask]
        keys = lambda j, kv: k_kv[kv][j * n_q:j * n_q + n_k]
        vals = lambda j, kv: v_kv[kv][j * n_q:j * n_q + n_k]
    else:
        n_c = n_k - n_q
        cat = lambda a, b, j: jnp.concatenate([a[j * n_c:(j + 1) * n_c, :],
                                               b[j * n_q:(j + 1) * n_q, :]], axis=0)
        keys = lambda j, kv: (cat(kp_ref, kc_ref, j) * kv_mask[kv]).astype(BF16)
        vals = lambda j, kv: (cat(vp_ref, vc_ref, j) * kv_mask[kv]).astype(BF16)

    q = (q_ref[...] * HEAD_DIM ** -0.5).astype(BF16)
    ones_k = jnp.ones((n_k, KV_W), BF16)
    sink = [jnp.concatenate([jnp.full((n_q, KV_W), sinks_ref[kv * SWA_G + gi], F32)
                             for gi in range(SWA_G)], axis=0) for kv in range(SWA_KV)]
    rows4 = SWA_G * n_q
    for j0 in range(0, units, batch):
        js = range(j0, min(j0 + batch, units))
        items = [(j, kv) for j in js for kv in range(SWA_KV)]
        s = {}
        for j in js:
            rows = slice(j * n_q, (j + 1) * n_q)
            q4 = jnp.concatenate([q[rows, g * KV_W:(g + 1) * KV_W] for g in range(n_grp)], axis=0)
            for kv in range(SWA_KV):
                sj = lax.dot_general(q4, keys(j, kv), (((1,), (1,)), ((), ())),
                                     preferred_element_type=F32) + bias_ref[kv]
                if prompt and j * n_q < WINDOW - N_META:
                    key_pos = j * n_q + lax.broadcasted_iota(jnp.int32, (1, n_k), 1)
                    n_invalid = jnp.where(first, WINDOW - N_META, 0)
                    sj = jnp.where(key_pos < n_invalid, -jnp.inf, sj)
                s[j, kv] = sj
        m = {it: jnp.maximum(jnp.broadcast_to(jnp.max(s[it], axis=-1, keepdims=True),
                                              (rows4, KV_W)), sink[it[1]]) for it in items}
        p = {it: jnp.exp(s[it] - jnp.concatenate([m[it], m[it][:, :n_k - KV_W]], axis=1))
             for it in items}
        pv = {it: jnp.dot(p[it].astype(BF16), jnp.concatenate([vals(*it), ones_k], axis=1),
                          preferred_element_type=F32) for it in items}
        o = {it: pv[it][:, :KV_W] * (1.0 / (pv[it][:, KV_W:] + jnp.exp(sink[it[1]] - m[it])))
             for it in items}
        for j in js:
            o2 = o[j, 0] + o[j, 1]
            for g in range(n_grp):
                o_ref[j * n_q:(j + 1) * n_q, g * KV_W:(g + 1) * KV_W] = (
                    o2[g * n_q:(g + 1) * n_q, :].astype(BF16))


def _swa(table, sinks, bucket, q, kv_args, kv_specs, n_steps, n_q, units, prompt, name):
    n_k = bucket.shape[1]
    smem = pl.BlockSpec(memory_space=pltpu.SMEM)
    rows = units * n_q
    return pl.pallas_call(
        functools.partial(_swa_kernel, n_q=n_q, units=units, prompt=prompt),
        out_shape=jax.ShapeDtypeStruct((n_steps * rows, SWA_W), BF16),
        grid=(n_steps,),
        in_specs=[smem, smem, pl.BlockSpec((n_q, n_k), lambda i: (0, 0)),
                  pl.BlockSpec((rows, SWA_W), lambda i: (i, 0))] + list(kv_specs),
        out_specs=pl.BlockSpec((rows, SWA_W), lambda i: (i, 0)),
        scratch_shapes=[pltpu.VMEM((SWA_KV, SWA_G * n_q, n_k), F32)],
        compiler_params=pltpu.CompilerParams(dimension_semantics=("arbitrary",),
                                             vmem_limit_bytes=VMEM_LIMIT),
        name=name,
    )(table, sinks, bucket, q, *kv_args)


def _t5_bucket(rel):
    half = N_BUCKETS // 2
    max_exact = half // 2
    n = jnp.abs(rel)
    large = max_exact + (jnp.log(jnp.maximum(n, 1).astype(F32) / max_exact)
                         / math.log(MAX_DIST / max_exact) * (half - max_exact)).astype(jnp.int32)
    large = jnp.minimum(large, half - 1)
    return jnp.where(rel > 0, half, 0) + jnp.where(n < max_exact, n, large)


def _bucket_map(n_q, n_k, key_offset):
    rel = (jnp.arange(n_k)[None, :] - key_offset) - jnp.arange(n_q)[:, None]
    return _t5_bucket(rel).astype(jnp.int32)


FF_BLOCK = 1024


def _post_kernel(x_ref, dn_ref, sw_ref, gin_ref, bin_ref, wo_ref, g1_ref, b1_ref,
                 w1_ref, w2_ref, g2_ref, b2_ref, y_ref):
    h = _layer_norm(x_ref[...], gin_ref[...], bin_ref[...])
    mix = (jnp.dot(dn_ref[...], wo_ref[:DN_W, :], preferred_element_type=F32)
           + jnp.dot(sw_ref[...], wo_ref[DN_W:, :], preferred_element_type=F32))
    h1 = _layer_norm(DEEP_ALPHA * h + mix, g1_ref[...], b1_ref[...])
    h1b = h1.astype(BF16)
    f = jnp.zeros_like(h1)
    for j in range(D_FF // FF_BLOCK):
        a = jnp.dot(h1b, w1_ref[:, j * FF_BLOCK:(j + 1) * FF_BLOCK], preferred_element_type=F32)
        a = jnp.square(jnp.maximum(a, 0.0)).astype(BF16)
        f = f + jnp.dot(a, w2_ref[j * FF_BLOCK:(j + 1) * FF_BLOCK, :], preferred_element_type=F32)
    y_ref[...] = _layer_norm(DEEP_ALPHA * h1 + f, g2_ref[...], b2_ref[...])


def _post(x, dn, sw, vecs, wo, w1, w2, tm):
    gin, bin_, g1, b1, g2, b2 = vecs
    rows = x.shape[0]
    const = lambda i: (0, 0)
    vec = pl.BlockSpec((1, D_MODEL), const)
    weight = lambda shape: pl.BlockSpec(shape, const, pipeline_mode=pl.Buffered(1))
    row_blk = lambda width: pl.BlockSpec((tm, width), lambda i: (i, 0))
    return pl.pallas_call(
        _post_kernel,
        out_shape=jax.ShapeDtypeStruct((rows, D_MODEL), F32),
        grid=(rows // tm,),
        in_specs=[row_blk(D_MODEL), row_blk(DN_W), row_blk(DN_W), vec, vec,
                  weight((2 * DN_W, D_MODEL)), vec, vec,
                  weight((D_MODEL, D_FF)), weight((D_FF, D_MODEL)), vec, vec],
        out_specs=row_blk(D_MODEL),
        compiler_params=pltpu.CompilerParams(dimension_semantics=("arbitrary",),
                                             vmem_limit_bytes=VMEM_LIMIT),
        name="post_mlp",
    )(x, dn, sw, gin, bin_, wo, g1, b1, w1, w2, g2, b2)


def kernel(x_prompt, x_sample, state_delta, state_conv, cache_swa_k, cache_swa_v, meta_tokens, ln_in_g, ln_in_b, w_in, w_conv, dn_a_log, dn_dt_bias, dn_norm, swa_sinks, rel_bias_table, w_o, ln1_g, ln1_b, w_ff1, w_ff2, ln2_g, ln2_b):
    assert w_in.shape[0] == DEPTH == 1
    n_seq_s, len_s = x_sample.shape[0], x_sample.shape[1]
    len_p = x_prompt.shape[1]
    cache_len = cache_swa_k.shape[2]
    chunk_p = 64
    row = lambda t: t.reshape(1, -1).astype(F32)

    w = w_in[0]
    w_sq = (w[:, OFF_SQ:OFF_SK].reshape(D_MODEL, SWA_KV, SWA_G, HEAD_DIM)
            .transpose(0, 2, 1, 3).reshape(D_MODEL, SWA_W))
    w_r = jnp.concatenate(
        [w[:, :OFF_A], w_sq, w[:, OFF_SK:], w[:, OFF_A:OFF_SQ],
         jnp.zeros((D_MODEL, LANES - 2 * HEADS), w.dtype)], axis=1).astype(BF16)
    gin, bin_ = row(ln_in_g), row(ln_in_b)
    pad_lanes = lambda t: jnp.pad(t.reshape(1, -1).astype(F32), ((0, 0), (0, LANES - HEADS)))
    lane_head = jnp.arange(DN_W) // HEAD_DIM
    seg = (lane_head[:, None] == lane_head[None, :]).astype(BF16)
    expand = (jnp.arange(LANES)[:, None] == lane_head[None, :]).astype(BF16)
    wconv = jnp.pad(w_conv[0].astype(F32), ((0, SUBLANES - CONV_W), (0, 0)))
    delta_consts = (pad_lanes(dn_a_log[0]), pad_lanes(dn_dt_bias[0]),
                    jnp.tile(dn_norm[0].astype(F32), HEADS).reshape(1, DN_W), seg, expand)
    post_vecs = (gin, bin_, row(ln1_g[0]), row(ln1_b[0]), row(ln2_g[0]), row(ln2_b[0]))
    wo_sw = (w_o[0][DN_W:].reshape(SWA_KV, SWA_G, HEAD_DIM, D_MODEL)
             .transpose(1, 0, 2, 3).reshape(SWA_W, D_MODEL))
    wo_b = jnp.concatenate([w_o[0][:DN_W], wo_sw], axis=0).astype(BF16)
    w1_b, w2_b = w_ff1[0].astype(BF16), w_ff2[0].astype(BF16)
    table = rel_bias_table.astype(F32).reshape(-1)
    sinks = swa_sinks[0].astype(F32)

    xp = x_prompt[0]
    xs = x_sample.reshape(n_seq_s * len_s, D_MODEL)

    conv_pad = SUBLANES - (CONV_W - 1)
    zero_conv = jnp.zeros((1, SUBLANES, CONV_CH), F32)
    conv_s = jnp.pad(state_conv[0].astype(F32), ((0, 0), (conv_pad, 0), (0, 0)))
    m_qkv, m_z, m_sq, m_sk, m_sv, m_ab, m_tail = _proj(
        meta_tokens.astype(F32), gin, bin_, w_r, zero_conv, wconv, N_META, N_META)
    p_qkv, p_z, p_sq, p_sk, p_sv, p_ab, p_tail = _proj(xp, gin, bin_, w_r, m_tail, wconv, 512, None)
    s_qkv, s_z, s_sq, s_sk, s_sv, s_ab, s_tail = _proj(xs, gin, bin_, w_r, conv_s, wconv, 512, len_s)

    zero_s = jnp.zeros((1, HEADS, HEAD_DIM, HEAD_DIM), F32)
    _, s_meta = _delta(m_qkv, m_z, m_ab, zero_s, delta_consts, 1, N_META, N_META, 1, 1)
    p_dn, p_s = _delta(p_qkv, p_z, p_ab, s_meta, delta_consts, 1, len_p, chunk_p, 4, 4)
    s_dn, s_s = _delta(s_qkv, s_z, s_ab, state_delta[0].astype(F32), delta_consts,
                       n_seq_s, len_s, len_s, 8, 2)

    span = WINDOW
    lead = jnp.zeros((span - N_META, KV_W), F32)
    units_p = 8
    rows_p = units_p * chunk_p
    prev_blocks = rows_p // span
    kv_specs_p = [pl.BlockSpec((span, KV_W), lambda i: (0, 0)),
                  pl.BlockSpec((span, KV_W), lambda i: (jnp.maximum(i * prev_blocks - 1, 0), 0)),
                  pl.BlockSpec((rows_p, KV_W), lambda i: (i, 0))] * 2
    p_sw = _swa(table, sinks, _bucket_map(chunk_p, span + chunk_p, span), p_sq,
                [jnp.concatenate([lead, m_sk], axis=0), p_sk, p_sk,
                 jnp.concatenate([lead, m_sv], axis=0), p_sv, p_sv],
                kv_specs_p, len_p // rows_p, chunk_p, units_p, True, "swa_prompt")
    ck = cache_swa_k[0].astype(F32).reshape(n_seq_s * cache_len, KV_W)
    cv = cache_swa_v[0].astype(F32).reshape(n_seq_s * cache_len, KV_W)
    units_s = 8
    kv_specs_s = [pl.BlockSpec((units_s * cache_len, KV_W), lambda i: (i, 0)),
                  pl.BlockSpec((units_s * len_s, KV_W), lambda i: (i, 0))] * 2
    s_sw = _swa(table, sinks, _bucket_map(len_s, cache_len + len_s, cache_len), s_sq,
                [ck, s_sk, cv, s_sv], kv_specs_s, n_seq_s // units_s, len_s, units_s, False,
                "swa_sample")

    y_p = _post(xp, p_dn, p_sw, post_vecs, wo_b, w1_b, w2_b, 512)
    y_s = _post(xs, s_dn, s_sw, post_vecs, wo_b, w1_b, w2_b, 512)

    kv_shape = lambda n, length: (1, n, length, SWA_KV, HEAD_DIM)
    return (y_p[None], y_s.reshape(x_sample.shape),
            p_s[None], p_tail[:, -(CONV_W - 1):][None],
            p_sk[-WINDOW:].reshape(kv_shape(1, WINDOW)), p_sv[-WINDOW:].reshape(kv_shape(1, WINDOW)),
            s_s[None], s_tail[:, -(CONV_W - 1):][None],
            s_sk.reshape(kv_shape(n_seq_s, len_s)), s_sv.reshape(kv_shape(n_seq_s, len_s)))
```

```python
import functools
import math

import jax
import jax.numpy as jnp
from jax import lax
from jax.experimental import pallas as pl
from jax.experimental.pallas import tpu as pltpu

F32 = jnp.float32
BF16 = jnp.bfloat16

D_MODEL = 1024
N_META = 16
HEADS = 8
HEAD_DIM = 64
DN_W = HEADS * HEAD_DIM
CONV_W = 4
CONV_CH = 3 * DN_W
SWA_HEADS = 8
SWA_KV = 2
SWA_G = SWA_HEADS // SWA_KV
KV_W = SWA_KV * HEAD_DIM
WINDOW = 128
N_BUCKETS = 32
MAX_DIST = 128
D_FF = 4 * D_MODEL
DEPTH = 1
DEEP_ALPHA = (2 * DEPTH) ** 0.25
LN_EPS = 1e-5
NORM_EPS = 1e-6
LANES = 128
SUBLANES = 8
VMEM_LIMIT = 56 * 1024 * 1024

OFF_Z = CONV_CH
OFF_A = OFF_Z + DN_W
OFF_B = OFF_A + HEADS
OFF_SQ = OFF_B + HEADS
OFF_SK = OFF_SQ + SWA_HEADS * HEAD_DIM
OFF_SV = OFF_SK + KV_W
PROJ_WIDTH = OFF_SV + KV_W
PROJ_SPLITS = (CONV_CH, DN_W, SWA_HEADS * HEAD_DIM, KV_W, KV_W, LANES)
PROJ_PAD_W = sum(PROJ_SPLITS)


def _layer_norm(x, g, b):
    mu = jnp.mean(x, axis=-1, keepdims=True)
    xc = x - mu
    var = jnp.mean(xc * xc, axis=-1, keepdims=True)
    return xc * lax.rsqrt(var + LN_EPS) * g + b


def _mm(a, b):
    return jnp.dot(a.astype(BF16), b.astype(BF16), preferred_element_type=F32)


def _mm_nt(a, b):
    return lax.dot_general(a.astype(BF16), b.astype(BF16), (((1,), (1,)), ((), ())),
                           preferred_element_type=F32)


def _split3(x):
    x1 = x.astype(BF16)
    r1 = x - x1.astype(F32)
    x2 = r1.astype(BF16)
    x3 = (r1 - x2.astype(F32)).astype(BF16)
    return x1, x2, x3


def _dot_exact_lhs(x, sel):
    m = x.shape[0]
    stacked = jnp.concatenate(_split3(x), axis=0)
    r = jnp.dot(stacked, sel, preferred_element_type=F32)
    return r[:m] + r[m:2 * m] + r[2 * m:]


CONV_BLOCK = 256


def _sigmoid(x):
    return 1.0 / (1.0 + jnp.exp(-x))


def _proj_kernel(x_ref, g_ref, b_ref, w_ref, wsq_ref, wkv_ref, wab_ref, conv0_ref, wconv_ref,
                 y_ref, z_ref, sq_ref, sk_ref, sv_ref, ab_ref, tail_ref, xbuf, *, seq_len):
    tm = x_ref.shape[0]
    h = _layer_norm(x_ref[...], g_ref[...], b_ref[...]).astype(BF16)

    chained = seq_len is None
    n_seq = 1 if chained else tm // seq_len
    L = tm if chained else seq_len
    if chained:
        @pl.when(pl.program_id(0) == 0)
        def _():
            xbuf[0, 0:SUBLANES, :] = conv0_ref[0]
    else:
        for s in range(n_seq):
            xbuf[s, 0:SUBLANES, :] = conv0_ref[s]
    blocks = [slice(cb, cb + CONV_BLOCK) for cb in range(0, CONV_CH, CONV_BLOCK)]

    def project(cols):
        raw = jnp.dot(h, w_ref[:, cols], preferred_element_type=F32)
        for s in range(n_seq):
            xbuf[s, SUBLANES:SUBLANES + L, cols] = raw[s * L:(s + 1) * L, :]

    def conv(cols):
        for s in range(n_seq):
            y = xbuf[s, SUBLANES:SUBLANES + L, cols] * wconv_ref[CONV_W - 1:CONV_W, cols]
            for j in range(CONV_W - 1):
                y = y + (xbuf[s, SUBLANES - 3 + j:SUBLANES - 3 + j + L, cols]
                         * wconv_ref[j:j + 1, cols])
            y_ref[s * L:(s + 1) * L, cols] = y * _sigmoid(y)
            tail_ref[s, :, cols] = xbuf[s, L:L + SUBLANES, cols]
            if chained:
                xbuf[s, 0:SUBLANES, cols] = xbuf[s, L:L + SUBLANES, cols]

    for o_ref, rhs in ((z_ref, w_ref[:, CONV_CH:]), (sq_ref, wsq_ref[...]),
                       (sk_ref, wkv_ref[:, :KV_W]), (sv_ref, wkv_ref[:, KV_W:]),
                       (ab_ref, wab_ref[...])):
        o_ref[...] = jnp.dot(h, rhs, preferred_element_type=F32)
    project(blocks[0])
    for bi, cols in enumerate(blocks):
        if bi + 1 < len(blocks):
            project(blocks[bi + 1])
        conv(cols)


def _proj(x, g, b, weights, conv0, wconv, tm, seq_len):
    rows = x.shape[0]
    w_main, w_sq, w_kv, w_ab = weights
    chained = seq_len is None
    n_tile_seq = 1 if chained else tm // seq_len
    n_seq = 1 if chained else rows // seq_len
    const = lambda i: (0, 0)
    row_blk = lambda width: pl.BlockSpec((tm, width), lambda i: (i, 0))
    seq_blk = pl.BlockSpec((n_tile_seq, SUBLANES, CONV_CH),
                           (lambda i: (0, 0, 0)) if chained else (lambda i: (i, 0, 0)))
    return pl.pallas_call(
        functools.partial(_proj_kernel, seq_len=seq_len),
        out_shape=[jax.ShapeDtypeStruct((rows, width), F32) for width in PROJ_SPLITS]
                  + [jax.ShapeDtypeStruct((n_seq, SUBLANES, CONV_CH), F32)],
        grid=(rows // tm,),
        in_specs=[row_blk(D_MODEL), pl.BlockSpec((1, D_MODEL), const),
                  pl.BlockSpec((1, D_MODEL), const)]
                 + [pl.BlockSpec(wt.shape, const) for wt in weights]
                 + [seq_blk, pl.BlockSpec((SUBLANES, CONV_CH), const)],
        out_specs=[row_blk(width) for width in PROJ_SPLITS] + [seq_blk],
        scratch_shapes=[pltpu.VMEM((n_tile_seq, SUBLANES + (tm if chained else seq_len), CONV_CH),
                                   F32)],
        compiler_params=pltpu.CompilerParams(dimension_semantics=("arbitrary",),
                                             vmem_limit_bytes=VMEM_LIMIT),
        name="ln_in_proj",
    )(x, g, b, w_main, w_sq, w_kv, w_ab, conv0, wconv)


GROUP_HEADS = 4


def _softplus(x):
    return jnp.maximum(x, 0.0) + jnp.log1p(jnp.exp(-jnp.abs(x)))


def _delta_kernel(qkv_ref, z_ref, ab_ref, s0_ref, alog_ref, dtb_ref,
                  normw_ref, seg_ref, expand_ref, o_ref, s_ref, sbd,
                  *, chunk, units, groups, sequential):
    C, U = chunk, units
    RG = U * C
    PAIRS = HEADS // GROUP_HEADS
    PW = GROUP_HEADS * HEAD_DIM
    step = pl.program_id(0)
    last_step = pl.num_programs(0) - 1
    expand = expand_ref[...]

    def seg_sum(t):
        tb = t.astype(BF16)
        return jnp.concatenate(
            [jnp.dot(tb[:, g * PW:(g + 1) * PW], seg_ref[g * PW:(g + 1) * PW, g * PW:(g + 1) * PW],
                     preferred_element_type=F32) for g in range(PAIRS)], axis=1)

    ri = lax.broadcasted_iota(jnp.int32, (C, PW), 0)
    ci = lax.broadcasted_iota(jnp.int32, (C, PW), 1) % HEAD_DIM
    causal = (ri >= ci) & (ci < C)
    strict = (ri > ci) & (ci < C)
    eye = (ri == ci).astype(F32)
    lane_head = lax.broadcasted_iota(jnp.int32, (1, PW), 1) // HEAD_DIM
    rs = lax.broadcasted_iota(jnp.int32, (LANES, LANES), 0)
    cs = lax.broadcasted_iota(jnp.int32, (LANES, LANES), 1)
    half_mask = ((rs // HEAD_DIM) == (cs // HEAD_DIM)).astype(F32)
    row_pad = [jnp.zeros((HEAD_DIM - C, PW), BF16)] if C < HEAD_DIM else []
    lane_pad = [jnp.zeros((1, HEAD_DIM - C), F32)] if C < HEAD_DIM else []
    rr = lax.broadcasted_iota(jnp.int32, (RG, RG), 0)
    cc = lax.broadcasted_iota(jnp.int32, (RG, RG), 1)
    tril = ((rr // C == cc // C) & (rr >= cc)).astype(BF16)

    def stack(y):
        yb = y.astype(BF16)
        zero = jnp.zeros_like(yb)
        pad = row_pad if y.shape[0] < HEAD_DIM else []
        blocks = []
        for h in range(GROUP_HEADS):
            blocks += [jnp.where(lane_head == h, yb, zero)] + pad
        return jnp.concatenate(blocks, axis=0)

    items = [(u, p) for u in range(U) for p in range(PAIRS)]
    sl = lambda u, p: (slice(u * C, (u + 1) * C), slice(p * PW, (p + 1) * PW))

    def front(gi):
        rg = slice(gi * RG, (gi + 1) * RG)
        q, k, v = (qkv_ref[rg, j * DN_W:(j + 1) * DN_W] for j in range(3))
        qn = q * (lax.rsqrt(seg_sum(q * q) + NORM_EPS) * HEAD_DIM ** -0.5)
        kn = k * lax.rsqrt(seg_sum(k * k) + NORM_EPS)
        yield

        ab = ab_ref[rg, :]
        g = -jnp.exp(alog_ref[...]) * _softplus(ab + dtb_ref[...])
        gs = jnp.dot(tril, jnp.concatenate(_split3(g), axis=1), preferred_element_type=F32)
        G = gs[:, :LANES] + gs[:, LANES:2 * LANES] + gs[:, 2 * LANES:]
        GT = G.T
        G_x = _dot_exact_lhs(G, expand)
        beta = _sigmoid(pltpu.roll(ab, LANES - HEADS, axis=1))
        b_hi = beta.astype(BF16)
        b_lo = (beta - b_hi.astype(F32)).astype(BF16)
        bx = jnp.dot(jnp.concatenate([b_hi, b_lo], axis=0), expand, preferred_element_type=F32)
        beta_x = bx[:RG] + bx[RG:]
        glast_x = jnp.concatenate(
            [jnp.broadcast_to(G_x[(u + 1) * C - 1:(u + 1) * C, :], (C, DN_W)) for u in range(U)],
            axis=0)
        yield
        eg_x = jnp.exp(G_x)
        kbeta = kn * beta_x
        return dict(qn=qn, kn=kn, kbeta=kbeta, bk=kbeta * eg_x, bv=v * beta_x, qd=qn * eg_x,
                    kd=kn * jnp.exp(glast_x - G_x), G_x=G_x, GT=GT, eg_x=eg_x)

    def middle(f):
        A, QK = {}, {}
        for (u, p) in items:
            ru, lp = sl(u, p)
            gr = []
            for h in range(p * GROUP_HEADS, (p + 1) * GROUP_HEADS):
                gr += [f["GT"][h:h + 1, ru]] + lane_pad
            gr = jnp.concatenate(gr, axis=1)
            decay = jnp.exp(jnp.where(causal, f["G_x"][ru, lp] - gr, -jnp.inf))
            r = _mm_nt(jnp.concatenate([f["kbeta"][ru, lp], f["qn"][ru, lp]], axis=0),
                       stack(f["kn"][ru, lp]))
            A[u, p] = jnp.where(strict, r[:C] * decay, 0.0)
            QK[u, p] = r[C:] * decay
        yield

        n_pow = int(math.log2(C)) - 1
        P = {it: eye - A[it] for it in items}
        Ap = {it: _mm(A[it], stack(A[it])) for it in items}
        yield
        for i in range(n_pow):
            if i < n_pow - 1:
                r = {it: _mm(jnp.concatenate([Ap[it], P[it]], axis=0), stack(Ap[it])) for it in items}
                Ap = {it: r[it][:C] for it in items}
                P = {it: P[it] + r[it][C:] for it in items}
            else:
                P = {it: P[it] + _mm(P[it], stack(Ap[it])) for it in items}
            yield

        halves = [slice(hb * LANES, (hb + 1) * LANES) for hb in range(PW // LANES)]
        wu = {it: _mm(P[it], jnp.concatenate([stack(f["bk"][sl(*it)]), stack(f["bv"][sl(*it)])],
                                             axis=1)) for it in items}
        kdT = {(it, hb): f["kd"][sl(*it)][:, cols].T for it in items
               for hb, cols in enumerate(halves)}
        yield
        mn = {(it, hb): _mm(kdT[it, hb], jnp.concatenate(
            [wu[it][:, cols], wu[it][:, PW:][:, cols]], axis=1))
              for it in items for hb, cols in enumerate(halves)}
        yield
        n_t = {key: (half_mask * mn[key][:, LANES:]).T for key in mn}
        out = {}
        for (u, p) in items:
            ru, lp = sl(u, p)
            it = (u, p)
            out[it] = dict(
                W=wu[it][:, :PW], U=wu[it][:, PW:],
                M=diag_blocks([half_mask * mn[it, hb][:, :LANES]
                               for hb in range(len(halves))]).astype(BF16),
                NT=jnp.concatenate([n_t[it, hb][:HEAD_DIM] + n_t[it, hb][HEAD_DIM:]
                                    for hb in range(len(halves))], axis=1),
                QK=QK[it], qd=f["qd"][ru, lp],
                eg_last=f["eg_x"][(u + 1) * C - 1:(u + 1) * C, lp])
        return out

    def diag_blocks(blocks):
        n = len(blocks)
        rows = []
        for i, blk in enumerate(blocks):
            rows.append(jnp.concatenate(
                [blk if j == i else jnp.zeros_like(blk) for j in range(n)], axis=1))
        return jnp.concatenate(rows, axis=0)

    def load_state(ref, prefix, p):
        return jnp.concatenate([ref[prefix + (p * GROUP_HEADS + h,)].T
                                for h in range(GROUP_HEADS)], axis=1)

    def store_state(ref, prefix, p, st):
        for h in range(GROUP_HEADS):
            ref[prefix + (p * GROUP_HEADS + h,)] = st[:, h * HEAD_DIM:(h + 1) * HEAD_DIM].T

    chunks = [(gi, u) for gi in range(groups) for u in range(U)]
    if sequential:
        @pl.when(step == 0)
        def _():
            for p in range(PAIRS):
                sbd[p] = load_state(s0_ref, (), p)
        s_start = {p: sbd[p] for p in range(PAIRS)}
    else:
        s_start = {(idx, p): load_state(s0_ref, (idx,), p)
                   for idx in range(len(chunks)) for p in range(PAIRS)}

    def chain(gi, out, s_in):
        if sequential:
            states = {p: [s_in[p]] for p in range(PAIRS)}
            for u in range(U):
                for p in range(PAIRS):
                    d, s_cur = out[u, p], states[p][-1]
                    states[p].append(s_cur * d["eg_last"] - _mm_nt(s_cur, d["M"]) + d["NT"])
            return ({(u, p): states[p][u] for (u, p) in items},
                    {p: states[p][-1] for p in range(PAIRS)})
        entering = {(u, p): s_in[gi * U + u, p] for (u, p) in items}
        s_out = {}
        for (u, p) in items:
            d, s_cur = out[u, p], entering[u, p]
            s_out[gi * U + u, p] = s_cur * d["eg_last"] - _mm_nt(s_cur, d["M"]) + d["NT"]
        return entering, s_out

    def finish(gi, out, entering):
        r = {it: _mm_nt(jnp.concatenate([out[it]["W"], out[it]["qd"]], axis=0),
                        stack(entering[it])) for it in items}
        yield
        tiles = {it: r[it][C:] + _mm(out[it]["QK"], stack(out[it]["U"] - r[it][:C]))
                 for it in items}
        yield
        rg = slice(gi * RG, (gi + 1) * RG)
        o = jnp.concatenate([jnp.concatenate([tiles[u, p] for p in range(PAIRS)], axis=1)
                             for u in range(U)], axis=0)
        ms = seg_sum(o * o) * (1.0 / HEAD_DIM)
        z = z_ref[rg, :]
        o_ref[rg, :] = (o * lax.rsqrt(ms + NORM_EPS) * normw_ref[...]
                        * (z * _sigmoid(z))).astype(BF16)

    def drive(gens):
        results = [None] * len(gens)
        active = list(enumerate(gens))
        while active:
            still = []
            for i, gen in active:
                try:
                    next(gen)
                    still.append((i, gen))
                except StopIteration as stop:
                    results[i] = stop.value
            active = still
        return results

    state = s_start
    s_final = {}
    f_cur = drive([front(0)])[0]
    pending = None
    for gi in range(groups + 1):
        gens = []
        if gi < groups:
            gens.append(middle(f_cur))
        if gi + 1 < groups:
            gens.append(front(gi + 1))
        if pending is not None:
            entering, s_out = chain(gi - 1, pending, state)
            s_final.update(s_out)
            state = s_out if sequential else state
            gens.append(finish(gi - 1, pending, entering))
        res = drive(gens)
        pending = res[0] if gi < groups else None
        f_cur = res[1] if gi + 1 < groups else None

    if sequential:
        for p in range(PAIRS):
            sbd[p] = s_final[p]

        @pl.when(step == last_step)
        def _():
            for p in range(PAIRS):
                store_state(s_ref, (), p, s_final[p])
    else:
        for (idx, p), s_new in s_final.items():
            store_state(s_ref, (idx,), p, s_new)


def _delta(qkv, z, ab, s0, consts, n_seq, seq_len, chunk, units, groups):
    alog, dtb, normw, seg, expand = consts
    sequential = n_seq == 1
    rows = n_seq * seq_len
    n_chunks = units * groups
    blk_rows = n_chunks * chunk
    assert rows % blk_rows == 0 and (sequential or seq_len == chunk)
    const2 = lambda shape: pl.BlockSpec(shape, lambda i: (0, 0))
    tok = lambda width: pl.BlockSpec((blk_rows, width), lambda i: (i, 0))
    group_w = GROUP_HEADS * HEAD_DIM
    if sequential:
        state = pl.BlockSpec((None, HEADS, HEAD_DIM, HEAD_DIM), lambda i: (0, 0, 0, 0))
    else:
        state = pl.BlockSpec((n_chunks, HEADS, HEAD_DIM, HEAD_DIM), lambda i: (i, 0, 0, 0))
    return pl.pallas_call(
        functools.partial(_delta_kernel, chunk=chunk, units=units, groups=groups,
                          sequential=sequential),
        out_shape=[jax.ShapeDtypeStruct((rows, DN_W), BF16),
                   jax.ShapeDtypeStruct((n_seq, HEADS, HEAD_DIM, HEAD_DIM), F32)],
        grid=(rows // blk_rows,),
        in_specs=[tok(CONV_CH), tok(DN_W), tok(LANES), state,
                  const2((1, LANES)), const2((1, LANES)),
                  const2((1, DN_W)), const2((DN_W, DN_W)), const2((LANES, DN_W))],
        out_specs=[tok(DN_W), state],
        scratch_shapes=[pltpu.VMEM((HEADS // GROUP_HEADS, HEAD_DIM, group_w), F32)],
        compiler_params=pltpu.CompilerParams(
            dimension_semantics=("arbitrary",), vmem_limit_bytes=VMEM_LIMIT,
        ),
        name=f"delta_c{chunk}",
    )(qkv, z, ab, s0, alog, dtb, normw, seg, expand)


SWA_W = SWA_HEADS * HEAD_DIM


def _swa_kernel(*refs, n_q, units, prompt, batch=4):
    if prompt:
        (table_ref, sinks_ref, bucket_ref, q_ref, km_ref, kp_ref, kc_ref, vm_ref, vp_ref, vc_ref,
         o_ref, bias_ref) = refs
    else:
        (table_ref, sinks_ref, bucket_ref, q_ref, kp_ref, kc_ref, vp_ref, vc_ref,
         o_ref, bias_ref) = refs
    n_k = bias_ref.shape[-1]
    n_grp = SWA_W // KV_W
    step = pl.program_id(0)

    @pl.when(step == 0)
    def _():
        bucket = bucket_ref[...]
        for h in range(SWA_HEADS):
            acc = jnp.zeros((n_q, n_k), F32)
            for b in range(N_BUCKETS):
                acc = jnp.where(bucket == b, table_ref[b * SWA_HEADS + h], acc)
            kv, gi = divmod(h, SWA_G)
            bias_ref[kv, gi * n_q:(gi + 1) * n_q, :] = acc

    lane = lax.broadcasted_iota(jnp.int32, (1, KV_W), 1)
    kv_mask = [(lane < HEAD_DIM).astype(F32), (lane >= HEAD_DIM).astype(F32)]
    if prompt:
        first = step == 0
        k_all = jnp.concatenate([jnp.where(first, km_ref[...], kp_ref[...]), kc_ref[...]], axis=0)
        v_all = jnp.concatenate([jnp.where(first, vm_ref[...], vp_ref[...]), vc_ref[...]], axis=0)
        k_kv = [(k_all * m).astype(BF16) for m in kv_mask]
        v_kv = [(v_all * m).astype(BF16) for m in kv_mask]
        keys = lambda j, kv: k_kv[kv][j * n_q:j * n_q + n_k]
        vals = lambda j, kv: v_kv[kv][j * n_q:j * n_q + n_k]
    else:
        n_c = n_k - n_q
        cat = lambda a, b, j: jnp.concatenate([a[j * n_c:(j + 1) * n_c, :],
                                               b[j * n_q:(j + 1) * n_q, :]], axis=0)
        keys = lambda j, kv: (cat(kp_ref, kc_ref, j) * kv_mask[kv]).astype(BF16)
        vals = lambda j, kv: (cat(vp_ref, vc_ref, j) * kv_mask[kv]).astype(BF16)

    q = (q_ref[...] * HEAD_DIM ** -0.5).astype(BF16)
    ones_k = jnp.ones((n_k, KV_W), BF16)
    sink = [jnp.concatenate([jnp.full((n_q, KV_W), sinks_ref[kv * SWA_G + gi], F32)
                             for gi in range(SWA_G)], axis=0) for kv in range(SWA_KV)]
    rows4 = SWA_G * n_q
    for j0 in range(0, units, batch):
        js = range(j0, min(j0 + batch, units))
        items = [(j, kv) for j in js for kv in range(SWA_KV)]
        s = {}
        for j in js:
            rows = slice(j * n_q, (j + 1) * n_q)
            q4 = jnp.concatenate([q[rows, g * KV_W:(g + 1) * KV_W] for g in range(n_grp)], axis=0)
            for kv in range(SWA_KV):
                sj = lax.dot_general(q4, keys(j, kv), (((1,), (1,)), ((), ())),
                                     preferred_element_type=F32) + bias_ref[kv]
                if prompt and j * n_q < WINDOW - N_META:
                    key_pos = j * n_q + lax.broadcasted_iota(jnp.int32, (1, n_k), 1)
                    n_invalid = jnp.where(first, WINDOW - N_META, 0)
                    sj = jnp.where(key_pos < n_invalid, -jnp.inf, sj)
                s[j, kv] = sj
        m = {it: jnp.maximum(jnp.broadcast_to(jnp.max(s[it], axis=-1, keepdims=True),
                                              (rows4, KV_W)), sink[it[1]]) for it in items}
        p = {it: jnp.exp(s[it] - jnp.concatenate([m[it], m[it][:, :n_k - KV_W]], axis=1))
             for it in items}
        pv = {it: jnp.dot(p[it].astype(BF16), jnp.concatenate([vals(*it), ones_k], axis=1),
                          preferred_element_type=F32) for it in items}
        o = {it: pv[it][:, :KV_W] * (1.0 / (pv[it][:, KV_W:] + jnp.exp(sink[it[1]] - m[it])))
             for it in items}
        for j in js:
            o2 = o[j, 0] + o[j, 1]
            for g in range(n_grp):
                o_ref[j * n_q:(j + 1) * n_q, g * KV_W:(g + 1) * KV_W] = (
                    o2[g * n_q:(g + 1) * n_q, :].astype(BF16))


def _swa(table, sinks, bucket, q, kv_args, kv_specs, n_steps, n_q, units, prompt, name):
    n_k = bucket.shape[1]
    smem = pl.BlockSpec(memory_space=pltpu.SMEM)
    rows = units * n_q
    return pl.pallas_call(
        functools.partial(_swa_kernel, n_q=n_q, units=units, prompt=prompt),
        out_shape=jax.ShapeDtypeStruct((n_steps * rows, SWA_W), BF16),
        grid=(n_steps,),
        in_specs=[smem, smem, pl.BlockSpec((n_q, n_k), lambda i: (0, 0)),
                  pl.BlockSpec((rows, SWA_W), lambda i: (i, 0))] + list(kv_specs),
        out_specs=pl.BlockSpec((rows, SWA_W), lambda i: (i, 0)),
        scratch_shapes=[pltpu.VMEM((SWA_KV, SWA_G * n_q, n_k), F32)],
        compiler_params=pltpu.CompilerParams(dimension_semantics=("arbitrary",),
                                             vmem_limit_bytes=VMEM_LIMIT),
        name=name,
    )(table, sinks, bucket, q, *kv_args)


def _t5_bucket(rel):
    half = N_BUCKETS // 2
    max_exact = half // 2
    n = jnp.abs(rel)
    large = max_exact + (jnp.log(jnp.maximum(n, 1).astype(F32) / max_exact)
                         / math.log(MAX_DIST / max_exact) * (half - max_exact)).astype(jnp.int32)
    large = jnp.minimum(large, half - 1)
    return jnp.where(rel > 0, half, 0) + jnp.where(n < max_exact, n, large)


def _bucket_map(n_q, n_k, key_offset):
    rel = (jnp.arange(n_k)[None, :] - key_offset) - jnp.arange(n_q)[:, None]
    return _t5_bucket(rel).astype(jnp.int32)


FF_BLOCK = 1024


def _post_kernel(xp_ref, dnp_ref, swp_ref, xs_ref, dns_ref, sws_ref, gin_ref, bin_ref,
                 wod_ref, wos_ref, g1_ref, b1_ref, w1_ref, w2_ref, g2_ref, b2_ref,
                 yp_ref, ys_ref, *, n_prompt_tiles):
    is_prompt = pl.program_id(0) < n_prompt_tiles

    @pl.when(is_prompt)
    def _():
        _post_tile(xp_ref, dnp_ref, swp_ref, gin_ref, bin_ref, wod_ref, wos_ref, g1_ref, b1_ref,
                   w1_ref, w2_ref, g2_ref, b2_ref, yp_ref)

    @pl.when(jnp.logical_not(is_prompt))
    def _():
        _post_tile(xs_ref, dns_ref, sws_ref, gin_ref, bin_ref, wod_ref, wos_ref, g1_ref, b1_ref,
                   w1_ref, w2_ref, g2_ref, b2_ref, ys_ref)


def _post_tile(x_ref, dn_ref, sw_ref, gin_ref, bin_ref, wod_ref, wos_ref, g1_ref, b1_ref,
               w1_ref, w2_ref, g2_ref, b2_ref, y_ref):
    h = _layer_norm(x_ref[...], gin_ref[...], bin_ref[...])
    mix = (jnp.dot(dn_ref[...], wod_ref[...], preferred_element_type=F32)
           + jnp.dot(sw_ref[...], wos_ref[...], preferred_element_type=F32))
    h1 = _layer_norm(DEEP_ALPHA * h + mix, g1_ref[...], b1_ref[...])
    h1b = h1.astype(BF16)
    f = jnp.zeros_like(h1)
    for j in range(D_FF // FF_BLOCK):
        a = jnp.dot(h1b, w1_ref[:, j * FF_BLOCK:(j + 1) * FF_BLOCK], preferred_element_type=F32)
        a = jnp.square(jnp.maximum(a, 0.0)).astype(BF16)
        f = f + jnp.dot(a, w2_ref[j * FF_BLOCK:(j + 1) * FF_BLOCK, :], preferred_element_type=F32)
    y_ref[...] = _layer_norm(DEEP_ALPHA * h1 + f, g2_ref[...], b2_ref[...])


def _post(prompt, sample, vecs, weights, tm):
    gin, bin_, g1, b1, g2, b2 = vecs
    wo_dn, wo_sw, w1, w2 = weights
    n_p, n_s = prompt[0].shape[0] // tm, sample[0].shape[0] // tm
    const = lambda i: (0, 0)
    vec = pl.BlockSpec((1, D_MODEL), const)
    weight = lambda wt: pl.BlockSpec(wt.shape, const, pipeline_mode=pl.Buffered(1))
    blk_p = lambda width: pl.BlockSpec((tm, width), lambda i: (jnp.minimum(i, n_p - 1), 0))
    blk_s = lambda width: pl.BlockSpec((tm, width), lambda i: (jnp.maximum(i - n_p, 0), 0))
    widths = (D_MODEL, DN_W, SWA_W)
    return pl.pallas_call(
        functools.partial(_post_kernel, n_prompt_tiles=n_p),
        out_shape=[jax.ShapeDtypeStruct((n_p * tm, D_MODEL), F32),
                   jax.ShapeDtypeStruct((n_s * tm, D_MODEL), F32)],
        grid=(n_p + n_s,),
        in_specs=[blk_p(wd) for wd in widths] + [blk_s(wd) for wd in widths]
                 + [vec, vec, weight(wo_dn), weight(wo_sw), vec, vec, weight(w1), weight(w2),
                    vec, vec],
        out_specs=[blk_p(D_MODEL), blk_s(D_MODEL)],
        compiler_params=pltpu.CompilerParams(dimension_semantics=("arbitrary",),
                                             vmem_limit_bytes=VMEM_LIMIT),
        name="post_mlp",
    )(*prompt, *sample, gin, bin_, wo_dn, wo_sw, g1, b1, w1, w2, g2, b2)


def kernel(x_prompt, x_sample, state_delta, state_conv, cache_swa_k, cache_swa_v, meta_tokens, ln_in_g, ln_in_b, w_in, w_conv, dn_a_log, dn_dt_bias, dn_norm, swa_sinks, rel_bias_table, w_o, ln1_g, ln1_b, w_ff1, w_ff2, ln2_g, ln2_b):
    assert w_in.shape[0] == DEPTH == 1
    n_seq_s, len_s = x_sample.shape[0], x_sample.shape[1]
    len_p = x_prompt.shape[1]
    cache_len = cache_swa_k.shape[2]
    chunk_p = 64
    row = lambda t: t.reshape(1, -1).astype(F32)

    w = w_in[0]
    w_sq = (w[:, OFF_SQ:OFF_SK].astype(BF16).reshape(D_MODEL, SWA_KV, SWA_G, HEAD_DIM)
            .transpose(0, 2, 1, 3).reshape(D_MODEL, SWA_W))
    w_r = (w[:, :OFF_A].astype(BF16), w_sq, w[:, OFF_SK:].astype(BF16),
           jnp.pad(w[:, OFF_A:OFF_SQ].astype(BF16), ((0, 0), (0, LANES - 2 * HEADS))))
    gin, bin_ = row(ln_in_g), row(ln_in_b)
    pad_lanes = lambda t: jnp.pad(t.reshape(1, -1).astype(F32), ((0, 0), (0, LANES - HEADS)))
    lane_head = jnp.arange(DN_W) // HEAD_DIM
    seg = (lane_head[:, None] == lane_head[None, :]).astype(BF16)
    expand = (jnp.arange(LANES)[:, None] == lane_head[None, :]).astype(BF16)
    wconv = jnp.pad(w_conv[0].astype(F32), ((0, SUBLANES - CONV_W), (0, 0)))
    delta_consts = (pad_lanes(dn_a_log[0]), pad_lanes(dn_dt_bias[0]),
                    jnp.tile(dn_norm[0].astype(F32), HEADS).reshape(1, DN_W), seg, expand)
    post_vecs = (gin, bin_, row(ln1_g[0]), row(ln1_b[0]), row(ln2_g[0]), row(ln2_b[0]))
    wo_sw = (w_o[0][DN_W:].astype(BF16).reshape(SWA_KV, SWA_G, HEAD_DIM, D_MODEL)
             .transpose(1, 0, 2, 3).reshape(SWA_W, D_MODEL))
    post_w = (w_o[0][:DN_W].astype(BF16), wo_sw, w_ff1[0].astype(BF16), w_ff2[0].astype(BF16))
    table = rel_bias_table.astype(F32).reshape(-1)
    sinks = swa_sinks[0].astype(F32)

    xp = x_prompt[0]
    xs = x_sample.reshape(n_seq_s * len_s, D_MODEL)

    conv_pad = SUBLANES - (CONV_W - 1)
    zero_conv = jnp.zeros((1, SUBLANES, CONV_CH), F32)
    conv_s = jnp.pad(state_conv[0].astype(F32), ((0, 0), (conv_pad, 0), (0, 0)))
    m_qkv, m_z, m_sq, m_sk, m_sv, m_ab, m_tail = _proj(
        meta_tokens.astype(F32), gin, bin_, w_r, zero_conv, wconv, N_META, N_META)
    p_qkv, p_z, p_sq, p_sk, p_sv, p_ab, p_tail = _proj(xp, gin, bin_, w_r, m_tail, wconv, 512, None)
    s_qkv, s_z, s_sq, s_sk, s_sv, s_ab, s_tail = _proj(xs, gin, bin_, w_r, conv_s, wconv, 512, len_s)

    zero_s = jnp.zeros((1, HEADS, HEAD_DIM, HEAD_DIM), F32)
    _, s_meta = _delta(m_qkv, m_z, m_ab, zero_s, delta_consts, 1, N_META, N_META, 1, 1)
    p_dn, p_s = _delta(p_qkv, p_z, p_ab, s_meta, delta_consts, 1, len_p, chunk_p, 4, 2)
    s_dn, s_s = _delta(s_qkv, s_z, s_ab, state_delta[0].astype(F32), delta_consts,
                       n_seq_s, len_s, len_s, 8, 2)

    span = WINDOW
    lead = jnp.zeros((span - N_META, KV_W), F32)
    units_p = 8
    rows_p = units_p * chunk_p
    prev_blocks = rows_p // span
    kv_specs_p = [pl.BlockSpec((span, KV_W), lambda i: (0, 0)),
                  pl.BlockSpec((span, KV_W), lambda i: (jnp.maximum(i * prev_blocks - 1, 0), 0)),
                  pl.BlockSpec((rows_p, KV_W), lambda i: (i, 0))] * 2
    p_sw = _swa(table, sinks, _bucket_map(chunk_p, span + chunk_p, span), p_sq,
                [jnp.concatenate([lead, m_sk], axis=0), p_sk, p_sk,
                 jnp.concatenate([lead, m_sv], axis=0), p_sv, p_sv],
                kv_specs_p, len_p // rows_p, chunk_p, units_p, True, "swa_prompt")
    ck = cache_swa_k[0].astype(F32).reshape(n_seq_s * cache_len, KV_W)
    cv = cache_swa_v[0].astype(F32).reshape(n_seq_s * cache_len, KV_W)
    units_s = 8
    kv_specs_s = [pl.BlockSpec((units_s * cache_len, KV_W), lambda i: (i, 0)),
                  pl.BlockSpec((units_s * len_s, KV_W), lambda i: (i, 0))] * 2
    s_sw = _swa(table, sinks, _bucket_map(len_s, cache_len + len_s, cache_len), s_sq,
                [ck, s_sk, cv, s_sv], kv_specs_s, n_seq_s // units_s, len_s, units_s, False,
                "swa_sample")

    y_p, y_s = _post((xp, p_dn, p_sw), (xs, s_dn, s_sw), post_vecs, post_w, 512)

    kv_shape = lambda n, length: (1, n, length, SWA_KV, HEAD_DIM)
    return (y_p[None], y_s.reshape(x_sample.shape),
            p_s[None], p_tail[:, -(CONV_W - 1):][None],
            p_sk[-WINDOW:].reshape(kv_shape(1, WINDOW)), p_sv[-WINDOW:].reshape(kv_shape(1, WINDOW)),
            s_s[None], s_tail[:, -(CONV_W - 1):][None],
            s_sk.reshape(kv_shape(n_seq_s, len_s)), s_sv.reshape(kv_shape(n_seq_s, len_s)))
```

```python
import functools
import math

import jax
import jax.numpy as jnp
from jax import lax
from jax.experimental import pallas as pl
from jax.experimental.pallas import tpu as pltpu

F32 = jnp.float32
BF16 = jnp.bfloat16

D_MODEL = 1024
N_META = 16
HEADS = 8
HEAD_DIM = 64
DN_W = HEADS * HEAD_DIM
CONV_W = 4
CONV_CH = 3 * DN_W
SWA_HEADS = 8
SWA_KV = 2
SWA_G = SWA_HEADS // SWA_KV
KV_W = SWA_KV * HEAD_DIM
WINDOW = 128
N_BUCKETS = 32
MAX_DIST = 128
D_FF = 4 * D_MODEL
DEPTH = 1
DEEP_ALPHA = (2 * DEPTH) ** 0.25
LN_EPS = 1e-5
NORM_EPS = 1e-6
LANES = 128
SUBLANES = 8
VMEM_LIMIT = 56 * 1024 * 1024

OFF_Z = CONV_CH
OFF_A = OFF_Z + DN_W
OFF_B = OFF_A + HEADS
OFF_SQ = OFF_B + HEADS
OFF_SK = OFF_SQ + SWA_HEADS * HEAD_DIM
OFF_SV = OFF_SK + KV_W
PROJ_WIDTH = OFF_SV + KV_W
PROJ_SPLITS = (CONV_CH, DN_W, SWA_HEADS * HEAD_DIM, KV_W, KV_W, LANES)
PROJ_PAD_W = sum(PROJ_SPLITS)


def _layer_norm(x, g, b):
    mu = jnp.mean(x, axis=-1, keepdims=True)
    xc = x - mu
    var = jnp.mean(xc * xc, axis=-1, keepdims=True)
    return xc * lax.rsqrt(var + LN_EPS) * g + b


def _mm(a, b):
    return jnp.dot(a.astype(BF16), b.astype(BF16), preferred_element_type=F32)


def _mm_nt(a, b):
    return lax.dot_general(a.astype(BF16), b.astype(BF16), (((1,), (1,)), ((), ())),
                           preferred_element_type=F32)


def _split3(x):
    x1 = x.astype(BF16)
    r1 = x - x1.astype(F32)
    x2 = r1.astype(BF16)
    x3 = (r1 - x2.astype(F32)).astype(BF16)
    return x1, x2, x3


def _dot_exact_lhs(x, sel):
    m = x.shape[0]
    stacked = jnp.concatenate(_split3(x), axis=0)
    r = jnp.dot(stacked, sel, preferred_element_type=F32)
    return r[:m] + r[m:2 * m] + r[2 * m:]


CONV_BLOCK = 256


def _sigmoid(x):
    return 1.0 / (1.0 + jnp.exp(-x))


def _proj_kernel(x_ref, g_ref, b_ref, w_ref, wsq_ref, wkv_ref, wab_ref, conv0_ref, wconv_ref,
                 y_ref, z_ref, sq_ref, sk_ref, sv_ref, ab_ref, tail_ref, xbuf, *, seq_len):
    tm = x_ref.shape[0]
    h = _layer_norm(x_ref[...], g_ref[...], b_ref[...]).astype(BF16)

    chained = seq_len is None
    n_seq = 1 if chained else tm // seq_len
    L = tm if chained else seq_len
    if chained:
        @pl.when(pl.program_id(0) == 0)
        def _():
            xbuf[0, 0:SUBLANES, :] = conv0_ref[0]
    else:
        for s in range(n_seq):
            xbuf[s, 0:SUBLANES, :] = conv0_ref[s]
    blocks = [slice(cb, cb + CONV_BLOCK) for cb in range(0, CONV_CH, CONV_BLOCK)]

    def project(cols):
        raw = jnp.dot(h, w_ref[:, cols], preferred_element_type=F32)
        for s in range(n_seq):
            xbuf[s, SUBLANES:SUBLANES + L, cols] = raw[s * L:(s + 1) * L, :]

    def conv(cols):
        for s in range(n_seq):
            ext = xbuf[s, :, cols]
            w0, w1, w2, w3 = (wconv_ref[j:j + 1, cols] for j in range(CONV_W))
            ext1 = pltpu.roll(ext, 1, axis=0)
            y = (w3 * ext + w2 * ext1)[SUBLANES:, :]
            y = y + pltpu.roll(w1 * ext + w0 * ext1, 2, axis=0)[SUBLANES:, :]
            half = 0.5 * y
            y_ref[s * L:(s + 1) * L, cols] = half * (1.0 + jnp.tanh(half))
            tail_ref[s, :, cols] = xbuf[s, L:L + SUBLANES, cols]
            if chained:
                xbuf[s, 0:SUBLANES, cols] = xbuf[s, L:L + SUBLANES, cols]

    for o_ref, rhs in ((z_ref, w_ref[:, CONV_CH:]), (sq_ref, wsq_ref[...]),
                       (sk_ref, wkv_ref[:, :KV_W]), (sv_ref, wkv_ref[:, KV_W:]),
                       (ab_ref, wab_ref[...])):
        o_ref[...] = jnp.dot(h, rhs, preferred_element_type=F32)
    project(blocks[0])
    for bi, cols in enumerate(blocks):
        if bi + 1 < len(blocks):
            project(blocks[bi + 1])
        conv(cols)


def _proj(x, g, b, weights, conv0, wconv, tm, seq_len):
    rows = x.shape[0]
    w_main, w_sq, w_kv, w_ab = weights
    chained = seq_len is None
    n_tile_seq = 1 if chained else tm // seq_len
    n_seq = 1 if chained else rows // seq_len
    const = lambda i: (0, 0)
    row_blk = lambda width: pl.BlockSpec((tm, width), lambda i: (i, 0))
    seq_blk = pl.BlockSpec((n_tile_seq, SUBLANES, CONV_CH),
                           (lambda i: (0, 0, 0)) if chained else (lambda i: (i, 0, 0)))
    return pl.pallas_call(
        functools.partial(_proj_kernel, seq_len=seq_len),
        out_shape=[jax.ShapeDtypeStruct((rows, width), F32) for width in PROJ_SPLITS]
                  + [jax.ShapeDtypeStruct((n_seq, SUBLANES, CONV_CH), F32)],
        grid=(rows // tm,),
        in_specs=[row_blk(D_MODEL), pl.BlockSpec((1, D_MODEL), const),
                  pl.BlockSpec((1, D_MODEL), const)]
                 + [pl.BlockSpec(wt.shape, const) for wt in weights]
                 + [seq_blk, pl.BlockSpec((SUBLANES, CONV_CH), const)],
        out_specs=[row_blk(width) for width in PROJ_SPLITS] + [seq_blk],
        scratch_shapes=[pltpu.VMEM((n_tile_seq, SUBLANES + (tm if chained else seq_len), CONV_CH),
                                   F32)],
        compiler_params=pltpu.CompilerParams(dimension_semantics=("arbitrary",),
                                             vmem_limit_bytes=VMEM_LIMIT),
        name="ln_in_proj",
    )(x, g, b, w_main, w_sq, w_kv, w_ab, conv0, wconv)


GROUP_HEADS = 4


def _softplus(x):
    return jnp.maximum(x, 0.0) + jnp.log1p(jnp.exp(-jnp.abs(x)))


def _delta_kernel(qkv_ref, z_ref, ab_ref, s0_ref, alog_ref, dtb_ref,
                  normw_ref, seg_ref, expand_ref, o_ref, s_ref, sbd,
                  *, chunk, units, groups, sequential):
    C, U = chunk, units
    RG = U * C
    PAIRS = HEADS // GROUP_HEADS
    PW = GROUP_HEADS * HEAD_DIM
    step = pl.program_id(0)
    last_step = pl.num_programs(0) - 1
    expand = expand_ref[...]

    def seg_sum(t):
        tb = t.astype(BF16)
        return jnp.concatenate(
            [jnp.dot(tb[:, g * PW:(g + 1) * PW], seg_ref[g * PW:(g + 1) * PW, g * PW:(g + 1) * PW],
                     preferred_element_type=F32) for g in range(PAIRS)], axis=1)

    ri = lax.broadcasted_iota(jnp.int32, (C, PW), 0)
    ci = lax.broadcasted_iota(jnp.int32, (C, PW), 1) % HEAD_DIM
    causal = (ri >= ci) & (ci < C)
    strict = (ri > ci) & (ci < C)
    eye = (ri == ci).astype(F32)
    lane_head = lax.broadcasted_iota(jnp.int32, (1, PW), 1) // HEAD_DIM
    rs = lax.broadcasted_iota(jnp.int32, (LANES, LANES), 0)
    cs = lax.broadcasted_iota(jnp.int32, (LANES, LANES), 1)
    half_mask = ((rs // HEAD_DIM) == (cs // HEAD_DIM)).astype(F32)
    row_pad = [jnp.zeros((HEAD_DIM - C, PW), BF16)] if C < HEAD_DIM else []
    lane_pad = [jnp.zeros((1, HEAD_DIM - C), F32)] if C < HEAD_DIM else []
    rr = lax.broadcasted_iota(jnp.int32, (RG, RG), 0)
    cc = lax.broadcasted_iota(jnp.int32, (RG, RG), 1)
    tril = ((rr // C == cc // C) & (rr >= cc)).astype(BF16)

    def stack(y):
        yb = y.astype(BF16)
        zero = jnp.zeros_like(yb)
        pad = row_pad if y.shape[0] < HEAD_DIM else []
        blocks = []
        for h in range(GROUP_HEADS):
            blocks += [jnp.where(lane_head == h, yb, zero)] + pad
        return jnp.concatenate(blocks, axis=0)

    items = [(u, p) for u in range(U) for p in range(PAIRS)]
    sl = lambda u, p: (slice(u * C, (u + 1) * C), slice(p * PW, (p + 1) * PW))

    def front(gi):
        rg = slice(gi * RG, (gi + 1) * RG)
        q, k, v = (qkv_ref[rg, j * DN_W:(j + 1) * DN_W] for j in range(3))
        qn = q * (lax.rsqrt(seg_sum(q * q) + NORM_EPS) * HEAD_DIM ** -0.5)
        kn = k * lax.rsqrt(seg_sum(k * k) + NORM_EPS)
        yield

        ab = ab_ref[rg, :]
        g = -jnp.exp(alog_ref[...]) * _softplus(ab + dtb_ref[...])
        gs = jnp.dot(tril, jnp.concatenate(_split3(g), axis=1), preferred_element_type=F32)
        G = gs[:, :LANES] + gs[:, LANES:2 * LANES] + gs[:, 2 * LANES:]
        GT = G.T
        G_x = _dot_exact_lhs(G, expand)
        beta = _sigmoid(pltpu.roll(ab, LANES - HEADS, axis=1))
        b_hi = beta.astype(BF16)
        b_lo = (beta - b_hi.astype(F32)).astype(BF16)
        bx = jnp.dot(jnp.concatenate([b_hi, b_lo], axis=0), expand, preferred_element_type=F32)
        beta_x = bx[:RG] + bx[RG:]
        glast_x = jnp.concatenate(
            [jnp.broadcast_to(G_x[(u + 1) * C - 1:(u + 1) * C, :], (C, DN_W)) for u in range(U)],
            axis=0)
        yield
        eg_x = jnp.exp(G_x)
        kbeta = kn * beta_x
        return dict(qn=qn, kn=kn, kbeta=kbeta, bk=kbeta * eg_x, bv=v * beta_x, qd=qn * eg_x,
                    kd=kn * jnp.exp(glast_x - G_x), G_x=G_x, GT=GT, eg_x=eg_x)

    def middle(f):
        A, QK = {}, {}
        for (u, p) in items:
            ru, lp = sl(u, p)
            gr = []
            for h in range(p * GROUP_HEADS, (p + 1) * GROUP_HEADS):
                gr += [f["GT"][h:h + 1, ru]] + lane_pad
            gr = jnp.concatenate(gr, axis=1)
            decay = jnp.exp(jnp.where(causal, f["G_x"][ru, lp] - gr, -jnp.inf))
            r = _mm_nt(jnp.concatenate([f["kbeta"][ru, lp], f["qn"][ru, lp]], axis=0),
                       stack(f["kn"][ru, lp]))
            A[u, p] = jnp.where(strict, r[:C] * decay, 0.0)
            QK[u, p] = r[C:] * decay
        yield

        n_pow = int(math.log2(C)) - 1
        P = {it: eye - A[it] for it in items}
        Ap = {it: _mm(A[it], stack(A[it])) for it in items}
        yield
        for i in range(n_pow):
            if i < n_pow - 1:
                r = {it: _mm(jnp.concatenate([Ap[it], P[it]], axis=0), stack(Ap[it])) for it in items}
                Ap = {it: r[it][:C] for it in items}
                P = {it: P[it] + r[it][C:] for it in items}
            else:
                P = {it: P[it] + _mm(P[it], stack(Ap[it])) for it in items}
            yield

        halves = [slice(hb * LANES, (hb + 1) * LANES) for hb in range(PW // LANES)]
        wu = {it: _mm(P[it], jnp.concatenate([stack(f["bk"][sl(*it)]), stack(f["bv"][sl(*it)])],
                                             axis=1)) for it in items}
        kdT = {(it, hb): f["kd"][sl(*it)][:, cols].T for it in items
               for hb, cols in enumerate(halves)}
        yield
        mn = {(it, hb): _mm(kdT[it, hb], jnp.concatenate(
            [wu[it][:, cols], wu[it][:, PW:][:, cols]], axis=1))
              for it in items for hb, cols in enumerate(halves)}
        yield
        n_t = {key: (half_mask * mn[key][:, LANES:]).T for key in mn}
        out = {}
        for (u, p) in items:
            ru, lp = sl(u, p)
            it = (u, p)
            out[it] = dict(
                W=wu[it][:, :PW], U=wu[it][:, PW:],
                M=diag_blocks([half_mask * mn[it, hb][:, :LANES]
                               for hb in range(len(halves))]).astype(BF16),
                NT=jnp.concatenate([n_t[it, hb][:HEAD_DIM] + n_t[it, hb][HEAD_DIM:]
                                    for hb in range(len(halves))], axis=1),
                QK=QK[it], qd=f["qd"][ru, lp],
                eg_last=f["eg_x"][(u + 1) * C - 1:(u + 1) * C, lp])
        return out

    def diag_blocks(blocks):
        n = len(blocks)
        rows = []
        for i, blk in enumerate(blocks):
            rows.append(jnp.concatenate(
                [blk if j == i else jnp.zeros_like(blk) for j in range(n)], axis=1))
        return jnp.concatenate(rows, axis=0)

    def load_state(ref, prefix, p):
        return jnp.concatenate([ref[prefix + (p * GROUP_HEADS + h,)].T
                                for h in range(GROUP_HEADS)], axis=1)

    def store_state(ref, prefix, p, st):
        for h in range(GROUP_HEADS):
            ref[prefix + (p * GROUP_HEADS + h,)] = st[:, h * HEAD_DIM:(h + 1) * HEAD_DIM].T

    chunks = [(gi, u) for gi in range(groups) for u in range(U)]
    if sequential:
        @pl.when(step == 0)
        def _():
            for p in range(PAIRS):
                sbd[p] = load_state(s0_ref, (), p)
        s_start = {p: sbd[p] for p in range(PAIRS)}
    else:
        s_start = {(idx, p): load_state(s0_ref, (idx,), p)
                   for idx in range(len(chunks)) for p in range(PAIRS)}

    def chain(gi, out, s_in):
        if sequential:
            states = {p: [s_in[p]] for p in range(PAIRS)}
            for u in range(U):
                for p in range(PAIRS):
                    d, s_cur = out[u, p], states[p][-1]
                    states[p].append(s_cur * d["eg_last"] - _mm_nt(s_cur, d["M"]) + d["NT"])
            return ({(u, p): states[p][u] for (u, p) in items},
                    {p: states[p][-1] for p in range(PAIRS)})
        entering = {(u, p): s_in[gi * U + u, p] for (u, p) in items}
        s_out = {}
        for (u, p) in items:
            d, s_cur = out[u, p], entering[u, p]
            s_out[gi * U + u, p] = s_cur * d["eg_last"] - _mm_nt(s_cur, d["M"]) + d["NT"]
        return entering, s_out

    def finish(gi, out, entering):
        r = {it: _mm_nt(jnp.concatenate([out[it]["W"], out[it]["qd"]], axis=0),
                        stack(entering[it])) for it in items}
        yield
        tiles = {it: r[it][C:] + _mm(out[it]["QK"], stack(out[it]["U"] - r[it][:C]))
                 for it in items}
        yield
        rg = slice(gi * RG, (gi + 1) * RG)
        o = jnp.concatenate([jnp.concatenate([tiles[u, p] for p in range(PAIRS)], axis=1)
                             for u in range(U)], axis=0)
        ms = seg_sum(o * o) * (1.0 / HEAD_DIM)
        z = z_ref[rg, :]
        o_ref[rg, :] = (o * lax.rsqrt(ms + NORM_EPS) * normw_ref[...]
                        * (z * _sigmoid(z))).astype(BF16)

    def drive(gens):
        results = [None] * len(gens)
        active = list(enumerate(gens))
        while active:
            still = []
            for i, gen in active:
                try:
                    next(gen)
                    still.append((i, gen))
                except StopIteration as stop:
                    results[i] = stop.value
            active = still
        return results

    state = s_start
    s_final = {}
    f_cur = drive([front(0)])[0]
    pending = None
    for gi in range(groups + 1):
        gens = []
        if gi < groups:
            gens.append(middle(f_cur))
        if gi + 1 < groups:
            gens.append(front(gi + 1))
        if pending is not None:
            entering, s_out = chain(gi - 1, pending, state)
            s_final.update(s_out)
            state = s_out if sequential else state
            gens.append(finish(gi - 1, pending, entering))
        res = drive(gens)
        pending = res[0] if gi < groups else None
        f_cur = res[1] if gi + 1 < groups else None

    if sequential:
        for p in range(PAIRS):
            sbd[p] = s_final[p]

        @pl.when(step == last_step)
        def _():
            for p in range(PAIRS):
                store_state(s_ref, (), p, s_final[p])
    else:
        for (idx, p), s_new in s_final.items():
            store_state(s_ref, (idx,), p, s_new)


def _delta(qkv, z, ab, s0, consts, n_seq, seq_len, chunk, units, groups):
    alog, dtb, normw, seg, expand = consts
    sequential = n_seq == 1
    rows = n_seq * seq_len
    n_chunks = units * groups
    blk_rows = n_chunks * chunk
    assert rows % blk_rows == 0 and (sequential or seq_len == chunk)
    const2 = lambda shape: pl.BlockSpec(shape, lambda i: (0, 0))
    tok = lambda width: pl.BlockSpec((blk_rows, width), lambda i: (i, 0))
    group_w = GROUP_HEADS * HEAD_DIM
    if sequential:
        state = pl.BlockSpec((None, HEADS, HEAD_DIM, HEAD_DIM), lambda i: (0, 0, 0, 0))
    else:
        state = pl.BlockSpec((n_chunks, HEADS, HEAD_DIM, HEAD_DIM), lambda i: (i, 0, 0, 0))
    return pl.pallas_call(
        functools.partial(_delta_kernel, chunk=chunk, units=units, groups=groups,
                          sequential=sequential),
        out_shape=[jax.ShapeDtypeStruct((rows, DN_W), BF16),
                   jax.ShapeDtypeStruct((n_seq, HEADS, HEAD_DIM, HEAD_DIM), F32)],
        grid=(rows // blk_rows,),
        in_specs=[tok(CONV_CH), tok(DN_W), tok(LANES), state,
                  const2((1, LANES)), const2((1, LANES)),
                  const2((1, DN_W)), const2((DN_W, DN_W)), const2((LANES, DN_W))],
        out_specs=[tok(DN_W), state],
        scratch_shapes=[pltpu.VMEM((HEADS // GROUP_HEADS, HEAD_DIM, group_w), F32)],
        compiler_params=pltpu.CompilerParams(
            dimension_semantics=("arbitrary",), vmem_limit_bytes=VMEM_LIMIT,
        ),
        name=f"delta_c{chunk}",
    )(qkv, z, ab, s0, alog, dtb, normw, seg, expand)


SWA_W = SWA_HEADS * HEAD_DIM


def _swa_kernel(*refs, n_q, units, prompt, batch=4):
    if prompt:
        (table_ref, sinks_ref, bucket_ref, q_ref, km_ref, kp_ref, kc_ref, vm_ref, vp_ref, vc_ref,
         o_ref, bias_ref) = refs
    else:
        (table_ref, sinks_ref, bucket_ref, q_ref, kp_ref, kc_ref, vp_ref, vc_ref,
         o_ref, bias_ref) = refs
    n_k = bias_ref.shape[-1]
    n_grp = SWA_W // KV_W
    step = pl.program_id(0)

    @pl.when(step == 0)
    def _():
        bucket = bucket_ref[...]
        for h in range(SWA_HEADS):
            acc = jnp.zeros((n_q, n_k), F32)
            for b in range(N_BUCKETS):
                acc = jnp.where(bucket == b, table_ref[b * SWA_HEADS + h], acc)
            kv, gi = divmod(h, SWA_G)
            bias_ref[kv, gi * n_q:(gi + 1) * n_q, :] = acc

    lane = lax.broadcasted_iota(jnp.int32, (1, KV_W), 1)
    kv_mask = [(lane < HEAD_DIM).astype(F32), (lane >= HEAD_DIM).astype(F32)]
    if prompt:
        first = step == 0
        k_all = jnp.concatenate([jnp.where(first, km_ref[...], kp_ref[...]), kc_ref[...]], axis=0)
        v_all = jnp.concatenate([jnp.where(first, vm_ref[...], vp_ref[...]), vc_ref[...]], axis=0)
        k_kv = [(k_all * m).astype(BF16) for m in kv_mask]
        v_kv = [(v_all * m).astype(BF16) for m in kv_mask]
        keys = lambda j, kv: k_kv[kv][j * n_q:j * n_q + n_k]
        vals = lambda j, kv: v_kv[kv][j * n_q:j * n_q + n_k]
    else:
        n_c = n_k - n_q
        cat = lambda a, b, j: jnp.concatenate([a[j * n_c:(j + 1) * n_c, :],
                                               b[j * n_q:(j + 1) * n_q, :]], axis=0)
        keys = lambda j, kv: (cat(kp_ref, kc_ref, j) * kv_mask[kv]).astype(BF16)
        vals = lambda j, kv: (cat(vp_ref, vc_ref, j) * kv_mask[kv]).astype(BF16)

    q = (q_ref[...] * HEAD_DIM ** -0.5).astype(BF16)
    ones_k = jnp.ones((n_k, KV_W), BF16)
    sink = [jnp.concatenate([jnp.full((n_q, KV_W), sinks_ref[kv * SWA_G + gi], F32)
                             for gi in range(SWA_G)], axis=0) for kv in range(SWA_KV)]
    rows4 = SWA_G * n_q
    for j0 in range(0, units, batch):
        js = range(j0, min(j0 + batch, units))
        items = [(j, kv) for j in js for kv in range(SWA_KV)]
        s = {}
        for j in js:
            rows = slice(j * n_q, (j + 1) * n_q)
            q4 = jnp.concatenate([q[rows, g * KV_W:(g + 1) * KV_W] for g in range(n_grp)], axis=0)
            for kv in range(SWA_KV):
                sj = lax.dot_general(q4, keys(j, kv), (((1,), (1,)), ((), ())),
                                     preferred_element_type=F32) + bias_ref[kv]
                if prompt and j * n_q < WINDOW - N_META:
                    key_pos = j * n_q + lax.broadcasted_iota(jnp.int32, (1, n_k), 1)
                    n_invalid = jnp.where(first, WINDOW - N_META, 0)
                    sj = jnp.where(key_pos < n_invalid, -jnp.inf, sj)
                s[j, kv] = sj
        m = {it: jnp.maximum(jnp.broadcast_to(jnp.max(s[it], axis=-1, keepdims=True),
                                              (rows4, KV_W)), sink[it[1]]) for it in items}
        p = {it: jnp.exp(s[it] - jnp.concatenate([m[it], m[it][:, :n_k - KV_W]], axis=1))
             for it in items}
        pv = {it: jnp.dot(p[it].astype(BF16), jnp.concatenate([vals(*it), ones_k], axis=1),
                          preferred_element_type=F32) for it in items}
        o = {it: pv[it][:, :KV_W] * (1.0 / (pv[it][:, KV_W:] + jnp.exp(sink[it[1]] - m[it])))
             for it in items}
        for j in js:
            o2 = o[j, 0] + o[j, 1]
            for g in range(n_grp):
                o_ref[j * n_q:(j + 1) * n_q, g * KV_W:(g + 1) * KV_W] = (
                    o2[g * n_q:(g + 1) * n_q, :].astype(BF16))


def _swa(table, sinks, bucket, q, kv_args, kv_specs, n_steps, n_q, units, prompt, name):
    n_k = bucket.shape[1]
    smem = pl.BlockSpec(memory_space=pltpu.SMEM)
    rows = units * n_q
    return pl.pallas_call(
        functools.partial(_swa_kernel, n_q=n_q, units=units, prompt=prompt),
        out_shape=jax.ShapeDtypeStruct((n_steps * rows, SWA_W), BF16),
        grid=(n_steps,),
        in_specs=[smem, smem, pl.BlockSpec((n_q, n_k), lambda i: (0, 0)),
                  pl.BlockSpec((rows, SWA_W), lambda i: (i, 0))] + list(kv_specs),
        out_specs=pl.BlockSpec((rows, SWA_W), lambda i: (i, 0)),
        scratch_shapes=[pltpu.VMEM((SWA_KV, SWA_G * n_q, n_k), F32)],
        compiler_params=pltpu.CompilerParams(dimension_semantics=("arbitrary",),
                                             vmem_limit_bytes=VMEM_LIMIT),
        name=name,
    )(table, sinks, bucket, q, *kv_args)


def _t5_bucket(rel):
    half = N_BUCKETS // 2
    max_exact = half // 2
    n = jnp.abs(rel)
    large = max_exact + (jnp.log(jnp.maximum(n, 1).astype(F32) / max_exact)
                         / math.log(MAX_DIST / max_exact) * (half - max_exact)).astype(jnp.int32)
    large = jnp.minimum(large, half - 1)
    return jnp.where(rel > 0, half, 0) + jnp.where(n < max_exact, n, large)


def _bucket_map(n_q, n_k, key_offset):
    rel = (jnp.arange(n_k)[None, :] - key_offset) - jnp.arange(n_q)[:, None]
    return _t5_bucket(rel).astype(jnp.int32)


FF_BLOCK = 1024


def _post_kernel(xp_ref, dnp_ref, swp_ref, xs_ref, dns_ref, sws_ref, gin_ref, bin_ref,
                 wod_ref, wos_ref, g1_ref, b1_ref, w1_ref, w2_ref, g2_ref, b2_ref,
                 yp_ref, ys_ref, *, n_prompt_tiles):
    is_prompt = pl.program_id(0) < n_prompt_tiles

    @pl.when(is_prompt)
    def _():
        _post_tile(xp_ref, dnp_ref, swp_ref, gin_ref, bin_ref, wod_ref, wos_ref, g1_ref, b1_ref,
                   w1_ref, w2_ref, g2_ref, b2_ref, yp_ref)

    @pl.when(jnp.logical_not(is_prompt))
    def _():
        _post_tile(xs_ref, dns_ref, sws_ref, gin_ref, bin_ref, wod_ref, wos_ref, g1_ref, b1_ref,
                   w1_ref, w2_ref, g2_ref, b2_ref, ys_ref)


def _post_tile(x_ref, dn_ref, sw_ref, gin_ref, bin_ref, wod_ref, wos_ref, g1_ref, b1_ref,
               w1_ref, w2_ref, g2_ref, b2_ref, y_ref):
    h = _layer_norm(x_ref[...], gin_ref[...], bin_ref[...])
    mix = (jnp.dot(dn_ref[...], wod_ref[...], preferred_element_type=F32)
           + jnp.dot(sw_ref[...], wos_ref[...], preferred_element_type=F32))
    h1 = _layer_norm(DEEP_ALPHA * h + mix, g1_ref[...], b1_ref[...])
    h1b = h1.astype(BF16)
    f = jnp.zeros_like(h1)
    for j in range(D_FF // FF_BLOCK):
        a = jnp.dot(h1b, w1_ref[:, j * FF_BLOCK:(j + 1) * FF_BLOCK], preferred_element_type=F32)
        a = jnp.square(jnp.maximum(a, 0.0)).astype(BF16)
        f = f + jnp.dot(a, w2_ref[j * FF_BLOCK:(j + 1) * FF_BLOCK, :], preferred_element_type=F32)
    y_ref[...] = _layer_norm(DEEP_ALPHA * h1 + f, g2_ref[...], b2_ref[...])


def _post(prompt, sample, vecs, weights, tm):
    gin, bin_, g1, b1, g2, b2 = vecs
    wo_dn, wo_sw, w1, w2 = weights
    n_p, n_s = prompt[0].shape[0] // tm, sample[0].shape[0] // tm
    const = lambda i: (0, 0)
    vec = pl.BlockSpec((1, D_MODEL), const)
    weight = lambda wt: pl.BlockSpec(wt.shape, const, pipeline_mode=pl.Buffered(1))
    blk_p = lambda width: pl.BlockSpec((tm, width), lambda i: (jnp.minimum(i, n_p - 1), 0))
    blk_s = lambda width: pl.BlockSpec((tm, width), lambda i: (jnp.maximum(i - n_p, 0), 0))
    widths = (D_MODEL, DN_W, SWA_W)
    return pl.pallas_call(
        functools.partial(_post_kernel, n_prompt_tiles=n_p),
        out_shape=[jax.ShapeDtypeStruct((n_p * tm, D_MODEL), F32),
                   jax.ShapeDtypeStruct((n_s * tm, D_MODEL), F32)],
        grid=(n_p + n_s,),
        in_specs=[blk_p(wd) for wd in widths] + [blk_s(wd) for wd in widths]
                 + [vec, vec, weight(wo_dn), weight(wo_sw), vec, vec, weight(w1), weight(w2),
                    vec, vec],
        out_specs=[blk_p(D_MODEL), blk_s(D_MODEL)],
        compiler_params=pltpu.CompilerParams(dimension_semantics=("arbitrary",),
                                             vmem_limit_bytes=VMEM_LIMIT),
        name="post_mlp",
    )(*prompt, *sample, gin, bin_, wo_dn, wo_sw, g1, b1, w1, w2, g2, b2)


def kernel(x_prompt, x_sample, state_delta, state_conv, cache_swa_k, cache_swa_v, meta_tokens, ln_in_g, ln_in_b, w_in, w_conv, dn_a_log, dn_dt_bias, dn_norm, swa_sinks, rel_bias_table, w_o, ln1_g, ln1_b, w_ff1, w_ff2, ln2_g, ln2_b):
    assert w_in.shape[0] == DEPTH == 1
    n_seq_s, len_s = x_sample.shape[0], x_sample.shape[1]
    len_p = x_prompt.shape[1]
    cache_len = cache_swa_k.shape[2]
    chunk_p = 64
    row = lambda t: t.reshape(1, -1).astype(F32)

    w = w_in[0]
    w_sq = (w[:, OFF_SQ:OFF_SK].astype(BF16).reshape(D_MODEL, SWA_KV, SWA_G, HEAD_DIM)
            .transpose(0, 2, 1, 3).reshape(D_MODEL, SWA_W))
    w_r = (w[:, :OFF_A].astype(BF16), w_sq, w[:, OFF_SK:].astype(BF16),
           jnp.pad(w[:, OFF_A:OFF_SQ].astype(BF16), ((0, 0), (0, LANES - 2 * HEADS))))
    gin, bin_ = row(ln_in_g), row(ln_in_b)
    pad_lanes = lambda t: jnp.pad(t.reshape(1, -1).astype(F32), ((0, 0), (0, LANES - HEADS)))
    lane_head = jnp.arange(DN_W) // HEAD_DIM
    seg = (lane_head[:, None] == lane_head[None, :]).astype(BF16)
    expand = (jnp.arange(LANES)[:, None] == lane_head[None, :]).astype(BF16)
    wconv = jnp.pad(w_conv[0].astype(F32), ((0, SUBLANES - CONV_W), (0, 0)))
    delta_consts = (pad_lanes(dn_a_log[0]), pad_lanes(dn_dt_bias[0]),
                    jnp.tile(dn_norm[0].astype(F32), HEADS).reshape(1, DN_W), seg, expand)
    post_vecs = (gin, bin_, row(ln1_g[0]), row(ln1_b[0]), row(ln2_g[0]), row(ln2_b[0]))
    wo_sw = (w_o[0][DN_W:].astype(BF16).reshape(SWA_KV, SWA_G, HEAD_DIM, D_MODEL)
             .transpose(1, 0, 2, 3).reshape(SWA_W, D_MODEL))
    post_w = (w_o[0][:DN_W].astype(BF16), wo_sw, w_ff1[0].astype(BF16), w_ff2[0].astype(BF16))
    table = rel_bias_table.astype(F32).reshape(-1)
    sinks = swa_sinks[0].astype(F32)

    xp = x_prompt[0]
    xs = x_sample.reshape(n_seq_s * len_s, D_MODEL)

    conv_pad = SUBLANES - (CONV_W - 1)
    zero_conv = jnp.zeros((1, SUBLANES, CONV_CH), F32)
    conv_s = jnp.pad(state_conv[0].astype(F32), ((0, 0), (conv_pad, 0), (0, 0)))
    m_qkv, m_z, m_sq, m_sk, m_sv, m_ab, m_tail = _proj(
        meta_tokens.astype(F32), gin, bin_, w_r, zero_conv, wconv, N_META, N_META)
    p_qkv, p_z, p_sq, p_sk, p_sv, p_ab, p_tail = _proj(xp, gin, bin_, w_r, m_tail, wconv, 512, None)
    s_qkv, s_z, s_sq, s_sk, s_sv, s_ab, s_tail = _proj(xs, gin, bin_, w_r, conv_s, wconv, 512, len_s)

    zero_s = jnp.zeros((1, HEADS, HEAD_DIM, HEAD_DIM), F32)
    _, s_meta = _delta(m_qkv, m_z, m_ab, zero_s, delta_consts, 1, N_META, N_META, 1, 1)
    p_dn, p_s = _delta(p_qkv, p_z, p_ab, s_meta, delta_consts, 1, len_p, chunk_p, 4, 2)
    s_dn, s_s = _delta(s_qkv, s_z, s_ab, state_delta[0].astype(F32), delta_consts,
                       n_seq_s, len_s, len_s, 8, 2)

    span = WINDOW
    lead = jnp.zeros((span - N_META, KV_W), F32)
    units_p = 8
    rows_p = units_p * chunk_p
    prev_blocks = rows_p // span
    kv_specs_p = [pl.BlockSpec((span, KV_W), lambda i: (0, 0)),
                  pl.BlockSpec((span, KV_W), lambda i: (jnp.maximum(i * prev_blocks - 1, 0), 0)),
                  pl.BlockSpec((rows_p, KV_W), lambda i: (i, 0))] * 2
    p_sw = _swa(table, sinks, _bucket_map(chunk_p, span + chunk_p, span), p_sq,
                [jnp.concatenate([lead, m_sk], axis=0), p_sk, p_sk,
                 jnp.concatenate([lead, m_sv], axis=0), p_sv, p_sv],
                kv_specs_p, len_p // rows_p, chunk_p, units_p, True, "swa_prompt")
    ck = cache_swa_k[0].astype(F32).reshape(n_seq_s * cache_len, KV_W)
    cv = cache_swa_v[0].astype(F32).reshape(n_seq_s * cache_len, KV_W)
    units_s = 8
    kv_specs_s = [pl.BlockSpec((units_s * cache_len, KV_W), lambda i: (i, 0)),
                  pl.BlockSpec((units_s * len_s, KV_W), lambda i: (i, 0))] * 2
    s_sw = _swa(table, sinks, _bucket_map(len_s, cache_len + len_s, cache_len), s_sq,
                [ck, s_sk, cv, s_sv], kv_specs_s, n_seq_s // units_s, len_s, units_s, False,
                "swa_sample")

    y_p, y_s = _post((xp, p_dn, p_sw), (xs, s_dn, s_sw), post_vecs, post_w, 512)

    kv_shape = lambda n, length: (1, n, length, SWA_KV, HEAD_DIM)
    return (y_p[None], y_s.reshape(x_sample.shape),
            p_s[None], p_tail[:, -(CONV_W - 1):][None],
            p_sk[-WINDOW:].reshape(kv_shape(1, WINDOW)), p_sv[-WINDOW:].reshape(kv_shape(1, WINDOW)),
            s_s[None], s_tail[:, -(CONV_W - 1):][None],
            s_sk.reshape(kv_shape(n_seq_s, len_s)), s_sv.reshape(kv_shape(n_seq_s, len_s)))
```

```python
import functools
import math

import jax
import jax.numpy as jnp
from jax import lax
from jax.experimental import pallas as pl
from jax.experimental.pallas import tpu as pltpu

F32 = jnp.float32
BF16 = jnp.bfloat16

D_MODEL = 1024
N_META = 16
HEADS = 8
HEAD_DIM = 64
DN_W = HEADS * HEAD_DIM
CONV_W = 4
CONV_CH = 3 * DN_W
SWA_HEADS = 8
SWA_KV = 2
SWA_G = SWA_HEADS // SWA_KV
KV_W = SWA_KV * HEAD_DIM
WINDOW = 128
N_BUCKETS = 32
MAX_DIST = 128
D_FF = 4 * D_MODEL
DEPTH = 1
DEEP_ALPHA = (2 * DEPTH) ** 0.25
LN_EPS = 1e-5
NORM_EPS = 1e-6
LOG2E = math.log2(math.e)
LANES = 128
SUBLANES = 8
VMEM_LIMIT = 56 * 1024 * 1024

OFF_Z = CONV_CH
OFF_A = OFF_Z + DN_W
OFF_B = OFF_A + HEADS
OFF_SQ = OFF_B + HEADS
OFF_SK = OFF_SQ + SWA_HEADS * HEAD_DIM
OFF_SV = OFF_SK + KV_W
PROJ_WIDTH = OFF_SV + KV_W
PROJ_SPLITS = (CONV_CH, DN_W, SWA_HEADS * HEAD_DIM, KV_W, KV_W, LANES)
PROJ_PAD_W = sum(PROJ_SPLITS)


def _layer_norm(x, g, b):
    mu = jnp.mean(x, axis=-1, keepdims=True)
    xc = x - mu
    var = jnp.mean(xc * xc, axis=-1, keepdims=True)
    return xc * lax.rsqrt(var + LN_EPS) * g + b


def _mm(a, b):
    return jnp.dot(a.astype(BF16), b.astype(BF16), preferred_element_type=F32)


def _mm_nt(a, b):
    return lax.dot_general(a.astype(BF16), b.astype(BF16), (((1,), (1,)), ((), ())),
                           preferred_element_type=F32)


def _split3(x):
    x1 = x.astype(BF16)
    r1 = x - x1.astype(F32)
    x2 = r1.astype(BF16)
    x3 = (r1 - x2.astype(F32)).astype(BF16)
    return x1, x2, x3


def _dot_exact_lhs(x, sel):
    m = x.shape[0]
    stacked = jnp.concatenate(_split3(x), axis=0)
    r = jnp.dot(stacked, sel, preferred_element_type=F32)
    return r[:m] + r[m:2 * m] + r[2 * m:]


CONV_BLOCK = 256


def _sigmoid(x):
    return 1.0 / (1.0 + jnp.exp(-x))


def _proj_kernel(x_ref, g_ref, b_ref, w_ref, wsq_ref, wkv_ref, wab_ref, conv0_ref, wconv_ref,
                 y_ref, z_ref, sq_ref, sk_ref, sv_ref, ab_ref, tail_ref, xbuf, *, seq_len):
    tm = x_ref.shape[0]
    h = _layer_norm(x_ref[...], g_ref[...], b_ref[...]).astype(BF16)

    chained = seq_len is None
    n_seq = 1 if chained else tm // seq_len
    L = tm if chained else seq_len
    if chained:
        @pl.when(pl.program_id(0) == 0)
        def _():
            xbuf[0, 0:SUBLANES, :] = conv0_ref[0]
    else:
        for s in range(n_seq):
            xbuf[s, 0:SUBLANES, :] = conv0_ref[s]
    blocks = [slice(cb, cb + CONV_BLOCK) for cb in range(0, CONV_CH, CONV_BLOCK)]

    def project(cols):
        raw = jnp.dot(h, w_ref[:, cols], preferred_element_type=F32)
        for s in range(n_seq):
            xbuf[s, SUBLANES:SUBLANES + L, cols] = raw[s * L:(s + 1) * L, :]

    def conv(cols):
        for s in range(n_seq):
            ext = xbuf[s, :, cols]
            w0, w1, w2, w3 = (0.5 * wconv_ref[j:j + 1, cols] for j in range(CONV_W))
            ext1 = pltpu.roll(ext, 1, axis=0)
            half = (w3 * ext + w2 * ext1)[SUBLANES:, :]
            half = half + pltpu.roll(w1 * ext + w0 * ext1, 2, axis=0)[SUBLANES:, :]
            y_ref[s * L:(s + 1) * L, cols] = half * (1.0 + jnp.tanh(half))
            tail_ref[s, :, cols] = xbuf[s, L:L + SUBLANES, cols]
            if chained:
                xbuf[s, 0:SUBLANES, cols] = xbuf[s, L:L + SUBLANES, cols]

    for o_ref, rhs in ((z_ref, w_ref[:, CONV_CH:]), (sq_ref, wsq_ref[...]),
                       (sk_ref, wkv_ref[:, :KV_W]), (sv_ref, wkv_ref[:, KV_W:]),
                       (ab_ref, wab_ref[...])):
        o_ref[...] = jnp.dot(h, rhs, preferred_element_type=F32)
    project(blocks[0])
    for bi, cols in enumerate(blocks):
        if bi + 1 < len(blocks):
            project(blocks[bi + 1])
        conv(cols)


def _proj(x, g, b, weights, conv0, wconv, tm, seq_len):
    rows = x.shape[0]
    w_main, w_sq, w_kv, w_ab = weights
    chained = seq_len is None
    n_tile_seq = 1 if chained else tm // seq_len
    n_seq = 1 if chained else rows // seq_len
    const = lambda i: (0, 0)
    row_blk = lambda width: pl.BlockSpec((tm, width), lambda i: (i, 0))
    seq_blk = pl.BlockSpec((n_tile_seq, SUBLANES, CONV_CH),
                           (lambda i: (0, 0, 0)) if chained else (lambda i: (i, 0, 0)))
    return pl.pallas_call(
        functools.partial(_proj_kernel, seq_len=seq_len),
        out_shape=[jax.ShapeDtypeStruct((rows, width), F32) for width in PROJ_SPLITS]
                  + [jax.ShapeDtypeStruct((n_seq, SUBLANES, CONV_CH), F32)],
        grid=(rows // tm,),
        in_specs=[row_blk(D_MODEL), pl.BlockSpec((1, D_MODEL), const),
                  pl.BlockSpec((1, D_MODEL), const)]
                 + [pl.BlockSpec(wt.shape, const) for wt in weights]
                 + [seq_blk, pl.BlockSpec((SUBLANES, CONV_CH), const)],
        out_specs=[row_blk(width) for width in PROJ_SPLITS] + [seq_blk],
        scratch_shapes=[pltpu.VMEM((n_tile_seq, SUBLANES + (tm if chained else seq_len), CONV_CH),
                                   F32)],
        compiler_params=pltpu.CompilerParams(dimension_semantics=("arbitrary",),
                                             vmem_limit_bytes=VMEM_LIMIT),
        name="ln_in_proj",
    )(x, g, b, w_main, w_sq, w_kv, w_ab, conv0, wconv)


GROUP_HEADS = 4


def _softplus(x):
    return jnp.maximum(x, 0.0) + jnp.log1p(jnp.exp(-jnp.abs(x)))


def _delta_kernel(qkv_ref, z_ref, ab_ref, s0_ref, alog_ref, dtb_ref,
                  normw_ref, seg_ref, expand_ref, o_ref, s_ref, sbd,
                  *, chunk, units, groups, sequential):
    C, U = chunk, units
    RG = U * C
    PAIRS = HEADS // GROUP_HEADS
    PW = GROUP_HEADS * HEAD_DIM
    step = pl.program_id(0)
    last_step = pl.num_programs(0) - 1
    expand = expand_ref[...]

    def seg_sum(t):
        tb = t.astype(BF16)
        return jnp.concatenate(
            [jnp.dot(tb[:, g * PW:(g + 1) * PW], seg_ref[g * PW:(g + 1) * PW, g * PW:(g + 1) * PW],
                     preferred_element_type=F32) for g in range(PAIRS)], axis=1)

    ri = lax.broadcasted_iota(jnp.int32, (C, PW), 0)
    ci = lax.broadcasted_iota(jnp.int32, (C, PW), 1) % HEAD_DIM
    causal = (ri >= ci) & (ci < C)
    strict = (ri > ci) & (ci < C)
    eye = (ri == ci).astype(F32)
    lane_head = lax.broadcasted_iota(jnp.int32, (1, PW), 1) // HEAD_DIM
    rs = lax.broadcasted_iota(jnp.int32, (LANES, LANES), 0)
    cs = lax.broadcasted_iota(jnp.int32, (LANES, LANES), 1)
    half_mask = ((rs // HEAD_DIM) == (cs // HEAD_DIM)).astype(F32)
    row_pad = [jnp.zeros((HEAD_DIM - C, PW), BF16)] if C < HEAD_DIM else []
    lane_pad = [jnp.zeros((1, HEAD_DIM - C), F32)] if C < HEAD_DIM else []
    rr = lax.broadcasted_iota(jnp.int32, (RG, RG), 0)
    cc = lax.broadcasted_iota(jnp.int32, (RG, RG), 1)
    tril = ((rr // C == cc // C) & (rr >= cc)).astype(BF16)

    def stack(y):
        yb = y.astype(BF16)
        zero = jnp.zeros_like(yb)
        pad = row_pad if y.shape[0] < HEAD_DIM else []
        blocks = []
        for h in range(GROUP_HEADS):
            blocks += [jnp.where(lane_head == h, yb, zero)] + pad
        return jnp.concatenate(blocks, axis=0)

    items = [(u, p) for u in range(U) for p in range(PAIRS)]
    sl = lambda u, p: (slice(u * C, (u + 1) * C), slice(p * PW, (p + 1) * PW))

    def front(gi):
        rg = slice(gi * RG, (gi + 1) * RG)
        q, k, v = (qkv_ref[rg, j * DN_W:(j + 1) * DN_W] for j in range(3))
        qn = q * (lax.rsqrt(seg_sum(q * q) + NORM_EPS) * HEAD_DIM ** -0.5)
        kn = k * lax.rsqrt(seg_sum(k * k) + NORM_EPS)
        yield

        ab = ab_ref[rg, :]
        g = -jnp.exp(alog_ref[...]) * _softplus(ab + dtb_ref[...])
        gs = jnp.dot(tril, jnp.concatenate(_split3(g), axis=1), preferred_element_type=F32)
        G = gs[:, :LANES] + gs[:, LANES:2 * LANES] + gs[:, 2 * LANES:]
        GT = G.T
        G_x = _dot_exact_lhs(G, expand)
        beta = _sigmoid(pltpu.roll(ab, LANES - HEADS, axis=1))
        b_hi = beta.astype(BF16)
        b_lo = (beta - b_hi.astype(F32)).astype(BF16)
        bx = jnp.dot(jnp.concatenate([b_hi, b_lo], axis=0), expand, preferred_element_type=F32)
        beta_x = bx[:RG] + bx[RG:]
        glast_x = jnp.concatenate(
            [jnp.broadcast_to(G_x[(u + 1) * C - 1:(u + 1) * C, :], (C, DN_W)) for u in range(U)],
            axis=0)
        yield
        eg_x = jnp.exp(G_x)
        kbeta = kn * beta_x
        return dict(qn=qn, kn=kn, kbeta=kbeta, bk=kbeta * eg_x, bv=v * beta_x, qd=qn * eg_x,
                    kd=kn * jnp.exp(glast_x - G_x), G_x=G_x, GT=GT, eg_x=eg_x)

    def middle(f):
        A, QK = {}, {}
        for (u, p) in items:
            ru, lp = sl(u, p)
            gr = []
            for h in range(p * GROUP_HEADS, (p + 1) * GROUP_HEADS):
                gr += [f["GT"][h:h + 1, ru]] + lane_pad
            gr = jnp.concatenate(gr, axis=1)
            decay = jnp.exp(jnp.where(causal, f["G_x"][ru, lp] - gr, -jnp.inf))
            r = _mm_nt(jnp.concatenate([f["kbeta"][ru, lp], f["qn"][ru, lp]], axis=0),
                       stack(f["kn"][ru, lp]))
            A[u, p] = jnp.where(strict, r[:C] * decay, 0.0)
            QK[u, p] = r[C:] * decay
        yield

        n_pow = int(math.log2(C)) - 1
        P = {it: eye - A[it] for it in items}
        Ap = {it: _mm(A[it], stack(A[it])) for it in items}
        yield
        for i in range(n_pow):
            if i < n_pow - 1:
                r = {it: _mm(jnp.concatenate([Ap[it], P[it]], axis=0), stack(Ap[it])) for it in items}
                Ap = {it: r[it][:C] for it in items}
                P = {it: P[it] + r[it][C:] for it in items}
            else:
                P = {it: P[it] + _mm(P[it], stack(Ap[it])) for it in items}
            yield

        halves = [slice(hb * LANES, (hb + 1) * LANES) for hb in range(PW // LANES)]
        wu = {it: _mm(P[it], jnp.concatenate([stack(f["bk"][sl(*it)]), stack(f["bv"][sl(*it)])],
                                             axis=1)) for it in items}
        kdT = {(it, hb): f["kd"][sl(*it)][:, cols].T for it in items
               for hb, cols in enumerate(halves)}
        yield
        mn = {(it, hb): _mm(kdT[it, hb], jnp.concatenate(
            [wu[it][:, cols], wu[it][:, PW:][:, cols]], axis=1))
              for it in items for hb, cols in enumerate(halves)}
        yield
        n_t = {key: (half_mask * mn[key][:, LANES:]).T for key in mn}
        out = {}
        for (u, p) in items:
            ru, lp = sl(u, p)
            it = (u, p)
            out[it] = dict(
                W=wu[it][:, :PW], U=wu[it][:, PW:],
                M=diag_blocks([half_mask * mn[it, hb][:, :LANES]
                               for hb in range(len(halves))]).astype(BF16),
                NT=jnp.concatenate([n_t[it, hb][:HEAD_DIM] + n_t[it, hb][HEAD_DIM:]
                                    for hb in range(len(halves))], axis=1),
                QK=QK[it], qd=f["qd"][ru, lp],
                eg_last=f["eg_x"][(u + 1) * C - 1:(u + 1) * C, lp])
        return out

    def diag_blocks(blocks):
        n = len(blocks)
        rows = []
        for i, blk in enumerate(blocks):
            rows.append(jnp.concatenate(
                [blk if j == i else jnp.zeros_like(blk) for j in range(n)], axis=1))
        return jnp.concatenate(rows, axis=0)

    def load_state(ref, prefix, p):
        return jnp.concatenate([ref[prefix + (p * GROUP_HEADS + h,)].T
                                for h in range(GROUP_HEADS)], axis=1)

    def store_state(ref, prefix, p, st):
        for h in range(GROUP_HEADS):
            ref[prefix + (p * GROUP_HEADS + h,)] = st[:, h * HEAD_DIM:(h + 1) * HEAD_DIM].T

    chunks = [(gi, u) for gi in range(groups) for u in range(U)]
    if sequential:
        @pl.when(step == 0)
        def _():
            for p in range(PAIRS):
                sbd[p] = load_state(s0_ref, (), p)
        s_start = {p: sbd[p] for p in range(PAIRS)}
    else:
        s_start = {(idx, p): load_state(s0_ref, (idx,), p)
                   for idx in range(len(chunks)) for p in range(PAIRS)}

    def chain(gi, out, s_in):
        if sequential:
            states = {p: [s_in[p]] for p in range(PAIRS)}
            for u in range(U):
                for p in range(PAIRS):
                    d, s_cur = out[u, p], states[p][-1]
                    states[p].append(s_cur * d["eg_last"] - _mm_nt(s_cur, d["M"]) + d["NT"])
            return ({(u, p): states[p][u] for (u, p) in items},
                    {p: states[p][-1] for p in range(PAIRS)})
        entering = {(u, p): s_in[gi * U + u, p] for (u, p) in items}
        s_out = {}
        for (u, p) in items:
            d, s_cur = out[u, p], entering[u, p]
            s_out[gi * U + u, p] = s_cur * d["eg_last"] - _mm_nt(s_cur, d["M"]) + d["NT"]
        return entering, s_out

    def finish(gi, out, entering):
        r = {it: _mm_nt(jnp.concatenate([out[it]["W"], out[it]["qd"]], axis=0),
                        stack(entering[it])) for it in items}
        yield
        tiles = {it: r[it][C:] + _mm(out[it]["QK"], stack(out[it]["U"] - r[it][:C]))
                 for it in items}
        yield
        rg = slice(gi * RG, (gi + 1) * RG)
        o = jnp.concatenate([jnp.concatenate([tiles[u, p] for p in range(PAIRS)], axis=1)
                             for u in range(U)], axis=0)
        ms = seg_sum(o * o) * (1.0 / HEAD_DIM)
        z = z_ref[rg, :]
        o_ref[rg, :] = (o * lax.rsqrt(ms + NORM_EPS) * normw_ref[...]
                        * (z * _sigmoid(z))).astype(BF16)

    def drive(gens):
        results = [None] * len(gens)
        active = list(enumerate(gens))
        while active:
            still = []
            for i, gen in active:
                try:
                    next(gen)
                    still.append((i, gen))
                except StopIteration as stop:
                    results[i] = stop.value
            active = still
        return results

    state = s_start
    s_final = {}
    f_cur = drive([front(0)])[0]
    pending = None
    for gi in range(groups + 1):
        gens = []
        if gi < groups:
            gens.append(middle(f_cur))
        if gi + 1 < groups:
            gens.append(front(gi + 1))
        if pending is not None:
            entering, s_out = chain(gi - 1, pending, state)
            s_final.update(s_out)
            state = s_out if sequential else state
            gens.append(finish(gi - 1, pending, entering))
        res = drive(gens)
        pending = res[0] if gi < groups else None
        f_cur = res[1] if gi + 1 < groups else None

    if sequential:
        for p in range(PAIRS):
            sbd[p] = s_final[p]

        @pl.when(step == last_step)
        def _():
            for p in range(PAIRS):
                store_state(s_ref, (), p, s_final[p])
    else:
        for (idx, p), s_new in s_final.items():
            store_state(s_ref, (idx,), p, s_new)


def _delta(qkv, z, ab, s0, consts, n_seq, seq_len, chunk, units, groups):
    alog, dtb, normw, seg, expand = consts
    sequential = n_seq == 1
    rows = n_seq * seq_len
    n_chunks = units * groups
    blk_rows = n_chunks * chunk
    assert rows % blk_rows == 0 and (sequential or seq_len == chunk)
    const2 = lambda shape: pl.BlockSpec(shape, lambda i: (0, 0))
    tok = lambda width: pl.BlockSpec((blk_rows, width), lambda i: (i, 0))
    group_w = GROUP_HEADS * HEAD_DIM
    if sequential:
        state = pl.BlockSpec((None, HEADS, HEAD_DIM, HEAD_DIM), lambda i: (0, 0, 0, 0))
    else:
        state = pl.BlockSpec((n_chunks, HEADS, HEAD_DIM, HEAD_DIM), lambda i: (i, 0, 0, 0))
    return pl.pallas_call(
        functools.partial(_delta_kernel, chunk=chunk, units=units, groups=groups,
                          sequential=sequential),
        out_shape=[jax.ShapeDtypeStruct((rows, DN_W), BF16),
                   jax.ShapeDtypeStruct((n_seq, HEADS, HEAD_DIM, HEAD_DIM), F32)],
        grid=(rows // blk_rows,),
        in_specs=[tok(CONV_CH), tok(DN_W), tok(LANES), state,
                  const2((1, LANES)), const2((1, LANES)),
                  const2((1, DN_W)), const2((DN_W, DN_W)), const2((LANES, DN_W))],
        out_specs=[tok(DN_W), state],
        scratch_shapes=[pltpu.VMEM((HEADS // GROUP_HEADS, HEAD_DIM, group_w), F32)],
        compiler_params=pltpu.CompilerParams(
            dimension_semantics=("arbitrary",), vmem_limit_bytes=VMEM_LIMIT,
        ),
        name=f"delta_c{chunk}",
    )(qkv, z, ab, s0, alog, dtb, normw, seg, expand)


SWA_W = SWA_HEADS * HEAD_DIM


def _swa_kernel(*refs, n_q, units, prompt, batch=4):
    if prompt:
        (table_ref, sinks_ref, bucket_ref, q_ref, km_ref, kp_ref, kc_ref, vm_ref, vp_ref, vc_ref,
         o_ref, bias_ref) = refs
    else:
        (table_ref, sinks_ref, bucket_ref, q_ref, kp_ref, kc_ref, vp_ref, vc_ref,
         o_ref, bias_ref) = refs
    n_k = bias_ref.shape[-1]
    n_grp = SWA_W // KV_W
    step = pl.program_id(0)

    @pl.when(step == 0)
    def _():
        bucket = bucket_ref[...]
        for h in range(SWA_HEADS):
            acc = jnp.zeros((n_q, n_k), F32)
            for b in range(N_BUCKETS):
                acc = jnp.where(bucket == b, table_ref[b * SWA_HEADS + h], acc)
            kv, gi = divmod(h, SWA_G)
            bias_ref[kv, gi * n_q:(gi + 1) * n_q, :] = acc * LOG2E

    lane = lax.broadcasted_iota(jnp.int32, (1, KV_W), 1)
    kv_mask = [(lane < HEAD_DIM).astype(F32), (lane >= HEAD_DIM).astype(F32)]
    if prompt:
        first = step == 0
        k_all = jnp.concatenate([jnp.where(first, km_ref[...], kp_ref[...]), kc_ref[...]], axis=0)
        v_all = jnp.concatenate([jnp.where(first, vm_ref[...], vp_ref[...]), vc_ref[...]], axis=0)
        k_kv = [(k_all * m).astype(BF16) for m in kv_mask]
        v_kv = [(v_all * m).astype(BF16) for m in kv_mask]
        keys = lambda j, kv: k_kv[kv][j * n_q:j * n_q + n_k]
        vals = lambda j, kv: v_kv[kv][j * n_q:j * n_q + n_k]
    else:
        n_c = n_k - n_q
        cat = lambda a, b, j: jnp.concatenate([a[j * n_c:(j + 1) * n_c, :],
                                               b[j * n_q:(j + 1) * n_q, :]], axis=0)
        keys = lambda j, kv: (cat(kp_ref, kc_ref, j) * kv_mask[kv]).astype(BF16)
        vals = lambda j, kv: (cat(vp_ref, vc_ref, j) * kv_mask[kv]).astype(BF16)

    q = (q_ref[...] * (HEAD_DIM ** -0.5 * LOG2E)).astype(BF16)
    ones_k = jnp.ones((n_k, KV_W), BF16)
    sink = [jnp.concatenate([jnp.full((n_q, KV_W), sinks_ref[kv * SWA_G + gi] * LOG2E, F32)
                             for gi in range(SWA_G)], axis=0) for kv in range(SWA_KV)]
    rows4 = SWA_G * n_q
    for j0 in range(0, units, batch):
        js = range(j0, min(j0 + batch, units))
        items = [(j, kv) for j in js for kv in range(SWA_KV)]
        s = {}
        for j in js:
            rows = slice(j * n_q, (j + 1) * n_q)
            q4 = jnp.concatenate([q[rows, g * KV_W:(g + 1) * KV_W] for g in range(n_grp)], axis=0)
            for kv in range(SWA_KV):
                sj = lax.dot_general(q4, keys(j, kv), (((1,), (1,)), ((), ())),
                                     preferred_element_type=F32) + bias_ref[kv]
                if prompt and j * n_q < WINDOW - N_META:
                    key_pos = j * n_q + lax.broadcasted_iota(jnp.int32, (1, n_k), 1)
                    n_invalid = jnp.where(first, WINDOW - N_META, 0)
                    sj = jnp.where(key_pos < n_invalid, -jnp.inf, sj)
                s[j, kv] = sj
        m = {it: jnp.maximum(jnp.broadcast_to(jnp.max(s[it], axis=-1, keepdims=True),
                                              (rows4, KV_W)), sink[it[1]]) for it in items}
        p = {it: jnp.exp2(s[it] - jnp.concatenate([m[it], m[it][:, :n_k - KV_W]], axis=1))
             for it in items}
        pv = {it: jnp.dot(p[it].astype(BF16), jnp.concatenate([vals(*it), ones_k], axis=1),
                          preferred_element_type=F32) for it in items}
        o = {it: pv[it][:, :KV_W] * (1.0 / (pv[it][:, KV_W:] + jnp.exp2(sink[it[1]] - m[it])))
             for it in items}
        for j in js:
            o2 = o[j, 0] + o[j, 1]
            for g in range(n_grp):
                o_ref[j * n_q:(j + 1) * n_q, g * KV_W:(g + 1) * KV_W] = (
                    o2[g * n_q:(g + 1) * n_q, :].astype(BF16))


def _swa(table, sinks, bucket, q, kv_args, kv_specs, n_steps, n_q, units, prompt, name):
    n_k = bucket.shape[1]
    smem = pl.BlockSpec(memory_space=pltpu.SMEM)
    rows = units * n_q
    return pl.pallas_call(
        functools.partial(_swa_kernel, n_q=n_q, units=units, prompt=prompt),
        out_shape=jax.ShapeDtypeStruct((n_steps * rows, SWA_W), BF16),
        grid=(n_steps,),
        in_specs=[smem, smem, pl.BlockSpec((n_q, n_k), lambda i: (0, 0)),
                  pl.BlockSpec((rows, SWA_W), lambda i: (i, 0))] + list(kv_specs),
        out_specs=pl.BlockSpec((rows, SWA_W), lambda i: (i, 0)),
        scratch_shapes=[pltpu.VMEM((SWA_KV, SWA_G * n_q, n_k), F32)],
        compiler_params=pltpu.CompilerParams(dimension_semantics=("arbitrary",),
                                             vmem_limit_bytes=VMEM_LIMIT),
        name=name,
    )(table, sinks, bucket, q, *kv_args)


def _t5_bucket(rel):
    half = N_BUCKETS // 2
    max_exact = half // 2
    n = jnp.abs(rel)
    large = max_exact + (jnp.log(jnp.maximum(n, 1).astype(F32) / max_exact)
                         / math.log(MAX_DIST / max_exact) * (half - max_exact)).astype(jnp.int32)
    large = jnp.minimum(large, half - 1)
    return jnp.where(rel > 0, half, 0) + jnp.where(n < max_exact, n, large)


def _bucket_map(n_q, n_k, key_offset):
    rel = (jnp.arange(n_k)[None, :] - key_offset) - jnp.arange(n_q)[:, None]
    return _t5_bucket(rel).astype(jnp.int32)


FF_BLOCK = 1024


def _post_kernel(xp_ref, dnp_ref, swp_ref, xs_ref, dns_ref, sws_ref, gin_ref, bin_ref,
                 wod_ref, wos_ref, g1_ref, b1_ref, w1_ref, w2_ref, g2_ref, b2_ref,
                 yp_ref, ys_ref, *, n_prompt_tiles):
    is_prompt = pl.program_id(0) < n_prompt_tiles

    @pl.when(is_prompt)
    def _():
        _post_tile(xp_ref, dnp_ref, swp_ref, gin_ref, bin_ref, wod_ref, wos_ref, g1_ref, b1_ref,
                   w1_ref, w2_ref, g2_ref, b2_ref, yp_ref)

    @pl.when(jnp.logical_not(is_prompt))
    def _():
        _post_tile(xs_ref, dns_ref, sws_ref, gin_ref, bin_ref, wod_ref, wos_ref, g1_ref, b1_ref,
                   w1_ref, w2_ref, g2_ref, b2_ref, ys_ref)


def _post_tile(x_ref, dn_ref, sw_ref, gin_ref, bin_ref, wod_ref, wos_ref, g1_ref, b1_ref,
               w1_ref, w2_ref, g2_ref, b2_ref, y_ref):
    h = _layer_norm(x_ref[...], gin_ref[...], bin_ref[...])
    mix = (jnp.dot(dn_ref[...], wod_ref[...], preferred_element_type=F32)
           + jnp.dot(sw_ref[...], wos_ref[...], preferred_element_type=F32))
    h1 = _layer_norm(DEEP_ALPHA * h + mix, g1_ref[...], b1_ref[...])
    h1b = h1.astype(BF16)
    f = jnp.zeros_like(h1)
    for j in range(D_FF // FF_BLOCK):
        a = jnp.dot(h1b, w1_ref[:, j * FF_BLOCK:(j + 1) * FF_BLOCK], preferred_element_type=F32)
        a = jnp.square(jnp.maximum(a, 0.0)).astype(BF16)
        f = f + jnp.dot(a, w2_ref[j * FF_BLOCK:(j + 1) * FF_BLOCK, :], preferred_element_type=F32)
    y_ref[...] = _layer_norm(DEEP_ALPHA * h1 + f, g2_ref[...], b2_ref[...])


def _post(prompt, sample, vecs, weights, tm):
    gin, bin_, g1, b1, g2, b2 = vecs
    wo_dn, wo_sw, w1, w2 = weights
    n_p, n_s = prompt[0].shape[0] // tm, sample[0].shape[0] // tm
    const = lambda i: (0, 0)
    vec = pl.BlockSpec((1, D_MODEL), const)
    weight = lambda wt: pl.BlockSpec(wt.shape, const, pipeline_mode=pl.Buffered(1))
    blk_p = lambda width: pl.BlockSpec((tm, width), lambda i: (jnp.minimum(i, n_p - 1), 0))
    blk_s = lambda width: pl.BlockSpec((tm, width), lambda i: (jnp.maximum(i - n_p, 0), 0))
    widths = (D_MODEL, DN_W, SWA_W)
    return pl.pallas_call(
        functools.partial(_post_kernel, n_prompt_tiles=n_p),
        out_shape=[jax.ShapeDtypeStruct((n_p * tm, D_MODEL), F32),
                   jax.ShapeDtypeStruct((n_s * tm, D_MODEL), F32)],
        grid=(n_p + n_s,),
        in_specs=[blk_p(wd) for wd in widths] + [blk_s(wd) for wd in widths]
                 + [vec, vec, weight(wo_dn), weight(wo_sw), vec, vec, weight(w1), weight(w2),
                    vec, vec],
        out_specs=[blk_p(D_MODEL), blk_s(D_MODEL)],
        compiler_params=pltpu.CompilerParams(dimension_semantics=("arbitrary",),
                                             vmem_limit_bytes=VMEM_LIMIT),
        name="post_mlp",
    )(*prompt, *sample, gin, bin_, wo_dn, wo_sw, g1, b1, w1, w2, g2, b2)


def kernel(x_prompt, x_sample, state_delta, state_conv, cache_swa_k, cache_swa_v, meta_tokens, ln_in_g, ln_in_b, w_in, w_conv, dn_a_log, dn_dt_bias, dn_norm, swa_sinks, rel_bias_table, w_o, ln1_g, ln1_b, w_ff1, w_ff2, ln2_g, ln2_b):
    assert w_in.shape[0] == DEPTH == 1
    n_seq_s, len_s = x_sample.shape[0], x_sample.shape[1]
    len_p = x_prompt.shape[1]
    cache_len = cache_swa_k.shape[2]
    chunk_p = 64
    row = lambda t: t.reshape(1, -1).astype(F32)

    w = w_in[0]
    w_sq = (w[:, OFF_SQ:OFF_SK].astype(BF16).reshape(D_MODEL, SWA_KV, SWA_G, HEAD_DIM)
            .transpose(0, 2, 1, 3).reshape(D_MODEL, SWA_W))
    w_r = (w[:, :OFF_A].astype(BF16), w_sq, w[:, OFF_SK:].astype(BF16),
           jnp.pad(w[:, OFF_A:OFF_SQ].astype(BF16), ((0, 0), (0, LANES - 2 * HEADS))))
    gin, bin_ = row(ln_in_g), row(ln_in_b)
    pad_lanes = lambda t: jnp.pad(t.reshape(1, -1).astype(F32), ((0, 0), (0, LANES - HEADS)))
    lane_head = jnp.arange(DN_W) // HEAD_DIM
    seg = (lane_head[:, None] == lane_head[None, :]).astype(BF16)
    expand = (jnp.arange(LANES)[:, None] == lane_head[None, :]).astype(BF16)
    wconv = jnp.pad(w_conv[0].astype(F32), ((0, SUBLANES - CONV_W), (0, 0)))
    delta_consts = (pad_lanes(dn_a_log[0]), pad_lanes(dn_dt_bias[0]),
                    jnp.tile(dn_norm[0].astype(F32), HEADS).reshape(1, DN_W), seg, expand)
    post_vecs = (gin, bin_, row(ln1_g[0]), row(ln1_b[0]), row(ln2_g[0]), row(ln2_b[0]))
    wo_sw = (w_o[0][DN_W:].astype(BF16).reshape(SWA_KV, SWA_G, HEAD_DIM, D_MODEL)
             .transpose(1, 0, 2, 3).reshape(SWA_W, D_MODEL))
    post_w = (w_o[0][:DN_W].astype(BF16), wo_sw, w_ff1[0].astype(BF16), w_ff2[0].astype(BF16))
    table = rel_bias_table.astype(F32).reshape(-1)
    sinks = swa_sinks[0].astype(F32)

    xp = x_prompt[0]
    xs = x_sample.reshape(n_seq_s * len_s, D_MODEL)

    conv_pad = SUBLANES - (CONV_W - 1)
    zero_conv = jnp.zeros((1, SUBLANES, CONV_CH), F32)
    conv_s = jnp.pad(state_conv[0].astype(F32), ((0, 0), (conv_pad, 0), (0, 0)))
    m_qkv, m_z, m_sq, m_sk, m_sv, m_ab, m_tail = _proj(
        meta_tokens.astype(F32), gin, bin_, w_r, zero_conv, wconv, N_META, N_META)
    p_qkv, p_z, p_sq, p_sk, p_sv, p_ab, p_tail = _proj(xp, gin, bin_, w_r, m_tail, wconv, 512, None)
    s_qkv, s_z, s_sq, s_sk, s_sv, s_ab, s_tail = _proj(xs, gin, bin_, w_r, conv_s, wconv, 512, len_s)

    zero_s = jnp.zeros((1, HEADS, HEAD_DIM, HEAD_DIM), F32)
    _, s_meta = _delta(m_qkv, m_z, m_ab, zero_s, delta_consts, 1, N_META, N_META, 1, 1)
    p_dn, p_s = _delta(p_qkv, p_z, p_ab, s_meta, delta_consts, 1, len_p, chunk_p, 4, 2)
    s_dn, s_s = _delta(s_qkv, s_z, s_ab, state_delta[0].astype(F32), delta_consts,
                       n_seq_s, len_s, len_s, 8, 2)

    span = WINDOW
    lead = jnp.zeros((span - N_META, KV_W), F32)
    units_p = 8
    rows_p = units_p * chunk_p
    prev_blocks = rows_p // span
    kv_specs_p = [pl.BlockSpec((span, KV_W), lambda i: (0, 0)),
                  pl.BlockSpec((span, KV_W), lambda i: (jnp.maximum(i * prev_blocks - 1, 0), 0)),
                  pl.BlockSpec((rows_p, KV_W), lambda i: (i, 0))] * 2
    p_sw = _swa(table, sinks, _bucket_map(chunk_p, span + chunk_p, span), p_sq,
                [jnp.concatenate([lead, m_sk], axis=0), p_sk, p_sk,
                 jnp.concatenate([lead, m_sv], axis=0), p_sv, p_sv],
                kv_specs_p, len_p // rows_p, chunk_p, units_p, True, "swa_prompt")
    ck = cache_swa_k[0].astype(F32).reshape(n_seq_s * cache_len, KV_W)
    cv = cache_swa_v[0].astype(F32).reshape(n_seq_s * cache_len, KV_W)
    units_s = 8
    kv_specs_s = [pl.BlockSpec((units_s * cache_len, KV_W), lambda i: (i, 0)),
                  pl.BlockSpec((units_s * len_s, KV_W), lambda i: (i, 0))] * 2
    s_sw = _swa(table, sinks, _bucket_map(len_s, cache_len + len_s, cache_len), s_sq,
                [ck, s_sk, cv, s_sv], kv_specs_s, n_seq_s // units_s, len_s, units_s, False,
                "swa_sample")

    y_p, y_s = _post((xp, p_dn, p_sw), (xs, s_dn, s_sw), post_vecs, post_w, 512)

    kv_shape = lambda n, length: (1, n, length, SWA_KV, HEAD_DIM)
    return (y_p[None], y_s.reshape(x_sample.shape),
            p_s[None], p_tail[:, -(CONV_W - 1):][None],
            p_sk[-WINDOW:].reshape(kv_shape(1, WINDOW)), p_sv[-WINDOW:].reshape(kv_shape(1, WINDOW)),
            s_s[None], s_tail[:, -(CONV_W - 1):][None],
            s_sk.reshape(kv_shape(n_seq_s, len_s)), s_sv.reshape(kv_shape(n_seq_s, len_s)))
```

```python
import functools
import math

import jax
import jax.numpy as jnp
from jax import lax
from jax.experimental import pallas as pl
from jax.experimental.pallas import tpu as pltpu

F32 = jnp.float32
BF16 = jnp.bfloat16

D_MODEL = 1024
N_META = 16
HEADS = 8
HEAD_DIM = 64
DN_W = HEADS * HEAD_DIM
CONV_W = 4
CONV_CH = 3 * DN_W
SWA_HEADS = 8
SWA_KV = 2
SWA_G = SWA_HEADS // SWA_KV
KV_W = SWA_KV * HEAD_DIM
WINDOW = 128
N_BUCKETS = 32
MAX_DIST = 128
D_FF = 4 * D_MODEL
DEPTH = 1
DEEP_ALPHA = (2 * DEPTH) ** 0.25
LN_EPS = 1e-5
NORM_EPS = 1e-6
LOG2E = math.log2(math.e)
LANES = 128
SUBLANES = 8
VMEM_LIMIT = 56 * 1024 * 1024

OFF_Z = CONV_CH
OFF_A = OFF_Z + DN_W
OFF_B = OFF_A + HEADS
OFF_SQ = OFF_B + HEADS
OFF_SK = OFF_SQ + SWA_HEADS * HEAD_DIM
OFF_SV = OFF_SK + KV_W
PROJ_WIDTH = OFF_SV + KV_W
PROJ_SPLITS = (CONV_CH, DN_W, SWA_HEADS * HEAD_DIM, KV_W, KV_W, LANES)
PROJ_PAD_W = sum(PROJ_SPLITS)


def _layer_norm(x, g, b):
    mu = jnp.mean(x, axis=-1, keepdims=True)
    xc = x - mu
    var = jnp.mean(xc * xc, axis=-1, keepdims=True)
    return xc * lax.rsqrt(var + LN_EPS) * g + b


def _mm(a, b):
    return jnp.dot(a.astype(BF16), b.astype(BF16), preferred_element_type=F32)


def _mm_nt(a, b):
    return lax.dot_general(a.astype(BF16), b.astype(BF16), (((1,), (1,)), ((), ())),
                           preferred_element_type=F32)


def _split3(x):
    x1 = x.astype(BF16)
    r1 = x - x1.astype(F32)
    x2 = r1.astype(BF16)
    x3 = (r1 - x2.astype(F32)).astype(BF16)
    return x1, x2, x3


def _dot_exact_lhs(x, sel):
    m = x.shape[0]
    stacked = jnp.concatenate(_split3(x), axis=0)
    r = jnp.dot(stacked, sel, preferred_element_type=F32)
    return r[:m] + r[m:2 * m] + r[2 * m:]


CONV_BLOCK = 256


def _sigmoid(x):
    return 1.0 / (1.0 + jnp.exp(-x))


def _proj_kernel(x_ref, g_ref, b_ref, w_ref, wsq_ref, wkv_ref, wab_ref, conv0_ref, wconv_ref,
                 y_ref, z_ref, sq_ref, sk_ref, sv_ref, ab_ref, tail_ref, h_ref, xbuf, *, seq_len):
    tm = x_ref.shape[0]
    h_ref[...] = _layer_norm(x_ref[...], g_ref[...], b_ref[...])
    h = h_ref[...].astype(BF16)

    chained = seq_len is None
    n_seq = 1 if chained else tm // seq_len
    L = tm if chained else seq_len
    if chained:
        @pl.when(pl.program_id(0) == 0)
        def _():
            xbuf[0, 0:SUBLANES, :] = conv0_ref[0]
    else:
        for s in range(n_seq):
            xbuf[s, 0:SUBLANES, :] = conv0_ref[s]
    blocks = [slice(cb, cb + CONV_BLOCK) for cb in range(0, CONV_CH, CONV_BLOCK)]

    def project(cols):
        raw = jnp.dot(h, w_ref[:, cols], preferred_element_type=F32)
        for s in range(n_seq):
            xbuf[s, SUBLANES:SUBLANES + L, cols] = raw[s * L:(s + 1) * L, :]

    def conv(cols):
        for s in range(n_seq):
            ext = xbuf[s, :, cols]
            w0, w1, w2, w3 = (0.5 * wconv_ref[j:j + 1, cols] for j in range(CONV_W))
            ext1 = pltpu.roll(ext, 1, axis=0)
            half = (w3 * ext + w2 * ext1)[SUBLANES:, :]
            half = half + pltpu.roll(w1 * ext + w0 * ext1, 2, axis=0)[SUBLANES:, :]
            y_ref[s * L:(s + 1) * L, cols] = half * (1.0 + jnp.tanh(half))
            tail_ref[s, :, cols] = xbuf[s, L:L + SUBLANES, cols]
            if chained:
                xbuf[s, 0:SUBLANES, cols] = xbuf[s, L:L + SUBLANES, cols]

    for o_ref, rhs in ((z_ref, w_ref[:, CONV_CH:]), (sq_ref, wsq_ref[...]),
                       (sk_ref, wkv_ref[:, :KV_W]), (sv_ref, wkv_ref[:, KV_W:]),
                       (ab_ref, wab_ref[...])):
        o_ref[...] = jnp.dot(h, rhs, preferred_element_type=F32)
    project(blocks[0])
    for bi, cols in enumerate(blocks):
        if bi + 1 < len(blocks):
            project(blocks[bi + 1])
        conv(cols)


def _proj(x, g, b, weights, conv0, wconv, tm, seq_len):
    rows = x.shape[0]
    w_main, w_sq, w_kv, w_ab = weights
    chained = seq_len is None
    n_tile_seq = 1 if chained else tm // seq_len
    n_seq = 1 if chained else rows // seq_len
    const = lambda i: (0, 0)
    row_blk = lambda width: pl.BlockSpec((tm, width), lambda i: (i, 0))
    seq_blk = pl.BlockSpec((n_tile_seq, SUBLANES, CONV_CH),
                           (lambda i: (0, 0, 0)) if chained else (lambda i: (i, 0, 0)))
    return pl.pallas_call(
        functools.partial(_proj_kernel, seq_len=seq_len),
        out_shape=[jax.ShapeDtypeStruct((rows, width), F32) for width in PROJ_SPLITS]
                  + [jax.ShapeDtypeStruct((n_seq, SUBLANES, CONV_CH), F32),
                     jax.ShapeDtypeStruct((rows, D_MODEL), F32)],
        grid=(rows // tm,),
        in_specs=[row_blk(D_MODEL), pl.BlockSpec((1, D_MODEL), const),
                  pl.BlockSpec((1, D_MODEL), const)]
                 + [pl.BlockSpec(wt.shape, const) for wt in weights]
                 + [seq_blk, pl.BlockSpec((SUBLANES, CONV_CH), const)],
        out_specs=[row_blk(width) for width in PROJ_SPLITS] + [seq_blk, row_blk(D_MODEL)],
        scratch_shapes=[pltpu.VMEM((n_tile_seq, SUBLANES + (tm if chained else seq_len), CONV_CH),
                                   F32)],
        compiler_params=pltpu.CompilerParams(dimension_semantics=("arbitrary",),
                                             vmem_limit_bytes=VMEM_LIMIT),
        name="ln_in_proj",
    )(x, g, b, w_main, w_sq, w_kv, w_ab, conv0, wconv)


GROUP_HEADS = 4


def _softplus(x):
    return jnp.maximum(x, 0.0) + jnp.log1p(jnp.exp(-jnp.abs(x)))


def _delta_kernel(qkv_ref, z_ref, ab_ref, s0_ref, alog_ref, dtb_ref,
                  normw_ref, seg_ref, expand_ref, o_ref, s_ref, sbd,
                  *, chunk, units, groups, sequential):
    C, U = chunk, units
    RG = U * C
    PAIRS = HEADS // GROUP_HEADS
    PW = GROUP_HEADS * HEAD_DIM
    step = pl.program_id(0)
    last_step = pl.num_programs(0) - 1
    expand = expand_ref[...]

    def seg_sum(t):
        tb = t.astype(BF16)
        return jnp.concatenate(
            [jnp.dot(tb[:, g * PW:(g + 1) * PW], seg_ref[g * PW:(g + 1) * PW, g * PW:(g + 1) * PW],
                     preferred_element_type=F32) for g in range(PAIRS)], axis=1)

    ri = lax.broadcasted_iota(jnp.int32, (C, PW), 0)
    ci = lax.broadcasted_iota(jnp.int32, (C, PW), 1) % HEAD_DIM
    causal = (ri >= ci) & (ci < C)
    strict = (ri > ci) & (ci < C)
    eye = (ri == ci).astype(F32)
    lane_head = lax.broadcasted_iota(jnp.int32, (1, PW), 1) // HEAD_DIM
    rs = lax.broadcasted_iota(jnp.int32, (LANES, LANES), 0)
    cs = lax.broadcasted_iota(jnp.int32, (LANES, LANES), 1)
    half_mask = ((rs // HEAD_DIM) == (cs // HEAD_DIM)).astype(F32)
    row_pad = [jnp.zeros((HEAD_DIM - C, PW), BF16)] if C < HEAD_DIM else []
    lane_pad = [jnp.zeros((1, HEAD_DIM - C), F32)] if C < HEAD_DIM else []
    rr = lax.broadcasted_iota(jnp.int32, (RG, RG), 0)
    cc = lax.broadcasted_iota(jnp.int32, (RG, RG), 1)
    tril = ((rr // C == cc // C) & (rr >= cc)).astype(BF16)

    def stack(y):
        yb = y.astype(BF16)
        zero = jnp.zeros_like(yb)
        pad = row_pad if y.shape[0] < HEAD_DIM else []
        blocks = []
        for h in range(GROUP_HEADS):
            blocks += [jnp.where(lane_head == h, yb, zero)] + pad
        return jnp.concatenate(blocks, axis=0)

    items = [(u, p) for u in range(U) for p in range(PAIRS)]
    sl = lambda u, p: (slice(u * C, (u + 1) * C), slice(p * PW, (p + 1) * PW))

    def front(gi):
        rg = slice(gi * RG, (gi + 1) * RG)
        q, k, v = (qkv_ref[rg, j * DN_W:(j + 1) * DN_W] for j in range(3))
        qn = q * (lax.rsqrt(seg_sum(q * q) + NORM_EPS) * HEAD_DIM ** -0.5)
        kn = k * lax.rsqrt(seg_sum(k * k) + NORM_EPS)
        yield

        ab = ab_ref[rg, :]
        g = -jnp.exp(alog_ref[...]) * _softplus(ab + dtb_ref[...])
        gs = jnp.dot(tril, jnp.concatenate(_split3(g), axis=1), preferred_element_type=F32)
        G = (gs[:, :LANES] + gs[:, LANES:2 * LANES] + gs[:, 2 * LANES:]) * LOG2E
        GT = G.T
        G_x = _dot_exact_lhs(G, expand)
        beta = _sigmoid(pltpu.roll(ab, LANES - HEADS, axis=1))
        b_hi = beta.astype(BF16)
        b_lo = (beta - b_hi.astype(F32)).astype(BF16)
        bx = jnp.dot(jnp.concatenate([b_hi, b_lo], axis=0), expand, preferred_element_type=F32)
        beta_x = bx[:RG] + bx[RG:]
        glast_x = jnp.concatenate(
            [jnp.broadcast_to(G_x[(u + 1) * C - 1:(u + 1) * C, :], (C, DN_W)) for u in range(U)],
            axis=0)
        yield
        eg_x = jnp.exp2(G_x)
        kbeta = kn * beta_x
        return dict(qn=qn, kn=kn, kbeta=kbeta, bk=kbeta * eg_x, bv=v * beta_x, qd=qn * eg_x,
                    kd=kn * jnp.exp2(glast_x - G_x), G_x=G_x, GT=GT, eg_x=eg_x)

    def middle(f):
        A, QK = {}, {}
        for (u, p) in items:
            ru, lp = sl(u, p)
            gr = []
            for h in range(p * GROUP_HEADS, (p + 1) * GROUP_HEADS):
                gr += [f["GT"][h:h + 1, ru]] + lane_pad
            gr = jnp.concatenate(gr, axis=1)
            decay = jnp.exp2(jnp.where(causal, f["G_x"][ru, lp] - gr, -jnp.inf))
            r = _mm_nt(jnp.concatenate([f["kbeta"][ru, lp], f["qn"][ru, lp]], axis=0),
                       stack(f["kn"][ru, lp]))
            A[u, p] = jnp.where(strict, r[:C] * decay, 0.0)
            QK[u, p] = r[C:] * decay
        yield

        n_pow = int(math.log2(C)) - 1
        P = {it: eye - A[it] for it in items}
        Ap = {it: _mm(A[it], stack(A[it])) for it in items}
        yield
        for i in range(n_pow):
            if i < n_pow - 1:
                r = {it: _mm(jnp.concatenate([Ap[it], P[it]], axis=0), stack(Ap[it])) for it in items}
                Ap = {it: r[it][:C] for it in items}
                P = {it: P[it] + r[it][C:] for it in items}
            else:
                P = {it: P[it] + _mm(P[it], stack(Ap[it])) for it in items}
            yield

        halves = [slice(hb * LANES, (hb + 1) * LANES) for hb in range(PW // LANES)]
        wu = {it: _mm(P[it], jnp.concatenate([stack(f["bk"][sl(*it)]), stack(f["bv"][sl(*it)])],
                                             axis=1)) for it in items}
        kdT = {(it, hb): f["kd"][sl(*it)][:, cols].T for it in items
               for hb, cols in enumerate(halves)}
        yield
        mn = {(it, hb): _mm(kdT[it, hb], jnp.concatenate(
            [wu[it][:, cols], wu[it][:, PW:][:, cols]], axis=1))
              for it in items for hb, cols in enumerate(halves)}
        yield
        n_t = {key: (half_mask * mn[key][:, LANES:]).T for key in mn}
        out = {}
        for (u, p) in items:
            ru, lp = sl(u, p)
            it = (u, p)
            out[it] = dict(
                W=wu[it][:, :PW], U=wu[it][:, PW:],
                M=diag_blocks([half_mask * mn[it, hb][:, :LANES]
                               for hb in range(len(halves))]).astype(BF16),
                NT=jnp.concatenate([n_t[it, hb][:HEAD_DIM] + n_t[it, hb][HEAD_DIM:]
                                    for hb in range(len(halves))], axis=1),
                QK=QK[it], qd=f["qd"][ru, lp],
                eg_last=f["eg_x"][(u + 1) * C - 1:(u + 1) * C, lp])
        return out

    def diag_blocks(blocks):
        n = len(blocks)
        rows = []
        for i, blk in enumerate(blocks):
            rows.append(jnp.concatenate(
                [blk if j == i else jnp.zeros_like(blk) for j in range(n)], axis=1))
        return jnp.concatenate(rows, axis=0)

    def load_state(ref, prefix, p):
        return jnp.concatenate([ref[prefix + (p * GROUP_HEADS + h,)].T
                                for h in range(GROUP_HEADS)], axis=1)

    def store_state(ref, prefix, p, st):
        for h in range(GROUP_HEADS):
            ref[prefix + (p * GROUP_HEADS + h,)] = st[:, h * HEAD_DIM:(h + 1) * HEAD_DIM].T

    chunks = [(gi, u) for gi in range(groups) for u in range(U)]
    if sequential:
        @pl.when(step == 0)
        def _():
            for p in range(PAIRS):
                sbd[p] = load_state(s0_ref, (), p)
        s_start = {p: sbd[p] for p in range(PAIRS)}
    else:
        s_start = {(idx, p): load_state(s0_ref, (idx,), p)
                   for idx in range(len(chunks)) for p in range(PAIRS)}

    def chain(gi, out, s_in):
        if sequential:
            states = {p: [s_in[p]] for p in range(PAIRS)}
            for u in range(U):
                for p in range(PAIRS):
                    d, s_cur = out[u, p], states[p][-1]
                    states[p].append(s_cur * d["eg_last"] - _mm_nt(s_cur, d["M"]) + d["NT"])
            return ({(u, p): states[p][u] for (u, p) in items},
                    {p: states[p][-1] for p in range(PAIRS)})
        entering = {(u, p): s_in[gi * U + u, p] for (u, p) in items}
        s_out = {}
        for (u, p) in items:
            d, s_cur = out[u, p], entering[u, p]
            s_out[gi * U + u, p] = s_cur * d["eg_last"] - _mm_nt(s_cur, d["M"]) + d["NT"]
        return entering, s_out

    def finish(gi, out, entering):
        r = {it: _mm_nt(jnp.concatenate([out[it]["W"], out[it]["qd"]], axis=0),
                        stack(entering[it])) for it in items}
        yield
        tiles = {it: r[it][C:] + _mm(out[it]["QK"], stack(out[it]["U"] - r[it][:C]))
                 for it in items}
        yield
        rg = slice(gi * RG, (gi + 1) * RG)
        o = jnp.concatenate([jnp.concatenate([tiles[u, p] for p in range(PAIRS)], axis=1)
                             for u in range(U)], axis=0)
        ms = seg_sum(o * o) * (1.0 / HEAD_DIM)
        z = z_ref[rg, :]
        o_ref[rg, :] = (o * lax.rsqrt(ms + NORM_EPS) * normw_ref[...]
                        * (z * _sigmoid(z))).astype(BF16)

    def drive(gens):
        results = [None] * len(gens)
        active = list(enumerate(gens))
        while active:
            still = []
            for i, gen in active:
                try:
                    next(gen)
                    still.append((i, gen))
                except StopIteration as stop:
                    results[i] = stop.value
            active = still
        return results

    state = s_start
    s_final = {}
    f_cur = drive([front(0)])[0]
    pending = None
    for gi in range(groups + 1):
        gens = []
        if gi < groups:
            gens.append(middle(f_cur))
        if gi + 1 < groups:
            gens.append(front(gi + 1))
        if pending is not None:
            entering, s_out = chain(gi - 1, pending, state)
            s_final.update(s_out)
            state = s_out if sequential else state
            gens.append(finish(gi - 1, pending, entering))
        res = drive(gens)
        pending = res[0] if gi < groups else None
        f_cur = res[1] if gi + 1 < groups else None

    if sequential:
        for p in range(PAIRS):
            sbd[p] = s_final[p]

        @pl.when(step == last_step)
        def _():
            for p in range(PAIRS):
                store_state(s_ref, (), p, s_final[p])
    else:
        for (idx, p), s_new in s_final.items():
            store_state(s_ref, (idx,), p, s_new)


def _delta(qkv, z, ab, s0, consts, n_seq, seq_len, chunk, units, groups):
    alog, dtb, normw, seg, expand = consts
    sequential = n_seq == 1
    rows = n_seq * seq_len
    n_chunks = units * groups
    blk_rows = n_chunks * chunk
    assert rows % blk_rows == 0 and (sequential or seq_len == chunk)
    const2 = lambda shape: pl.BlockSpec(shape, lambda i: (0, 0))
    tok = lambda width: pl.BlockSpec((blk_rows, width), lambda i: (i, 0))
    group_w = GROUP_HEADS * HEAD_DIM
    if sequential:
        state = pl.BlockSpec((None, HEADS, HEAD_DIM, HEAD_DIM), lambda i: (0, 0, 0, 0))
    else:
        state = pl.BlockSpec((n_chunks, HEADS, HEAD_DIM, HEAD_DIM), lambda i: (i, 0, 0, 0))
    return pl.pallas_call(
        functools.partial(_delta_kernel, chunk=chunk, units=units, groups=groups,
                          sequential=sequential),
        out_shape=[jax.ShapeDtypeStruct((rows, DN_W), BF16),
                   jax.ShapeDtypeStruct((n_seq, HEADS, HEAD_DIM, HEAD_DIM), F32)],
        grid=(rows // blk_rows,),
        in_specs=[tok(CONV_CH), tok(DN_W), tok(LANES), state,
                  const2((1, LANES)), const2((1, LANES)),
                  const2((1, DN_W)), const2((DN_W, DN_W)), const2((LANES, DN_W))],
        out_specs=[tok(DN_W), state],
        scratch_shapes=[pltpu.VMEM((HEADS // GROUP_HEADS, HEAD_DIM, group_w), F32)],
        compiler_params=pltpu.CompilerParams(
            dimension_semantics=("arbitrary",), vmem_limit_bytes=VMEM_LIMIT,
        ),
        name=f"delta_c{chunk}",
    )(qkv, z, ab, s0, alog, dtb, normw, seg, expand)


SWA_W = SWA_HEADS * HEAD_DIM


def _swa_kernel(*refs, n_q, units, prompt, batch=4):
    if prompt:
        (table_ref, sinks_ref, bucket_ref, q_ref, km_ref, kp_ref, kc_ref, vm_ref, vp_ref, vc_ref,
         o_ref, bias_ref) = refs
    else:
        (table_ref, sinks_ref, bucket_ref, q_ref, kp_ref, kc_ref, vp_ref, vc_ref,
         o_ref, bias_ref) = refs
    n_k = bias_ref.shape[-1]
    n_grp = SWA_W // KV_W
    step = pl.program_id(0)

    @pl.when(step == 0)
    def _():
        bucket = bucket_ref[...]
        for h in range(SWA_HEADS):
            acc = jnp.zeros((n_q, n_k), F32)
            for b in range(N_BUCKETS):
                acc = jnp.where(bucket == b, table_ref[b * SWA_HEADS + h], acc)
            kv, gi = divmod(h, SWA_G)
            bias_ref[kv, gi * n_q:(gi + 1) * n_q, :] = acc * LOG2E

    lane = lax.broadcasted_iota(jnp.int32, (1, KV_W), 1)
    kv_mask = [(lane < HEAD_DIM).astype(F32), (lane >= HEAD_DIM).astype(F32)]
    if prompt:
        first = step == 0
        k_all = jnp.concatenate([jnp.where(first, km_ref[...], kp_ref[...]), kc_ref[...]], axis=0)
        v_all = jnp.concatenate([jnp.where(first, vm_ref[...], vp_ref[...]), vc_ref[...]], axis=0)
        k_kv = [(k_all * m).astype(BF16) for m in kv_mask]
        v_kv = [(v_all * m).astype(BF16) for m in kv_mask]
        keys = lambda j, kv: k_kv[kv][j * n_q:j * n_q + n_k]
        vals = lambda j, kv: v_kv[kv][j * n_q:j * n_q + n_k]
    else:
        n_c = n_k - n_q
        cat = lambda a, b, j: jnp.concatenate([a[j * n_c:(j + 1) * n_c, :],
                                               b[j * n_q:(j + 1) * n_q, :]], axis=0)
        keys = lambda j, kv: (cat(kp_ref, kc_ref, j) * kv_mask[kv]).astype(BF16)
        vals = lambda j, kv: (cat(vp_ref, vc_ref, j) * kv_mask[kv]).astype(BF16)

    q = (q_ref[...] * (HEAD_DIM ** -0.5 * LOG2E)).astype(BF16)
    ones_k = jnp.ones((n_k, KV_W), BF16)
    sink = [jnp.concatenate([jnp.full((n_q, KV_W), sinks_ref[kv * SWA_G + gi] * LOG2E, F32)
                             for gi in range(SWA_G)], axis=0) for kv in range(SWA_KV)]
    rows4 = SWA_G * n_q
    for j0 in range(0, units, batch):
        js = range(j0, min(j0 + batch, units))
        items = [(j, kv) for j in js for kv in range(SWA_KV)]
        s = {}
        for j in js:
            rows = slice(j * n_q, (j + 1) * n_q)
            q4 = jnp.concatenate([q[rows, g * KV_W:(g + 1) * KV_W] for g in range(n_grp)], axis=0)
            for kv in range(SWA_KV):
                sj = lax.dot_general(q4, keys(j, kv), (((1,), (1,)), ((), ())),
                                     preferred_element_type=F32) + bias_ref[kv]
                if prompt and j * n_q < WINDOW - N_META:
                    key_pos = j * n_q + lax.broadcasted_iota(jnp.int32, (1, n_k), 1)
                    n_invalid = jnp.where(first, WINDOW - N_META, 0)
                    sj = jnp.where(key_pos < n_invalid, -jnp.inf, sj)
                s[j, kv] = sj
        m = {it: jnp.maximum(jnp.broadcast_to(jnp.max(s[it], axis=-1, keepdims=True),
                                              (rows4, KV_W)), sink[it[1]]) for it in items}
        p = {it: jnp.exp2(s[it] - jnp.concatenate([m[it], m[it][:, :n_k - KV_W]], axis=1))
             for it in items}
        pv = {it: jnp.dot(p[it].astype(BF16), jnp.concatenate([vals(*it), ones_k], axis=1),
                          preferred_element_type=F32) for it in items}
        o = {it: pv[it][:, :KV_W] * (1.0 / (pv[it][:, KV_W:] + jnp.exp2(sink[it[1]] - m[it])))
             for it in items}
        for j in js:
            o2 = o[j, 0] + o[j, 1]
            for g in range(n_grp):
                o_ref[j * n_q:(j + 1) * n_q, g * KV_W:(g + 1) * KV_W] = (
                    o2[g * n_q:(g + 1) * n_q, :].astype(BF16))


def _swa(table, sinks, bucket, q, kv_args, kv_specs, n_steps, n_q, units, prompt, name):
    n_k = bucket.shape[1]
    smem = pl.BlockSpec(memory_space=pltpu.SMEM)
    rows = units * n_q
    return pl.pallas_call(
        functools.partial(_swa_kernel, n_q=n_q, units=units, prompt=prompt),
        out_shape=jax.ShapeDtypeStruct((n_steps * rows, SWA_W), BF16),
        grid=(n_steps,),
        in_specs=[smem, smem, pl.BlockSpec((n_q, n_k), lambda i: (0, 0)),
                  pl.BlockSpec((rows, SWA_W), lambda i: (i, 0))] + list(kv_specs),
        out_specs=pl.BlockSpec((rows, SWA_W), lambda i: (i, 0)),
        scratch_shapes=[pltpu.VMEM((SWA_KV, SWA_G * n_q, n_k), F32)],
        compiler_params=pltpu.CompilerParams(dimension_semantics=("arbitrary",),
                                             vmem_limit_bytes=VMEM_LIMIT),
        name=name,
    )(table, sinks, bucket, q, *kv_args)


def _t5_bucket(rel):
    half = N_BUCKETS // 2
    max_exact = half // 2
    n = jnp.abs(rel)
    large = max_exact + (jnp.log(jnp.maximum(n, 1).astype(F32) / max_exact)
                         / math.log(MAX_DIST / max_exact) * (half - max_exact)).astype(jnp.int32)
    large = jnp.minimum(large, half - 1)
    return jnp.where(rel > 0, half, 0) + jnp.where(n < max_exact, n, large)


def _bucket_map(n_q, n_k, key_offset):
    rel = (jnp.arange(n_k)[None, :] - key_offset) - jnp.arange(n_q)[:, None]
    return _t5_bucket(rel).astype(jnp.int32)


FF_BLOCK = 1024


def _post_kernel(hp_ref, dnp_ref, swp_ref, hs_ref, dns_ref, sws_ref,
                 wod_ref, wos_ref, g1_ref, b1_ref, w1_ref, w2_ref, g2_ref, b2_ref,
                 yp_ref, ys_ref, *, n_prompt_tiles):
    is_prompt = pl.program_id(0) < n_prompt_tiles

    @pl.when(is_prompt)
    def _():
        _post_tile(hp_ref, dnp_ref, swp_ref, wod_ref, wos_ref, g1_ref, b1_ref,
                   w1_ref, w2_ref, g2_ref, b2_ref, yp_ref)

    @pl.when(jnp.logical_not(is_prompt))
    def _():
        _post_tile(hs_ref, dns_ref, sws_ref, wod_ref, wos_ref, g1_ref, b1_ref,
                   w1_ref, w2_ref, g2_ref, b2_ref, ys_ref)


def _post_tile(h_ref, dn_ref, sw_ref, wod_ref, wos_ref, g1_ref, b1_ref,
               w1_ref, w2_ref, g2_ref, b2_ref, y_ref):
    mix = (jnp.dot(dn_ref[...], wod_ref[...], preferred_element_type=F32)
           + jnp.dot(sw_ref[...], wos_ref[...], preferred_element_type=F32))
    h1 = _layer_norm(DEEP_ALPHA * h_ref[...] + mix, g1_ref[...], b1_ref[...])
    h1b = h1.astype(BF16)
    f = jnp.zeros_like(h1)
    for j in range(D_FF // FF_BLOCK):
        a = jnp.dot(h1b, w1_ref[:, j * FF_BLOCK:(j + 1) * FF_BLOCK], preferred_element_type=F32)
        a = jnp.square(jnp.maximum(a, 0.0)).astype(BF16)
        f = f + jnp.dot(a, w2_ref[j * FF_BLOCK:(j + 1) * FF_BLOCK, :], preferred_element_type=F32)
    y_ref[...] = _layer_norm(DEEP_ALPHA * h1 + f, g2_ref[...], b2_ref[...])


def _post(prompt, sample, vecs, weights, tm):
    g1, b1, g2, b2 = vecs
    wo_dn, wo_sw, w1, w2 = weights
    n_p, n_s = prompt[0].shape[0] // tm, sample[0].shape[0] // tm
    const = lambda i: (0, 0)
    vec = pl.BlockSpec((1, D_MODEL), const)
    weight = lambda wt: pl.BlockSpec(wt.shape, const, pipeline_mode=pl.Buffered(1))
    blk_p = lambda width: pl.BlockSpec((tm, width), lambda i: (jnp.minimum(i, n_p - 1), 0))
    blk_s = lambda width: pl.BlockSpec((tm, width), lambda i: (jnp.maximum(i - n_p, 0), 0))
    widths = (D_MODEL, DN_W, SWA_W)
    return pl.pallas_call(
        functools.partial(_post_kernel, n_prompt_tiles=n_p),
        out_shape=[jax.ShapeDtypeStruct((n_p * tm, D_MODEL), F32),
                   jax.ShapeDtypeStruct((n_s * tm, D_MODEL), F32)],
        grid=(n_p + n_s,),
        in_specs=[blk_p(wd) for wd in widths] + [blk_s(wd) for wd in widths]
                 + [weight(wo_dn), weight(wo_sw), vec, vec, weight(w1), weight(w2), vec, vec],
        out_specs=[blk_p(D_MODEL), blk_s(D_MODEL)],
        compiler_params=pltpu.CompilerParams(dimension_semantics=("arbitrary",),
                                             vmem_limit_bytes=VMEM_LIMIT),
        name="post_mlp",
    )(*prompt, *sample, wo_dn, wo_sw, g1, b1, w1, w2, g2, b2)


def kernel(x_prompt, x_sample, state_delta, state_conv, cache_swa_k, cache_swa_v, meta_tokens, ln_in_g, ln_in_b, w_in, w_conv, dn_a_log, dn_dt_bias, dn_norm, swa_sinks, rel_bias_table, w_o, ln1_g, ln1_b, w_ff1, w_ff2, ln2_g, ln2_b):
    assert w_in.shape[0] == DEPTH == 1
    n_seq_s, len_s = x_sample.shape[0], x_sample.shape[1]
    len_p = x_prompt.shape[1]
    cache_len = cache_swa_k.shape[2]
    chunk_p = 64
    row = lambda t: t.reshape(1, -1).astype(F32)

    w = w_in[0]
    w_sq = (w[:, OFF_SQ:OFF_SK].astype(BF16).reshape(D_MODEL, SWA_KV, SWA_G, HEAD_DIM)
            .transpose(0, 2, 1, 3).reshape(D_MODEL, SWA_W))
    w_r = (w[:, :OFF_A].astype(BF16), w_sq, w[:, OFF_SK:].astype(BF16),
           jnp.pad(w[:, OFF_A:OFF_SQ].astype(BF16), ((0, 0), (0, LANES - 2 * HEADS))))
    gin, bin_ = row(ln_in_g), row(ln_in_b)
    pad_lanes = lambda t: jnp.pad(t.reshape(1, -1).astype(F32), ((0, 0), (0, LANES - HEADS)))
    lane_head = jnp.arange(DN_W) // HEAD_DIM
    seg = (lane_head[:, None] == lane_head[None, :]).astype(BF16)
    expand = (jnp.arange(LANES)[:, None] == lane_head[None, :]).astype(BF16)
    wconv = jnp.pad(w_conv[0].astype(F32), ((0, SUBLANES - CONV_W), (0, 0)))
    delta_consts = (pad_lanes(dn_a_log[0]), pad_lanes(dn_dt_bias[0]),
                    jnp.tile(dn_norm[0].astype(F32), HEADS).reshape(1, DN_W), seg, expand)
    post_vecs = (row(ln1_g[0]), row(ln1_b[0]), row(ln2_g[0]), row(ln2_b[0]))
    wo_sw = (w_o[0][DN_W:].astype(BF16).reshape(SWA_KV, SWA_G, HEAD_DIM, D_MODEL)
             .transpose(1, 0, 2, 3).reshape(SWA_W, D_MODEL))
    post_w = (w_o[0][:DN_W].astype(BF16), wo_sw, w_ff1[0].astype(BF16), w_ff2[0].astype(BF16))
    table = rel_bias_table.astype(F32).reshape(-1)
    sinks = swa_sinks[0].astype(F32)

    xp = x_prompt[0]
    xs = x_sample.reshape(n_seq_s * len_s, D_MODEL)

    conv_pad = SUBLANES - (CONV_W - 1)
    zero_conv = jnp.zeros((1, SUBLANES, CONV_CH), F32)
    conv_s = jnp.pad(state_conv[0].astype(F32), ((0, 0), (conv_pad, 0), (0, 0)))
    m_qkv, m_z, m_sq, m_sk, m_sv, m_ab, m_tail, _ = _proj(
        meta_tokens.astype(F32), gin, bin_, w_r, zero_conv, wconv, N_META, N_META)
    p_qkv, p_z, p_sq, p_sk, p_sv, p_ab, p_tail, p_h = _proj(
        xp, gin, bin_, w_r, m_tail, wconv, 512, None)
    s_qkv, s_z, s_sq, s_sk, s_sv, s_ab, s_tail, s_h = _proj(
        xs, gin, bin_, w_r, conv_s, wconv, 512, len_s)

    zero_s = jnp.zeros((1, HEADS, HEAD_DIM, HEAD_DIM), F32)
    _, s_meta = _delta(m_qkv, m_z, m_ab, zero_s, delta_consts, 1, N_META, N_META, 1, 1)
    p_dn, p_s = _delta(p_qkv, p_z, p_ab, s_meta, delta_consts, 1, len_p, chunk_p, 4, 2)
    s_dn, s_s = _delta(s_qkv, s_z, s_ab, state_delta[0].astype(F32), delta_consts,
                       n_seq_s, len_s, len_s, 8, 2)

    span = WINDOW
    lead = jnp.zeros((span - N_META, KV_W), F32)
    units_p = 16
    rows_p = units_p * chunk_p
    prev_blocks = rows_p // span
    kv_specs_p = [pl.BlockSpec((span, KV_W), lambda i: (0, 0)),
                  pl.BlockSpec((span, KV_W), lambda i: (jnp.maximum(i * prev_blocks - 1, 0), 0)),
                  pl.BlockSpec((rows_p, KV_W), lambda i: (i, 0))] * 2
    p_sw = _swa(table, sinks, _bucket_map(chunk_p, span + chunk_p, span), p_sq,
                [jnp.concatenate([lead, m_sk], axis=0), p_sk, p_sk,
                 jnp.concatenate([lead, m_sv], axis=0), p_sv, p_sv],
                kv_specs_p, len_p // rows_p, chunk_p, units_p, True, "swa_prompt")
    ck = cache_swa_k[0].astype(F32).reshape(n_seq_s * cache_len, KV_W)
    cv = cache_swa_v[0].astype(F32).reshape(n_seq_s * cache_len, KV_W)
    units_s = 8
    kv_specs_s = [pl.BlockSpec((units_s * cache_len, KV_W), lambda i: (i, 0)),
                  pl.BlockSpec((units_s * len_s, KV_W), lambda i: (i, 0))] * 2
    s_sw = _swa(table, sinks, _bucket_map(len_s, cache_len + len_s, cache_len), s_sq,
                [ck, s_sk, cv, s_sv], kv_specs_s, n_seq_s // units_s, len_s, units_s, False,
                "swa_sample")

    y_p, y_s = _post((p_h, p_dn, p_sw), (s_h, s_dn, s_sw), post_vecs, post_w, 512)

    kv_shape = lambda n, length: (1, n, length, SWA_KV, HEAD_DIM)
    return (y_p[None], y_s.reshape(x_sample.shape),
            p_s[None], p_tail[:, -(CONV_W - 1):][None],
            p_sk[-WINDOW:].reshape(kv_shape(1, WINDOW)), p_sv[-WINDOW:].reshape(kv_shape(1, WINDOW)),
            s_s[None], s_tail[:, -(CONV_W - 1):][None],
            s_sk.reshape(kv_shape(n_seq_s, len_s)), s_sv.reshape(kv_shape(n_seq_s, len_s)))
```

```python
import functools
import math

import jax
import jax.numpy as jnp
from jax import lax
from jax.experimental import pallas as pl
from jax.experimental.pallas import tpu as pltpu

F32 = jnp.float32
BF16 = jnp.bfloat16

D_MODEL = 1024
N_META = 16
HEADS = 8
HEAD_DIM = 64
DN_W = HEADS * HEAD_DIM
CONV_W = 4
CONV_CH = 3 * DN_W
SWA_HEADS = 8
SWA_KV = 2
SWA_G = SWA_HEADS // SWA_KV
KV_W = SWA_KV * HEAD_DIM
WINDOW = 128
N_BUCKETS = 32
MAX_DIST = 128
D_FF = 4 * D_MODEL
DEPTH = 1
DEEP_ALPHA = (2 * DEPTH) ** 0.25
LN_EPS = 1e-5
NORM_EPS = 1e-6
LOG2E = math.log2(math.e)
LANES = 128
SUBLANES = 8
VMEM_LIMIT = 56 * 1024 * 1024

OFF_Z = CONV_CH
OFF_A = OFF_Z + DN_W
OFF_B = OFF_A + HEADS
OFF_SQ = OFF_B + HEADS
OFF_SK = OFF_SQ + SWA_HEADS * HEAD_DIM
PROJ_SPLITS = (CONV_CH, DN_W, SWA_HEADS * HEAD_DIM, KV_W, KV_W, LANES)

ROW_TILE = 512
DELTA_CHUNK = 64
DELTA_PROMPT = (4, 2)
DELTA_SAMPLE = (8, 2)
SWA_UNITS_PROMPT = 16
SWA_UNITS_SAMPLE = 8
SWA_BATCH = 4


def _layer_norm(x, g, b):
    mu = jnp.mean(x, axis=-1, keepdims=True)
    xc = x - mu
    var = jnp.mean(xc * xc, axis=-1, keepdims=True)
    return xc * lax.rsqrt(var + LN_EPS) * g + b


def _mm(a, b):
    return jnp.dot(a.astype(BF16), b.astype(BF16), preferred_element_type=F32)


def _mm_nt(a, b):
    return lax.dot_general(a.astype(BF16), b.astype(BF16), (((1,), (1,)), ((), ())),
                           preferred_element_type=F32)


def _split3(x):
    x1 = x.astype(BF16)
    r1 = x - x1.astype(F32)
    x2 = r1.astype(BF16)
    x3 = (r1 - x2.astype(F32)).astype(BF16)
    return x1, x2, x3


def _dot_exact_lhs(x, sel):
    m = x.shape[0]
    stacked = jnp.concatenate(_split3(x), axis=0)
    r = jnp.dot(stacked, sel, preferred_element_type=F32)
    return r[:m] + r[m:2 * m] + r[2 * m:]


CONV_BLOCK = 256


def _sigmoid(x):
    return 1.0 / (1.0 + jnp.exp(-x))


def _proj_kernel(x_ref, g_ref, b_ref, w_ref, wsq_ref, wkv_ref, wab_ref, conv0_ref, wconv_ref,
                 y_ref, z_ref, sq_ref, sk_ref, sv_ref, ab_ref, tail_ref, h_ref, xbuf, *, seq_len):
    tm = x_ref.shape[0]
    h_ref[...] = _layer_norm(x_ref[...], g_ref[...], b_ref[...])
    h = h_ref[...].astype(BF16)

    chained = seq_len is None
    n_seq = 1 if chained else tm // seq_len
    L = tm if chained else seq_len
    if chained:
        @pl.when(pl.program_id(0) == 0)
        def _():
            xbuf[0, 0:SUBLANES, :] = conv0_ref[0]
    else:
        for s in range(n_seq):
            xbuf[s, 0:SUBLANES, :] = conv0_ref[s]
    blocks = [slice(cb, cb + CONV_BLOCK) for cb in range(0, CONV_CH, CONV_BLOCK)]

    def project(cols):
        raw = jnp.dot(h, w_ref[:, cols], preferred_element_type=F32)
        for s in range(n_seq):
            xbuf[s, SUBLANES:SUBLANES + L, cols] = raw[s * L:(s + 1) * L, :]

    def conv(cols):
        for s in range(n_seq):
            ext = xbuf[s, :, cols]
            w0, w1, w2, w3 = (0.5 * wconv_ref[j:j + 1, cols] for j in range(CONV_W))
            ext1 = pltpu.roll(ext, 1, axis=0)
            half = (w3 * ext + w2 * ext1)[SUBLANES:, :]
            half = half + pltpu.roll(w1 * ext + w0 * ext1, 2, axis=0)[SUBLANES:, :]
            y_ref[s * L:(s + 1) * L, cols] = half * (1.0 + jnp.tanh(half))
            tail_ref[s, :, cols] = xbuf[s, L:L + SUBLANES, cols]
            if chained:
                xbuf[s, 0:SUBLANES, cols] = xbuf[s, L:L + SUBLANES, cols]

    for o_ref, rhs in ((z_ref, w_ref[:, CONV_CH:]), (sq_ref, wsq_ref[...]),
                       (sk_ref, wkv_ref[:, :KV_W]), (sv_ref, wkv_ref[:, KV_W:]),
                       (ab_ref, wab_ref[...])):
        o_ref[...] = jnp.dot(h, rhs, preferred_element_type=F32)
    project(blocks[0])
    for bi, cols in enumerate(blocks):
        if bi + 1 < len(blocks):
            project(blocks[bi + 1])
        conv(cols)


def _proj(x, g, b, weights, conv0, wconv, tm, seq_len):
    rows = x.shape[0]
    w_main, w_sq, w_kv, w_ab = weights
    chained = seq_len is None
    n_tile_seq = 1 if chained else tm // seq_len
    n_seq = 1 if chained else rows // seq_len
    const = lambda i: (0, 0)
    row_blk = lambda width: pl.BlockSpec((tm, width), lambda i: (i, 0))
    seq_blk = pl.BlockSpec((n_tile_seq, SUBLANES, CONV_CH),
                           (lambda i: (0, 0, 0)) if chained else (lambda i: (i, 0, 0)))
    return pl.pallas_call(
        functools.partial(_proj_kernel, seq_len=seq_len),
        out_shape=[jax.ShapeDtypeStruct((rows, width), F32) for width in PROJ_SPLITS]
                  + [jax.ShapeDtypeStruct((n_seq, SUBLANES, CONV_CH), F32),
                     jax.ShapeDtypeStruct((rows, D_MODEL), F32)],
        grid=(rows // tm,),
        in_specs=[row_blk(D_MODEL), pl.BlockSpec((1, D_MODEL), const),
                  pl.BlockSpec((1, D_MODEL), const)]
                 + [pl.BlockSpec(wt.shape, const) for wt in weights]
                 + [seq_blk, pl.BlockSpec((SUBLANES, CONV_CH), const)],
        out_specs=[row_blk(width) for width in PROJ_SPLITS] + [seq_blk, row_blk(D_MODEL)],
        scratch_shapes=[pltpu.VMEM((n_tile_seq, SUBLANES + (tm if chained else seq_len), CONV_CH),
                                   F32)],
        compiler_params=pltpu.CompilerParams(dimension_semantics=("arbitrary",),
                                             vmem_limit_bytes=VMEM_LIMIT),
        name="ln_in_proj",
    )(x, g, b, w_main, w_sq, w_kv, w_ab, conv0, wconv)


GROUP_HEADS = 4


def _softplus(x):
    return jnp.maximum(x, 0.0) + jnp.log1p(jnp.exp(-jnp.abs(x)))


def _delta_kernel(qkv_ref, z_ref, ab_ref, s0_ref, alog_ref, dtb_ref,
                  normw_ref, seg_ref, expand_ref, o_ref, s_ref, sbd,
                  *, chunk, units, groups, sequential):
    C, U = chunk, units
    RG = U * C
    PAIRS = HEADS // GROUP_HEADS
    PW = GROUP_HEADS * HEAD_DIM
    step = pl.program_id(0)
    last_step = pl.num_programs(0) - 1
    expand = expand_ref[...]

    def seg_sum(t):
        tb = t.astype(BF16)
        return jnp.concatenate(
            [jnp.dot(tb[:, g * PW:(g + 1) * PW], seg_ref[g * PW:(g + 1) * PW, g * PW:(g + 1) * PW],
                     preferred_element_type=F32) for g in range(PAIRS)], axis=1)

    ri = lax.broadcasted_iota(jnp.int32, (C, PW), 0)
    ci = lax.broadcasted_iota(jnp.int32, (C, PW), 1) % HEAD_DIM
    causal = (ri >= ci) & (ci < C)
    strict = (ri > ci) & (ci < C)
    eye = (ri == ci).astype(F32)
    lane_head = lax.broadcasted_iota(jnp.int32, (1, PW), 1) // HEAD_DIM
    rs = lax.broadcasted_iota(jnp.int32, (LANES, LANES), 0)
    cs = lax.broadcasted_iota(jnp.int32, (LANES, LANES), 1)
    half_mask = ((rs // HEAD_DIM) == (cs // HEAD_DIM)).astype(F32)
    row_pad = [jnp.zeros((HEAD_DIM - C, PW), BF16)] if C < HEAD_DIM else []
    lane_pad = [jnp.zeros((1, HEAD_DIM - C), F32)] if C < HEAD_DIM else []
    rr = lax.broadcasted_iota(jnp.int32, (RG, RG), 0)
    cc = lax.broadcasted_iota(jnp.int32, (RG, RG), 1)
    tril = ((rr // C == cc // C) & (rr >= cc)).astype(BF16)

    def stack(y):
        yb = y.astype(BF16)
        zero = jnp.zeros_like(yb)
        pad = row_pad if y.shape[0] < HEAD_DIM else []
        blocks = []
        for h in range(GROUP_HEADS):
            blocks += [jnp.where(lane_head == h, yb, zero)] + pad
        return jnp.concatenate(blocks, axis=0)

    items = [(u, p) for u in range(U) for p in range(PAIRS)]
    sl = lambda u, p: (slice(u * C, (u + 1) * C), slice(p * PW, (p + 1) * PW))

    def front(gi):
        rg = slice(gi * RG, (gi + 1) * RG)
        q, k, v = (qkv_ref[rg, j * DN_W:(j + 1) * DN_W] for j in range(3))
        qn = q * (lax.rsqrt(seg_sum(q * q) + NORM_EPS) * HEAD_DIM ** -0.5)
        kn = k * lax.rsqrt(seg_sum(k * k) + NORM_EPS)
        yield

        ab = ab_ref[rg, :]
        g = -jnp.exp(alog_ref[...]) * _softplus(ab + dtb_ref[...])
        gs = jnp.dot(tril, jnp.concatenate(_split3(g), axis=1), preferred_element_type=F32)
        G = (gs[:, :LANES] + gs[:, LANES:2 * LANES] + gs[:, 2 * LANES:]) * LOG2E
        GT = G.T
        G_x = _dot_exact_lhs(G, expand)
        beta = _sigmoid(pltpu.roll(ab, LANES - HEADS, axis=1))
        b_hi = beta.astype(BF16)
        b_lo = (beta - b_hi.astype(F32)).astype(BF16)
        bx = jnp.dot(jnp.concatenate([b_hi, b_lo], axis=0), expand, preferred_element_type=F32)
        beta_x = bx[:RG] + bx[RG:]
        glast_x = jnp.concatenate(
            [jnp.broadcast_to(G_x[(u + 1) * C - 1:(u + 1) * C, :], (C, DN_W)) for u in range(U)],
            axis=0)
        yield
        eg_x = jnp.exp2(G_x)
        kbeta = kn * beta_x
        return dict(qn=qn, kn=kn, kbeta=kbeta, bk=kbeta * eg_x, bv=v * beta_x, qd=qn * eg_x,
                    kd=kn * jnp.exp2(glast_x - G_x), G_x=G_x, GT=GT, eg_x=eg_x)

    def middle(f):
        A, QK = {}, {}
        for (u, p) in items:
            ru, lp = sl(u, p)
            gr = []
            for h in range(p * GROUP_HEADS, (p + 1) * GROUP_HEADS):
                gr += [f["GT"][h:h + 1, ru]] + lane_pad
            gr = jnp.concatenate(gr, axis=1)
            decay = jnp.exp2(jnp.where(causal, f["G_x"][ru, lp] - gr, -jnp.inf))
            r = _mm_nt(jnp.concatenate([f["kbeta"][ru, lp], f["qn"][ru, lp]], axis=0),
                       stack(f["kn"][ru, lp]))
            A[u, p] = jnp.where(strict, r[:C] * decay, 0.0)
            QK[u, p] = r[C:] * decay
        yield

        n_pow = int(math.log2(C)) - 1
        P = {it: eye - A[it] for it in items}
        Ap = {it: _mm(A[it], stack(A[it])) for it in items}
        yield
        for i in range(n_pow):
            if i < n_pow - 1:
                r = {it: _mm(jnp.concatenate([Ap[it], P[it]], axis=0), stack(Ap[it])) for it in items}
                Ap = {it: r[it][:C] for it in items}
                P = {it: P[it] + r[it][C:] for it in items}
            else:
                P = {it: P[it] + _mm(P[it], stack(Ap[it])) for it in items}
            yield

        halves = [slice(hb * LANES, (hb + 1) * LANES) for hb in range(PW // LANES)]
        wu = {it: _mm(P[it], jnp.concatenate([stack(f["bk"][sl(*it)]), stack(f["bv"][sl(*it)])],
                                             axis=1)) for it in items}
        kdT = {(it, hb): f["kd"][sl(*it)][:, cols].T for it in items
               for hb, cols in enumerate(halves)}
        yield
        mn = {(it, hb): _mm(kdT[it, hb], jnp.concatenate(
            [wu[it][:, cols], wu[it][:, PW:][:, cols]], axis=1))
              for it in items for hb, cols in enumerate(halves)}
        yield
        n_t = {key: (half_mask * mn[key][:, LANES:]).T for key in mn}
        out = {}
        for (u, p) in items:
            ru, lp = sl(u, p)
            it = (u, p)
            out[it] = dict(
                W=wu[it][:, :PW], U=wu[it][:, PW:],
                M=diag_blocks([half_mask * mn[it, hb][:, :LANES]
                               for hb in range(len(halves))]).astype(BF16),
                NT=jnp.concatenate([n_t[it, hb][:HEAD_DIM] + n_t[it, hb][HEAD_DIM:]
                                    for hb in range(len(halves))], axis=1),
                QK=QK[it], qd=f["qd"][ru, lp],
                eg_last=f["eg_x"][(u + 1) * C - 1:(u + 1) * C, lp])
        return out

    def diag_blocks(blocks):
        n = len(blocks)
        rows = []
        for i, blk in enumerate(blocks):
            rows.append(jnp.concatenate(
                [blk if j == i else jnp.zeros_like(blk) for j in range(n)], axis=1))
        return jnp.concatenate(rows, axis=0)

    def load_state(ref, prefix, p):
        return jnp.concatenate([ref[prefix + (p * GROUP_HEADS + h,)].T
                                for h in range(GROUP_HEADS)], axis=1)

    def store_state(ref, prefix, p, st):
        for h in range(GROUP_HEADS):
            ref[prefix + (p * GROUP_HEADS + h,)] = st[:, h * HEAD_DIM:(h + 1) * HEAD_DIM].T

    chunks = [(gi, u) for gi in range(groups) for u in range(U)]
    if sequential:
        @pl.when(step == 0)
        def _():
            for p in range(PAIRS):
                sbd[p] = load_state(s0_ref, (), p)
        s_start = {p: sbd[p] for p in range(PAIRS)}
    else:
        s_start = {(idx, p): load_state(s0_ref, (idx,), p)
                   for idx in range(len(chunks)) for p in range(PAIRS)}

    def chain(gi, out, s_in):
        if sequential:
            states = {p: [s_in[p]] for p in range(PAIRS)}
            for u in range(U):
                for p in range(PAIRS):
                    d, s_cur = out[u, p], states[p][-1]
                    states[p].append(s_cur * d["eg_last"] - _mm_nt(s_cur, d["M"]) + d["NT"])
            return ({(u, p): states[p][u] for (u, p) in items},
                    {p: states[p][-1] for p in range(PAIRS)})
        entering = {(u, p): s_in[gi * U + u, p] for (u, p) in items}
        s_out = {}
        for (u, p) in items:
            d, s_cur = out[u, p], entering[u, p]
            s_out[gi * U + u, p] = s_cur * d["eg_last"] - _mm_nt(s_cur, d["M"]) + d["NT"]
        return entering, s_out

    def finish(gi, out, entering):
        r = {it: _mm_nt(jnp.concatenate([out[it]["W"], out[it]["qd"]], axis=0),
                        stack(entering[it])) for it in items}
        yield
        tiles = {it: r[it][C:] + _mm(out[it]["QK"], stack(out[it]["U"] - r[it][:C]))
                 for it in items}
        yield
        rg = slice(gi * RG, (gi + 1) * RG)
        o = jnp.concatenate([jnp.concatenate([tiles[u, p] for p in range(PAIRS)], axis=1)
                             for u in range(U)], axis=0)
        ms = seg_sum(o * o) * (1.0 / HEAD_DIM)
        hz = 0.5 * z_ref[rg, :]
        o_ref[rg, :] = (o * lax.rsqrt(ms + NORM_EPS) * normw_ref[...]
                        * (hz * (1.0 + jnp.tanh(hz)))).astype(BF16)

    def drive(gens):
        results = [None] * len(gens)
        active = list(enumerate(gens))
        while active:
            still = []
            for i, gen in active:
                try:
                    next(gen)
                    still.append((i, gen))
                except StopIteration as stop:
                    results[i] = stop.value
            active = still
        return results

    state = s_start
    s_final = {}
    f_cur = drive([front(0)])[0]
    pending = None
    for gi in range(groups + 1):
        gens = []
        if gi < groups:
            gens.append(middle(f_cur))
        if gi + 1 < groups:
            gens.append(front(gi + 1))
        if pending is not None:
            entering, s_out = chain(gi - 1, pending, state)
            s_final.update(s_out)
            state = s_out if sequential else state
            gens.append(finish(gi - 1, pending, entering))
        res = drive(gens)
        pending = res[0] if gi < groups else None
        f_cur = res[1] if gi + 1 < groups else None

    if sequential:
        for p in range(PAIRS):
            sbd[p] = s_final[p]

        @pl.when(step == last_step)
        def _():
            for p in range(PAIRS):
                store_state(s_ref, (), p, s_final[p])
    else:
        for (idx, p), s_new in s_final.items():
            store_state(s_ref, (idx,), p, s_new)


def _delta(qkv, z, ab, s0, consts, n_seq, seq_len, chunk, units, groups):
    alog, dtb, normw, seg, expand = consts
    sequential = n_seq == 1
    rows = n_seq * seq_len
    n_chunks = units * groups
    blk_rows = n_chunks * chunk
    assert rows % blk_rows == 0 and (sequential or seq_len == chunk)
    const2 = lambda shape: pl.BlockSpec(shape, lambda i: (0, 0))
    tok = lambda width: pl.BlockSpec((blk_rows, width), lambda i: (i, 0))
    group_w = GROUP_HEADS * HEAD_DIM
    if sequential:
        state = pl.BlockSpec((None, HEADS, HEAD_DIM, HEAD_DIM), lambda i: (0, 0, 0, 0))
    else:
        state = pl.BlockSpec((n_chunks, HEADS, HEAD_DIM, HEAD_DIM), lambda i: (i, 0, 0, 0))
    return pl.pallas_call(
        functools.partial(_delta_kernel, chunk=chunk, units=units, groups=groups,
                          sequential=sequential),
        out_shape=[jax.ShapeDtypeStruct((rows, DN_W), BF16),
                   jax.ShapeDtypeStruct((n_seq, HEADS, HEAD_DIM, HEAD_DIM), F32)],
        grid=(rows // blk_rows,),
        in_specs=[tok(CONV_CH), tok(DN_W), tok(LANES), state,
                  const2((1, LANES)), const2((1, LANES)),
                  const2((1, DN_W)), const2((DN_W, DN_W)), const2((LANES, DN_W))],
        out_specs=[tok(DN_W), state],
        scratch_shapes=[pltpu.VMEM((HEADS // GROUP_HEADS, HEAD_DIM, group_w), F32)],
        compiler_params=pltpu.CompilerParams(
            dimension_semantics=("arbitrary",), vmem_limit_bytes=VMEM_LIMIT,
        ),
        name=f"delta_c{chunk}",
    )(qkv, z, ab, s0, alog, dtb, normw, seg, expand)


SWA_W = SWA_HEADS * HEAD_DIM


def _swa_kernel(*refs, n_q, units, prompt, batch=SWA_BATCH):
    if prompt:
        (table_ref, sinks_ref, bucket_ref, q_ref, km_ref, kp_ref, kc_ref, vm_ref, vp_ref, vc_ref,
         o_ref, bias_ref) = refs
    else:
        (table_ref, sinks_ref, bucket_ref, q_ref, kp_ref, kc_ref, vp_ref, vc_ref,
         o_ref, bias_ref) = refs
    n_k = bias_ref.shape[-1]
    n_grp = SWA_W // KV_W
    step = pl.program_id(0)

    @pl.when(step == 0)
    def _():
        bucket = bucket_ref[...]
        for h in range(SWA_HEADS):
            acc = jnp.zeros((n_q, n_k), F32)
            for b in range(N_BUCKETS):
                acc = jnp.where(bucket == b, table_ref[b * SWA_HEADS + h], acc)
            kv, gi = divmod(h, SWA_G)
            bias_ref[kv, gi * n_q:(gi + 1) * n_q, :] = acc * LOG2E

    lane = lax.broadcasted_iota(jnp.int32, (1, KV_W), 1)
    kv_mask = [(lane < HEAD_DIM).astype(F32), (lane >= HEAD_DIM).astype(F32)]
    if prompt:
        first = step == 0
        k_all = jnp.concatenate([jnp.where(first, km_ref[...], kp_ref[...]), kc_ref[...]], axis=0)
        v_all = jnp.concatenate([jnp.where(first, vm_ref[...], vp_ref[...]), vc_ref[...]], axis=0)
        k_kv = [(k_all * m).astype(BF16) for m in kv_mask]
        v_kv = [(v_all * m).astype(BF16) for m in kv_mask]
        keys = lambda j, kv: k_kv[kv][j * n_q:j * n_q + n_k]
        vals = lambda j, kv: v_kv[kv][j * n_q:j * n_q + n_k]
    else:
        n_c = n_k - n_q
        cat = lambda a, b, j: jnp.concatenate([a[j * n_c:(j + 1) * n_c, :],
                                               b[j * n_q:(j + 1) * n_q, :]], axis=0)
        keys = lambda j, kv: (cat(kp_ref, kc_ref, j) * kv_mask[kv]).astype(BF16)
        vals = lambda j, kv: (cat(vp_ref, vc_ref, j) * kv_mask[kv]).astype(BF16)

    q = (q_ref[...] * (HEAD_DIM ** -0.5 * LOG2E)).astype(BF16)
    ones_k = jnp.ones((n_k, KV_W), BF16)
    sink = [jnp.concatenate([jnp.full((n_q, KV_W), sinks_ref[kv * SWA_G + gi] * LOG2E, F32)
                             for gi in range(SWA_G)], axis=0) for kv in range(SWA_KV)]
    rows4 = SWA_G * n_q
    for j0 in range(0, units, batch):
        js = range(j0, min(j0 + batch, units))
        items = [(j, kv) for j in js for kv in range(SWA_KV)]
        s = {}
        for j in js:
            rows = slice(j * n_q, (j + 1) * n_q)
            q4 = jnp.concatenate([q[rows, g * KV_W:(g + 1) * KV_W] for g in range(n_grp)], axis=0)
            for kv in range(SWA_KV):
                sj = lax.dot_general(q4, keys(j, kv), (((1,), (1,)), ((), ())),
                                     preferred_element_type=F32) + bias_ref[kv]
                if prompt and j * n_q < WINDOW - N_META:
                    key_pos = j * n_q + lax.broadcasted_iota(jnp.int32, (1, n_k), 1)
                    n_invalid = jnp.where(first, WINDOW - N_META, 0)
                    sj = jnp.where(key_pos < n_invalid, -jnp.inf, sj)
                s[j, kv] = sj
        m = {it: jnp.maximum(jnp.broadcast_to(jnp.max(s[it], axis=-1, keepdims=True),
                                              (rows4, KV_W)), sink[it[1]]) for it in items}
        p = {it: jnp.exp2(s[it] - jnp.concatenate([m[it], m[it][:, :n_k - KV_W]], axis=1))
             for it in items}
        pv = {it: jnp.dot(p[it].astype(BF16), jnp.concatenate([vals(*it), ones_k], axis=1),
                          preferred_element_type=F32) for it in items}
        o = {it: pv[it][:, :KV_W] * (1.0 / (pv[it][:, KV_W:] + jnp.exp2(sink[it[1]] - m[it])))
             for it in items}
        for j in js:
            o2 = o[j, 0] + o[j, 1]
            for g in range(n_grp):
                o_ref[j * n_q:(j + 1) * n_q, g * KV_W:(g + 1) * KV_W] = (
                    o2[g * n_q:(g + 1) * n_q, :].astype(BF16))


def _swa(table, sinks, bucket, q, kv_args, kv_specs, n_steps, n_q, units, prompt, name):
    n_k = bucket.shape[1]
    smem = pl.BlockSpec(memory_space=pltpu.SMEM)
    rows = units * n_q
    return pl.pallas_call(
        functools.partial(_swa_kernel, n_q=n_q, units=units, prompt=prompt),
        out_shape=jax.ShapeDtypeStruct((n_steps * rows, SWA_W), BF16),
        grid=(n_steps,),
        in_specs=[smem, smem, pl.BlockSpec((n_q, n_k), lambda i: (0, 0)),
                  pl.BlockSpec((rows, SWA_W), lambda i: (i, 0))] + list(kv_specs),
        out_specs=pl.BlockSpec((rows, SWA_W), lambda i: (i, 0)),
        scratch_shapes=[pltpu.VMEM((SWA_KV, SWA_G * n_q, n_k), F32)],
        compiler_params=pltpu.CompilerParams(dimension_semantics=("arbitrary",),
                                             vmem_limit_bytes=VMEM_LIMIT),
        name=name,
    )(table, sinks, bucket, q, *kv_args)


def _t5_bucket(rel):
    half = N_BUCKETS // 2
    max_exact = half // 2
    n = jnp.abs(rel)
    large = max_exact + (jnp.log(jnp.maximum(n, 1).astype(F32) / max_exact)
                         / math.log(MAX_DIST / max_exact) * (half - max_exact)).astype(jnp.int32)
    large = jnp.minimum(large, half - 1)
    return jnp.where(rel > 0, half, 0) + jnp.where(n < max_exact, n, large)


def _bucket_map(n_q, n_k, key_offset):
    rel = (jnp.arange(n_k)[None, :] - key_offset) - jnp.arange(n_q)[:, None]
    return _t5_bucket(rel).astype(jnp.int32)


FF_BLOCK = 1024


def _post_kernel(hp_ref, dnp_ref, swp_ref, hs_ref, dns_ref, sws_ref,
                 wod_ref, wos_ref, g1_ref, b1_ref, w1_ref, w2_ref, g2_ref, b2_ref,
                 yp_ref, ys_ref, *, n_prompt_tiles):
    is_prompt = pl.program_id(0) < n_prompt_tiles

    @pl.when(is_prompt)
    def _():
        _post_tile(hp_ref, dnp_ref, swp_ref, wod_ref, wos_ref, g1_ref, b1_ref,
                   w1_ref, w2_ref, g2_ref, b2_ref, yp_ref)

    @pl.when(jnp.logical_not(is_prompt))
    def _():
        _post_tile(hs_ref, dns_ref, sws_ref, wod_ref, wos_ref, g1_ref, b1_ref,
                   w1_ref, w2_ref, g2_ref, b2_ref, ys_ref)


def _post_tile(h_ref, dn_ref, sw_ref, wod_ref, wos_ref, g1_ref, b1_ref,
               w1_ref, w2_ref, g2_ref, b2_ref, y_ref):
    mix = (jnp.dot(dn_ref[...], wod_ref[...], preferred_element_type=F32)
           + jnp.dot(sw_ref[...], wos_ref[...], preferred_element_type=F32))
    h1 = _layer_norm(DEEP_ALPHA * h_ref[...] + mix, g1_ref[...], b1_ref[...])
    h1b = h1.astype(BF16)
    f = jnp.zeros_like(h1)
    for j in range(D_FF // FF_BLOCK):
        a = jnp.dot(h1b, w1_ref[:, j * FF_BLOCK:(j + 1) * FF_BLOCK], preferred_element_type=F32)
        a = jnp.square(jnp.maximum(a, 0.0)).astype(BF16)
        f = f + jnp.dot(a, w2_ref[j * FF_BLOCK:(j + 1) * FF_BLOCK, :], preferred_element_type=F32)
    y_ref[...] = _layer_norm(DEEP_ALPHA * h1 + f, g2_ref[...], b2_ref[...])


def _post(prompt, sample, vecs, weights, tm):
    g1, b1, g2, b2 = vecs
    wo_dn, wo_sw, w1, w2 = weights
    n_p, n_s = prompt[0].shape[0] // tm, sample[0].shape[0] // tm
    const = lambda i: (0, 0)
    vec = pl.BlockSpec((1, D_MODEL), const)
    weight = lambda wt: pl.BlockSpec(wt.shape, const, pipeline_mode=pl.Buffered(1))
    blk_p = lambda width: pl.BlockSpec((tm, width), lambda i: (jnp.minimum(i, n_p - 1), 0))
    blk_s = lambda width: pl.BlockSpec((tm, width), lambda i: (jnp.maximum(i - n_p, 0), 0))
    widths = (D_MODEL, DN_W, SWA_W)
    return pl.pallas_call(
        functools.partial(_post_kernel, n_prompt_tiles=n_p),
        out_shape=[jax.ShapeDtypeStruct((n_p * tm, D_MODEL), F32),
                   jax.ShapeDtypeStruct((n_s * tm, D_MODEL), F32)],
        grid=(n_p + n_s,),
        in_specs=[blk_p(wd) for wd in widths] + [blk_s(wd) for wd in widths]
                 + [weight(wo_dn), weight(wo_sw), vec, vec, weight(w1), weight(w2), vec, vec],
        out_specs=[blk_p(D_MODEL), blk_s(D_MODEL)],
        compiler_params=pltpu.CompilerParams(dimension_semantics=("arbitrary",),
                                             vmem_limit_bytes=VMEM_LIMIT),
        name="post_mlp",
    )(*prompt, *sample, wo_dn, wo_sw, g1, b1, w1, w2, g2, b2)


def kernel(x_prompt, x_sample, state_delta, state_conv, cache_swa_k, cache_swa_v, meta_tokens, ln_in_g, ln_in_b, w_in, w_conv, dn_a_log, dn_dt_bias, dn_norm, swa_sinks, rel_bias_table, w_o, ln1_g, ln1_b, w_ff1, w_ff2, ln2_g, ln2_b):
    assert w_in.shape[0] == DEPTH == 1
    n_seq_s, len_s = x_sample.shape[0], x_sample.shape[1]
    len_p = x_prompt.shape[1]
    cache_len = cache_swa_k.shape[2]
    chunk_p = DELTA_CHUNK
    row = lambda t: t.reshape(1, -1).astype(F32)

    w = w_in[0]
    w_sq = (w[:, OFF_SQ:OFF_SK].astype(BF16).reshape(D_MODEL, SWA_KV, SWA_G, HEAD_DIM)
            .transpose(0, 2, 1, 3).reshape(D_MODEL, SWA_W))
    w_r = (w[:, :OFF_A].astype(BF16), w_sq, w[:, OFF_SK:].astype(BF16),
           jnp.pad(w[:, OFF_A:OFF_SQ].astype(BF16), ((0, 0), (0, LANES - 2 * HEADS))))
    gin, bin_ = row(ln_in_g), row(ln_in_b)
    pad_lanes = lambda t: jnp.pad(t.reshape(1, -1).astype(F32), ((0, 0), (0, LANES - HEADS)))
    lane_head = jnp.arange(DN_W) // HEAD_DIM
    seg = (lane_head[:, None] == lane_head[None, :]).astype(BF16)
    expand = (jnp.arange(LANES)[:, None] == lane_head[None, :]).astype(BF16)
    wconv = jnp.pad(w_conv[0].astype(F32), ((0, SUBLANES - CONV_W), (0, 0)))
    delta_consts = (pad_lanes(dn_a_log[0]), pad_lanes(dn_dt_bias[0]),
                    jnp.tile(dn_norm[0].astype(F32), HEADS).reshape(1, DN_W), seg, expand)
    post_vecs = (row(ln1_g[0]), row(ln1_b[0]), row(ln2_g[0]), row(ln2_b[0]))
    wo_sw = (w_o[0][DN_W:].astype(BF16).reshape(SWA_KV, SWA_G, HEAD_DIM, D_MODEL)
             .transpose(1, 0, 2, 3).reshape(SWA_W, D_MODEL))
    post_w = (w_o[0][:DN_W].astype(BF16), wo_sw, w_ff1[0].astype(BF16), w_ff2[0].astype(BF16))
    table = rel_bias_table.astype(F32).reshape(-1)
    sinks = swa_sinks[0].astype(F32)

    xp = x_prompt[0]
    xs = x_sample.reshape(n_seq_s * len_s, D_MODEL)

    conv_pad = SUBLANES - (CONV_W - 1)
    zero_conv = jnp.zeros((1, SUBLANES, CONV_CH), F32)
    conv_s = jnp.pad(state_conv[0].astype(F32), ((0, 0), (conv_pad, 0), (0, 0)))
    m_qkv, m_z, m_sq, m_sk, m_sv, m_ab, m_tail, _ = _proj(
        meta_tokens.astype(F32), gin, bin_, w_r, zero_conv, wconv, N_META, N_META)
    p_qkv, p_z, p_sq, p_sk, p_sv, p_ab, p_tail, p_h = _proj(
        xp, gin, bin_, w_r, m_tail, wconv, ROW_TILE, None)
    s_qkv, s_z, s_sq, s_sk, s_sv, s_ab, s_tail, s_h = _proj(
        xs, gin, bin_, w_r, conv_s, wconv, ROW_TILE, len_s)

    zero_s = jnp.zeros((1, HEADS, HEAD_DIM, HEAD_DIM), F32)
    _, s_meta = _delta(m_qkv, m_z, m_ab, zero_s, delta_consts, 1, N_META, N_META, 1, 1)
    p_dn, p_s = _delta(p_qkv, p_z, p_ab, s_meta, delta_consts, 1, len_p, chunk_p, *DELTA_PROMPT)
    s_dn, s_s = _delta(s_qkv, s_z, s_ab, state_delta[0].astype(F32), delta_consts,
                       n_seq_s, len_s, len_s, *DELTA_SAMPLE)

    span = WINDOW
    lead = jnp.zeros((span - N_META, KV_W), F32)
    units_p = SWA_UNITS_PROMPT
    rows_p = units_p * chunk_p
    prev_blocks = rows_p // span
    kv_specs_p = [pl.BlockSpec((span, KV_W), lambda i: (0, 0)),
                  pl.BlockSpec((span, KV_W), lambda i: (jnp.maximum(i * prev_blocks - 1, 0), 0)),
                  pl.BlockSpec((rows_p, KV_W), lambda i: (i, 0))] * 2
    p_sw = _swa(table, sinks, _bucket_map(chunk_p, span + chunk_p, span), p_sq,
                [jnp.concatenate([lead, m_sk], axis=0), p_sk, p_sk,
                 jnp.concatenate([lead, m_sv], axis=0), p_sv, p_sv],
                kv_specs_p, len_p // rows_p, chunk_p, units_p, True, "swa_prompt")
    ck = cache_swa_k[0].astype(F32).reshape(n_seq_s * cache_len, KV_W)
    cv = cache_swa_v[0].astype(F32).reshape(n_seq_s * cache_len, KV_W)
    units_s = SWA_UNITS_SAMPLE
    kv_specs_s = [pl.BlockSpec((units_s * cache_len, KV_W), lambda i: (i, 0)),
                  pl.BlockSpec((units_s * len_s, KV_W), lambda i: (i, 0))] * 2
    s_sw = _swa(table, sinks, _bucket_map(len_s, cache_len + len_s, cache_len), s_sq,
                [ck, s_sk, cv, s_sv], kv_specs_s, n_seq_s // units_s, len_s, units_s, False,
                "swa_sample")

    y_p, y_s = _post((p_h, p_dn, p_sw), (s_h, s_dn, s_sw), post_vecs, post_w, ROW_TILE)

    kv_shape = lambda n, length: (1, n, length, SWA_KV, HEAD_DIM)
    return (y_p[None], y_s.reshape(x_sample.shape),
            p_s[None], p_tail[:, -(CONV_W - 1):][None],
            p_sk[-WINDOW:].reshape(kv_shape(1, WINDOW)), p_sv[-WINDOW:].reshape(kv_shape(1, WINDOW)),
            s_s[None], s_tail[:, -(CONV_W - 1):][None],
            s_sk.reshape(kv_shape(n_seq_s, len_s)), s_sv.reshape(kv_shape(n_seq_s, len_s)))
```

```python
import functools
import math

import jax
import jax.numpy as jnp
from jax import lax
from jax.experimental import pallas as pl
from jax.experimental.pallas import tpu as pltpu

F32 = jnp.float32
BF16 = jnp.bfloat16

D_MODEL = 1024
N_META = 16
HEADS = 8
HEAD_DIM = 64
DN_W = HEADS * HEAD_DIM
CONV_W = 4
CONV_CH = 3 * DN_W
SWA_HEADS = 8
SWA_KV = 2
SWA_G = SWA_HEADS // SWA_KV
KV_W = SWA_KV * HEAD_DIM
WINDOW = 128
N_BUCKETS = 32
MAX_DIST = 128
D_FF = 4 * D_MODEL
DEPTH = 1
DEEP_ALPHA = (2 * DEPTH) ** 0.25
LN_EPS = 1e-5
NORM_EPS = 1e-6
LOG2E = math.log2(math.e)
LANES = 128
SUBLANES = 8
VMEM_LIMIT = 56 * 1024 * 1024

OFF_Z = CONV_CH
OFF_A = OFF_Z + DN_W
OFF_B = OFF_A + HEADS
OFF_SQ = OFF_B + HEADS
OFF_SK = OFF_SQ + SWA_HEADS * HEAD_DIM
PROJ_SPLITS = (CONV_CH, DN_W, SWA_HEADS * HEAD_DIM, KV_W, KV_W, LANES)

ROW_TILE = 512
DELTA_CHUNK = 64
DELTA_PROMPT = (4, 2)
DELTA_SAMPLE = (8, 2)
SWA_UNITS_PROMPT = 16
SWA_UNITS_SAMPLE = 8
SWA_BATCH = 4


def _layer_norm(x, g, b):
    mu = jnp.mean(x, axis=-1, keepdims=True)
    xc = x - mu
    var = jnp.mean(xc * xc, axis=-1, keepdims=True)
    return xc * lax.rsqrt(var + LN_EPS) * g + b


def _mm(a, b):
    return jnp.dot(a.astype(BF16), b.astype(BF16), preferred_element_type=F32)


def _mm_nt(a, b):
    return lax.dot_general(a.astype(BF16), b.astype(BF16), (((1,), (1,)), ((), ())),
                           preferred_element_type=F32)


def _split3(x):
    x1 = x.astype(BF16)
    r1 = x - x1.astype(F32)
    x2 = r1.astype(BF16)
    x3 = (r1 - x2.astype(F32)).astype(BF16)
    return x1, x2, x3


def _dot_exact_lhs(x, sel):
    return jnp.dot(jnp.concatenate(_split3(x), axis=1), jnp.concatenate([sel] * 3, axis=0),
                   preferred_element_type=F32)


CONV_BLOCK = 256


def _sigmoid(x):
    return 1.0 / (1.0 + jnp.exp(-x))


def _proj_kernel(x_ref, g_ref, b_ref, w_ref, wsq_ref, wkv_ref, wab_ref, conv0_ref, wconv_ref,
                 y_ref, z_ref, sq_ref, sk_ref, sv_ref, ab_ref, tail_ref, h_ref, xbuf, *, seq_len):
    tm = x_ref.shape[0]
    h_ref[...] = _layer_norm(x_ref[...], g_ref[...], b_ref[...])
    h = h_ref[...].astype(BF16)

    chained = seq_len is None
    n_seq = 1 if chained else tm // seq_len
    L = tm if chained else seq_len
    if chained:
        @pl.when(pl.program_id(0) == 0)
        def _():
            xbuf[0, 0:SUBLANES, :] = conv0_ref[0]
    else:
        for s in range(n_seq):
            xbuf[s, 0:SUBLANES, :] = conv0_ref[s]
    blocks = [slice(cb, cb + CONV_BLOCK) for cb in range(0, CONV_CH, CONV_BLOCK)]

    def project(cols):
        raw = jnp.dot(h, w_ref[:, cols], preferred_element_type=F32)
        for s in range(n_seq):
            xbuf[s, SUBLANES:SUBLANES + L, cols] = raw[s * L:(s + 1) * L, :]

    def conv(cols):
        for s in range(n_seq):
            ext = xbuf[s, :, cols]
            w0, w1, w2, w3 = (0.5 * wconv_ref[j:j + 1, cols] for j in range(CONV_W))
            ext1 = pltpu.roll(ext, 1, axis=0)
            half = (w3 * ext + w2 * ext1)[SUBLANES:, :]
            half = half + pltpu.roll(w1 * ext + w0 * ext1, 2, axis=0)[SUBLANES:, :]
            y_ref[s * L:(s + 1) * L, cols] = half * (1.0 + jnp.tanh(half))
            tail_ref[s, :, cols] = xbuf[s, L:L + SUBLANES, cols]
            if chained:
                xbuf[s, 0:SUBLANES, cols] = xbuf[s, L:L + SUBLANES, cols]

    for o_ref, rhs in ((z_ref, w_ref[:, CONV_CH:]), (sq_ref, wsq_ref[...]),
                       (sk_ref, wkv_ref[:, :KV_W]), (sv_ref, wkv_ref[:, KV_W:]),
                       (ab_ref, wab_ref[...])):
        o_ref[...] = jnp.dot(h, rhs, preferred_element_type=F32)
    project(blocks[0])
    for bi, cols in enumerate(blocks):
        if bi + 1 < len(blocks):
            project(blocks[bi + 1])
        conv(cols)


def _proj(x, g, b, weights, conv0, wconv, tm, seq_len):
    rows = x.shape[0]
    w_main, w_sq, w_kv, w_ab = weights
    chained = seq_len is None
    n_tile_seq = 1 if chained else tm // seq_len
    n_seq = 1 if chained else rows // seq_len
    const = lambda i: (0, 0)
    row_blk = lambda width: pl.BlockSpec((tm, width), lambda i: (i, 0))
    seq_blk = pl.BlockSpec((n_tile_seq, SUBLANES, CONV_CH),
                           (lambda i: (0, 0, 0)) if chained else (lambda i: (i, 0, 0)))
    return pl.pallas_call(
        functools.partial(_proj_kernel, seq_len=seq_len),
        out_shape=[jax.ShapeDtypeStruct((rows, width), F32) for width in PROJ_SPLITS]
                  + [jax.ShapeDtypeStruct((n_seq, SUBLANES, CONV_CH), F32),
                     jax.ShapeDtypeStruct((rows, D_MODEL), F32)],
        grid=(rows // tm,),
        in_specs=[row_blk(D_MODEL), pl.BlockSpec((1, D_MODEL), const),
                  pl.BlockSpec((1, D_MODEL), const)]
                 + [pl.BlockSpec(wt.shape, const) for wt in weights]
                 + [seq_blk, pl.BlockSpec((SUBLANES, CONV_CH), const)],
        out_specs=[row_blk(width) for width in PROJ_SPLITS] + [seq_blk, row_blk(D_MODEL)],
        scratch_shapes=[pltpu.VMEM((n_tile_seq, SUBLANES + (tm if chained else seq_len), CONV_CH),
                                   F32)],
        compiler_params=pltpu.CompilerParams(dimension_semantics=("arbitrary",),
                                             vmem_limit_bytes=VMEM_LIMIT),
        name="ln_in_proj",
    )(x, g, b, w_main, w_sq, w_kv, w_ab, conv0, wconv)


GROUP_HEADS = 4


def _softplus(x):
    return jnp.maximum(x, 0.0) + jnp.log1p(jnp.exp(-jnp.abs(x)))


def _delta_kernel(qkv_ref, z_ref, ab_ref, s0_ref, alog_ref, dtb_ref,
                  normw_ref, seg_ref, expand_ref, o_ref, s_ref, sbd,
                  *, chunk, units, groups, sequential):
    C, U = chunk, units
    RG = U * C
    PAIRS = HEADS // GROUP_HEADS
    PW = GROUP_HEADS * HEAD_DIM
    step = pl.program_id(0)
    last_step = pl.num_programs(0) - 1
    expand = expand_ref[...]

    def seg_sum(t):
        tb = t.astype(BF16)
        return jnp.concatenate(
            [jnp.dot(tb[:, g * PW:(g + 1) * PW], seg_ref[g * PW:(g + 1) * PW, g * PW:(g + 1) * PW],
                     preferred_element_type=F32) for g in range(PAIRS)], axis=1)

    ri = lax.broadcasted_iota(jnp.int32, (C, PW), 0)
    ci = lax.broadcasted_iota(jnp.int32, (C, PW), 1) % HEAD_DIM
    causal = (ri >= ci) & (ci < C)
    strict = (ri > ci) & (ci < C)
    eye = (ri == ci).astype(F32)
    lane_head = lax.broadcasted_iota(jnp.int32, (1, PW), 1) // HEAD_DIM
    rs = lax.broadcasted_iota(jnp.int32, (LANES, LANES), 0)
    cs = lax.broadcasted_iota(jnp.int32, (LANES, LANES), 1)
    half_mask = ((rs // HEAD_DIM) == (cs // HEAD_DIM)).astype(F32)
    row_pad = [jnp.zeros((HEAD_DIM - C, PW), BF16)] if C < HEAD_DIM else []
    lane_pad = [jnp.zeros((1, HEAD_DIM - C), F32)] if C < HEAD_DIM else []
    rr = lax.broadcasted_iota(jnp.int32, (RG, RG), 0)
    cc = lax.broadcasted_iota(jnp.int32, (RG, RG), 1)
    tril = ((rr // C == cc // C) & (rr >= cc)).astype(BF16)

    def stack(y):
        yb = y.astype(BF16)
        zero = jnp.zeros_like(yb)
        pad = row_pad if y.shape[0] < HEAD_DIM else []
        blocks = []
        for h in range(GROUP_HEADS):
            blocks += [jnp.where(lane_head == h, yb, zero)] + pad
        return jnp.concatenate(blocks, axis=0)

    items = [(u, p) for u in range(U) for p in range(PAIRS)]
    sl = lambda u, p: (slice(u * C, (u + 1) * C), slice(p * PW, (p + 1) * PW))

    def front(gi):
        rg = slice(gi * RG, (gi + 1) * RG)
        q, k, v = (qkv_ref[rg, j * DN_W:(j + 1) * DN_W] for j in range(3))
        qn = q * (lax.rsqrt(seg_sum(q * q) + NORM_EPS) * HEAD_DIM ** -0.5)
        kn = k * lax.rsqrt(seg_sum(k * k) + NORM_EPS)
        yield

        ab = ab_ref[rg, :]
        g = -jnp.exp(alog_ref[...]) * _softplus(ab + dtb_ref[...])
        gs = jnp.dot(tril, jnp.concatenate(_split3(g), axis=1), preferred_element_type=F32)
        G = (gs[:, :LANES] + gs[:, LANES:2 * LANES] + gs[:, 2 * LANES:]) * LOG2E
        GT = G.T
        G_x = _dot_exact_lhs(G, expand)
        beta = _sigmoid(pltpu.roll(ab, LANES - HEADS, axis=1))
        b_hi = beta.astype(BF16)
        b_lo = (beta - b_hi.astype(F32)).astype(BF16)
        beta_x = jnp.dot(jnp.concatenate([b_hi, b_lo], axis=1),
                         jnp.concatenate([expand, expand], axis=0), preferred_element_type=F32)
        glast_x = jnp.concatenate(
            [jnp.broadcast_to(G_x[(u + 1) * C - 1:(u + 1) * C, :], (C, DN_W)) for u in range(U)],
            axis=0)
        yield
        eg_x = jnp.exp2(G_x)
        kbeta = kn * beta_x
        return dict(qn=qn, kn=kn, kbeta=kbeta, bk=kbeta * eg_x, bv=v * beta_x, qd=qn * eg_x,
                    kd=kn * jnp.exp2(glast_x - G_x), G_x=G_x, GT=GT, eg_x=eg_x)

    def middle(f):
        A, QK = {}, {}
        for (u, p) in items:
            ru, lp = sl(u, p)
            gr = []
            for h in range(p * GROUP_HEADS, (p + 1) * GROUP_HEADS):
                gr += [f["GT"][h:h + 1, ru]] + lane_pad
            gr = jnp.concatenate(gr, axis=1)
            decay = jnp.exp2(jnp.where(causal, f["G_x"][ru, lp] - gr, -jnp.inf))
            r = _mm_nt(jnp.concatenate([f["kbeta"][ru, lp], f["qn"][ru, lp]], axis=0),
                       stack(f["kn"][ru, lp]))
            A[u, p] = jnp.where(strict, r[:C] * decay, 0.0)
            QK[u, p] = r[C:] * decay
        yield

        n_pow = int(math.log2(C)) - 1
        P = {it: eye - A[it] for it in items}
        Ap = {it: _mm(A[it], stack(A[it])) for it in items}
        yield
        for i in range(n_pow):
            if i < n_pow - 1:
                r = {it: _mm(jnp.concatenate([Ap[it], P[it]], axis=0), stack(Ap[it])) for it in items}
                Ap = {it: r[it][:C] for it in items}
                P = {it: P[it] + r[it][C:] for it in items}
            else:
                P = {it: P[it] + _mm(P[it], stack(Ap[it])) for it in items}
            yield

        halves = [slice(hb * LANES, (hb + 1) * LANES) for hb in range(PW // LANES)]
        wu = {it: _mm(P[it], jnp.concatenate([stack(f["bk"][sl(*it)]), stack(f["bv"][sl(*it)])],
                                             axis=1)) for it in items}
        kdT = {(it, hb): f["kd"][sl(*it)][:, cols].T for it in items
               for hb, cols in enumerate(halves)}
        yield
        mn = {(it, hb): _mm(kdT[it, hb], jnp.concatenate(
            [wu[it][:, cols], wu[it][:, PW:][:, cols]], axis=1))
              for it in items for hb, cols in enumerate(halves)}
        yield
        n_t = {key: (half_mask * mn[key][:, LANES:]).T for key in mn}
        out = {}
        for (u, p) in items:
            ru, lp = sl(u, p)
            it = (u, p)
            out[it] = dict(
                W=wu[it][:, :PW], U=wu[it][:, PW:],
                M=diag_blocks([half_mask * mn[it, hb][:, :LANES]
                               for hb in range(len(halves))]).astype(BF16),
                NT=jnp.concatenate([n_t[it, hb][:HEAD_DIM] + n_t[it, hb][HEAD_DIM:]
                                    for hb in range(len(halves))], axis=1),
                QK=QK[it], qd=f["qd"][ru, lp],
                eg_last=f["eg_x"][(u + 1) * C - 1:(u + 1) * C, lp])
        return out

    def diag_blocks(blocks):
        n = len(blocks)
        rows = []
        for i, blk in enumerate(blocks):
            rows.append(jnp.concatenate(
                [blk if j == i else jnp.zeros_like(blk) for j in range(n)], axis=1))
        return jnp.concatenate(rows, axis=0)

    def load_state(ref, prefix, p):
        return jnp.concatenate([ref[prefix + (p * GROUP_HEADS + h,)].T
                                for h in range(GROUP_HEADS)], axis=1)

    def store_state(ref, prefix, p, st):
        for h in range(GROUP_HEADS):
            ref[prefix + (p * GROUP_HEADS + h,)] = st[:, h * HEAD_DIM:(h + 1) * HEAD_DIM].T

    chunks = [(gi, u) for gi in range(groups) for u in range(U)]
    if sequential:
        @pl.when(step == 0)
        def _():
            for p in range(PAIRS):
                sbd[p] = load_state(s0_ref, (), p)
        s_start = {p: sbd[p] for p in range(PAIRS)}
    else:
        s_start = {(idx, p): load_state(s0_ref, (idx,), p)
                   for idx in range(len(chunks)) for p in range(PAIRS)}

    def chain(gi, out, s_in):
        if sequential:
            states = {p: [s_in[p]] for p in range(PAIRS)}
            for u in range(U):
                for p in range(PAIRS):
                    d, s_cur = out[u, p], states[p][-1]
                    states[p].append(s_cur * d["eg_last"] - _mm_nt(s_cur, d["M"]) + d["NT"])
            return ({(u, p): states[p][u] for (u, p) in items},
                    {p: states[p][-1] for p in range(PAIRS)})
        entering = {(u, p): s_in[gi * U + u, p] for (u, p) in items}
        s_out = {}
        for (u, p) in items:
            d, s_cur = out[u, p], entering[u, p]
            s_out[gi * U + u, p] = s_cur * d["eg_last"] - _mm_nt(s_cur, d["M"]) + d["NT"]
        return entering, s_out

    def finish(gi, out, entering):
        r = {it: _mm_nt(jnp.concatenate([out[it]["W"], out[it]["qd"]], axis=0),
                        stack(entering[it])) for it in items}
        yield
        tiles = {it: r[it][C:] + _mm(out[it]["QK"], stack(out[it]["U"] - r[it][:C]))
                 for it in items}
        yield
        rg = slice(gi * RG, (gi + 1) * RG)
        o = jnp.concatenate([jnp.concatenate([tiles[u, p] for p in range(PAIRS)], axis=1)
                             for u in range(U)], axis=0)
        ms = seg_sum(o * o) * (1.0 / HEAD_DIM)
        hz = 0.5 * z_ref[rg, :]
        o_ref[rg, :] = (o * lax.rsqrt(ms + NORM_EPS) * normw_ref[...]
                        * (hz * (1.0 + jnp.tanh(hz)))).astype(BF16)

    def drive(gens):
        results = [None] * len(gens)
        active = list(enumerate(gens))
        while active:
            still = []
            for i, gen in active:
                try:
                    next(gen)
                    still.append((i, gen))
                except StopIteration as stop:
                    results[i] = stop.value
            active = still
        return results

    state = s_start
    s_final = {}
    f_cur = drive([front(0)])[0]
    pending = None
    for gi in range(groups + 1):
        gens = []
        if gi < groups:
            gens.append(middle(f_cur))
        if gi + 1 < groups:
            gens.append(front(gi + 1))
        if pending is not None:
            entering, s_out = chain(gi - 1, pending, state)
            s_final.update(s_out)
            state = s_out if sequential else state
            gens.append(finish(gi - 1, pending, entering))
        res = drive(gens)
        pending = res[0] if gi < groups else None
        f_cur = res[1] if gi + 1 < groups else None

    if sequential:
        for p in range(PAIRS):
            sbd[p] = s_final[p]

        @pl.when(step == last_step)
        def _():
            for p in range(PAIRS):
                store_state(s_ref, (), p, s_final[p])
    else:
        for (idx, p), s_new in s_final.items():
            store_state(s_ref, (idx,), p, s_new)


def _delta(qkv, z, ab, s0, consts, n_seq, seq_len, chunk, units, groups):
    alog, dtb, normw, seg, expand = consts
    sequential = n_seq == 1
    rows = n_seq * seq_len
    n_chunks = units * groups
    blk_rows = n_chunks * chunk
    assert rows % blk_rows == 0 and (sequential or seq_len == chunk)
    const2 = lambda shape: pl.BlockSpec(shape, lambda i: (0, 0))
    tok = lambda width: pl.BlockSpec((blk_rows, width), lambda i: (i, 0))
    group_w = GROUP_HEADS * HEAD_DIM
    if sequential:
        state = pl.BlockSpec((None, HEADS, HEAD_DIM, HEAD_DIM), lambda i: (0, 0, 0, 0))
    else:
        state = pl.BlockSpec((n_chunks, HEADS, HEAD_DIM, HEAD_DIM), lambda i: (i, 0, 0, 0))
    return pl.pallas_call(
        functools.partial(_delta_kernel, chunk=chunk, units=units, groups=groups,
                          sequential=sequential),
        out_shape=[jax.ShapeDtypeStruct((rows, DN_W), BF16),
                   jax.ShapeDtypeStruct((n_seq, HEADS, HEAD_DIM, HEAD_DIM), F32)],
        grid=(rows // blk_rows,),
        in_specs=[tok(CONV_CH), tok(DN_W), tok(LANES), state,
                  const2((1, LANES)), const2((1, LANES)),
                  const2((1, DN_W)), const2((DN_W, DN_W)), const2((LANES, DN_W))],
        out_specs=[tok(DN_W), state],
        scratch_shapes=[pltpu.VMEM((HEADS // GROUP_HEADS, HEAD_DIM, group_w), F32)],
        compiler_params=pltpu.CompilerParams(
            dimension_semantics=("arbitrary",), vmem_limit_bytes=VMEM_LIMIT,
        ),
        name=f"delta_c{chunk}",
    )(qkv, z, ab, s0, alog, dtb, normw, seg, expand)


SWA_W = SWA_HEADS * HEAD_DIM


def _swa_kernel(*refs, n_q, units, prompt, batch=SWA_BATCH):
    if prompt:
        (table_ref, sinks_ref, bucket_ref, q_ref, km_ref, kp_ref, kc_ref, vm_ref, vp_ref, vc_ref,
         o_ref, bias_ref) = refs
    else:
        (table_ref, sinks_ref, bucket_ref, q_ref, kp_ref, kc_ref, vp_ref, vc_ref,
         o_ref, bias_ref) = refs
    n_k = bias_ref.shape[-1]
    n_grp = SWA_W // KV_W
    step = pl.program_id(0)

    @pl.when(step == 0)
    def _():
        bucket = bucket_ref[...]
        for h in range(SWA_HEADS):
            acc = jnp.zeros((n_q, n_k), F32)
            for b in range(N_BUCKETS):
                acc = jnp.where(bucket == b, table_ref[b * SWA_HEADS + h], acc)
            kv, gi = divmod(h, SWA_G)
            bias_ref[kv, gi * n_q:(gi + 1) * n_q, :] = acc * LOG2E

    lane = lax.broadcasted_iota(jnp.int32, (1, KV_W), 1)
    kv_mask = [(lane < HEAD_DIM).astype(F32), (lane >= HEAD_DIM).astype(F32)]
    if prompt:
        first = step == 0
        k_all = jnp.concatenate([jnp.where(first, km_ref[...], kp_ref[...]), kc_ref[...]], axis=0)
        v_all = jnp.concatenate([jnp.where(first, vm_ref[...], vp_ref[...]), vc_ref[...]], axis=0)
        k_kv = [(k_all * m).astype(BF16) for m in kv_mask]
        v_kv = [(v_all * m).astype(BF16) for m in kv_mask]
        keys = lambda j, kv: k_kv[kv][j * n_q:j * n_q + n_k]
        vals = lambda j, kv: v_kv[kv][j * n_q:j * n_q + n_k]
    else:
        n_c = n_k - n_q
        cat = lambda a, b, j: jnp.concatenate([a[j * n_c:(j + 1) * n_c, :],
                                               b[j * n_q:(j + 1) * n_q, :]], axis=0)
        keys = lambda j, kv: (cat(kp_ref, kc_ref, j) * kv_mask[kv]).astype(BF16)
        vals = lambda j, kv: (cat(vp_ref, vc_ref, j) * kv_mask[kv]).astype(BF16)

    q = (q_ref[...] * (HEAD_DIM ** -0.5 * LOG2E)).astype(BF16)
    ones_k = jnp.ones((n_k, KV_W), BF16)
    sink = [jnp.concatenate([jnp.full((n_q, KV_W), sinks_ref[kv * SWA_G + gi] * LOG2E, F32)
                             for gi in range(SWA_G)], axis=0) for kv in range(SWA_KV)]
    rows4 = SWA_G * n_q
    for j0 in range(0, units, batch):
        js = range(j0, min(j0 + batch, units))
        items = [(j, kv) for j in js for kv in range(SWA_KV)]
        s = {}
        for j in js:
            rows = slice(j * n_q, (j + 1) * n_q)
            q4 = jnp.concatenate([q[rows, g * KV_W:(g + 1) * KV_W] for g in range(n_grp)], axis=0)
            for kv in range(SWA_KV):
                sj = lax.dot_general(q4, keys(j, kv), (((1,), (1,)), ((), ())),
                                     preferred_element_type=F32) + bias_ref[kv]
                if prompt and j * n_q < WINDOW - N_META:
                    key_pos = j * n_q + lax.broadcasted_iota(jnp.int32, (1, n_k), 1)
                    n_invalid = jnp.where(first, WINDOW - N_META, 0)
                    sj = jnp.where(key_pos < n_invalid, -jnp.inf, sj)
                s[j, kv] = sj
        m = {it: jnp.maximum(jnp.broadcast_to(jnp.max(s[it], axis=-1, keepdims=True),
                                              (rows4, KV_W)), sink[it[1]]) for it in items}
        p = {it: jnp.exp2(s[it] - jnp.concatenate([m[it], m[it][:, :n_k - KV_W]], axis=1))
             for it in items}
        pv = {it: jnp.dot(p[it].astype(BF16), jnp.concatenate([vals(*it), ones_k], axis=1),
                          preferred_element_type=F32) for it in items}
        o = {it: pv[it][:, :KV_W] * (1.0 / (pv[it][:, KV_W:] + jnp.exp2(sink[it[1]] - m[it])))
             for it in items}
        for j in js:
            o2 = o[j, 0] + o[j, 1]
            for g in range(n_grp):
                o_ref[j * n_q:(j + 1) * n_q, g * KV_W:(g + 1) * KV_W] = (
                    o2[g * n_q:(g + 1) * n_q, :].astype(BF16))


def _swa(table, sinks, bucket, q, kv_args, kv_specs, n_steps, n_q, units, prompt, name):
    n_k = bucket.shape[1]
    smem = pl.BlockSpec(memory_space=pltpu.SMEM)
    rows = units * n_q
    return pl.pallas_call(
        functools.partial(_swa_kernel, n_q=n_q, units=units, prompt=prompt),
        out_shape=jax.ShapeDtypeStruct((n_steps * rows, SWA_W), BF16),
        grid=(n_steps,),
        in_specs=[smem, smem, pl.BlockSpec((n_q, n_k), lambda i: (0, 0)),
                  pl.BlockSpec((rows, SWA_W), lambda i: (i, 0))] + list(kv_specs),
        out_specs=pl.BlockSpec((rows, SWA_W), lambda i: (i, 0)),
        scratch_shapes=[pltpu.VMEM((SWA_KV, SWA_G * n_q, n_k), F32)],
        compiler_params=pltpu.CompilerParams(dimension_semantics=("arbitrary",),
                                             vmem_limit_bytes=VMEM_LIMIT),
        name=name,
    )(table, sinks, bucket, q, *kv_args)


def _t5_bucket(rel):
    half = N_BUCKETS // 2
    max_exact = half // 2
    n = jnp.abs(rel)
    large = max_exact + (jnp.log(jnp.maximum(n, 1).astype(F32) / max_exact)
                         / math.log(MAX_DIST / max_exact) * (half - max_exact)).astype(jnp.int32)
    large = jnp.minimum(large, half - 1)
    return jnp.where(rel > 0, half, 0) + jnp.where(n < max_exact, n, large)


def _bucket_map(n_q, n_k, key_offset):
    rel = (jnp.arange(n_k)[None, :] - key_offset) - jnp.arange(n_q)[:, None]
    return _t5_bucket(rel).astype(jnp.int32)


FF_BLOCK = 1024


def _post_kernel(hp_ref, dnp_ref, swp_ref, hs_ref, dns_ref, sws_ref,
                 wod_ref, wos_ref, g1_ref, b1_ref, w1_ref, w2_ref, g2_ref, b2_ref,
                 yp_ref, ys_ref, *, n_prompt_tiles):
    is_prompt = pl.program_id(0) < n_prompt_tiles

    @pl.when(is_prompt)
    def _():
        _post_tile(hp_ref, dnp_ref, swp_ref, wod_ref, wos_ref, g1_ref, b1_ref,
                   w1_ref, w2_ref, g2_ref, b2_ref, yp_ref)

    @pl.when(jnp.logical_not(is_prompt))
    def _():
        _post_tile(hs_ref, dns_ref, sws_ref, wod_ref, wos_ref, g1_ref, b1_ref,
                   w1_ref, w2_ref, g2_ref, b2_ref, ys_ref)


def _post_tile(h_ref, dn_ref, sw_ref, wod_ref, wos_ref, g1_ref, b1_ref,
               w1_ref, w2_ref, g2_ref, b2_ref, y_ref):
    mix = (jnp.dot(dn_ref[...], wod_ref[...], preferred_element_type=F32)
           + jnp.dot(sw_ref[...], wos_ref[...], preferred_element_type=F32))
    h1 = _layer_norm(DEEP_ALPHA * h_ref[...] + mix, g1_ref[...], b1_ref[...])
    h1b = h1.astype(BF16)
    f = jnp.zeros_like(h1)
    for j in range(D_FF // FF_BLOCK):
        a = jnp.dot(h1b, w1_ref[:, j * FF_BLOCK:(j + 1) * FF_BLOCK], preferred_element_type=F32)
        a = jnp.square(jnp.maximum(a, 0.0)).astype(BF16)
        f = f + jnp.dot(a, w2_ref[j * FF_BLOCK:(j + 1) * FF_BLOCK, :], preferred_element_type=F32)
    y_ref[...] = _layer_norm(DEEP_ALPHA * h1 + f, g2_ref[...], b2_ref[...])


def _post(prompt, sample, vecs, weights, tm):
    g1, b1, g2, b2 = vecs
    wo_dn, wo_sw, w1, w2 = weights
    n_p, n_s = prompt[0].shape[0] // tm, sample[0].shape[0] // tm
    const = lambda i: (0, 0)
    vec = pl.BlockSpec((1, D_MODEL), const)
    weight = lambda wt: pl.BlockSpec(wt.shape, const, pipeline_mode=pl.Buffered(1))
    blk_p = lambda width: pl.BlockSpec((tm, width), lambda i: (jnp.minimum(i, n_p - 1), 0))
    blk_s = lambda width: pl.BlockSpec((tm, width), lambda i: (jnp.maximum(i - n_p, 0), 0))
    widths = (D_MODEL, DN_W, SWA_W)
    return pl.pallas_call(
        functools.partial(_post_kernel, n_prompt_tiles=n_p),
        out_shape=[jax.ShapeDtypeStruct((n_p * tm, D_MODEL), F32),
                   jax.ShapeDtypeStruct((n_s * tm, D_MODEL), F32)],
        grid=(n_p + n_s,),
        in_specs=[blk_p(wd) for wd in widths] + [blk_s(wd) for wd in widths]
                 + [weight(wo_dn), weight(wo_sw), vec, vec, weight(w1), weight(w2), vec, vec],
        out_specs=[blk_p(D_MODEL), blk_s(D_MODEL)],
        compiler_params=pltpu.CompilerParams(dimension_semantics=("arbitrary",),
                                             vmem_limit_bytes=VMEM_LIMIT),
        name="post_mlp",
    )(*prompt, *sample, wo_dn, wo_sw, g1, b1, w1, w2, g2, b2)


def kernel(x_prompt, x_sample, state_delta, state_conv, cache_swa_k, cache_swa_v, meta_tokens, ln_in_g, ln_in_b, w_in, w_conv, dn_a_log, dn_dt_bias, dn_norm, swa_sinks, rel_bias_table, w_o, ln1_g, ln1_b, w_ff1, w_ff2, ln2_g, ln2_b):
    assert w_in.shape[0] == DEPTH == 1
    n_seq_s, len_s = x_sample.shape[0], x_sample.shape[1]
    len_p = x_prompt.shape[1]
    cache_len = cache_swa_k.shape[2]
    chunk_p = DELTA_CHUNK
    row = lambda t: t.reshape(1, -1).astype(F32)

    w = w_in[0]
    w_sq = (w[:, OFF_SQ:OFF_SK].astype(BF16).reshape(D_MODEL, SWA_KV, SWA_G, HEAD_DIM)
            .transpose(0, 2, 1, 3).reshape(D_MODEL, SWA_W))
    w_r = (w[:, :OFF_A].astype(BF16), w_sq, w[:, OFF_SK:].astype(BF16),
           jnp.pad(w[:, OFF_A:OFF_SQ].astype(BF16), ((0, 0), (0, LANES - 2 * HEADS))))
    gin, bin_ = row(ln_in_g), row(ln_in_b)
    pad_lanes = lambda t: jnp.pad(t.reshape(1, -1).astype(F32), ((0, 0), (0, LANES - HEADS)))
    lane_head = jnp.arange(DN_W) // HEAD_DIM
    seg = (lane_head[:, None] == lane_head[None, :]).astype(BF16)
    expand = (jnp.arange(LANES)[:, None] == lane_head[None, :]).astype(BF16)
    wconv = jnp.pad(w_conv[0].astype(F32), ((0, SUBLANES - CONV_W), (0, 0)))
    delta_consts = (pad_lanes(dn_a_log[0]), pad_lanes(dn_dt_bias[0]),
                    jnp.tile(dn_norm[0].astype(F32), HEADS).reshape(1, DN_W), seg, expand)
    post_vecs = (row(ln1_g[0]), row(ln1_b[0]), row(ln2_g[0]), row(ln2_b[0]))
    wo_sw = (w_o[0][DN_W:].astype(BF16).reshape(SWA_KV, SWA_G, HEAD_DIM, D_MODEL)
             .transpose(1, 0, 2, 3).reshape(SWA_W, D_MODEL))
    post_w = (w_o[0][:DN_W].astype(BF16), wo_sw, w_ff1[0].astype(BF16), w_ff2[0].astype(BF16))
    table = rel_bias_table.astype(F32).reshape(-1)
    sinks = swa_sinks[0].astype(F32)

    xp = x_prompt[0]
    xs = x_sample.reshape(n_seq_s * len_s, D_MODEL)

    conv_pad = SUBLANES - (CONV_W - 1)
    zero_conv = jnp.zeros((1, SUBLANES, CONV_CH), F32)
    conv_s = jnp.pad(state_conv[0].astype(F32), ((0, 0), (conv_pad, 0), (0, 0)))
    m_qkv, m_z, m_sq, m_sk, m_sv, m_ab, m_tail, _ = _proj(
        meta_tokens.astype(F32), gin, bin_, w_r, zero_conv, wconv, N_META, N_META)
    p_qkv, p_z, p_sq, p_sk, p_sv, p_ab, p_tail, p_h = _proj(
        xp, gin, bin_, w_r, m_tail, wconv, ROW_TILE, None)
    s_qkv, s_z, s_sq, s_sk, s_sv, s_ab, s_tail, s_h = _proj(
        xs, gin, bin_, w_r, conv_s, wconv, ROW_TILE, len_s)

    zero_s = jnp.zeros((1, HEADS, HEAD_DIM, HEAD_DIM), F32)
    _, s_meta = _delta(m_qkv, m_z, m_ab, zero_s, delta_consts, 1, N_META, N_META, 1, 1)
    p_dn, p_s = _delta(p_qkv, p_z, p_ab, s_meta, delta_consts, 1, len_p, chunk_p, *DELTA_PROMPT)
    s_dn, s_s = _delta(s_qkv, s_z, s_ab, state_delta[0].astype(F32), delta_consts,
                       n_seq_s, len_s, len_s, *DELTA_SAMPLE)

    span = WINDOW
    lead = jnp.zeros((span - N_META, KV_W), F32)
    units_p = SWA_UNITS_PROMPT
    rows_p = units_p * chunk_p
    prev_blocks = rows_p // span
    kv_specs_p = [pl.BlockSpec((span, KV_W), lambda i: (0, 0)),
                  pl.BlockSpec((span, KV_W), lambda i: (jnp.maximum(i * prev_blocks - 1, 0), 0)),
                  pl.BlockSpec((rows_p, KV_W), lambda i: (i, 0))] * 2
    p_sw = _swa(table, sinks, _bucket_map(chunk_p, span + chunk_p, span), p_sq,
                [jnp.concatenate([lead, m_sk], axis=0), p_sk, p_sk,
                 jnp.concatenate([lead, m_sv], axis=0), p_sv, p_sv],
                kv_specs_p, len_p // rows_p, chunk_p, units_p, True, "swa_prompt")
    ck = cache_swa_k[0].astype(F32).reshape(n_seq_s * cache_len, KV_W)
    cv = cache_swa_v[0].astype(F32).reshape(n_seq_s * cache_len, KV_W)
    units_s = SWA_UNITS_SAMPLE
    kv_specs_s = [pl.BlockSpec((units_s * cache_len, KV_W), lambda i: (i, 0)),
                  pl.BlockSpec((units_s * len_s, KV_W), lambda i: (i, 0))] * 2
    s_sw = _swa(table, sinks, _bucket_map(len_s, cache_len + len_s, cache_len), s_sq,
                [ck, s_sk, cv, s_sv], kv_specs_s, n_seq_s // units_s, len_s, units_s, False,
                "swa_sample")

    y_p, y_s = _post((p_h, p_dn, p_sw), (s_h, s_dn, s_sw), post_vecs, post_w, ROW_TILE)

    kv_shape = lambda n, length: (1, n, length, SWA_KV, HEAD_DIM)
    return (y_p[None], y_s.reshape(x_sample.shape),
            p_s[None], p_tail[:, -(CONV_W - 1):][None],
            p_sk[-WINDOW:].reshape(kv_shape(1, WINDOW)), p_sv[-WINDOW:].reshape(kv_shape(1, WINDOW)),
            s_s[None], s_tail[:, -(CONV_W - 1):][None],
            s_sk.reshape(kv_shape(n_seq_s, len_s)), s_sv.reshape(kv_shape(n_seq_s, len_s)))
```

```python
import functools
import math

import jax
import jax.numpy as jnp
from jax import lax
from jax.experimental import pallas as pl
from jax.experimental.pallas import tpu as pltpu

F32 = jnp.float32
BF16 = jnp.bfloat16

D_MODEL = 1024
N_META = 16
HEADS = 8
HEAD_DIM = 64
DN_W = HEADS * HEAD_DIM
CONV_W = 4
CONV_CH = 3 * DN_W
SWA_HEADS = 8
SWA_KV = 2
SWA_G = SWA_HEADS // SWA_KV
KV_W = SWA_KV * HEAD_DIM
WINDOW = 128
N_BUCKETS = 32
MAX_DIST = 128
D_FF = 4 * D_MODEL
DEPTH = 1
DEEP_ALPHA = (2 * DEPTH) ** 0.25
LN_EPS = 1e-5
NORM_EPS = 1e-6
LOG2E = math.log2(math.e)
LANES = 128
SUBLANES = 8
VMEM_LIMIT = 56 * 1024 * 1024

OFF_Z = CONV_CH
OFF_A = OFF_Z + DN_W
OFF_B = OFF_A + HEADS
OFF_SQ = OFF_B + HEADS
OFF_SK = OFF_SQ + SWA_HEADS * HEAD_DIM
PROJ_SPLITS = (CONV_CH, DN_W, SWA_HEADS * HEAD_DIM, KV_W, KV_W, LANES)

ROW_TILE = 512
DELTA_CHUNK = 64
DELTA_PROMPT = (4, 2)
DELTA_SAMPLE = (8, 2)
SWA_UNITS_PROMPT = 16
SWA_UNITS_SAMPLE = 8
SWA_BATCH = 4


def _layer_norm(x, g, b):
    mu = jnp.mean(x, axis=-1, keepdims=True)
    xc = x - mu
    var = jnp.mean(xc * xc, axis=-1, keepdims=True)
    return xc * lax.rsqrt(var + LN_EPS) * g + b


def _mm(a, b):
    return jnp.dot(a.astype(BF16), b.astype(BF16), preferred_element_type=F32)


def _mm_nt(a, b):
    return lax.dot_general(a.astype(BF16), b.astype(BF16), (((1,), (1,)), ((), ())),
                           preferred_element_type=F32)


def _split3(x):
    x1 = x.astype(BF16)
    r1 = x - x1.astype(F32)
    x2 = r1.astype(BF16)
    x3 = (r1 - x2.astype(F32)).astype(BF16)
    return x1, x2, x3


def _dot_exact_lhs(x, sel):
    return jnp.dot(jnp.concatenate(_split3(x), axis=1), jnp.concatenate([sel] * 3, axis=0),
                   preferred_element_type=F32)


def _w_in_kernel(w_ref, main_ref, sq_ref, kv_ref, ab_ref):
    main_ref[...] = w_ref[:, :OFF_A].astype(BF16)
    tail = w_ref[:, OFF_A:].astype(BF16)
    n_ab = OFF_SQ - OFF_A
    ab_ref[...] = jnp.concatenate(
        [tail[:, :n_ab], jnp.zeros((tail.shape[0], LANES - n_ab), BF16)], axis=1)
    heads = [tail[:, n_ab + h * HEAD_DIM:n_ab + (h + 1) * HEAD_DIM] for h in range(SWA_HEADS)]
    sq_ref[...] = jnp.concatenate([heads[kv * SWA_G + g] for g in range(SWA_G)
                                   for kv in range(SWA_KV)], axis=1)
    kv_ref[...] = tail[:, OFF_SK - OFF_A:]


def _prep_w_in(w):
    rows, tile = w.shape[0], 256
    blk = lambda width: pl.BlockSpec((tile, width), lambda i: (i, 0))
    widths = (OFF_A, SWA_W, 2 * KV_W, LANES)
    return pl.pallas_call(
        _w_in_kernel,
        out_shape=[jax.ShapeDtypeStruct((rows, wd), BF16) for wd in widths],
        grid=(rows // tile,),
        in_specs=[blk(w.shape[1])],
        out_specs=[blk(wd) for wd in widths],
        compiler_params=pltpu.CompilerParams(dimension_semantics=("arbitrary",),
                                             vmem_limit_bytes=VMEM_LIMIT),
        name="w_in_layout",
    )(w)


CONV_BLOCK = 256


def _sigmoid(x):
    return 1.0 / (1.0 + jnp.exp(-x))


def _proj_kernel(x_ref, g_ref, b_ref, w_ref, wsq_ref, wkv_ref, wab_ref, conv0_ref, wconv_ref,
                 y_ref, z_ref, sq_ref, sk_ref, sv_ref, ab_ref, tail_ref, h_ref, xbuf, *, seq_len):
    tm = x_ref.shape[0]
    h_ref[...] = _layer_norm(x_ref[...], g_ref[...], b_ref[...])
    h = h_ref[...].astype(BF16)

    chained = seq_len is None
    n_seq = 1 if chained else tm // seq_len
    L = tm if chained else seq_len
    if chained:
        @pl.when(pl.program_id(0) == 0)
        def _():
            xbuf[0, 0:SUBLANES, :] = conv0_ref[0]
    else:
        for s in range(n_seq):
            xbuf[s, 0:SUBLANES, :] = conv0_ref[s]
    blocks = [slice(cb, cb + CONV_BLOCK) for cb in range(0, CONV_CH, CONV_BLOCK)]

    def project(cols):
        raw = jnp.dot(h, w_ref[:, cols], preferred_element_type=F32)
        for s in range(n_seq):
            xbuf[s, SUBLANES:SUBLANES + L, cols] = raw[s * L:(s + 1) * L, :]

    def conv(cols):
        for s in range(n_seq):
            ext = xbuf[s, :, cols]
            w0, w1, w2, w3 = (0.5 * wconv_ref[j:j + 1, cols] for j in range(CONV_W))
            ext1 = pltpu.roll(ext, 1, axis=0)
            half = (w3 * ext + w2 * ext1)[SUBLANES:, :]
            half = half + pltpu.roll(w1 * ext + w0 * ext1, 2, axis=0)[SUBLANES:, :]
            y_ref[s * L:(s + 1) * L, cols] = half * (1.0 + jnp.tanh(half))
            tail_ref[s, :, cols] = xbuf[s, L:L + SUBLANES, cols]
            if chained:
                xbuf[s, 0:SUBLANES, cols] = xbuf[s, L:L + SUBLANES, cols]

    for o_ref, rhs in ((z_ref, w_ref[:, CONV_CH:]), (sq_ref, wsq_ref[...]),
                       (sk_ref, wkv_ref[:, :KV_W]), (sv_ref, wkv_ref[:, KV_W:]),
                       (ab_ref, wab_ref[...])):
        o_ref[...] = jnp.dot(h, rhs, preferred_element_type=F32)
    project(blocks[0])
    for bi, cols in enumerate(blocks):
        if bi + 1 < len(blocks):
            project(blocks[bi + 1])
        conv(cols)


def _proj(x, g, b, weights, conv0, wconv, tm, seq_len):
    rows = x.shape[0]
    w_main, w_sq, w_kv, w_ab = weights
    chained = seq_len is None
    n_tile_seq = 1 if chained else tm // seq_len
    n_seq = 1 if chained else rows // seq_len
    const = lambda i: (0, 0)
    row_blk = lambda width: pl.BlockSpec((tm, width), lambda i: (i, 0))
    seq_blk = pl.BlockSpec((n_tile_seq, SUBLANES, CONV_CH),
                           (lambda i: (0, 0, 0)) if chained else (lambda i: (i, 0, 0)))
    return pl.pallas_call(
        functools.partial(_proj_kernel, seq_len=seq_len),
        out_shape=[jax.ShapeDtypeStruct((rows, width), F32) for width in PROJ_SPLITS]
                  + [jax.ShapeDtypeStruct((n_seq, SUBLANES, CONV_CH), F32),
                     jax.ShapeDtypeStruct((rows, D_MODEL), F32)],
        grid=(rows // tm,),
        in_specs=[row_blk(D_MODEL), pl.BlockSpec((1, D_MODEL), const),
                  pl.BlockSpec((1, D_MODEL), const)]
                 + [pl.BlockSpec(wt.shape, const) for wt in weights]
                 + [seq_blk, pl.BlockSpec((SUBLANES, CONV_CH), const)],
        out_specs=[row_blk(width) for width in PROJ_SPLITS] + [seq_blk, row_blk(D_MODEL)],
        scratch_shapes=[pltpu.VMEM((n_tile_seq, SUBLANES + (tm if chained else seq_len), CONV_CH),
                                   F32)],
        compiler_params=pltpu.CompilerParams(dimension_semantics=("arbitrary",),
                                             vmem_limit_bytes=VMEM_LIMIT),
        name="ln_in_proj",
    )(x, g, b, w_main, w_sq, w_kv, w_ab, conv0, wconv)


GROUP_HEADS = 4


def _softplus(x):
    return jnp.maximum(x, 0.0) + jnp.log1p(jnp.exp(-jnp.abs(x)))


def _delta_kernel(qkv_ref, z_ref, ab_ref, s0_ref, alog_ref, dtb_ref,
                  normw_ref, seg_ref, expand_ref, o_ref, s_ref, sbd,
                  *, chunk, units, groups, sequential):
    C, U = chunk, units
    RG = U * C
    PAIRS = HEADS // GROUP_HEADS
    PW = GROUP_HEADS * HEAD_DIM
    step = pl.program_id(0)
    last_step = pl.num_programs(0) - 1
    expand = expand_ref[...]

    def seg_sum(t):
        tb = t.astype(BF16)
        return jnp.concatenate(
            [jnp.dot(tb[:, g * PW:(g + 1) * PW], seg_ref[g * PW:(g + 1) * PW, g * PW:(g + 1) * PW],
                     preferred_element_type=F32) for g in range(PAIRS)], axis=1)

    ri = lax.broadcasted_iota(jnp.int32, (C, PW), 0)
    ci = lax.broadcasted_iota(jnp.int32, (C, PW), 1) % HEAD_DIM
    causal = (ri >= ci) & (ci < C)
    strict = (ri > ci) & (ci < C)
    eye = (ri == ci).astype(F32)
    lane_head = lax.broadcasted_iota(jnp.int32, (1, PW), 1) // HEAD_DIM
    rs = lax.broadcasted_iota(jnp.int32, (LANES, LANES), 0)
    cs = lax.broadcasted_iota(jnp.int32, (LANES, LANES), 1)
    half_mask = ((rs // HEAD_DIM) == (cs // HEAD_DIM)).astype(F32)
    row_pad = [jnp.zeros((HEAD_DIM - C, PW), BF16)] if C < HEAD_DIM else []
    lane_pad = [jnp.zeros((1, HEAD_DIM - C), F32)] if C < HEAD_DIM else []
    rr = lax.broadcasted_iota(jnp.int32, (RG, RG), 0)
    cc = lax.broadcasted_iota(jnp.int32, (RG, RG), 1)
    tril = ((rr // C == cc // C) & (rr >= cc)).astype(BF16)

    def stack(y):
        yb = y.astype(BF16)
        zero = jnp.zeros_like(yb)
        pad = row_pad if y.shape[0] < HEAD_DIM else []
        blocks = []
        for h in range(GROUP_HEADS):
            blocks += [jnp.where(lane_head == h, yb, zero)] + pad
        return jnp.concatenate(blocks, axis=0)

    items = [(u, p) for u in range(U) for p in range(PAIRS)]
    sl = lambda u, p: (slice(u * C, (u + 1) * C), slice(p * PW, (p + 1) * PW))

    def front(gi):
        rg = slice(gi * RG, (gi + 1) * RG)
        q, k, v = (qkv_ref[rg, j * DN_W:(j + 1) * DN_W] for j in range(3))
        qn = q * (lax.rsqrt(seg_sum(q * q) + NORM_EPS) * HEAD_DIM ** -0.5)
        kn = k * lax.rsqrt(seg_sum(k * k) + NORM_EPS)
        yield

        ab = ab_ref[rg, :]
        g = -jnp.exp(alog_ref[...]) * _softplus(ab + dtb_ref[...])
        gs = jnp.dot(tril, jnp.concatenate(_split3(g), axis=1), preferred_element_type=F32)
        G = (gs[:, :LANES] + gs[:, LANES:2 * LANES] + gs[:, 2 * LANES:]) * LOG2E
        GT = G.T
        G_x = _dot_exact_lhs(G, expand)
        beta = _sigmoid(pltpu.roll(ab, LANES - HEADS, axis=1))
        b_hi = beta.astype(BF16)
        b_lo = (beta - b_hi.astype(F32)).astype(BF16)
        beta_x = jnp.dot(jnp.concatenate([b_hi, b_lo], axis=1),
                         jnp.concatenate([expand, expand], axis=0), preferred_element_type=F32)
        glast_x = jnp.concatenate(
            [jnp.broadcast_to(G_x[(u + 1) * C - 1:(u + 1) * C, :], (C, DN_W)) for u in range(U)],
            axis=0)
        yield
        eg_x = jnp.exp2(G_x)
        kbeta = kn * beta_x
        return dict(qn=qn, kn=kn, kbeta=kbeta, bk=kbeta * eg_x, bv=v * beta_x, qd=qn * eg_x,
                    kd=kn * jnp.exp2(glast_x - G_x), G_x=G_x, GT=GT, eg_x=eg_x)

    def middle(f):
        A, QK = {}, {}
        for (u, p) in items:
            ru, lp = sl(u, p)
            gr = []
            for h in range(p * GROUP_HEADS, (p + 1) * GROUP_HEADS):
                gr += [f["GT"][h:h + 1, ru]] + lane_pad
            gr = jnp.concatenate(gr, axis=1)
            decay = jnp.exp2(jnp.where(causal, f["G_x"][ru, lp] - gr, -jnp.inf))
            r = _mm_nt(jnp.concatenate([f["kbeta"][ru, lp], f["qn"][ru, lp]], axis=0),
                       stack(f["kn"][ru, lp]))
            A[u, p] = jnp.where(strict, r[:C] * decay, 0.0)
            QK[u, p] = r[C:] * decay
        yield

        n_pow = int(math.log2(C)) - 1
        P = {it: eye - A[it] for it in items}
        Ap = {it: _mm(A[it], stack(A[it])) for it in items}
        yield
        for i in range(n_pow):
            if i < n_pow - 1:
                r = {it: _mm(jnp.concatenate([Ap[it], P[it]], axis=0), stack(Ap[it])) for it in items}
                Ap = {it: r[it][:C] for it in items}
                P = {it: P[it] + r[it][C:] for it in items}
            else:
                P = {it: P[it] + _mm(P[it], stack(Ap[it])) for it in items}
            yield

        halves = [slice(hb * LANES, (hb + 1) * LANES) for hb in range(PW // LANES)]
        wu = {it: _mm(P[it], jnp.concatenate([stack(f["bk"][sl(*it)]), stack(f["bv"][sl(*it)])],
                                             axis=1)) for it in items}
        kdT = {(it, hb): f["kd"][sl(*it)][:, cols].T for it in items
               for hb, cols in enumerate(halves)}
        yield
        mn = {(it, hb): _mm(kdT[it, hb], jnp.concatenate(
            [wu[it][:, cols], wu[it][:, PW:][:, cols]], axis=1))
              for it in items for hb, cols in enumerate(halves)}
        yield
        n_t = {key: (half_mask * mn[key][:, LANES:]).T for key in mn}
        out = {}
        for (u, p) in items:
            ru, lp = sl(u, p)
            it = (u, p)
            out[it] = dict(
                W=wu[it][:, :PW], U=wu[it][:, PW:],
                M=diag_blocks([half_mask * mn[it, hb][:, :LANES]
                               for hb in range(len(halves))]).astype(BF16),
                NT=jnp.concatenate([n_t[it, hb][:HEAD_DIM] + n_t[it, hb][HEAD_DIM:]
                                    for hb in range(len(halves))], axis=1),
                QK=QK[it], qd=f["qd"][ru, lp],
                eg_last=f["eg_x"][(u + 1) * C - 1:(u + 1) * C, lp])
        return out

    def diag_blocks(blocks):
        n = len(blocks)
        rows = []
        for i, blk in enumerate(blocks):
            rows.append(jnp.concatenate(
                [blk if j == i else jnp.zeros_like(blk) for j in range(n)], axis=1))
        return jnp.concatenate(rows, axis=0)

    def load_state(ref, prefix, p):
        return jnp.concatenate([ref[prefix + (p * GROUP_HEADS + h,)].T
                                for h in range(GROUP_HEADS)], axis=1)

    def store_state(ref, prefix, p, st):
        for h in range(GROUP_HEADS):
            ref[prefix + (p * GROUP_HEADS + h,)] = st[:, h * HEAD_DIM:(h + 1) * HEAD_DIM].T

    chunks = [(gi, u) for gi in range(groups) for u in range(U)]
    if sequential:
        @pl.when(step == 0)
        def _():
            for p in range(PAIRS):
                sbd[p] = load_state(s0_ref, (), p)
        s_start = {p: sbd[p] for p in range(PAIRS)}
    else:
        s_start = {(idx, p): load_state(s0_ref, (idx,), p)
                   for idx in range(len(chunks)) for p in range(PAIRS)}

    def chain(gi, out, s_in):
        if sequential:
            states = {p: [s_in[p]] for p in range(PAIRS)}
            for u in range(U):
                for p in range(PAIRS):
                    d, s_cur = out[u, p], states[p][-1]
                    states[p].append(s_cur * d["eg_last"] - _mm_nt(s_cur, d["M"]) + d["NT"])
            return ({(u, p): states[p][u] for (u, p) in items},
                    {p: states[p][-1] for p in range(PAIRS)})
        entering = {(u, p): s_in[gi * U + u, p] for (u, p) in items}
        s_out = {}
        for (u, p) in items:
            d, s_cur = out[u, p], entering[u, p]
            s_out[gi * U + u, p] = s_cur * d["eg_last"] - _mm_nt(s_cur, d["M"]) + d["NT"]
        return entering, s_out

    def finish(gi, out, entering):
        r = {it: _mm_nt(jnp.concatenate([out[it]["W"], out[it]["qd"]], axis=0),
                        stack(entering[it])) for it in items}
        yield
        tiles = {it: r[it][C:] + _mm(out[it]["QK"], stack(out[it]["U"] - r[it][:C]))
                 for it in items}
        yield
        rg = slice(gi * RG, (gi + 1) * RG)
        o = jnp.concatenate([jnp.concatenate([tiles[u, p] for p in range(PAIRS)], axis=1)
                             for u in range(U)], axis=0)
        ms = seg_sum(o * o) * (1.0 / HEAD_DIM)
        hz = 0.5 * z_ref[rg, :]
        o_ref[rg, :] = (o * lax.rsqrt(ms + NORM_EPS) * normw_ref[...]
                        * (hz * (1.0 + jnp.tanh(hz)))).astype(BF16)

    def drive(gens):
        results = [None] * len(gens)
        active = list(enumerate(gens))
        while active:
            still = []
            for i, gen in active:
                try:
                    next(gen)
                    still.append((i, gen))
                except StopIteration as stop:
                    results[i] = stop.value
            active = still
        return results

    state = s_start
    s_final = {}
    f_cur = drive([front(0)])[0]
    pending = None
    for gi in range(groups + 1):
        gens = []
        if gi < groups:
            gens.append(middle(f_cur))
        if gi + 1 < groups:
            gens.append(front(gi + 1))
        if pending is not None:
            entering, s_out = chain(gi - 1, pending, state)
            s_final.update(s_out)
            state = s_out if sequential else state
            gens.append(finish(gi - 1, pending, entering))
        res = drive(gens)
        pending = res[0] if gi < groups else None
        f_cur = res[1] if gi + 1 < groups else None

    if sequential:
        for p in range(PAIRS):
            sbd[p] = s_final[p]

        @pl.when(step == last_step)
        def _():
            for p in range(PAIRS):
                store_state(s_ref, (), p, s_final[p])
    else:
        for (idx, p), s_new in s_final.items():
            store_state(s_ref, (idx,), p, s_new)


def _delta(qkv, z, ab, s0, consts, n_seq, seq_len, chunk, units, groups):
    alog, dtb, normw, seg, expand = consts
    sequential = n_seq == 1
    rows = n_seq * seq_len
    n_chunks = units * groups
    blk_rows = n_chunks * chunk
    assert rows % blk_rows == 0 and (sequential or seq_len == chunk)
    const2 = lambda shape: pl.BlockSpec(shape, lambda i: (0, 0))
    tok = lambda width: pl.BlockSpec((blk_rows, width), lambda i: (i, 0))
    group_w = GROUP_HEADS * HEAD_DIM
    if sequential:
        state = pl.BlockSpec((None, HEADS, HEAD_DIM, HEAD_DIM), lambda i: (0, 0, 0, 0))
    else:
        state = pl.BlockSpec((n_chunks, HEADS, HEAD_DIM, HEAD_DIM), lambda i: (i, 0, 0, 0))
    return pl.pallas_call(
        functools.partial(_delta_kernel, chunk=chunk, units=units, groups=groups,
                          sequential=sequential),
        out_shape=[jax.ShapeDtypeStruct((rows, DN_W), BF16),
                   jax.ShapeDtypeStruct((n_seq, HEADS, HEAD_DIM, HEAD_DIM), F32)],
        grid=(rows // blk_rows,),
        in_specs=[tok(CONV_CH), tok(DN_W), tok(LANES), state,
                  const2((1, LANES)), const2((1, LANES)),
                  const2((1, DN_W)), const2((DN_W, DN_W)), const2((LANES, DN_W))],
        out_specs=[tok(DN_W), state],
        scratch_shapes=[pltpu.VMEM((HEADS // GROUP_HEADS, HEAD_DIM, group_w), F32)],
        compiler_params=pltpu.CompilerParams(
            dimension_semantics=("arbitrary",), vmem_limit_bytes=VMEM_LIMIT,
        ),
        name=f"delta_c{chunk}",
    )(qkv, z, ab, s0, alog, dtb, normw, seg, expand)


SWA_W = SWA_HEADS * HEAD_DIM


def _swa_kernel(*refs, n_q, units, prompt, batch=SWA_BATCH):
    if prompt:
        (table_ref, sinks_ref, bucket_ref, q_ref, km_ref, kp_ref, kc_ref, vm_ref, vp_ref, vc_ref,
         o_ref, bias_ref) = refs
    else:
        (table_ref, sinks_ref, bucket_ref, q_ref, kp_ref, kc_ref, vp_ref, vc_ref,
         o_ref, bias_ref) = refs
    n_k = bias_ref.shape[-1]
    n_grp = SWA_W // KV_W
    step = pl.program_id(0)

    @pl.when(step == 0)
    def _():
        bucket = bucket_ref[...]
        for h in range(SWA_HEADS):
            acc = jnp.zeros((n_q, n_k), F32)
            for b in range(N_BUCKETS):
                acc = jnp.where(bucket == b, table_ref[b * SWA_HEADS + h], acc)
            kv, gi = divmod(h, SWA_G)
            bias_ref[kv, gi * n_q:(gi + 1) * n_q, :] = acc * LOG2E

    lane = lax.broadcasted_iota(jnp.int32, (1, KV_W), 1)
    kv_mask = [(lane < HEAD_DIM).astype(F32), (lane >= HEAD_DIM).astype(F32)]
    if prompt:
        first = step == 0
        k_all = jnp.concatenate([jnp.where(first, km_ref[...], kp_ref[...]), kc_ref[...]], axis=0)
        v_all = jnp.concatenate([jnp.where(first, vm_ref[...], vp_ref[...]), vc_ref[...]], axis=0)
        k_kv = [(k_all * m).astype(BF16) for m in kv_mask]
        v_kv = [(v_all * m).astype(BF16) for m in kv_mask]
        keys = lambda j, kv: k_kv[kv][j * n_q:j * n_q + n_k]
        vals = lambda j, kv: v_kv[kv][j * n_q:j * n_q + n_k]
    else:
        n_c = n_k - n_q
        cat = lambda a, b, j: jnp.concatenate([a[j * n_c:(j + 1) * n_c, :],
                                               b[j * n_q:(j + 1) * n_q, :]], axis=0)
        keys = lambda j, kv: (cat(kp_ref, kc_ref, j) * kv_mask[kv]).astype(BF16)
        vals = lambda j, kv: (cat(vp_ref, vc_ref, j) * kv_mask[kv]).astype(BF16)

    q = (q_ref[...] * (HEAD_DIM ** -0.5 * LOG2E)).astype(BF16)
    ones_k = jnp.ones((n_k, KV_W), BF16)
    sink = [jnp.concatenate([jnp.full((n_q, KV_W), sinks_ref[kv * SWA_G + gi] * LOG2E, F32)
                             for gi in range(SWA_G)], axis=0) for kv in range(SWA_KV)]
    rows4 = SWA_G * n_q
    for j0 in range(0, units, batch):
        js = range(j0, min(j0 + batch, units))
        items = [(j, kv) for j in js for kv in range(SWA_KV)]
        s = {}
        for j in js:
            rows = slice(j * n_q, (j + 1) * n_q)
            q4 = jnp.concatenate([q[rows, g * KV_W:(g + 1) * KV_W] for g in range(n_grp)], axis=0)
            for kv in range(SWA_KV):
                sj = lax.dot_general(q4, keys(j, kv), (((1,), (1,)), ((), ())),
                                     preferred_element_type=F32) + bias_ref[kv]
                if prompt and j * n_q < WINDOW - N_META:
                    key_pos = j * n_q + lax.broadcasted_iota(jnp.int32, (1, n_k), 1)
                    n_invalid = jnp.where(first, WINDOW - N_META, 0)
                    sj = jnp.where(key_pos < n_invalid, -jnp.inf, sj)
                s[j, kv] = sj
        m = {it: jnp.maximum(jnp.broadcast_to(jnp.max(s[it], axis=-1, keepdims=True),
                                              (rows4, KV_W)), sink[it[1]]) for it in items}
        p = {it: jnp.exp2(s[it] - jnp.concatenate([m[it], m[it][:, :n_k - KV_W]], axis=1))
             for it in items}
        pv = {it: jnp.dot(p[it].astype(BF16), jnp.concatenate([vals(*it), ones_k], axis=1),
                          preferred_element_type=F32) for it in items}
        o = {it: pv[it][:, :KV_W] * (1.0 / (pv[it][:, KV_W:] + jnp.exp2(sink[it[1]] - m[it])))
             for it in items}
        for j in js:
            o2 = o[j, 0] + o[j, 1]
            for g in range(n_grp):
                o_ref[j * n_q:(j + 1) * n_q, g * KV_W:(g + 1) * KV_W] = (
                    o2[g * n_q:(g + 1) * n_q, :].astype(BF16))


def _swa(table, sinks, bucket, q, kv_args, kv_specs, n_steps, n_q, units, prompt, name):
    n_k = bucket.shape[1]
    smem = pl.BlockSpec(memory_space=pltpu.SMEM)
    rows = units * n_q
    return pl.pallas_call(
        functools.partial(_swa_kernel, n_q=n_q, units=units, prompt=prompt),
        out_shape=jax.ShapeDtypeStruct((n_steps * rows, SWA_W), BF16),
        grid=(n_steps,),
        in_specs=[smem, smem, pl.BlockSpec((n_q, n_k), lambda i: (0, 0)),
                  pl.BlockSpec((rows, SWA_W), lambda i: (i, 0))] + list(kv_specs),
        out_specs=pl.BlockSpec((rows, SWA_W), lambda i: (i, 0)),
        scratch_shapes=[pltpu.VMEM((SWA_KV, SWA_G * n_q, n_k), F32)],
        compiler_params=pltpu.CompilerParams(dimension_semantics=("arbitrary",),
                                             vmem_limit_bytes=VMEM_LIMIT),
        name=name,
    )(table, sinks, bucket, q, *kv_args)


def _t5_bucket(rel):
    half = N_BUCKETS // 2
    max_exact = half // 2
    n = jnp.abs(rel)
    large = max_exact + (jnp.log(jnp.maximum(n, 1).astype(F32) / max_exact)
                         / math.log(MAX_DIST / max_exact) * (half - max_exact)).astype(jnp.int32)
    large = jnp.minimum(large, half - 1)
    return jnp.where(rel > 0, half, 0) + jnp.where(n < max_exact, n, large)


def _bucket_map(n_q, n_k, key_offset):
    rel = (jnp.arange(n_k)[None, :] - key_offset) - jnp.arange(n_q)[:, None]
    return _t5_bucket(rel).astype(jnp.int32)


FF_BLOCK = 1024


def _post_kernel(hp_ref, dnp_ref, swp_ref, hs_ref, dns_ref, sws_ref,
                 wod_ref, wos_ref, g1_ref, b1_ref, w1_ref, w2_ref, g2_ref, b2_ref,
                 yp_ref, ys_ref, *, n_prompt_tiles):
    is_prompt = pl.program_id(0) < n_prompt_tiles

    @pl.when(is_prompt)
    def _():
        _post_tile(hp_ref, dnp_ref, swp_ref, wod_ref, wos_ref, g1_ref, b1_ref,
                   w1_ref, w2_ref, g2_ref, b2_ref, yp_ref)

    @pl.when(jnp.logical_not(is_prompt))
    def _():
        _post_tile(hs_ref, dns_ref, sws_ref, wod_ref, wos_ref, g1_ref, b1_ref,
                   w1_ref, w2_ref, g2_ref, b2_ref, ys_ref)


def _post_tile(h_ref, dn_ref, sw_ref, wod_ref, wos_ref, g1_ref, b1_ref,
               w1_ref, w2_ref, g2_ref, b2_ref, y_ref):
    mix = (jnp.dot(dn_ref[...], wod_ref[...], preferred_element_type=F32)
           + jnp.dot(sw_ref[...], wos_ref[...], preferred_element_type=F32))
    h1 = _layer_norm(DEEP_ALPHA * h_ref[...] + mix, g1_ref[...], b1_ref[...])
    h1b = h1.astype(BF16)
    f = jnp.zeros_like(h1)
    for j in range(D_FF // FF_BLOCK):
        a = jnp.dot(h1b, w1_ref[:, j * FF_BLOCK:(j + 1) * FF_BLOCK], preferred_element_type=F32)
        a = jnp.square(jnp.maximum(a, 0.0)).astype(BF16)
        f = f + jnp.dot(a, w2_ref[j * FF_BLOCK:(j + 1) * FF_BLOCK, :], preferred_element_type=F32)
    y_ref[...] = _layer_norm(DEEP_ALPHA * h1 + f, g2_ref[...], b2_ref[...])


def _post(prompt, sample, vecs, weights, tm):
    g1, b1, g2, b2 = vecs
    wo_dn, wo_sw, w1, w2 = weights
    n_p, n_s = prompt[0].shape[0] // tm, sample[0].shape[0] // tm
    const = lambda i: (0, 0)
    vec = pl.BlockSpec((1, D_MODEL), const)
    weight = lambda wt: pl.BlockSpec(wt.shape, const, pipeline_mode=pl.Buffered(1))
    blk_p = lambda width: pl.BlockSpec((tm, width), lambda i: (jnp.minimum(i, n_p - 1), 0))
    blk_s = lambda width: pl.BlockSpec((tm, width), lambda i: (jnp.maximum(i - n_p, 0), 0))
    widths = (D_MODEL, DN_W, SWA_W)
    return pl.pallas_call(
        functools.partial(_post_kernel, n_prompt_tiles=n_p),
        out_shape=[jax.ShapeDtypeStruct((n_p * tm, D_MODEL), F32),
                   jax.ShapeDtypeStruct((n_s * tm, D_MODEL), F32)],
        grid=(n_p + n_s,),
        in_specs=[blk_p(wd) for wd in widths] + [blk_s(wd) for wd in widths]
                 + [weight(wo_dn), weight(wo_sw), vec, vec, weight(w1), weight(w2), vec, vec],
        out_specs=[blk_p(D_MODEL), blk_s(D_MODEL)],
        compiler_params=pltpu.CompilerParams(dimension_semantics=("arbitrary",),
                                             vmem_limit_bytes=VMEM_LIMIT),
        name="post_mlp",
    )(*prompt, *sample, wo_dn, wo_sw, g1, b1, w1, w2, g2, b2)


def kernel(x_prompt, x_sample, state_delta, state_conv, cache_swa_k, cache_swa_v, meta_tokens, ln_in_g, ln_in_b, w_in, w_conv, dn_a_log, dn_dt_bias, dn_norm, swa_sinks, rel_bias_table, w_o, ln1_g, ln1_b, w_ff1, w_ff2, ln2_g, ln2_b):
    assert w_in.shape[0] == DEPTH == 1
    n_seq_s, len_s = x_sample.shape[0], x_sample.shape[1]
    len_p = x_prompt.shape[1]
    cache_len = cache_swa_k.shape[2]
    chunk_p = DELTA_CHUNK
    row = lambda t: t.reshape(1, -1).astype(F32)

    w_r = _prep_w_in(w_in[0].astype(F32))
    gin, bin_ = row(ln_in_g), row(ln_in_b)
    pad_lanes = lambda t: jnp.pad(t.reshape(1, -1).astype(F32), ((0, 0), (0, LANES - HEADS)))
    lane_head = jnp.arange(DN_W) // HEAD_DIM
    seg = (lane_head[:, None] == lane_head[None, :]).astype(BF16)
    expand = (jnp.arange(LANES)[:, None] == lane_head[None, :]).astype(BF16)
    wconv = jnp.pad(w_conv[0].astype(F32), ((0, SUBLANES - CONV_W), (0, 0)))
    delta_consts = (pad_lanes(dn_a_log[0]), pad_lanes(dn_dt_bias[0]),
                    jnp.tile(dn_norm[0].astype(F32), HEADS).reshape(1, DN_W), seg, expand)
    post_vecs = (row(ln1_g[0]), row(ln1_b[0]), row(ln2_g[0]), row(ln2_b[0]))
    wo_sw = (w_o[0][DN_W:].astype(BF16).reshape(SWA_KV, SWA_G, HEAD_DIM, D_MODEL)
             .transpose(1, 0, 2, 3).reshape(SWA_W, D_MODEL))
    post_w = (w_o[0][:DN_W].astype(BF16), wo_sw, w_ff1[0].astype(BF16), w_ff2[0].astype(BF16))
    table = rel_bias_table.astype(F32).reshape(-1)
    sinks = swa_sinks[0].astype(F32)

    xp = x_prompt[0]
    xs = x_sample.reshape(n_seq_s * len_s, D_MODEL)

    conv_pad = SUBLANES - (CONV_W - 1)
    zero_conv = jnp.zeros((1, SUBLANES, CONV_CH), F32)
    conv_s = jnp.pad(state_conv[0].astype(F32), ((0, 0), (conv_pad, 0), (0, 0)))
    m_qkv, m_z, m_sq, m_sk, m_sv, m_ab, m_tail, _ = _proj(
        meta_tokens.astype(F32), gin, bin_, w_r, zero_conv, wconv, N_META, N_META)
    p_qkv, p_z, p_sq, p_sk, p_sv, p_ab, p_tail, p_h = _proj(
        xp, gin, bin_, w_r, m_tail, wconv, ROW_TILE, None)
    s_qkv, s_z, s_sq, s_sk, s_sv, s_ab, s_tail, s_h = _proj(
        xs, gin, bin_, w_r, conv_s, wconv, ROW_TILE, len_s)

    zero_s = jnp.zeros((1, HEADS, HEAD_DIM, HEAD_DIM), F32)
    _, s_meta = _delta(m_qkv, m_z, m_ab, zero_s, delta_consts, 1, N_META, N_META, 1, 1)
    p_dn, p_s = _delta(p_qkv, p_z, p_ab, s_meta, delta_consts, 1, len_p, chunk_p, *DELTA_PROMPT)
    s_dn, s_s = _delta(s_qkv, s_z, s_ab, state_delta[0].astype(F32), delta_consts,
                       n_seq_s, len_s, len_s, *DELTA_SAMPLE)

    span = WINDOW
    lead = jnp.zeros((span - N_META, KV_W), F32)
    units_p = SWA_UNITS_PROMPT
    rows_p = units_p * chunk_p
    prev_blocks = rows_p // span
    kv_specs_p = [pl.BlockSpec((span, KV_W), lambda i: (0, 0)),
                  pl.BlockSpec((span, KV_W), lambda i: (jnp.maximum(i * prev_blocks - 1, 0), 0)),
                  pl.BlockSpec((rows_p, KV_W), lambda i: (i, 0))] * 2
    p_sw = _swa(table, sinks, _bucket_map(chunk_p, span + chunk_p, span), p_sq,
                [jnp.concatenate([lead, m_sk], axis=0), p_sk, p_sk,
                 jnp.concatenate([lead, m_sv], axis=0), p_sv, p_sv],
                kv_specs_p, len_p // rows_p, chunk_p, units_p, True, "swa_prompt")
    ck = cache_swa_k[0].astype(F32).reshape(n_seq_s * cache_len, KV_W)
    cv = cache_swa_v[0].astype(F32).reshape(n_seq_s * cache_len, KV_W)
    units_s = SWA_UNITS_SAMPLE
    kv_specs_s = [pl.BlockSpec((units_s * cache_len, KV_W), lambda i: (i, 0)),
                  pl.BlockSpec((units_s * len_s, KV_W), lambda i: (i, 0))] * 2
    s_sw = _swa(table, sinks, _bucket_map(len_s, cache_len + len_s, cache_len), s_sq,
                [ck, s_sk, cv, s_sv], kv_specs_s, n_seq_s // units_s, len_s, units_s, False,
                "swa_sample")

    y_p, y_s = _post((p_h, p_dn, p_sw), (s_h, s_dn, s_sw), post_vecs, post_w, ROW_TILE)

    kv_shape = lambda n, length: (1, n, length, SWA_KV, HEAD_DIM)
    return (y_p[None], y_s.reshape(x_sample.shape),
            p_s[None], p_tail[:, -(CONV_W - 1):][None],
            p_sk[-WINDOW:].reshape(kv_shape(1, WINDOW)), p_sv[-WINDOW:].reshape(kv_shape(1, WINDOW)),
            s_s[None], s_tail[:, -(CONV_W - 1):][None],
            s_sk.reshape(kv_shape(n_seq_s, len_s)), s_sv.reshape(kv_shape(n_seq_s, len_s)))
```

```python
import functools
import math

import jax
import jax.numpy as jnp
from jax import lax
from jax.experimental import pallas as pl
from jax.experimental.pallas import tpu as pltpu

F32 = jnp.float32
BF16 = jnp.bfloat16

D_MODEL = 1024
N_META = 16
HEADS = 8
HEAD_DIM = 64
DN_W = HEADS * HEAD_DIM
CONV_W = 4
CONV_CH = 3 * DN_W
SWA_HEADS = 8
SWA_KV = 2
SWA_G = SWA_HEADS // SWA_KV
KV_W = SWA_KV * HEAD_DIM
WINDOW = 128
N_BUCKETS = 32
MAX_DIST = 128
D_FF = 4 * D_MODEL
DEPTH = 1
DEEP_ALPHA = (2 * DEPTH) ** 0.25
LN_EPS = 1e-5
NORM_EPS = 1e-6
LOG2E = math.log2(math.e)
LANES = 128
SUBLANES = 8
VMEM_LIMIT = 56 * 1024 * 1024

OFF_Z = CONV_CH
OFF_A = OFF_Z + DN_W
OFF_B = OFF_A + HEADS
OFF_SQ = OFF_B + HEADS
OFF_SK = OFF_SQ + SWA_HEADS * HEAD_DIM
PROJ_SPLITS = (CONV_CH, DN_W, SWA_HEADS * HEAD_DIM, KV_W, KV_W, LANES)

ROW_TILE = 512
PROMPT_PROJ_TILE = 1024
DELTA_CHUNK = 64
DELTA_PROMPT = (4, 2)
DELTA_SAMPLE = (8, 2)
SWA_UNITS_PROMPT = 16
SWA_UNITS_SAMPLE = 8
SWA_BATCH = 4


def _layer_norm(x, g, b):
    mu = jnp.mean(x, axis=-1, keepdims=True)
    xc = x - mu
    var = jnp.mean(xc * xc, axis=-1, keepdims=True)
    return xc * lax.rsqrt(var + LN_EPS) * g + b


def _mm(a, b):
    return jnp.dot(a.astype(BF16), b.astype(BF16), preferred_element_type=F32)


def _mm_nt(a, b):
    return lax.dot_general(a.astype(BF16), b.astype(BF16), (((1,), (1,)), ((), ())),
                           preferred_element_type=F32)


def _split3(x):
    x1 = x.astype(BF16)
    r1 = x - x1.astype(F32)
    x2 = r1.astype(BF16)
    x3 = (r1 - x2.astype(F32)).astype(BF16)
    return x1, x2, x3


def _dot_exact_lhs(x, sel):
    return jnp.dot(jnp.concatenate(_split3(x), axis=1), jnp.concatenate([sel] * 3, axis=0),
                   preferred_element_type=F32)


CONV_BLOCK = 256


def _sigmoid(x):
    return 1.0 / (1.0 + jnp.exp(-x))


def _proj_kernel(x_ref, g_ref, b_ref, w_ref, wsq_ref, wkv_ref, wab_ref, conv0_ref, wconv_ref,
                 y_ref, z_ref, sq_ref, sk_ref, sv_ref, ab_ref, tail_ref, h_ref, xbuf, *, seq_len):
    tm = x_ref.shape[0]
    h_ref[...] = _layer_norm(x_ref[...], g_ref[...], b_ref[...])
    h = h_ref[...].astype(BF16)

    chained = seq_len is None
    n_seq = 1 if chained else tm // seq_len
    L = tm if chained else seq_len
    if chained:
        @pl.when(pl.program_id(0) == 0)
        def _():
            xbuf[0, 0:SUBLANES, :] = conv0_ref[0]
    else:
        for s in range(n_seq):
            xbuf[s, 0:SUBLANES, :] = conv0_ref[s]
    blocks = [slice(cb, cb + CONV_BLOCK) for cb in range(0, CONV_CH, CONV_BLOCK)]

    def project(cols):
        raw = jnp.dot(h, w_ref[:, cols], preferred_element_type=F32)
        for s in range(n_seq):
            xbuf[s, SUBLANES:SUBLANES + L, cols] = raw[s * L:(s + 1) * L, :]

    def conv(cols):
        for s in range(n_seq):
            ext = xbuf[s, :, cols]
            w0, w1, w2, w3 = (0.5 * wconv_ref[j:j + 1, cols] for j in range(CONV_W))
            ext1 = pltpu.roll(ext, 1, axis=0)
            half = (w3 * ext + w2 * ext1)[SUBLANES:, :]
            half = half + pltpu.roll(w1 * ext + w0 * ext1, 2, axis=0)[SUBLANES:, :]
            y_ref[s * L:(s + 1) * L, cols] = half * (1.0 + jnp.tanh(half))
            tail_ref[s, :, cols] = xbuf[s, L:L + SUBLANES, cols]
            if chained:
                xbuf[s, 0:SUBLANES, cols] = xbuf[s, L:L + SUBLANES, cols]

    for o_ref, rhs in ((z_ref, w_ref[:, CONV_CH:]), (sq_ref, wsq_ref[...]),
                       (sk_ref, wkv_ref[:, :KV_W]), (sv_ref, wkv_ref[:, KV_W:]),
                       (ab_ref, wab_ref[...])):
        o_ref[...] = jnp.dot(h, rhs, preferred_element_type=F32)
    project(blocks[0])
    for bi, cols in enumerate(blocks):
        if bi + 1 < len(blocks):
            project(blocks[bi + 1])
        conv(cols)


def _proj(x, g, b, weights, conv0, wconv, tm, seq_len):
    rows = x.shape[0]
    w_main, w_sq, w_kv, w_ab = weights
    chained = seq_len is None
    n_tile_seq = 1 if chained else tm // seq_len
    n_seq = 1 if chained else rows // seq_len
    const = lambda i: (0, 0)
    row_blk = lambda width: pl.BlockSpec((tm, width), lambda i: (i, 0))
    seq_blk = pl.BlockSpec((n_tile_seq, SUBLANES, CONV_CH),
                           (lambda i: (0, 0, 0)) if chained else (lambda i: (i, 0, 0)))
    return pl.pallas_call(
        functools.partial(_proj_kernel, seq_len=seq_len),
        out_shape=[jax.ShapeDtypeStruct((rows, width), F32) for width in PROJ_SPLITS]
                  + [jax.ShapeDtypeStruct((n_seq, SUBLANES, CONV_CH), F32),
                     jax.ShapeDtypeStruct((rows, D_MODEL), F32)],
        grid=(rows // tm,),
        in_specs=[row_blk(D_MODEL), pl.BlockSpec((1, D_MODEL), const),
                  pl.BlockSpec((1, D_MODEL), const)]
                 + [pl.BlockSpec(wt.shape, const, pipeline_mode=pl.Buffered(1)) for wt in weights]
                 + [seq_blk, pl.BlockSpec((SUBLANES, CONV_CH), const)],
        out_specs=[row_blk(width) for width in PROJ_SPLITS] + [seq_blk, row_blk(D_MODEL)],
        scratch_shapes=[pltpu.VMEM((n_tile_seq, SUBLANES + (tm if chained else seq_len), CONV_CH),
                                   F32)],
        compiler_params=pltpu.CompilerParams(dimension_semantics=("arbitrary",),
                                             vmem_limit_bytes=VMEM_LIMIT),
        name="ln_in_proj",
    )(x, g, b, w_main, w_sq, w_kv, w_ab, conv0, wconv)


GROUP_HEADS = 4


def _softplus(x):
    return jnp.maximum(x, 0.0) + jnp.log1p(jnp.exp(-jnp.abs(x)))


def _delta_kernel(qkv_ref, z_ref, ab_ref, s0_ref, alog_ref, dtb_ref,
                  normw_ref, seg_ref, expand_ref, o_ref, s_ref, sbd,
                  *, chunk, units, groups, sequential):
    C, U = chunk, units
    RG = U * C
    PAIRS = HEADS // GROUP_HEADS
    PW = GROUP_HEADS * HEAD_DIM
    step = pl.program_id(0)
    last_step = pl.num_programs(0) - 1
    expand = expand_ref[...]

    def seg_sum(t):
        tb = t.astype(BF16)
        return jnp.concatenate(
            [jnp.dot(tb[:, g * PW:(g + 1) * PW], seg_ref[g * PW:(g + 1) * PW, g * PW:(g + 1) * PW],
                     preferred_element_type=F32) for g in range(PAIRS)], axis=1)

    ri = lax.broadcasted_iota(jnp.int32, (C, PW), 0)
    ci = lax.broadcasted_iota(jnp.int32, (C, PW), 1) % HEAD_DIM
    causal = (ri >= ci) & (ci < C)
    strict = (ri > ci) & (ci < C)
    eye = (ri == ci).astype(F32)
    lane_head = lax.broadcasted_iota(jnp.int32, (1, PW), 1) // HEAD_DIM
    rs = lax.broadcasted_iota(jnp.int32, (LANES, LANES), 0)
    cs = lax.broadcasted_iota(jnp.int32, (LANES, LANES), 1)
    half_mask = ((rs // HEAD_DIM) == (cs // HEAD_DIM)).astype(F32)
    row_pad = [jnp.zeros((HEAD_DIM - C, PW), BF16)] if C < HEAD_DIM else []
    lane_pad = [jnp.zeros((1, HEAD_DIM - C), F32)] if C < HEAD_DIM else []
    rr = lax.broadcasted_iota(jnp.int32, (RG, RG), 0)
    cc = lax.broadcasted_iota(jnp.int32, (RG, RG), 1)
    tril = ((rr // C == cc // C) & (rr >= cc)).astype(BF16)

    def stack(y):
        yb = y.astype(BF16)
        zero = jnp.zeros_like(yb)
        pad = row_pad if y.shape[0] < HEAD_DIM else []
        blocks = []
        for h in range(GROUP_HEADS):
            blocks += [jnp.where(lane_head == h, yb, zero)] + pad
        return jnp.concatenate(blocks, axis=0)

    items = [(u, p) for u in range(U) for p in range(PAIRS)]
    sl = lambda u, p: (slice(u * C, (u + 1) * C), slice(p * PW, (p + 1) * PW))

    def front(gi):
        rg = slice(gi * RG, (gi + 1) * RG)
        q, k, v = (qkv_ref[rg, j * DN_W:(j + 1) * DN_W] for j in range(3))
        qn = q * (lax.rsqrt(seg_sum(q * q) + NORM_EPS) * HEAD_DIM ** -0.5)
        kn = k * lax.rsqrt(seg_sum(k * k) + NORM_EPS)
        yield

        ab = ab_ref[rg, :]
        g = -jnp.exp(alog_ref[...]) * _softplus(ab + dtb_ref[...])
        gs = jnp.dot(tril, jnp.concatenate(_split3(g), axis=1), preferred_element_type=F32)
        G = (gs[:, :LANES] + gs[:, LANES:2 * LANES] + gs[:, 2 * LANES:]) * LOG2E
        GT = G.T
        G_x = _dot_exact_lhs(G, expand)
        beta = _sigmoid(pltpu.roll(ab, LANES - HEADS, axis=1))
        b_hi = beta.astype(BF16)
        b_lo = (beta - b_hi.astype(F32)).astype(BF16)
        beta_x = jnp.dot(jnp.concatenate([b_hi, b_lo], axis=1),
                         jnp.concatenate([expand, expand], axis=0), preferred_element_type=F32)
        glast_x = jnp.concatenate(
            [jnp.broadcast_to(G_x[(u + 1) * C - 1:(u + 1) * C, :], (C, DN_W)) for u in range(U)],
            axis=0)
        yield
        eg_x = jnp.exp2(G_x)
        kbeta = kn * beta_x
        return dict(qn=qn, kn=kn, kbeta=kbeta, bk=kbeta * eg_x, bv=v * beta_x, qd=qn * eg_x,
                    kd=kn * jnp.exp2(glast_x - G_x), G_x=G_x, GT=GT, eg_x=eg_x)

    def middle(f):
        A, QK = {}, {}
        for (u, p) in items:
            ru, lp = sl(u, p)
            gr = []
            for h in range(p * GROUP_HEADS, (p + 1) * GROUP_HEADS):
                gr += [f["GT"][h:h + 1, ru]] + lane_pad
            gr = jnp.concatenate(gr, axis=1)
            decay = jnp.exp2(jnp.where(causal, f["G_x"][ru, lp] - gr, -jnp.inf))
            r = _mm_nt(jnp.concatenate([f["kbeta"][ru, lp], f["qn"][ru, lp]], axis=0),
                       stack(f["kn"][ru, lp]))
            A[u, p] = jnp.where(strict, r[:C] * decay, 0.0)
            QK[u, p] = r[C:] * decay
        yield

        n_pow = int(math.log2(C)) - 1
        P = {it: eye - A[it] for it in items}
        Ap = {it: _mm(A[it], stack(A[it])) for it in items}
        yield
        for i in range(n_pow):
            if i < n_pow - 1:
                r = {it: _mm(jnp.concatenate([Ap[it], P[it]], axis=0), stack(Ap[it])) for it in items}
                Ap = {it: r[it][:C] for it in items}
                P = {it: P[it] + r[it][C:] for it in items}
            else:
                P = {it: P[it] + _mm(P[it], stack(Ap[it])) for it in items}
            yield

        halves = [slice(hb * LANES, (hb + 1) * LANES) for hb in range(PW // LANES)]
        wu = {it: _mm(P[it], jnp.concatenate([stack(f["bk"][sl(*it)]), stack(f["bv"][sl(*it)])],
                                             axis=1)) for it in items}
        kdT = {(it, hb): f["kd"][sl(*it)][:, cols].T for it in items
               for hb, cols in enumerate(halves)}
        yield
        mn = {(it, hb): _mm(kdT[it, hb], jnp.concatenate(
            [wu[it][:, cols], wu[it][:, PW:][:, cols]], axis=1))
              for it in items for hb, cols in enumerate(halves)}
        yield
        n_t = {key: (half_mask * mn[key][:, LANES:]).T for key in mn}
        out = {}
        for (u, p) in items:
            ru, lp = sl(u, p)
            it = (u, p)
            out[it] = dict(
                W=wu[it][:, :PW], U=wu[it][:, PW:],
                M=diag_blocks([half_mask * mn[it, hb][:, :LANES]
                               for hb in range(len(halves))]).astype(BF16),
                NT=jnp.concatenate([n_t[it, hb][:HEAD_DIM] + n_t[it, hb][HEAD_DIM:]
                                    for hb in range(len(halves))], axis=1),
                QK=QK[it], qd=f["qd"][ru, lp],
                eg_last=f["eg_x"][(u + 1) * C - 1:(u + 1) * C, lp])
        return out

    def diag_blocks(blocks):
        n = len(blocks)
        rows = []
        for i, blk in enumerate(blocks):
            rows.append(jnp.concatenate(
                [blk if j == i else jnp.zeros_like(blk) for j in range(n)], axis=1))
        return jnp.concatenate(rows, axis=0)

    def load_state(ref, prefix, p):
        return jnp.concatenate([ref[prefix + (p * GROUP_HEADS + h,)].T
                                for h in range(GROUP_HEADS)], axis=1)

    def store_state(ref, prefix, p, st):
        for h in range(GROUP_HEADS):
            ref[prefix + (p * GROUP_HEADS + h,)] = st[:, h * HEAD_DIM:(h + 1) * HEAD_DIM].T

    chunks = [(gi, u) for gi in range(groups) for u in range(U)]
    if sequential:
        @pl.when(step == 0)
        def _():
            for p in range(PAIRS):
                sbd[p] = load_state(s0_ref, (), p)
        s_start = {p: sbd[p] for p in range(PAIRS)}
    else:
        s_start = {(idx, p): load_state(s0_ref, (idx,), p)
                   for idx in range(len(chunks)) for p in range(PAIRS)}

    def chain(gi, out, s_in):
        if sequential:
            states = {p: [s_in[p]] for p in range(PAIRS)}
            for u in range(U):
                for p in range(PAIRS):
                    d, s_cur = out[u, p], states[p][-1]
                    states[p].append(s_cur * d["eg_last"] - _mm_nt(s_cur, d["M"]) + d["NT"])
            return ({(u, p): states[p][u] for (u, p) in items},
                    {p: states[p][-1] for p in range(PAIRS)})
        entering = {(u, p): s_in[gi * U + u, p] for (u, p) in items}
        s_out = {}
        for (u, p) in items:
            d, s_cur = out[u, p], entering[u, p]
            s_out[gi * U + u, p] = s_cur * d["eg_last"] - _mm_nt(s_cur, d["M"]) + d["NT"]
        return entering, s_out

    def finish(gi, out, entering):
        r = {it: _mm_nt(jnp.concatenate([out[it]["W"], out[it]["qd"]], axis=0),
                        stack(entering[it])) for it in items}
        yield
        tiles = {it: r[it][C:] + _mm(out[it]["QK"], stack(out[it]["U"] - r[it][:C]))
                 for it in items}
        yield
        rg = slice(gi * RG, (gi + 1) * RG)
        o = jnp.concatenate([jnp.concatenate([tiles[u, p] for p in range(PAIRS)], axis=1)
                             for u in range(U)], axis=0)
        ms = seg_sum(o * o) * (1.0 / HEAD_DIM)
        hz = 0.5 * z_ref[rg, :]
        o_ref[rg, :] = (o * lax.rsqrt(ms + NORM_EPS) * normw_ref[...]
                        * (hz * (1.0 + jnp.tanh(hz)))).astype(BF16)

    def drive(gens):
        results = [None] * len(gens)
        active = list(enumerate(gens))
        while active:
            still = []
            for i, gen in active:
                try:
                    next(gen)
                    still.append((i, gen))
                except StopIteration as stop:
                    results[i] = stop.value
            active = still
        return results

    state = s_start
    s_final = {}
    f_cur = drive([front(0)])[0]
    pending = None
    for gi in range(groups + 1):
        gens = []
        if gi < groups:
            gens.append(middle(f_cur))
        if gi + 1 < groups:
            gens.append(front(gi + 1))
        if pending is not None:
            entering, s_out = chain(gi - 1, pending, state)
            s_final.update(s_out)
            state = s_out if sequential else state
            gens.append(finish(gi - 1, pending, entering))
        res = drive(gens)
        pending = res[0] if gi < groups else None
        f_cur = res[1] if gi + 1 < groups else None

    if sequential:
        for p in range(PAIRS):
            sbd[p] = s_final[p]

        @pl.when(step == last_step)
        def _():
            for p in range(PAIRS):
                store_state(s_ref, (), p, s_final[p])
    else:
        for (idx, p), s_new in s_final.items():
            store_state(s_ref, (idx,), p, s_new)


def _delta(qkv, z, ab, s0, consts, n_seq, seq_len, chunk, units, groups):
    alog, dtb, normw, seg, expand = consts
    sequential = n_seq == 1
    rows = n_seq * seq_len
    n_chunks = units * groups
    blk_rows = n_chunks * chunk
    assert rows % blk_rows == 0 and (sequential or seq_len == chunk)
    const2 = lambda shape: pl.BlockSpec(shape, lambda i: (0, 0))
    tok = lambda width: pl.BlockSpec((blk_rows, width), lambda i: (i, 0))
    group_w = GROUP_HEADS * HEAD_DIM
    if sequential:
        state = pl.BlockSpec((None, HEADS, HEAD_DIM, HEAD_DIM), lambda i: (0, 0, 0, 0))
    else:
        state = pl.BlockSpec((n_chunks, HEADS, HEAD_DIM, HEAD_DIM), lambda i: (i, 0, 0, 0))
    return pl.pallas_call(
        functools.partial(_delta_kernel, chunk=chunk, units=units, groups=groups,
                          sequential=sequential),
        out_shape=[jax.ShapeDtypeStruct((rows, DN_W), BF16),
                   jax.ShapeDtypeStruct((n_seq, HEADS, HEAD_DIM, HEAD_DIM), F32)],
        grid=(rows // blk_rows,),
        in_specs=[tok(CONV_CH), tok(DN_W), tok(LANES), state,
                  const2((1, LANES)), const2((1, LANES)),
                  const2((1, DN_W)), const2((DN_W, DN_W)), const2((LANES, DN_W))],
        out_specs=[tok(DN_W), state],
        scratch_shapes=[pltpu.VMEM((HEADS // GROUP_HEADS, HEAD_DIM, group_w), F32)],
        compiler_params=pltpu.CompilerParams(
            dimension_semantics=("arbitrary",), vmem_limit_bytes=VMEM_LIMIT,
        ),
        name=f"delta_c{chunk}",
    )(qkv, z, ab, s0, alog, dtb, normw, seg, expand)


SWA_W = SWA_HEADS * HEAD_DIM


def _swa_kernel(*refs, n_q, units, prompt, batch=SWA_BATCH):
    if prompt:
        (table_ref, sinks_ref, bucket_ref, q_ref, km_ref, kp_ref, kc_ref, vm_ref, vp_ref, vc_ref,
         o_ref, bias_ref) = refs
    else:
        (table_ref, sinks_ref, bucket_ref, q_ref, kp_ref, kc_ref, vp_ref, vc_ref,
         o_ref, bias_ref) = refs
    n_k = bias_ref.shape[-1]
    n_grp = SWA_W // KV_W
    step = pl.program_id(0)

    @pl.when(step == 0)
    def _():
        bucket = bucket_ref[...]
        for h in range(SWA_HEADS):
            acc = jnp.zeros((n_q, n_k), F32)
            for b in range(N_BUCKETS):
                acc = jnp.where(bucket == b, table_ref[b * SWA_HEADS + h], acc)
            kv, gi = divmod(h, SWA_G)
            bias_ref[kv, gi * n_q:(gi + 1) * n_q, :] = acc * LOG2E

    lane = lax.broadcasted_iota(jnp.int32, (1, KV_W), 1)
    kv_mask = [(lane < HEAD_DIM).astype(F32), (lane >= HEAD_DIM).astype(F32)]
    if prompt:
        first = step == 0
        k_all = jnp.concatenate([jnp.where(first, km_ref[...], kp_ref[...]), kc_ref[...]], axis=0)
        v_all = jnp.concatenate([jnp.where(first, vm_ref[...], vp_ref[...]), vc_ref[...]], axis=0)
        k_kv = [(k_all * m).astype(BF16) for m in kv_mask]
        v_kv = [(v_all * m).astype(BF16) for m in kv_mask]
        keys = lambda j, kv: k_kv[kv][j * n_q:j * n_q + n_k]
        vals = lambda j, kv: v_kv[kv][j * n_q:j * n_q + n_k]
    else:
        n_c = n_k - n_q
        cat = lambda a, b, j: jnp.concatenate([a[j * n_c:(j + 1) * n_c, :],
                                               b[j * n_q:(j + 1) * n_q, :]], axis=0)
        keys = lambda j, kv: (cat(kp_ref, kc_ref, j) * kv_mask[kv]).astype(BF16)
        vals = lambda j, kv: (cat(vp_ref, vc_ref, j) * kv_mask[kv]).astype(BF16)

    q = (q_ref[...] * (HEAD_DIM ** -0.5 * LOG2E)).astype(BF16)
    ones_k = jnp.ones((n_k, KV_W), BF16)
    sink = [jnp.concatenate([jnp.full((n_q, KV_W), sinks_ref[kv * SWA_G + gi] * LOG2E, F32)
                             for gi in range(SWA_G)], axis=0) for kv in range(SWA_KV)]
    rows4 = SWA_G * n_q
    for j0 in range(0, units, batch):
        js = range(j0, min(j0 + batch, units))
        items = [(j, kv) for j in js for kv in range(SWA_KV)]
        s = {}
        for j in js:
            rows = slice(j * n_q, (j + 1) * n_q)
            q4 = jnp.concatenate([q[rows, g * KV_W:(g + 1) * KV_W] for g in range(n_grp)], axis=0)
            for kv in range(SWA_KV):
                sj = lax.dot_general(q4, keys(j, kv), (((1,), (1,)), ((), ())),
                                     preferred_element_type=F32) + bias_ref[kv]
                if prompt and j * n_q < WINDOW - N_META:
                    key_pos = j * n_q + lax.broadcasted_iota(jnp.int32, (1, n_k), 1)
                    n_invalid = jnp.where(first, WINDOW - N_META, 0)
                    sj = jnp.where(key_pos < n_invalid, -jnp.inf, sj)
                s[j, kv] = sj
        m = {it: jnp.maximum(jnp.broadcast_to(jnp.max(s[it], axis=-1, keepdims=True),
                                              (rows4, KV_W)), sink[it[1]]) for it in items}
        p = {it: jnp.exp2(s[it] - jnp.concatenate([m[it], m[it][:, :n_k - KV_W]], axis=1))
             for it in items}
        pv = {it: jnp.dot(p[it].astype(BF16), jnp.concatenate([vals(*it), ones_k], axis=1),
                          preferred_element_type=F32) for it in items}
        o = {it: pv[it][:, :KV_W] * (1.0 / (pv[it][:, KV_W:] + jnp.exp2(sink[it[1]] - m[it])))
             for it in items}
        for j in js:
            o2 = o[j, 0] + o[j, 1]
            for g in range(n_grp):
                o_ref[j * n_q:(j + 1) * n_q, g * KV_W:(g + 1) * KV_W] = (
                    o2[g * n_q:(g + 1) * n_q, :].astype(BF16))


def _swa(table, sinks, bucket, q, kv_args, kv_specs, n_steps, n_q, units, prompt, name):
    n_k = bucket.shape[1]
    smem = pl.BlockSpec(memory_space=pltpu.SMEM)
    rows = units * n_q
    return pl.pallas_call(
        functools.partial(_swa_kernel, n_q=n_q, units=units, prompt=prompt),
        out_shape=jax.ShapeDtypeStruct((n_steps * rows, SWA_W), BF16),
        grid=(n_steps,),
        in_specs=[smem, smem, pl.BlockSpec((n_q, n_k), lambda i: (0, 0)),
                  pl.BlockSpec((rows, SWA_W), lambda i: (i, 0))] + list(kv_specs),
        out_specs=pl.BlockSpec((rows, SWA_W), lambda i: (i, 0)),
        scratch_shapes=[pltpu.VMEM((SWA_KV, SWA_G * n_q, n_k), F32)],
        compiler_params=pltpu.CompilerParams(dimension_semantics=("arbitrary",),
                                             vmem_limit_bytes=VMEM_LIMIT),
        name=name,
    )(table, sinks, bucket, q, *kv_args)


def _t5_bucket(rel):
    half = N_BUCKETS // 2
    max_exact = half // 2
    n = jnp.abs(rel)
    large = max_exact + (jnp.log(jnp.maximum(n, 1).astype(F32) / max_exact)
                         / math.log(MAX_DIST / max_exact) * (half - max_exact)).astype(jnp.int32)
    large = jnp.minimum(large, half - 1)
    return jnp.where(rel > 0, half, 0) + jnp.where(n < max_exact, n, large)


def _bucket_map(n_q, n_k, key_offset):
    rel = (jnp.arange(n_k)[None, :] - key_offset) - jnp.arange(n_q)[:, None]
    return _t5_bucket(rel).astype(jnp.int32)


FF_BLOCK = 1024


FF_STAGE = 512


def _post_kernel(hp_ref, dnp_ref, swp_ref, hs_ref, dns_ref, sws_ref,
                 wod_ref, wos_ref, g1_ref, b1_ref, w1_hbm, w2_hbm, g2_ref, b2_ref,
                 yp_ref, ys_ref, w1_ref, w2_ref, stage1, stage2, sems, *, n_prompt_tiles):
    step = pl.program_id(0)
    is_prompt = step < n_prompt_tiles
    n_stage = D_FF // FF_STAGE

    def copies(c):
        slot, blk = c % 2, pl.ds(c * FF_STAGE, FF_STAGE)
        return (pltpu.make_async_copy(w1_hbm.at[0, :, blk], stage1.at[slot], sems.at[0, slot]),
                pltpu.make_async_copy(w2_hbm.at[0, blk, :], stage2.at[slot], sems.at[1, slot]))

    def start(c):
        for cp in copies(c):
            cp.start()

    def fetch(j):
        per = FF_BLOCK // FF_STAGE
        for c in range(j * per, (j + 1) * per):
            for cp in copies(c):
                cp.wait()
            w1_ref[:, c * FF_STAGE:(c + 1) * FF_STAGE] = stage1[c % 2].astype(BF16)
            w2_ref[c * FF_STAGE:(c + 1) * FF_STAGE, :] = stage2[c % 2].astype(BF16)
            if c + 2 < n_stage:
                start(c + 2)

    @pl.when(step == 0)
    def _():
        start(0)
        start(1)
        _post_tile(hp_ref, dnp_ref, swp_ref, wod_ref, wos_ref, g1_ref, b1_ref,
                   w1_ref, w2_ref, g2_ref, b2_ref, yp_ref, fetch)

    @pl.when(jnp.logical_and(step > 0, is_prompt))
    def _():
        _post_tile(hp_ref, dnp_ref, swp_ref, wod_ref, wos_ref, g1_ref, b1_ref,
                   w1_ref, w2_ref, g2_ref, b2_ref, yp_ref)

    @pl.when(jnp.logical_not(is_prompt))
    def _():
        _post_tile(hs_ref, dns_ref, sws_ref, wod_ref, wos_ref, g1_ref, b1_ref,
                   w1_ref, w2_ref, g2_ref, b2_ref, ys_ref)


def _post_tile(h_ref, dn_ref, sw_ref, wod_ref, wos_ref, g1_ref, b1_ref,
               w1_ref, w2_ref, g2_ref, b2_ref, y_ref, fetch=None):
    mix = (jnp.dot(dn_ref[...], wod_ref[...], preferred_element_type=F32)
           + jnp.dot(sw_ref[...], wos_ref[...], preferred_element_type=F32))
    h1 = _layer_norm(DEEP_ALPHA * h_ref[...] + mix, g1_ref[...], b1_ref[...])
    h1b = h1.astype(BF16)
    f = jnp.zeros_like(h1)
    for j in range(D_FF // FF_BLOCK):
        if fetch is not None:
            fetch(j)
        a = jnp.dot(h1b, w1_ref[:, j * FF_BLOCK:(j + 1) * FF_BLOCK], preferred_element_type=F32)
        a = jnp.square(jnp.maximum(a, 0.0)).astype(BF16)
        f = f + jnp.dot(a, w2_ref[j * FF_BLOCK:(j + 1) * FF_BLOCK, :], preferred_element_type=F32)
    y_ref[...] = _layer_norm(DEEP_ALPHA * h1 + f, g2_ref[...], b2_ref[...])


def _post(prompt, sample, vecs, weights, tm):
    g1, b1, g2, b2 = vecs
    wo_dn, wo_sw, w1, w2 = weights
    n_p, n_s = prompt[0].shape[0] // tm, sample[0].shape[0] // tm
    assert n_p >= 1
    hbm = pl.BlockSpec(memory_space=pl.ANY)
    const = lambda i: (0, 0)
    vec = pl.BlockSpec((1, D_MODEL), const)
    weight = lambda wt: pl.BlockSpec(wt.shape, const, pipeline_mode=pl.Buffered(1))
    blk_p = lambda width: pl.BlockSpec((tm, width), lambda i: (jnp.minimum(i, n_p - 1), 0))
    blk_s = lambda width: pl.BlockSpec((tm, width), lambda i: (jnp.maximum(i - n_p, 0), 0))
    widths = (D_MODEL, DN_W, SWA_W)
    return pl.pallas_call(
        functools.partial(_post_kernel, n_prompt_tiles=n_p),
        out_shape=[jax.ShapeDtypeStruct((n_p * tm, D_MODEL), F32),
                   jax.ShapeDtypeStruct((n_s * tm, D_MODEL), F32)],
        grid=(n_p + n_s,),
        in_specs=[blk_p(wd) for wd in widths] + [blk_s(wd) for wd in widths]
                 + [weight(wo_dn), weight(wo_sw), vec, vec, hbm, hbm, vec, vec],
        out_specs=[blk_p(D_MODEL), blk_s(D_MODEL)],
        scratch_shapes=[pltpu.VMEM((D_MODEL, D_FF), BF16), pltpu.VMEM((D_FF, D_MODEL), BF16),
                        pltpu.VMEM((2, D_MODEL, FF_STAGE), F32),
                        pltpu.VMEM((2, FF_STAGE, D_MODEL), F32),
                        pltpu.SemaphoreType.DMA((2, 2))],
        compiler_params=pltpu.CompilerParams(dimension_semantics=("arbitrary",),
                                             vmem_limit_bytes=VMEM_LIMIT),
        name="post_mlp",
    )(*prompt, *sample, wo_dn, wo_sw, g1, b1, w1, w2, g2, b2)


def kernel(x_prompt, x_sample, state_delta, state_conv, cache_swa_k, cache_swa_v, meta_tokens, ln_in_g, ln_in_b, w_in, w_conv, dn_a_log, dn_dt_bias, dn_norm, swa_sinks, rel_bias_table, w_o, ln1_g, ln1_b, w_ff1, w_ff2, ln2_g, ln2_b):
    assert w_in.shape[0] == DEPTH == 1
    n_seq_s, len_s = x_sample.shape[0], x_sample.shape[1]
    len_p = x_prompt.shape[1]
    cache_len = cache_swa_k.shape[2]
    chunk_p = DELTA_CHUNK
    row = lambda t: t.reshape(1, -1).astype(F32)

    w = w_in[0]
    w_sq = (w[:, OFF_SQ:OFF_SK].astype(BF16).reshape(D_MODEL, SWA_KV, SWA_G, HEAD_DIM)
            .transpose(0, 2, 1, 3).reshape(D_MODEL, SWA_W))
    w_r = (w[:, :OFF_A].astype(BF16), w_sq, w[:, OFF_SK:].astype(BF16),
           jnp.pad(w[:, OFF_A:OFF_SQ].astype(BF16), ((0, 0), (0, LANES - 2 * HEADS))))
    gin, bin_ = row(ln_in_g), row(ln_in_b)
    pad_lanes = lambda t: jnp.pad(t.reshape(1, -1).astype(F32), ((0, 0), (0, LANES - HEADS)))
    lane_head = jnp.arange(DN_W) // HEAD_DIM
    seg = (lane_head[:, None] == lane_head[None, :]).astype(BF16)
    expand = (jnp.arange(LANES)[:, None] == lane_head[None, :]).astype(BF16)
    wconv = jnp.pad(w_conv[0].astype(F32), ((0, SUBLANES - CONV_W), (0, 0)))
    delta_consts = (pad_lanes(dn_a_log[0]), pad_lanes(dn_dt_bias[0]),
                    jnp.tile(dn_norm[0].astype(F32), HEADS).reshape(1, DN_W), seg, expand)
    post_vecs = (row(ln1_g[0]), row(ln1_b[0]), row(ln2_g[0]), row(ln2_b[0]))
    wo_sw = (w_o[0][DN_W:].astype(BF16).reshape(SWA_KV, SWA_G, HEAD_DIM, D_MODEL)
             .transpose(1, 0, 2, 3).reshape(SWA_W, D_MODEL))
    post_w = (w_o[0][:DN_W].astype(BF16), wo_sw, w_ff1.astype(F32), w_ff2.astype(F32))
    table = rel_bias_table.astype(F32).reshape(-1)
    sinks = swa_sinks[0].astype(F32)

    xp = x_prompt[0]
    xs = x_sample.reshape(n_seq_s * len_s, D_MODEL)

    conv_pad = SUBLANES - (CONV_W - 1)
    zero_conv = jnp.zeros((1, SUBLANES, CONV_CH), F32)
    conv_s = jnp.pad(state_conv[0].astype(F32), ((0, 0), (conv_pad, 0), (0, 0)))
    m_qkv, m_z, m_sq, m_sk, m_sv, m_ab, m_tail, _ = _proj(
        meta_tokens.astype(F32), gin, bin_, w_r, zero_conv, wconv, N_META, N_META)
    p_qkv, p_z, p_sq, p_sk, p_sv, p_ab, p_tail, p_h = _proj(
        xp, gin, bin_, w_r, m_tail, wconv, PROMPT_PROJ_TILE, None)
    s_qkv, s_z, s_sq, s_sk, s_sv, s_ab, s_tail, s_h = _proj(
        xs, gin, bin_, w_r, conv_s, wconv, ROW_TILE, len_s)

    zero_s = jnp.zeros((1, HEADS, HEAD_DIM, HEAD_DIM), F32)
    _, s_meta = _delta(m_qkv, m_z, m_ab, zero_s, delta_consts, 1, N_META, N_META, 1, 1)
    p_dn, p_s = _delta(p_qkv, p_z, p_ab, s_meta, delta_consts, 1, len_p, chunk_p, *DELTA_PROMPT)
    s_dn, s_s = _delta(s_qkv, s_z, s_ab, state_delta[0].astype(F32), delta_consts,
                       n_seq_s, len_s, len_s, *DELTA_SAMPLE)

    span = WINDOW
    lead = jnp.zeros((span - N_META, KV_W), F32)
    units_p = SWA_UNITS_PROMPT
    rows_p = units_p * chunk_p
    prev_blocks = rows_p // span
    kv_specs_p = [pl.BlockSpec((span, KV_W), lambda i: (0, 0)),
                  pl.BlockSpec((span, KV_W), lambda i: (jnp.maximum(i * prev_blocks - 1, 0), 0)),
                  pl.BlockSpec((rows_p, KV_W), lambda i: (i, 0))] * 2
    p_sw = _swa(table, sinks, _bucket_map(chunk_p, span + chunk_p, span), p_sq,
                [jnp.concatenate([lead, m_sk], axis=0), p_sk, p_sk,
                 jnp.concatenate([lead, m_sv], axis=0), p_sv, p_sv],
                kv_specs_p, len_p // rows_p, chunk_p, units_p, True, "swa_prompt")
    ck = cache_swa_k[0].astype(F32).reshape(n_seq_s * cache_len, KV_W)
    cv = cache_swa_v[0].astype(F32).reshape(n_seq_s * cache_len, KV_W)
    units_s = SWA_UNITS_SAMPLE
    kv_specs_s = [pl.BlockSpec((units_s * cache_len, KV_W), lambda i: (i, 0)),
                  pl.BlockSpec((units_s * len_s, KV_W), lambda i: (i, 0))] * 2
    s_sw = _swa(table, sinks, _bucket_map(len_s, cache_len + len_s, cache_len), s_sq,
                [ck, s_sk, cv, s_sv], kv_specs_s, n_seq_s // units_s, len_s, units_s, False,
                "swa_sample")

    y_p, y_s = _post((p_h, p_dn, p_sw), (s_h, s_dn, s_sw), post_vecs, post_w, ROW_TILE)

    kv_shape = lambda n, length: (1, n, length, SWA_KV, HEAD_DIM)
    return (y_p[None], y_s.reshape(x_sample.shape),
            p_s[None], p_tail[:, -(CONV_W - 1):][None],
            p_sk[-WINDOW:].reshape(kv_shape(1, WINDOW)), p_sv[-WINDOW:].reshape(kv_shape(1, WINDOW)),
            s_s[None], s_tail[:, -(CONV_W - 1):][None],
            s_sk.reshape(kv_shape(n_seq_s, len_s)), s_sv.reshape(kv_shape(n_seq_s, len_s)))
```

```python
import functools
import math

import jax
import jax.numpy as jnp
from jax import lax
from jax.experimental import pallas as pl
from jax.experimental.pallas import tpu as pltpu

F32 = jnp.float32
BF16 = jnp.bfloat16

D_MODEL = 1024
N_META = 16
HEADS = 8
HEAD_DIM = 64
DN_W = HEADS * HEAD_DIM
CONV_W = 4
CONV_CH = 3 * DN_W
SWA_HEADS = 8
SWA_KV = 2
SWA_G = SWA_HEADS // SWA_KV
KV_W = SWA_KV * HEAD_DIM
WINDOW = 128
N_BUCKETS = 32
MAX_DIST = 128
D_FF = 4 * D_MODEL
DEPTH = 1
DEEP_ALPHA = (2 * DEPTH) ** 0.25
LN_EPS = 1e-5
NORM_EPS = 1e-6
LOG2E = math.log2(math.e)
LANES = 128
SUBLANES = 8
VMEM_LIMIT = 56 * 1024 * 1024

OFF_Z = CONV_CH
OFF_A = OFF_Z + DN_W
OFF_B = OFF_A + HEADS
OFF_SQ = OFF_B + HEADS
OFF_SK = OFF_SQ + SWA_HEADS * HEAD_DIM
PROJ_SPLITS = (CONV_CH, DN_W, SWA_HEADS * HEAD_DIM, KV_W, KV_W, LANES)

ROW_TILE = 512
PROMPT_PROJ_TILE = 1024
DELTA_CHUNK = 64
DELTA_PROMPT = (4, 2)
DELTA_SAMPLE = (8, 2)
SWA_UNITS_PROMPT = 16
SWA_UNITS_SAMPLE = 8
SWA_BATCH = 4


def _layer_norm(x, g, b):
    mu = jnp.mean(x, axis=-1, keepdims=True)
    xc = x - mu
    var = jnp.mean(xc * xc, axis=-1, keepdims=True)
    return xc * lax.rsqrt(var + LN_EPS) * g + b


def _mm(a, b):
    return jnp.dot(a.astype(BF16), b.astype(BF16), preferred_element_type=F32)


def _mm_nt(a, b):
    return lax.dot_general(a.astype(BF16), b.astype(BF16), (((1,), (1,)), ((), ())),
                           preferred_element_type=F32)


def _split3(x):
    x1 = x.astype(BF16)
    r1 = x - x1.astype(F32)
    x2 = r1.astype(BF16)
    x3 = (r1 - x2.astype(F32)).astype(BF16)
    return x1, x2, x3


def _dot_exact_lhs(x, sel):
    return jnp.dot(jnp.concatenate(_split3(x), axis=1), jnp.concatenate([sel] * 3, axis=0),
                   preferred_element_type=F32)


CONV_BLOCK = 256


def _sigmoid(x):
    return 1.0 / (1.0 + jnp.exp(-x))


def _proj_kernel(x_ref, g_ref, b_ref, w_ref, wsq_ref, wkv_ref, wab_ref, conv0_ref, wconv_ref,
                 y_ref, z_ref, sq_ref, sk_ref, sv_ref, ab_ref, tail_ref, h_ref, xbuf, *, seq_len):
    tm = x_ref.shape[0]
    h_ref[...] = _layer_norm(x_ref[...], g_ref[...], b_ref[...])
    h = h_ref[...].astype(BF16)

    chained = seq_len is None
    n_seq = 1 if chained else tm // seq_len
    L = tm if chained else seq_len
    if chained:
        @pl.when(pl.program_id(0) == 0)
        def _():
            xbuf[0, 0:SUBLANES, :] = conv0_ref[0]
    else:
        for s in range(n_seq):
            xbuf[s, 0:SUBLANES, :] = conv0_ref[s]
    blocks = [slice(cb, cb + CONV_BLOCK) for cb in range(0, CONV_CH, CONV_BLOCK)]

    def project(cols):
        raw = jnp.dot(h, w_ref[:, cols], preferred_element_type=F32)
        for s in range(n_seq):
            xbuf[s, SUBLANES:SUBLANES + L, cols] = raw[s * L:(s + 1) * L, :]

    def conv(cols):
        for s in range(n_seq):
            ext = xbuf[s, :, cols]
            w0, w1, w2, w3 = (0.5 * wconv_ref[j:j + 1, cols] for j in range(CONV_W))
            ext1 = pltpu.roll(ext, 1, axis=0)
            half = (w3 * ext + w2 * ext1)[SUBLANES:, :]
            half = half + pltpu.roll(w1 * ext + w0 * ext1, 2, axis=0)[SUBLANES:, :]
            y_ref[s * L:(s + 1) * L, cols] = half * (1.0 + jnp.tanh(half))
            tail_ref[s, :, cols] = xbuf[s, L:L + SUBLANES, cols]
            if chained:
                xbuf[s, 0:SUBLANES, cols] = xbuf[s, L:L + SUBLANES, cols]

    for o_ref, rhs in ((z_ref, w_ref[:, CONV_CH:]), (sq_ref, wsq_ref[...]),
                       (sk_ref, wkv_ref[:, :KV_W]), (sv_ref, wkv_ref[:, KV_W:]),
                       (ab_ref, wab_ref[...])):
        o_ref[...] = jnp.dot(h, rhs, preferred_element_type=F32)
    project(blocks[0])
    for bi, cols in enumerate(blocks):
        if bi + 1 < len(blocks):
            project(blocks[bi + 1])
        conv(cols)


def _proj(x, g, b, weights, conv0, wconv, tm, seq_len):
    rows = x.shape[0]
    w_main, w_sq, w_kv, w_ab = weights
    chained = seq_len is None
    n_tile_seq = 1 if chained else tm // seq_len
    n_seq = 1 if chained else rows // seq_len
    const = lambda i: (0, 0)
    row_blk = lambda width: pl.BlockSpec((tm, width), lambda i: (i, 0))
    seq_blk = pl.BlockSpec((n_tile_seq, SUBLANES, CONV_CH),
                           (lambda i: (0, 0, 0)) if chained else (lambda i: (i, 0, 0)))
    return pl.pallas_call(
        functools.partial(_proj_kernel, seq_len=seq_len),
        out_shape=[jax.ShapeDtypeStruct((rows, width), F32) for width in PROJ_SPLITS]
                  + [jax.ShapeDtypeStruct((n_seq, SUBLANES, CONV_CH), F32),
                     jax.ShapeDtypeStruct((rows, D_MODEL), F32)],
        grid=(rows // tm,),
        in_specs=[row_blk(D_MODEL), pl.BlockSpec((1, D_MODEL), const),
                  pl.BlockSpec((1, D_MODEL), const)]
                 + [pl.BlockSpec(wt.shape, const, pipeline_mode=pl.Buffered(1)) for wt in weights]
                 + [seq_blk, pl.BlockSpec((SUBLANES, CONV_CH), const)],
        out_specs=[row_blk(width) for width in PROJ_SPLITS] + [seq_blk, row_blk(D_MODEL)],
        scratch_shapes=[pltpu.VMEM((n_tile_seq, SUBLANES + (tm if chained else seq_len), CONV_CH),
                                   F32)],
        compiler_params=pltpu.CompilerParams(dimension_semantics=("arbitrary",),
                                             vmem_limit_bytes=VMEM_LIMIT),
        name="ln_in_proj",
    )(x, g, b, w_main, w_sq, w_kv, w_ab, conv0, wconv)


GROUP_HEADS = 4


def _softplus(x):
    return jnp.maximum(x, 0.0) + jnp.log1p(jnp.exp(-jnp.abs(x)))


def _delta_kernel(qkv_ref, z_ref, ab_ref, s0_ref, alog_ref, dtb_ref,
                  normw_ref, seg_ref, expand_ref, o_ref, s_ref, sbd,
                  *, chunk, units, groups, sequential):
    C, U = chunk, units
    RG = U * C
    PAIRS = HEADS // GROUP_HEADS
    PW = GROUP_HEADS * HEAD_DIM
    step = pl.program_id(0)
    last_step = pl.num_programs(0) - 1
    expand = expand_ref[...]

    def seg_sum(t):
        tb = t.astype(BF16)
        return jnp.concatenate(
            [jnp.dot(tb[:, g * PW:(g + 1) * PW], seg_ref[g * PW:(g + 1) * PW, g * PW:(g + 1) * PW],
                     preferred_element_type=F32) for g in range(PAIRS)], axis=1)

    ri = lax.broadcasted_iota(jnp.int32, (C, PW), 0)
    ci = lax.broadcasted_iota(jnp.int32, (C, PW), 1) % HEAD_DIM
    causal = (ri >= ci) & (ci < C)
    strict = (ri > ci) & (ci < C)
    eye = (ri == ci).astype(F32)
    lane_head = lax.broadcasted_iota(jnp.int32, (1, PW), 1) // HEAD_DIM
    rs = lax.broadcasted_iota(jnp.int32, (LANES, LANES), 0)
    cs = lax.broadcasted_iota(jnp.int32, (LANES, LANES), 1)
    half_mask = ((rs // HEAD_DIM) == (cs // HEAD_DIM)).astype(F32)
    row_pad = [jnp.zeros((HEAD_DIM - C, PW), BF16)] if C < HEAD_DIM else []
    lane_pad = [jnp.zeros((1, HEAD_DIM - C), F32)] if C < HEAD_DIM else []
    rr = lax.broadcasted_iota(jnp.int32, (RG, RG), 0)
    cc = lax.broadcasted_iota(jnp.int32, (RG, RG), 1)
    tril = ((rr // C == cc // C) & (rr >= cc)).astype(BF16)

    def stack(y):
        yb = y.astype(BF16)
        zero = jnp.zeros_like(yb)
        pad = row_pad if y.shape[0] < HEAD_DIM else []
        blocks = []
        for h in range(GROUP_HEADS):
            blocks += [jnp.where(lane_head == h, yb, zero)] + pad
        return jnp.concatenate(blocks, axis=0)

    items = [(u, p) for u in range(U) for p in range(PAIRS)]
    sl = lambda u, p: (slice(u * C, (u + 1) * C), slice(p * PW, (p + 1) * PW))

    def front(gi):
        rg = slice(gi * RG, (gi + 1) * RG)
        q, k, v = (qkv_ref[rg, j * DN_W:(j + 1) * DN_W] for j in range(3))
        qn = q * (lax.rsqrt(seg_sum(q * q) + NORM_EPS) * HEAD_DIM ** -0.5)
        kn = k * lax.rsqrt(seg_sum(k * k) + NORM_EPS)
        yield

        ab = ab_ref[rg, :]
        g = -jnp.exp(alog_ref[...]) * _softplus(ab + dtb_ref[...])
        gs = jnp.dot(tril, jnp.concatenate(_split3(g), axis=1), preferred_element_type=F32)
        G = (gs[:, :LANES] + gs[:, LANES:2 * LANES] + gs[:, 2 * LANES:]) * LOG2E
        GT = G.T
        G_x = _dot_exact_lhs(G, expand)
        beta = _sigmoid(pltpu.roll(ab, LANES - HEADS, axis=1))
        b_hi = beta.astype(BF16)
        b_lo = (beta - b_hi.astype(F32)).astype(BF16)
        beta_x = jnp.dot(jnp.concatenate([b_hi, b_lo], axis=1),
                         jnp.concatenate([expand, expand], axis=0), preferred_element_type=F32)
        glast_x = jnp.concatenate(
            [jnp.broadcast_to(G_x[(u + 1) * C - 1:(u + 1) * C, :], (C, DN_W)) for u in range(U)],
            axis=0)
        yield
        eg_x = jnp.exp2(G_x)
        kbeta = kn * beta_x
        return dict(qn=qn, kn=kn, kbeta=kbeta, bk=kbeta * eg_x, bv=v * beta_x, qd=qn * eg_x,
                    kd=kn * jnp.exp2(glast_x - G_x), G_x=G_x, GT=GT, eg_x=eg_x)

    def middle(f):
        A, QK = {}, {}
        for (u, p) in items:
            ru, lp = sl(u, p)
            gr = []
            for h in range(p * GROUP_HEADS, (p + 1) * GROUP_HEADS):
                gr += [f["GT"][h:h + 1, ru]] + lane_pad
            gr = jnp.concatenate(gr, axis=1)
            decay = jnp.exp2(jnp.where(causal, f["G_x"][ru, lp] - gr, -jnp.inf))
            r = _mm_nt(jnp.concatenate([f["kbeta"][ru, lp], f["qn"][ru, lp]], axis=0),
                       stack(f["kn"][ru, lp]))
            A[u, p] = jnp.where(strict, r[:C] * decay, 0.0)
            QK[u, p] = r[C:] * decay
        yield

        n_pow = int(math.log2(C)) - 1
        P = {it: eye - A[it] for it in items}
        Ap = {it: _mm(A[it], stack(A[it])) for it in items}
        yield
        for i in range(n_pow):
            if i < n_pow - 1:
                r = {it: _mm(jnp.concatenate([Ap[it], P[it]], axis=0), stack(Ap[it])) for it in items}
                Ap = {it: r[it][:C] for it in items}
                P = {it: P[it] + r[it][C:] for it in items}
            else:
                P = {it: P[it] + _mm(P[it], stack(Ap[it])) for it in items}
            yield

        halves = [slice(hb * LANES, (hb + 1) * LANES) for hb in range(PW // LANES)]
        wu = {it: _mm(P[it], jnp.concatenate([stack(f["bk"][sl(*it)]), stack(f["bv"][sl(*it)])],
                                             axis=1)) for it in items}
        kdT = {(it, hb): f["kd"][sl(*it)][:, cols].T for it in items
               for hb, cols in enumerate(halves)}
        yield
        mn = {(it, hb): _mm(kdT[it, hb], jnp.concatenate(
            [wu[it][:, cols], wu[it][:, PW:][:, cols]], axis=1))
              for it in items for hb, cols in enumerate(halves)}
        yield
        n_t = {key: (half_mask * mn[key][:, LANES:]).T for key in mn}
        out = {}
        for (u, p) in items:
            ru, lp = sl(u, p)
            it = (u, p)
            out[it] = dict(
                W=wu[it][:, :PW], U=wu[it][:, PW:],
                M=diag_blocks([half_mask * mn[it, hb][:, :LANES]
                               for hb in range(len(halves))]).astype(BF16),
                NT=jnp.concatenate([n_t[it, hb][:HEAD_DIM] + n_t[it, hb][HEAD_DIM:]
                                    for hb in range(len(halves))], axis=1),
                QK=QK[it], qd=f["qd"][ru, lp],
                eg_last=f["eg_x"][(u + 1) * C - 1:(u + 1) * C, lp])
        return out

    def diag_blocks(blocks):
        n = len(blocks)
        rows = []
        for i, blk in enumerate(blocks):
            rows.append(jnp.concatenate(
                [blk if j == i else jnp.zeros_like(blk) for j in range(n)], axis=1))
        return jnp.concatenate(rows, axis=0)

    def load_state(ref, prefix, p):
        return jnp.concatenate([ref[prefix + (p * GROUP_HEADS + h,)].T
                                for h in range(GROUP_HEADS)], axis=1)

    def store_state(ref, prefix, p, st):
        for h in range(GROUP_HEADS):
            ref[prefix + (p * GROUP_HEADS + h,)] = st[:, h * HEAD_DIM:(h + 1) * HEAD_DIM].T

    chunks = [(gi, u) for gi in range(groups) for u in range(U)]
    if sequential:
        @pl.when(step == 0)
        def _():
            for p in range(PAIRS):
                sbd[p] = load_state(s0_ref, (), p)
        s_start = {p: sbd[p] for p in range(PAIRS)}
    else:
        s_start = {(idx, p): load_state(s0_ref, (idx,), p)
                   for idx in range(len(chunks)) for p in range(PAIRS)}

    def chain(gi, out, s_in):
        if sequential:
            states = {p: [s_in[p]] for p in range(PAIRS)}
            for u in range(U):
                for p in range(PAIRS):
                    d, s_cur = out[u, p], states[p][-1]
                    states[p].append(s_cur * d["eg_last"] - _mm_nt(s_cur, d["M"]) + d["NT"])
            return ({(u, p): states[p][u] for (u, p) in items},
                    {p: states[p][-1] for p in range(PAIRS)})
        entering = {(u, p): s_in[gi * U + u, p] for (u, p) in items}
        s_out = {}
        for (u, p) in items:
            d, s_cur = out[u, p], entering[u, p]
            s_out[gi * U + u, p] = s_cur * d["eg_last"] - _mm_nt(s_cur, d["M"]) + d["NT"]
        return entering, s_out

    def finish(gi, out, entering):
        r = {it: _mm_nt(jnp.concatenate([out[it]["W"], out[it]["qd"]], axis=0),
                        stack(entering[it])) for it in items}
        yield
        tiles = {it: r[it][C:] + _mm(out[it]["QK"], stack(out[it]["U"] - r[it][:C]))
                 for it in items}
        yield
        rg = slice(gi * RG, (gi + 1) * RG)
        o = jnp.concatenate([jnp.concatenate([tiles[u, p] for p in range(PAIRS)], axis=1)
                             for u in range(U)], axis=0)
        ms = seg_sum(o * o) * (1.0 / HEAD_DIM)
        hz = 0.5 * z_ref[rg, :]
        o_ref[rg, :] = (o * lax.rsqrt(ms + NORM_EPS) * normw_ref[...]
                        * (hz * (1.0 + jnp.tanh(hz)))).astype(BF16)

    def drive(gens):
        results = [None] * len(gens)
        active = list(enumerate(gens))
        while active:
            still = []
            for i, gen in active:
                try:
                    next(gen)
                    still.append((i, gen))
                except StopIteration as stop:
                    results[i] = stop.value
            active = still
        return results

    state = s_start
    s_final = {}
    f_cur = drive([front(0)])[0]
    pending = None
    for gi in range(groups + 1):
        gens = []
        if gi < groups:
            gens.append(middle(f_cur))
        if gi + 1 < groups:
            gens.append(front(gi + 1))
        if pending is not None:
            entering, s_out = chain(gi - 1, pending, state)
            s_final.update(s_out)
            state = s_out if sequential else state
            gens.append(finish(gi - 1, pending, entering))
        res = drive(gens)
        pending = res[0] if gi < groups else None
        f_cur = res[1] if gi + 1 < groups else None

    if sequential:
        for p in range(PAIRS):
            sbd[p] = s_final[p]

        @pl.when(step == last_step)
        def _():
            for p in range(PAIRS):
                store_state(s_ref, (), p, s_final[p])
    else:
        for (idx, p), s_new in s_final.items():
            store_state(s_ref, (idx,), p, s_new)


def _delta(qkv, z, ab, s0, consts, n_seq, seq_len, chunk, units, groups):
    alog, dtb, normw, seg, expand = consts
    sequential = n_seq == 1
    rows = n_seq * seq_len
    n_chunks = units * groups
    blk_rows = n_chunks * chunk
    assert rows % blk_rows == 0 and (sequential or seq_len == chunk)
    const2 = lambda shape: pl.BlockSpec(shape, lambda i: (0, 0))
    tok = lambda width: pl.BlockSpec((blk_rows, width), lambda i: (i, 0))
    group_w = GROUP_HEADS * HEAD_DIM
    if sequential:
        state = pl.BlockSpec((None, HEADS, HEAD_DIM, HEAD_DIM), lambda i: (0, 0, 0, 0))
    else:
        state = pl.BlockSpec((n_chunks, HEADS, HEAD_DIM, HEAD_DIM), lambda i: (i, 0, 0, 0))
    return pl.pallas_call(
        functools.partial(_delta_kernel, chunk=chunk, units=units, groups=groups,
                          sequential=sequential),
        out_shape=[jax.ShapeDtypeStruct((rows, DN_W), BF16),
                   jax.ShapeDtypeStruct((n_seq, HEADS, HEAD_DIM, HEAD_DIM), F32)],
        grid=(rows // blk_rows,),
        in_specs=[tok(CONV_CH), tok(DN_W), tok(LANES), state,
                  const2((1, LANES)), const2((1, LANES)),
                  const2((1, DN_W)), const2((DN_W, DN_W)), const2((LANES, DN_W))],
        out_specs=[tok(DN_W), state],
        scratch_shapes=[pltpu.VMEM((HEADS // GROUP_HEADS, HEAD_DIM, group_w), F32)],
        compiler_params=pltpu.CompilerParams(
            dimension_semantics=("arbitrary",), vmem_limit_bytes=VMEM_LIMIT,
        ),
        name=f"delta_c{chunk}",
    )(qkv, z, ab, s0, alog, dtb, normw, seg, expand)


SWA_W = SWA_HEADS * HEAD_DIM


def _swa_kernel(*refs, n_q, units, prompt, batch=SWA_BATCH):
    if prompt:
        (table_ref, sinks_ref, bucket_ref, q_ref, km_ref, kp_ref, kc_ref, vm_ref, vp_ref, vc_ref,
         o_ref, bias_ref) = refs
    else:
        (table_ref, sinks_ref, bucket_ref, q_ref, kp_ref, kc_ref, vp_ref, vc_ref,
         o_ref, bias_ref) = refs
    n_k = bias_ref.shape[-1]
    n_grp = SWA_W // KV_W
    step = pl.program_id(0)

    @pl.when(step == 0)
    def _():
        bucket = bucket_ref[...]
        for h in range(SWA_HEADS):
            acc = jnp.zeros((n_q, n_k), F32)
            for b in range(N_BUCKETS):
                acc = jnp.where(bucket == b, table_ref[b * SWA_HEADS + h], acc)
            kv, gi = divmod(h, SWA_G)
            bias_ref[kv, gi * n_q:(gi + 1) * n_q, :] = acc * LOG2E

    lane = lax.broadcasted_iota(jnp.int32, (1, KV_W), 1)
    kv_mask = [(lane < HEAD_DIM).astype(F32), (lane >= HEAD_DIM).astype(F32)]
    if prompt:
        first = step == 0
        k_all = jnp.concatenate([jnp.where(first, km_ref[...], kp_ref[...]), kc_ref[...]], axis=0)
        v_all = jnp.concatenate([jnp.where(first, vm_ref[...], vp_ref[...]), vc_ref[...]], axis=0)
        k_kv = [(k_all * m).astype(BF16) for m in kv_mask]
        v_kv = [(v_all * m).astype(BF16) for m in kv_mask]
        keys = lambda j, kv: k_kv[kv][j * n_q:j * n_q + n_k]
        vals = lambda j, kv: v_kv[kv][j * n_q:j * n_q + n_k]
    else:
        n_c = n_k - n_q
        sub = lax.broadcasted_iota(jnp.int32, (KV_W, 1), 0)
        row_mask = [(sub < HEAD_DIM).astype(F32), (sub >= HEAD_DIM).astype(F32)]
        past = lambda ref, j, kv: (ref[j * KV_W:(j + 1) * KV_W, :] * row_mask[kv]).astype(BF16)
        new = lambda ref, j, kv: (ref[j * n_q:(j + 1) * n_q, :] * kv_mask[kv]).astype(BF16)
        ones_c, ones_q = jnp.ones((KV_W, n_c), BF16), jnp.ones((n_q, KV_W), BF16)

    q = (q_ref[...] * (HEAD_DIM ** -0.5 * LOG2E)).astype(BF16)
    ones_k = jnp.ones((n_k, KV_W), BF16)
    sink = [jnp.concatenate([jnp.full((n_q, KV_W), sinks_ref[kv * SWA_G + gi] * LOG2E, F32)
                             for gi in range(SWA_G)], axis=0) for kv in range(SWA_KV)]
    rows4 = SWA_G * n_q
    for j0 in range(0, units, batch):
        js = range(j0, min(j0 + batch, units))
        items = [(j, kv) for j in js for kv in range(SWA_KV)]
        s = {}
        for j in js:
            rows = slice(j * n_q, (j + 1) * n_q)
            q4 = jnp.concatenate([q[rows, g * KV_W:(g + 1) * KV_W] for g in range(n_grp)], axis=0)
            for kv in range(SWA_KV):
                if prompt:
                    sj = _mm_nt(q4, keys(j, kv))
                else:
                    sj = jnp.concatenate([_mm(q4, past(kp_ref, j, kv)),
                                          _mm_nt(q4, new(kc_ref, j, kv))], axis=1)
                sj = sj + bias_ref[kv]
                if prompt and j * n_q < WINDOW - N_META:
                    key_pos = j * n_q + lax.broadcasted_iota(jnp.int32, (1, n_k), 1)
                    n_invalid = jnp.where(first, WINDOW - N_META, 0)
                    sj = jnp.where(key_pos < n_invalid, -jnp.inf, sj)
                s[j, kv] = sj
        m = {it: jnp.maximum(jnp.broadcast_to(jnp.max(s[it], axis=-1, keepdims=True),
                                              (rows4, KV_W)), sink[it[1]]) for it in items}
        p = {it: jnp.exp2(s[it] - jnp.concatenate([m[it], m[it][:, :n_k - KV_W]], axis=1))
             for it in items}
        if prompt:
            pv = {it: _mm(p[it], jnp.concatenate([vals(*it), ones_k], axis=1)) for it in items}
        else:
            pb = {it: p[it].astype(BF16) for it in items}
            pv = {it: _mm_nt(pb[it][:, :n_c], jnp.concatenate([past(vp_ref, *it), ones_c], axis=0))
                  + _mm(pb[it][:, n_c:], jnp.concatenate([new(vc_ref, *it), ones_q], axis=1))
                  for it in items}
        o = {it: pv[it][:, :KV_W] * (1.0 / (pv[it][:, KV_W:] + jnp.exp2(sink[it[1]] - m[it])))
             for it in items}
        for j in js:
            o2 = o[j, 0] + o[j, 1]
            for g in range(n_grp):
                o_ref[j * n_q:(j + 1) * n_q, g * KV_W:(g + 1) * KV_W] = (
                    o2[g * n_q:(g + 1) * n_q, :].astype(BF16))


def _swa(table, sinks, bucket, q, kv_args, kv_specs, n_steps, n_q, units, prompt, name):
    n_k = bucket.shape[1]
    smem = pl.BlockSpec(memory_space=pltpu.SMEM)
    rows = units * n_q
    return pl.pallas_call(
        functools.partial(_swa_kernel, n_q=n_q, units=units, prompt=prompt),
        out_shape=jax.ShapeDtypeStruct((n_steps * rows, SWA_W), BF16),
        grid=(n_steps,),
        in_specs=[smem, smem, pl.BlockSpec((n_q, n_k), lambda i: (0, 0)),
                  pl.BlockSpec((rows, SWA_W), lambda i: (i, 0))] + list(kv_specs),
        out_specs=pl.BlockSpec((rows, SWA_W), lambda i: (i, 0)),
        scratch_shapes=[pltpu.VMEM((SWA_KV, SWA_G * n_q, n_k), F32)],
        compiler_params=pltpu.CompilerParams(dimension_semantics=("arbitrary",),
                                             vmem_limit_bytes=VMEM_LIMIT),
        name=name,
    )(table, sinks, bucket, q, *kv_args)


def _t5_bucket(rel):
    half = N_BUCKETS // 2
    max_exact = half // 2
    n = jnp.abs(rel)
    large = max_exact + (jnp.log(jnp.maximum(n, 1).astype(F32) / max_exact)
                         / math.log(MAX_DIST / max_exact) * (half - max_exact)).astype(jnp.int32)
    large = jnp.minimum(large, half - 1)
    return jnp.where(rel > 0, half, 0) + jnp.where(n < max_exact, n, large)


def _bucket_map(n_q, n_k, key_offset):
    rel = (jnp.arange(n_k)[None, :] - key_offset) - jnp.arange(n_q)[:, None]
    return _t5_bucket(rel).astype(jnp.int32)


FF_BLOCK = 1024


FF_STAGE = 512


def _post_kernel(hp_ref, dnp_ref, swp_ref, hs_ref, dns_ref, sws_ref,
                 wod_ref, wos_ref, g1_ref, b1_ref, w1_hbm, w2_hbm, g2_ref, b2_ref,
                 yp_ref, ys_ref, w1_ref, w2_ref, stage1, stage2, sems, *, n_prompt_tiles):
    step = pl.program_id(0)
    is_prompt = step < n_prompt_tiles
    n_stage = D_FF // FF_STAGE

    def copies(c):
        slot, blk = c % 2, pl.ds(c * FF_STAGE, FF_STAGE)
        return (pltpu.make_async_copy(w1_hbm.at[0, :, blk], stage1.at[slot], sems.at[0, slot]),
                pltpu.make_async_copy(w2_hbm.at[0, blk, :], stage2.at[slot], sems.at[1, slot]))

    def start(c):
        for cp in copies(c):
            cp.start()

    def fetch(j):
        per = FF_BLOCK // FF_STAGE
        for c in range(j * per, (j + 1) * per):
            for cp in copies(c):
                cp.wait()
            w1_ref[:, c * FF_STAGE:(c + 1) * FF_STAGE] = stage1[c % 2].astype(BF16)
            w2_ref[c * FF_STAGE:(c + 1) * FF_STAGE, :] = stage2[c % 2].astype(BF16)
            if c + 2 < n_stage:
                start(c + 2)

    @pl.when(step == 0)
    def _():
        start(0)
        start(1)
        _post_tile(hp_ref, dnp_ref, swp_ref, wod_ref, wos_ref, g1_ref, b1_ref,
                   w1_ref, w2_ref, g2_ref, b2_ref, yp_ref, fetch)

    @pl.when(jnp.logical_and(step > 0, is_prompt))
    def _():
        _post_tile(hp_ref, dnp_ref, swp_ref, wod_ref, wos_ref, g1_ref, b1_ref,
                   w1_ref, w2_ref, g2_ref, b2_ref, yp_ref)

    @pl.when(jnp.logical_not(is_prompt))
    def _():
        _post_tile(hs_ref, dns_ref, sws_ref, wod_ref, wos_ref, g1_ref, b1_ref,
                   w1_ref, w2_ref, g2_ref, b2_ref, ys_ref)


def _post_tile(h_ref, dn_ref, sw_ref, wod_ref, wos_ref, g1_ref, b1_ref,
               w1_ref, w2_ref, g2_ref, b2_ref, y_ref, fetch=None):
    mix = (jnp.dot(dn_ref[...], wod_ref[...], preferred_element_type=F32)
           + jnp.dot(sw_ref[...], wos_ref[...], preferred_element_type=F32))
    h1 = _layer_norm(DEEP_ALPHA * h_ref[...] + mix, g1_ref[...], b1_ref[...])
    h1b = h1.astype(BF16)
    f = jnp.zeros_like(h1)
    for j in range(D_FF // FF_BLOCK):
        if fetch is not None:
            fetch(j)
        a = jnp.dot(h1b, w1_ref[:, j * FF_BLOCK:(j + 1) * FF_BLOCK], preferred_element_type=F32)
        a = jnp.square(jnp.maximum(a, 0.0)).astype(BF16)
        f = f + jnp.dot(a, w2_ref[j * FF_BLOCK:(j + 1) * FF_BLOCK, :], preferred_element_type=F32)
    y_ref[...] = _layer_norm(DEEP_ALPHA * h1 + f, g2_ref[...], b2_ref[...])


def _post(prompt, sample, vecs, weights, tm):
    g1, b1, g2, b2 = vecs
    wo_dn, wo_sw, w1, w2 = weights
    n_p, n_s = prompt[0].shape[0] // tm, sample[0].shape[0] // tm
    assert n_p >= 1
    hbm = pl.BlockSpec(memory_space=pl.ANY)
    const = lambda i: (0, 0)
    vec = pl.BlockSpec((1, D_MODEL), const)
    weight = lambda wt: pl.BlockSpec(wt.shape, const, pipeline_mode=pl.Buffered(1))
    blk_p = lambda width: pl.BlockSpec((tm, width), lambda i: (jnp.minimum(i, n_p - 1), 0))
    blk_s = lambda width: pl.BlockSpec((tm, width), lambda i: (jnp.maximum(i - n_p, 0), 0))
    widths = (D_MODEL, DN_W, SWA_W)
    return pl.pallas_call(
        functools.partial(_post_kernel, n_prompt_tiles=n_p),
        out_shape=[jax.ShapeDtypeStruct((n_p * tm, D_MODEL), F32),
                   jax.ShapeDtypeStruct((n_s * tm, D_MODEL), F32)],
        grid=(n_p + n_s,),
        in_specs=[blk_p(wd) for wd in widths] + [blk_s(wd) for wd in widths]
                 + [weight(wo_dn), weight(wo_sw), vec, vec, hbm, hbm, vec, vec],
        out_specs=[blk_p(D_MODEL), blk_s(D_MODEL)],
        scratch_shapes=[pltpu.VMEM((D_MODEL, D_FF), BF16), pltpu.VMEM((D_FF, D_MODEL), BF16),
                        pltpu.VMEM((2, D_MODEL, FF_STAGE), F32),
                        pltpu.VMEM((2, FF_STAGE, D_MODEL), F32),
                        pltpu.SemaphoreType.DMA((2, 2))],
        compiler_params=pltpu.CompilerParams(dimension_semantics=("arbitrary",),
                                             vmem_limit_bytes=VMEM_LIMIT),
        name="post_mlp",
    )(*prompt, *sample, wo_dn, wo_sw, g1, b1, w1, w2, g2, b2)


def kernel(x_prompt, x_sample, state_delta, state_conv, cache_swa_k, cache_swa_v, meta_tokens, ln_in_g, ln_in_b, w_in, w_conv, dn_a_log, dn_dt_bias, dn_norm, swa_sinks, rel_bias_table, w_o, ln1_g, ln1_b, w_ff1, w_ff2, ln2_g, ln2_b):
    assert w_in.shape[0] == DEPTH == 1
    n_seq_s, len_s = x_sample.shape[0], x_sample.shape[1]
    len_p = x_prompt.shape[1]
    cache_len = cache_swa_k.shape[2]
    chunk_p = DELTA_CHUNK
    row = lambda t: t.reshape(1, -1).astype(F32)

    w = w_in[0]
    w_sq = (w[:, OFF_SQ:OFF_SK].astype(BF16).reshape(D_MODEL, SWA_KV, SWA_G, HEAD_DIM)
            .transpose(0, 2, 1, 3).reshape(D_MODEL, SWA_W))
    w_r = (w[:, :OFF_A].astype(BF16), w_sq, w[:, OFF_SK:].astype(BF16),
           jnp.pad(w[:, OFF_A:OFF_SQ].astype(BF16), ((0, 0), (0, LANES - 2 * HEADS))))
    gin, bin_ = row(ln_in_g), row(ln_in_b)
    pad_lanes = lambda t: jnp.pad(t.reshape(1, -1).astype(F32), ((0, 0), (0, LANES - HEADS)))
    lane_head = jnp.arange(DN_W) // HEAD_DIM
    seg = (lane_head[:, None] == lane_head[None, :]).astype(BF16)
    expand = (jnp.arange(LANES)[:, None] == lane_head[None, :]).astype(BF16)
    wconv = jnp.pad(w_conv[0].astype(F32), ((0, SUBLANES - CONV_W), (0, 0)))
    delta_consts = (pad_lanes(dn_a_log[0]), pad_lanes(dn_dt_bias[0]),
                    jnp.tile(dn_norm[0].astype(F32), HEADS).reshape(1, DN_W), seg, expand)
    post_vecs = (row(ln1_g[0]), row(ln1_b[0]), row(ln2_g[0]), row(ln2_b[0]))
    wo_sw = (w_o[0][DN_W:].astype(BF16).reshape(SWA_KV, SWA_G, HEAD_DIM, D_MODEL)
             .transpose(1, 0, 2, 3).reshape(SWA_W, D_MODEL))
    post_w = (w_o[0][:DN_W].astype(BF16), wo_sw, w_ff1.astype(F32), w_ff2.astype(F32))
    table = rel_bias_table.astype(F32).reshape(-1)
    sinks = swa_sinks[0].astype(F32)

    xp = x_prompt[0]
    xs = x_sample.reshape(n_seq_s * len_s, D_MODEL)

    conv_pad = SUBLANES - (CONV_W - 1)
    zero_conv = jnp.zeros((1, SUBLANES, CONV_CH), F32)
    conv_s = jnp.pad(state_conv[0].astype(F32), ((0, 0), (conv_pad, 0), (0, 0)))
    m_qkv, m_z, m_sq, m_sk, m_sv, m_ab, m_tail, _ = _proj(
        meta_tokens.astype(F32), gin, bin_, w_r, zero_conv, wconv, N_META, N_META)
    p_qkv, p_z, p_sq, p_sk, p_sv, p_ab, p_tail, p_h = _proj(
        xp, gin, bin_, w_r, m_tail, wconv, PROMPT_PROJ_TILE, None)
    s_qkv, s_z, s_sq, s_sk, s_sv, s_ab, s_tail, s_h = _proj(
        xs, gin, bin_, w_r, conv_s, wconv, ROW_TILE, len_s)

    zero_s = jnp.zeros((1, HEADS, HEAD_DIM, HEAD_DIM), F32)
    _, s_meta = _delta(m_qkv, m_z, m_ab, zero_s, delta_consts, 1, N_META, N_META, 1, 1)
    p_dn, p_s = _delta(p_qkv, p_z, p_ab, s_meta, delta_consts, 1, len_p, chunk_p, *DELTA_PROMPT)
    s_dn, s_s = _delta(s_qkv, s_z, s_ab, state_delta[0].astype(F32), delta_consts,
                       n_seq_s, len_s, len_s, *DELTA_SAMPLE)

    span = WINDOW
    lead = jnp.zeros((span - N_META, KV_W), F32)
    units_p = SWA_UNITS_PROMPT
    rows_p = units_p * chunk_p
    prev_blocks = rows_p // span
    kv_specs_p = [pl.BlockSpec((span, KV_W), lambda i: (0, 0)),
                  pl.BlockSpec((span, KV_W), lambda i: (jnp.maximum(i * prev_blocks - 1, 0), 0)),
                  pl.BlockSpec((rows_p, KV_W), lambda i: (i, 0))] * 2
    p_sw = _swa(table, sinks, _bucket_map(chunk_p, span + chunk_p, span), p_sq,
                [jnp.concatenate([lead, m_sk], axis=0), p_sk, p_sk,
                 jnp.concatenate([lead, m_sv], axis=0), p_sv, p_sv],
                kv_specs_p, len_p // rows_p, chunk_p, units_p, True, "swa_prompt")
    cache_t = lambda c: (jnp.transpose(c[0].astype(F32), (0, 2, 3, 1))
                         .reshape(n_seq_s * KV_W, cache_len))
    ck, cv = cache_t(cache_swa_k), cache_t(cache_swa_v)
    units_s = SWA_UNITS_SAMPLE
    kv_specs_s = [pl.BlockSpec((units_s * KV_W, cache_len), lambda i: (i, 0)),
                  pl.BlockSpec((units_s * len_s, KV_W), lambda i: (i, 0))] * 2
    s_sw = _swa(table, sinks, _bucket_map(len_s, cache_len + len_s, cache_len), s_sq,
                [ck, s_sk, cv, s_sv], kv_specs_s, n_seq_s // units_s, len_s, units_s, False,
                "swa_sample")

    y_p, y_s = _post((p_h, p_dn, p_sw), (s_h, s_dn, s_sw), post_vecs, post_w, ROW_TILE)

    kv_shape = lambda n, length: (1, n, length, SWA_KV, HEAD_DIM)
    return (y_p[None], y_s.reshape(x_sample.shape),
            p_s[None], p_tail[:, -(CONV_W - 1):][None],
            p_sk[-WINDOW:].reshape(kv_shape(1, WINDOW)), p_sv[-WINDOW:].reshape(kv_shape(1, WINDOW)),
            s_s[None], s_tail[:, -(CONV_W - 1):][None],
            s_sk.reshape(kv_shape(n_seq_s, len_s)), s_sv.reshape(kv_shape(n_seq_s, len_s)))
```

```python
import functools
import math

import jax
import jax.numpy as jnp
from jax import lax
from jax.experimental import pallas as pl
from jax.experimental.pallas import tpu as pltpu

F32 = jnp.float32
BF16 = jnp.bfloat16

D_MODEL = 1024
N_META = 16
HEADS = 8
HEAD_DIM = 64
DN_W = HEADS * HEAD_DIM
CONV_W = 4
CONV_CH = 3 * DN_W
SWA_HEADS = 8
SWA_KV = 2
SWA_G = SWA_HEADS // SWA_KV
KV_W = SWA_KV * HEAD_DIM
WINDOW = 128
N_BUCKETS = 32
MAX_DIST = 128
D_FF = 4 * D_MODEL
DEPTH = 1
DEEP_ALPHA = (2 * DEPTH) ** 0.25
LN_EPS = 1e-5
NORM_EPS = 1e-6
LOG2E = math.log2(math.e)
LANES = 128
SUBLANES = 8
VMEM_LIMIT = 56 * 1024 * 1024

OFF_Z = CONV_CH
OFF_A = OFF_Z + DN_W
OFF_B = OFF_A + HEADS
OFF_SQ = OFF_B + HEADS
OFF_SK = OFF_SQ + SWA_HEADS * HEAD_DIM
PROJ_SPLITS = (CONV_CH, DN_W, SWA_HEADS * HEAD_DIM, KV_W, KV_W, LANES)

ROW_TILE = 512
PROMPT_PROJ_TILE = 1024
DELTA_CHUNK = 64
DELTA_PROMPT = (4, 2)
DELTA_SAMPLE = (8, 2)
SWA_UNITS_PROMPT = 16
SWA_UNITS_SAMPLE = 8
SWA_BATCH = 4


def _layer_norm(x, g, b):
    mu = jnp.mean(x, axis=-1, keepdims=True)
    xc = x - mu
    var = jnp.mean(xc * xc, axis=-1, keepdims=True)
    return xc * lax.rsqrt(var + LN_EPS) * g + b


def _mm(a, b):
    return jnp.dot(a.astype(BF16), b.astype(BF16), preferred_element_type=F32)


def _mm_nt(a, b):
    return lax.dot_general(a.astype(BF16), b.astype(BF16), (((1,), (1,)), ((), ())),
                           preferred_element_type=F32)


def _split3(x):
    x1 = x.astype(BF16)
    r1 = x - x1.astype(F32)
    x2 = r1.astype(BF16)
    x3 = (r1 - x2.astype(F32)).astype(BF16)
    return x1, x2, x3


def _dot_exact_lhs(x, sel):
    return jnp.dot(jnp.concatenate(_split3(x), axis=1), jnp.concatenate([sel] * 3, axis=0),
                   preferred_element_type=F32)


CONV_BLOCK = 256


def _sigmoid(x):
    return 1.0 / (1.0 + jnp.exp(-x))


def _proj_kernel(x_ref, g_ref, b_ref, w_ref, wsq_ref, wkv_ref, wab_ref, conv0_ref, wconv_ref,
                 y_ref, z_ref, sq_ref, sk_ref, sv_ref, ab_ref, tail_ref, h_ref, xbuf, *, seq_len):
    tm = x_ref.shape[0]
    h_ref[...] = _layer_norm(x_ref[...], g_ref[...], b_ref[...])
    h = h_ref[...].astype(BF16)

    chained = seq_len is None
    n_seq = 1 if chained else tm // seq_len
    L = tm if chained else seq_len
    if chained:
        @pl.when(pl.program_id(0) == 0)
        def _():
            xbuf[0, 0:SUBLANES, :] = conv0_ref[0]
    else:
        for s in range(n_seq):
            xbuf[s, 0:SUBLANES, :] = conv0_ref[s]
    blocks = [slice(cb, cb + CONV_BLOCK) for cb in range(0, CONV_CH, CONV_BLOCK)]

    def project(cols):
        raw = _mm_nt(h, w_ref[cols, :])
        for s in range(n_seq):
            xbuf[s, SUBLANES:SUBLANES + L, cols] = raw[s * L:(s + 1) * L, :]

    def conv(cols):
        for s in range(n_seq):
            ext = xbuf[s, :, cols]
            w0, w1, w2, w3 = (0.5 * wconv_ref[j:j + 1, cols] for j in range(CONV_W))
            ext1 = pltpu.roll(ext, 1, axis=0)
            half = (w3 * ext + w2 * ext1)[SUBLANES:, :]
            half = half + pltpu.roll(w1 * ext + w0 * ext1, 2, axis=0)[SUBLANES:, :]
            y_ref[s * L:(s + 1) * L, cols] = half * (1.0 + jnp.tanh(half))
            tail_ref[s, :, cols] = xbuf[s, L:L + SUBLANES, cols]
            if chained:
                xbuf[s, 0:SUBLANES, cols] = xbuf[s, L:L + SUBLANES, cols]

    for o_ref, rhs in ((z_ref, w_ref[CONV_CH:, :]), (sq_ref, wsq_ref[...]),
                       (sk_ref, wkv_ref[:KV_W, :]), (sv_ref, wkv_ref[KV_W:, :]),
                       (ab_ref, wab_ref[...])):
        o_ref[...] = _mm_nt(h, rhs)
    project(blocks[0])
    for bi, cols in enumerate(blocks):
        if bi + 1 < len(blocks):
            project(blocks[bi + 1])
        conv(cols)


def _proj(x, g, b, weights, conv0, wconv, tm, seq_len):
    rows = x.shape[0]
    w_main, w_sq, w_kv, w_ab = weights
    chained = seq_len is None
    n_tile_seq = 1 if chained else tm // seq_len
    n_seq = 1 if chained else rows // seq_len
    const = lambda i: (0, 0)
    row_blk = lambda width: pl.BlockSpec((tm, width), lambda i: (i, 0))
    seq_blk = pl.BlockSpec((n_tile_seq, SUBLANES, CONV_CH),
                           (lambda i: (0, 0, 0)) if chained else (lambda i: (i, 0, 0)))
    return pl.pallas_call(
        functools.partial(_proj_kernel, seq_len=seq_len),
        out_shape=[jax.ShapeDtypeStruct((rows, width), F32) for width in PROJ_SPLITS]
                  + [jax.ShapeDtypeStruct((n_seq, SUBLANES, CONV_CH), F32),
                     jax.ShapeDtypeStruct((rows, D_MODEL), F32)],
        grid=(rows // tm,),
        in_specs=[row_blk(D_MODEL), pl.BlockSpec((1, D_MODEL), const),
                  pl.BlockSpec((1, D_MODEL), const)]
                 + [pl.BlockSpec(wt.shape, const, pipeline_mode=pl.Buffered(1)) for wt in weights]
                 + [seq_blk, pl.BlockSpec((SUBLANES, CONV_CH), const)],
        out_specs=[row_blk(width) for width in PROJ_SPLITS] + [seq_blk, row_blk(D_MODEL)],
        scratch_shapes=[pltpu.VMEM((n_tile_seq, SUBLANES + (tm if chained else seq_len), CONV_CH),
                                   F32)],
        compiler_params=pltpu.CompilerParams(dimension_semantics=("arbitrary",),
                                             vmem_limit_bytes=VMEM_LIMIT),
        name="ln_in_proj",
    )(x, g, b, w_main, w_sq, w_kv, w_ab, conv0, wconv)


GROUP_HEADS = 4


def _softplus(x):
    return jnp.maximum(x, 0.0) + jnp.log1p(jnp.exp(-jnp.abs(x)))


def _delta_kernel(qkv_ref, z_ref, ab_ref, s0_ref, alog_ref, dtb_ref,
                  normw_ref, seg_ref, expand_ref, o_ref, s_ref, sbd,
                  *, chunk, units, groups, sequential):
    C, U = chunk, units
    RG = U * C
    PAIRS = HEADS // GROUP_HEADS
    PW = GROUP_HEADS * HEAD_DIM
    step = pl.program_id(0)
    last_step = pl.num_programs(0) - 1
    expand = expand_ref[...]

    def seg_sum(t):
        tb = t.astype(BF16)
        return jnp.concatenate(
            [jnp.dot(tb[:, g * PW:(g + 1) * PW], seg_ref[g * PW:(g + 1) * PW, g * PW:(g + 1) * PW],
                     preferred_element_type=F32) for g in range(PAIRS)], axis=1)

    ri = lax.broadcasted_iota(jnp.int32, (C, PW), 0)
    ci = lax.broadcasted_iota(jnp.int32, (C, PW), 1) % HEAD_DIM
    causal = (ri >= ci) & (ci < C)
    strict = (ri > ci) & (ci < C)
    eye = (ri == ci).astype(F32)
    lane_head = lax.broadcasted_iota(jnp.int32, (1, PW), 1) // HEAD_DIM
    rs = lax.broadcasted_iota(jnp.int32, (LANES, LANES), 0)
    cs = lax.broadcasted_iota(jnp.int32, (LANES, LANES), 1)
    half_mask = ((rs // HEAD_DIM) == (cs // HEAD_DIM)).astype(F32)
    row_pad = [jnp.zeros((HEAD_DIM - C, PW), BF16)] if C < HEAD_DIM else []
    lane_pad = [jnp.zeros((1, HEAD_DIM - C), F32)] if C < HEAD_DIM else []
    rr = lax.broadcasted_iota(jnp.int32, (RG, RG), 0)
    cc = lax.broadcasted_iota(jnp.int32, (RG, RG), 1)
    tril = ((rr // C == cc // C) & (rr >= cc)).astype(BF16)

    def stack(y):
        yb = y.astype(BF16)
        zero = jnp.zeros_like(yb)
        pad = row_pad if y.shape[0] < HEAD_DIM else []
        blocks = []
        for h in range(GROUP_HEADS):
            blocks += [jnp.where(lane_head == h, yb, zero)] + pad
        return jnp.concatenate(blocks, axis=0)

    items = [(u, p) for u in range(U) for p in range(PAIRS)]
    sl = lambda u, p: (slice(u * C, (u + 1) * C), slice(p * PW, (p + 1) * PW))

    def front(gi):
        rg = slice(gi * RG, (gi + 1) * RG)
        q, k, v = (qkv_ref[rg, j * DN_W:(j + 1) * DN_W] for j in range(3))
        qn = q * (lax.rsqrt(seg_sum(q * q) + NORM_EPS) * HEAD_DIM ** -0.5)
        kn = k * lax.rsqrt(seg_sum(k * k) + NORM_EPS)
        yield

        ab = ab_ref[rg, :]
        g = -jnp.exp(alog_ref[...]) * _softplus(ab + dtb_ref[...])
        gs = jnp.dot(tril, jnp.concatenate(_split3(g), axis=1), preferred_element_type=F32)
        G = (gs[:, :LANES] + gs[:, LANES:2 * LANES] + gs[:, 2 * LANES:]) * LOG2E
        GT = G.T
        G_x = _dot_exact_lhs(G, expand)
        beta = _sigmoid(pltpu.roll(ab, LANES - HEADS, axis=1))
        b_hi = beta.astype(BF16)
        b_lo = (beta - b_hi.astype(F32)).astype(BF16)
        beta_x = jnp.dot(jnp.concatenate([b_hi, b_lo], axis=1),
                         jnp.concatenate([expand, expand], axis=0), preferred_element_type=F32)
        glast_x = jnp.concatenate(
            [jnp.broadcast_to(G_x[(u + 1) * C - 1:(u + 1) * C, :], (C, DN_W)) for u in range(U)],
            axis=0)
        yield
        eg_x = jnp.exp2(G_x)
        kbeta = kn * beta_x
        return dict(qn=qn, kn=kn, kbeta=kbeta, bk=kbeta * eg_x, bv=v * beta_x, qd=qn * eg_x,
                    kd=kn * jnp.exp2(glast_x - G_x), G_x=G_x, GT=GT, eg_x=eg_x)

    def middle(f):
        A, QK = {}, {}
        for (u, p) in items:
            ru, lp = sl(u, p)
            gr = []
            for h in range(p * GROUP_HEADS, (p + 1) * GROUP_HEADS):
                gr += [f["GT"][h:h + 1, ru]] + lane_pad
            gr = jnp.concatenate(gr, axis=1)
            decay = jnp.exp2(jnp.where(causal, f["G_x"][ru, lp] - gr, -jnp.inf))
            r = _mm_nt(jnp.concatenate([f["kbeta"][ru, lp], f["qn"][ru, lp]], axis=0),
                       stack(f["kn"][ru, lp]))
            A[u, p] = jnp.where(strict, r[:C] * decay, 0.0)
            QK[u, p] = r[C:] * decay
        yield

        n_pow = int(math.log2(C)) - 1
        P = {it: eye - A[it] for it in items}
        Ap = {it: _mm(A[it], stack(A[it])) for it in items}
        yield
        for i in range(n_pow):
            if i < n_pow - 1:
                r = {it: _mm(jnp.concatenate([Ap[it], P[it]], axis=0), stack(Ap[it])) for it in items}
                Ap = {it: r[it][:C] for it in items}
                P = {it: P[it] + r[it][C:] for it in items}
            else:
                P = {it: P[it] + _mm(P[it], stack(Ap[it])) for it in items}
            yield

        halves = [slice(hb * LANES, (hb + 1) * LANES) for hb in range(PW // LANES)]
        wu = {it: _mm(P[it], jnp.concatenate([stack(f["bk"][sl(*it)]), stack(f["bv"][sl(*it)])],
                                             axis=1)) for it in items}
        kdT = {(it, hb): f["kd"][sl(*it)][:, cols].T for it in items
               for hb, cols in enumerate(halves)}
        yield
        mn = {(it, hb): _mm(kdT[it, hb], jnp.concatenate(
            [wu[it][:, cols], wu[it][:, PW:][:, cols]], axis=1))
              for it in items for hb, cols in enumerate(halves)}
        yield
        n_t = {key: (half_mask * mn[key][:, LANES:]).T for key in mn}
        out = {}
        for (u, p) in items:
            ru, lp = sl(u, p)
            it = (u, p)
            out[it] = dict(
                W=wu[it][:, :PW], U=wu[it][:, PW:],
                M=diag_blocks([half_mask * mn[it, hb][:, :LANES]
                               for hb in range(len(halves))]).astype(BF16),
                NT=jnp.concatenate([n_t[it, hb][:HEAD_DIM] + n_t[it, hb][HEAD_DIM:]
                                    for hb in range(len(halves))], axis=1),
                QK=QK[it], qd=f["qd"][ru, lp],
                eg_last=f["eg_x"][(u + 1) * C - 1:(u + 1) * C, lp])
        return out

    def diag_blocks(blocks):
        n = len(blocks)
        rows = []
        for i, blk in enumerate(blocks):
            rows.append(jnp.concatenate(
                [blk if j == i else jnp.zeros_like(blk) for j in range(n)], axis=1))
        return jnp.concatenate(rows, axis=0)

    def load_state(ref, prefix, p):
        return jnp.concatenate([ref[prefix + (p * GROUP_HEADS + h,)].T
                                for h in range(GROUP_HEADS)], axis=1)

    def store_state(ref, prefix, p, st):
        for h in range(GROUP_HEADS):
            ref[prefix + (p * GROUP_HEADS + h,)] = st[:, h * HEAD_DIM:(h + 1) * HEAD_DIM].T

    chunks = [(gi, u) for gi in range(groups) for u in range(U)]
    if sequential:
        @pl.when(step == 0)
        def _():
            for p in range(PAIRS):
                sbd[p] = load_state(s0_ref, (), p)
        s_start = {p: sbd[p] for p in range(PAIRS)}
    else:
        s_start = {(idx, p): load_state(s0_ref, (idx,), p)
                   for idx in range(len(chunks)) for p in range(PAIRS)}

    def chain(gi, out, s_in):
        if sequential:
            states = {p: [s_in[p]] for p in range(PAIRS)}
            for u in range(U):
                for p in range(PAIRS):
                    d, s_cur = out[u, p], states[p][-1]
                    states[p].append(s_cur * d["eg_last"] - _mm_nt(s_cur, d["M"]) + d["NT"])
            return ({(u, p): states[p][u] for (u, p) in items},
                    {p: states[p][-1] for p in range(PAIRS)})
        entering = {(u, p): s_in[gi * U + u, p] for (u, p) in items}
        s_out = {}
        for (u, p) in items:
            d, s_cur = out[u, p], entering[u, p]
            s_out[gi * U + u, p] = s_cur * d["eg_last"] - _mm_nt(s_cur, d["M"]) + d["NT"]
        return entering, s_out

    def finish(gi, out, entering):
        r = {it: _mm_nt(jnp.concatenate([out[it]["W"], out[it]["qd"]], axis=0),
                        stack(entering[it])) for it in items}
        yield
        tiles = {it: r[it][C:] + _mm(out[it]["QK"], stack(out[it]["U"] - r[it][:C]))
                 for it in items}
        yield
        rg = slice(gi * RG, (gi + 1) * RG)
        o = jnp.concatenate([jnp.concatenate([tiles[u, p] for p in range(PAIRS)], axis=1)
                             for u in range(U)], axis=0)
        ms = seg_sum(o * o) * (1.0 / HEAD_DIM)
        hz = 0.5 * z_ref[rg, :]
        o_ref[rg, :] = (o * lax.rsqrt(ms + NORM_EPS) * normw_ref[...]
                        * (hz * (1.0 + jnp.tanh(hz)))).astype(BF16)

    def drive(gens):
        results = [None] * len(gens)
        active = list(enumerate(gens))
        while active:
            still = []
            for i, gen in active:
                try:
                    next(gen)
                    still.append((i, gen))
                except StopIteration as stop:
                    results[i] = stop.value
            active = still
        return results

    state = s_start
    s_final = {}
    f_cur = drive([front(0)])[0]
    pending = None
    for gi in range(groups + 1):
        gens = []
        if gi < groups:
            gens.append(middle(f_cur))
        if gi + 1 < groups:
            gens.append(front(gi + 1))
        if pending is not None:
            entering, s_out = chain(gi - 1, pending, state)
            s_final.update(s_out)
            state = s_out if sequential else state
            gens.append(finish(gi - 1, pending, entering))
        res = drive(gens)
        pending = res[0] if gi < groups else None
        f_cur = res[1] if gi + 1 < groups else None

    if sequential:
        for p in range(PAIRS):
            sbd[p] = s_final[p]

        @pl.when(step == last_step)
        def _():
            for p in range(PAIRS):
                store_state(s_ref, (), p, s_final[p])
    else:
        for (idx, p), s_new in s_final.items():
            store_state(s_ref, (idx,), p, s_new)


def _delta(qkv, z, ab, s0, consts, n_seq, seq_len, chunk, units, groups):
    alog, dtb, normw, seg, expand = consts
    sequential = n_seq == 1
    rows = n_seq * seq_len
    n_chunks = units * groups
    blk_rows = n_chunks * chunk
    assert rows % blk_rows == 0 and (sequential or seq_len == chunk)
    const2 = lambda shape: pl.BlockSpec(shape, lambda i: (0, 0))
    tok = lambda width: pl.BlockSpec((blk_rows, width), lambda i: (i, 0))
    group_w = GROUP_HEADS * HEAD_DIM
    if sequential:
        state = pl.BlockSpec((None, HEADS, HEAD_DIM, HEAD_DIM), lambda i: (0, 0, 0, 0))
    else:
        state = pl.BlockSpec((n_chunks, HEADS, HEAD_DIM, HEAD_DIM), lambda i: (i, 0, 0, 0))
    return pl.pallas_call(
        functools.partial(_delta_kernel, chunk=chunk, units=units, groups=groups,
                          sequential=sequential),
        out_shape=[jax.ShapeDtypeStruct((rows, DN_W), BF16),
                   jax.ShapeDtypeStruct((n_seq, HEADS, HEAD_DIM, HEAD_DIM), F32)],
        grid=(rows // blk_rows,),
        in_specs=[tok(CONV_CH), tok(DN_W), tok(LANES), state,
                  const2((1, LANES)), const2((1, LANES)),
                  const2((1, DN_W)), const2((DN_W, DN_W)), const2((LANES, DN_W))],
        out_specs=[tok(DN_W), state],
        scratch_shapes=[pltpu.VMEM((HEADS // GROUP_HEADS, HEAD_DIM, group_w), F32)],
        compiler_params=pltpu.CompilerParams(
            dimension_semantics=("arbitrary",), vmem_limit_bytes=VMEM_LIMIT,
        ),
        name=f"delta_c{chunk}",
    )(qkv, z, ab, s0, alog, dtb, normw, seg, expand)


SWA_W = SWA_HEADS * HEAD_DIM


def _swa_kernel(*refs, n_q, units, prompt, batch=SWA_BATCH):
    if prompt:
        (table_ref, sinks_ref, bucket_ref, q_ref, km_ref, kp_ref, kc_ref, vm_ref, vp_ref, vc_ref,
         o_ref, bias_ref) = refs
    else:
        (table_ref, sinks_ref, bucket_ref, q_ref, kp_ref, kc_ref, vp_ref, vc_ref,
         o_ref, bias_ref) = refs
    n_k = bias_ref.shape[-1]
    n_grp = SWA_W // KV_W
    step = pl.program_id(0)

    @pl.when(step == 0)
    def _():
        bucket = bucket_ref[...]
        for h in range(SWA_HEADS):
            acc = jnp.zeros((n_q, n_k), F32)
            for b in range(N_BUCKETS):
                acc = jnp.where(bucket == b, table_ref[b * SWA_HEADS + h], acc)
            kv, gi = divmod(h, SWA_G)
            bias_ref[kv, gi * n_q:(gi + 1) * n_q, :] = acc * LOG2E

    lane = lax.broadcasted_iota(jnp.int32, (1, KV_W), 1)
    kv_mask = [(lane < HEAD_DIM).astype(F32), (lane >= HEAD_DIM).astype(F32)]
    if prompt:
        first = step == 0
        k_all = jnp.concatenate([jnp.where(first, km_ref[...], kp_ref[...]), kc_ref[...]], axis=0)
        v_all = jnp.concatenate([jnp.where(first, vm_ref[...], vp_ref[...]), vc_ref[...]], axis=0)
        k_kv = [(k_all * m).astype(BF16) for m in kv_mask]
        v_kv = [(v_all * m).astype(BF16) for m in kv_mask]
        keys = lambda j, kv: k_kv[kv][j * n_q:j * n_q + n_k]
        vals = lambda j, kv: v_kv[kv][j * n_q:j * n_q + n_k]
    else:
        n_c = n_k - n_q
        sub = lax.broadcasted_iota(jnp.int32, (KV_W, 1), 0)
        row_mask = [(sub < HEAD_DIM).astype(F32), (sub >= HEAD_DIM).astype(F32)]
        past = lambda ref, j, kv: (ref[j * KV_W:(j + 1) * KV_W, :] * row_mask[kv]).astype(BF16)
        new = lambda ref, j, kv: (ref[j * n_q:(j + 1) * n_q, :] * kv_mask[kv]).astype(BF16)
        ones_c, ones_q = jnp.ones((KV_W, n_c), BF16), jnp.ones((n_q, KV_W), BF16)

    q = (q_ref[...] * (HEAD_DIM ** -0.5 * LOG2E)).astype(BF16)
    ones_k = jnp.ones((n_k, KV_W), BF16)
    sink = [jnp.concatenate([jnp.full((n_q, KV_W), sinks_ref[kv * SWA_G + gi] * LOG2E, F32)
                             for gi in range(SWA_G)], axis=0) for kv in range(SWA_KV)]
    rows4 = SWA_G * n_q
    for j0 in range(0, units, batch):
        js = range(j0, min(j0 + batch, units))
        items = [(j, kv) for j in js for kv in range(SWA_KV)]
        s = {}
        for j in js:
            rows = slice(j * n_q, (j + 1) * n_q)
            q4 = jnp.concatenate([q[rows, g * KV_W:(g + 1) * KV_W] for g in range(n_grp)], axis=0)
            for kv in range(SWA_KV):
                if prompt:
                    sj = _mm_nt(q4, keys(j, kv))
                else:
                    sj = jnp.concatenate([_mm(q4, past(kp_ref, j, kv)),
                                          _mm_nt(q4, new(kc_ref, j, kv))], axis=1)
                sj = sj + bias_ref[kv]
                if prompt and j * n_q < WINDOW - N_META:
                    key_pos = j * n_q + lax.broadcasted_iota(jnp.int32, (1, n_k), 1)
                    n_invalid = jnp.where(first, WINDOW - N_META, 0)
                    sj = jnp.where(key_pos < n_invalid, -jnp.inf, sj)
                s[j, kv] = sj
        m = {it: jnp.maximum(jnp.broadcast_to(jnp.max(s[it], axis=-1, keepdims=True),
                                              (rows4, KV_W)), sink[it[1]]) for it in items}
        p = {it: jnp.exp2(s[it] - jnp.concatenate([m[it], m[it][:, :n_k - KV_W]], axis=1))
             for it in items}
        if prompt:
            pv = {it: _mm(p[it], jnp.concatenate([vals(*it), ones_k], axis=1)) for it in items}
        else:
            pb = {it: p[it].astype(BF16) for it in items}
            pv = {it: _mm_nt(pb[it][:, :n_c], jnp.concatenate([past(vp_ref, *it), ones_c], axis=0))
                  + _mm(pb[it][:, n_c:], jnp.concatenate([new(vc_ref, *it), ones_q], axis=1))
                  for it in items}
        o = {it: pv[it][:, :KV_W] * (1.0 / (pv[it][:, KV_W:] + jnp.exp2(sink[it[1]] - m[it])))
             for it in items}
        for j in js:
            o2 = o[j, 0] + o[j, 1]
            for g in range(n_grp):
                o_ref[j * n_q:(j + 1) * n_q, g * KV_W:(g + 1) * KV_W] = (
                    o2[g * n_q:(g + 1) * n_q, :].astype(BF16))


def _swa(table, sinks, bucket, q, kv_args, kv_specs, n_steps, n_q, units, prompt, name):
    n_k = bucket.shape[1]
    smem = pl.BlockSpec(memory_space=pltpu.SMEM)
    rows = units * n_q
    return pl.pallas_call(
        functools.partial(_swa_kernel, n_q=n_q, units=units, prompt=prompt),
        out_shape=jax.ShapeDtypeStruct((n_steps * rows, SWA_W), BF16),
        grid=(n_steps,),
        in_specs=[smem, smem, pl.BlockSpec((n_q, n_k), lambda i: (0, 0)),
                  pl.BlockSpec((rows, SWA_W), lambda i: (i, 0))] + list(kv_specs),
        out_specs=pl.BlockSpec((rows, SWA_W), lambda i: (i, 0)),
        scratch_shapes=[pltpu.VMEM((SWA_KV, SWA_G * n_q, n_k), F32)],
        compiler_params=pltpu.CompilerParams(dimension_semantics=("arbitrary",),
                                             vmem_limit_bytes=VMEM_LIMIT),
        name=name,
    )(table, sinks, bucket, q, *kv_args)


def _t5_bucket(rel):
    half = N_BUCKETS // 2
    max_exact = half // 2
    n = jnp.abs(rel)
    large = max_exact + (jnp.log(jnp.maximum(n, 1).astype(F32) / max_exact)
                         / math.log(MAX_DIST / max_exact) * (half - max_exact)).astype(jnp.int32)
    large = jnp.minimum(large, half - 1)
    return jnp.where(rel > 0, half, 0) + jnp.where(n < max_exact, n, large)


def _bucket_map(n_q, n_k, key_offset):
    rel = (jnp.arange(n_k)[None, :] - key_offset) - jnp.arange(n_q)[:, None]
    return _t5_bucket(rel).astype(jnp.int32)


FF_BLOCK = 1024


FF_STAGE = 512


def _post_kernel(hp_ref, dnp_ref, swp_ref, hs_ref, dns_ref, sws_ref,
                 wod_ref, wos_ref, g1_ref, b1_ref, w1_hbm, w2_hbm, g2_ref, b2_ref,
                 yp_ref, ys_ref, w1_ref, w2_ref, stage1, stage2, sems, *, n_prompt_tiles):
    step = pl.program_id(0)
    is_prompt = step < n_prompt_tiles
    n_stage = D_FF // FF_STAGE

    def copies(c):
        slot, blk = c % 2, pl.ds(c * FF_STAGE, FF_STAGE)
        return (pltpu.make_async_copy(w1_hbm.at[0, :, blk], stage1.at[slot], sems.at[0, slot]),
                pltpu.make_async_copy(w2_hbm.at[0, blk, :], stage2.at[slot], sems.at[1, slot]))

    def start(c):
        for cp in copies(c):
            cp.start()

    def fetch(j):
        per = FF_BLOCK // FF_STAGE
        for c in range(j * per, (j + 1) * per):
            for cp in copies(c):
                cp.wait()
            w1_ref[:, c * FF_STAGE:(c + 1) * FF_STAGE] = stage1[c % 2].astype(BF16)
            w2_ref[c * FF_STAGE:(c + 1) * FF_STAGE, :] = stage2[c % 2].astype(BF16)
            if c + 2 < n_stage:
                start(c + 2)

    @pl.when(step == 0)
    def _():
        start(0)
        start(1)
        _post_tile(hp_ref, dnp_ref, swp_ref, wod_ref, wos_ref, g1_ref, b1_ref,
                   w1_ref, w2_ref, g2_ref, b2_ref, yp_ref, fetch)

    @pl.when(jnp.logical_and(step > 0, is_prompt))
    def _():
        _post_tile(hp_ref, dnp_ref, swp_ref, wod_ref, wos_ref, g1_ref, b1_ref,
                   w1_ref, w2_ref, g2_ref, b2_ref, yp_ref)

    @pl.when(jnp.logical_not(is_prompt))
    def _():
        _post_tile(hs_ref, dns_ref, sws_ref, wod_ref, wos_ref, g1_ref, b1_ref,
                   w1_ref, w2_ref, g2_ref, b2_ref, ys_ref)


def _post_tile(h_ref, dn_ref, sw_ref, wod_ref, wos_ref, g1_ref, b1_ref,
               w1_ref, w2_ref, g2_ref, b2_ref, y_ref, fetch=None):
    mix = (jnp.dot(dn_ref[...], wod_ref[...], preferred_element_type=F32)
           + jnp.dot(sw_ref[...], wos_ref[...], preferred_element_type=F32))
    h1 = _layer_norm(DEEP_ALPHA * h_ref[...] + mix, g1_ref[...], b1_ref[...])
    h1b = h1.astype(BF16)
    f = jnp.zeros_like(h1)
    for j in range(D_FF // FF_BLOCK):
        if fetch is not None:
            fetch(j)
        a = jnp.dot(h1b, w1_ref[:, j * FF_BLOCK:(j + 1) * FF_BLOCK], preferred_element_type=F32)
        a = jnp.square(jnp.maximum(a, 0.0)).astype(BF16)
        f = f + jnp.dot(a, w2_ref[j * FF_BLOCK:(j + 1) * FF_BLOCK, :], preferred_element_type=F32)
    y_ref[...] = _layer_norm(DEEP_ALPHA * h1 + f, g2_ref[...], b2_ref[...])


def _post(prompt, sample, vecs, weights, tm):
    g1, b1, g2, b2 = vecs
    wo_dn, wo_sw, w1, w2 = weights
    n_p, n_s = prompt[0].shape[0] // tm, sample[0].shape[0] // tm
    assert n_p >= 1
    hbm = pl.BlockSpec(memory_space=pl.ANY)
    const = lambda i: (0, 0)
    vec = pl.BlockSpec((1, D_MODEL), const)
    weight = lambda wt: pl.BlockSpec(wt.shape, const, pipeline_mode=pl.Buffered(1))
    blk_p = lambda width: pl.BlockSpec((tm, width), lambda i: (jnp.minimum(i, n_p - 1), 0))
    blk_s = lambda width: pl.BlockSpec((tm, width), lambda i: (jnp.maximum(i - n_p, 0), 0))
    widths = (D_MODEL, DN_W, SWA_W)
    return pl.pallas_call(
        functools.partial(_post_kernel, n_prompt_tiles=n_p),
        out_shape=[jax.ShapeDtypeStruct((n_p * tm, D_MODEL), F32),
                   jax.ShapeDtypeStruct((n_s * tm, D_MODEL), F32)],
        grid=(n_p + n_s,),
        in_specs=[blk_p(wd) for wd in widths] + [blk_s(wd) for wd in widths]
                 + [weight(wo_dn), weight(wo_sw), vec, vec, hbm, hbm, vec, vec],
        out_specs=[blk_p(D_MODEL), blk_s(D_MODEL)],
        scratch_shapes=[pltpu.VMEM((D_MODEL, D_FF), BF16), pltpu.VMEM((D_FF, D_MODEL), BF16),
                        pltpu.VMEM((2, D_MODEL, FF_STAGE), F32),
                        pltpu.VMEM((2, FF_STAGE, D_MODEL), F32),
                        pltpu.SemaphoreType.DMA((2, 2))],
        compiler_params=pltpu.CompilerParams(dimension_semantics=("arbitrary",),
                                             vmem_limit_bytes=VMEM_LIMIT),
        name="post_mlp",
    )(*prompt, *sample, wo_dn, wo_sw, g1, b1, w1, w2, g2, b2)


def kernel(x_prompt, x_sample, state_delta, state_conv, cache_swa_k, cache_swa_v, meta_tokens, ln_in_g, ln_in_b, w_in, w_conv, dn_a_log, dn_dt_bias, dn_norm, swa_sinks, rel_bias_table, w_o, ln1_g, ln1_b, w_ff1, w_ff2, ln2_g, ln2_b):
    assert w_in.shape[0] == DEPTH == 1
    n_seq_s, len_s = x_sample.shape[0], x_sample.shape[1]
    len_p = x_prompt.shape[1]
    cache_len = cache_swa_k.shape[2]
    chunk_p = DELTA_CHUNK
    row = lambda t: t.reshape(1, -1).astype(F32)

    w = jnp.transpose(w_in[0])
    w_sq = (w[OFF_SQ:OFF_SK].astype(BF16).reshape(SWA_KV, SWA_G, HEAD_DIM, D_MODEL)
            .transpose(1, 0, 2, 3).reshape(SWA_W, D_MODEL))
    w_r = (w[:OFF_A].astype(BF16), w_sq, w[OFF_SK:].astype(BF16),
           jnp.pad(w[OFF_A:OFF_SQ].astype(BF16), ((0, LANES - 2 * HEADS), (0, 0))))
    gin, bin_ = row(ln_in_g), row(ln_in_b)
    pad_lanes = lambda t: jnp.pad(t.reshape(1, -1).astype(F32), ((0, 0), (0, LANES - HEADS)))
    lane_head = jnp.arange(DN_W) // HEAD_DIM
    seg = (lane_head[:, None] == lane_head[None, :]).astype(BF16)
    expand = (jnp.arange(LANES)[:, None] == lane_head[None, :]).astype(BF16)
    wconv = jnp.pad(w_conv[0].astype(F32), ((0, SUBLANES - CONV_W), (0, 0)))
    delta_consts = (pad_lanes(dn_a_log[0]), pad_lanes(dn_dt_bias[0]),
                    jnp.tile(dn_norm[0].astype(F32), HEADS).reshape(1, DN_W), seg, expand)
    post_vecs = (row(ln1_g[0]), row(ln1_b[0]), row(ln2_g[0]), row(ln2_b[0]))
    wo_sw = (w_o[0][DN_W:].astype(BF16).reshape(SWA_KV, SWA_G, HEAD_DIM, D_MODEL)
             .transpose(1, 0, 2, 3).reshape(SWA_W, D_MODEL))
    post_w = (w_o[0][:DN_W].astype(BF16), wo_sw, w_ff1.astype(F32), w_ff2.astype(F32))
    table = rel_bias_table.astype(F32).reshape(-1)
    sinks = swa_sinks[0].astype(F32)

    xp = x_prompt[0]
    xs = x_sample.reshape(n_seq_s * len_s, D_MODEL)

    conv_pad = SUBLANES - (CONV_W - 1)
    zero_conv = jnp.zeros((1, SUBLANES, CONV_CH), F32)
    conv_s = jnp.pad(state_conv[0].astype(F32), ((0, 0), (conv_pad, 0), (0, 0)))
    m_qkv, m_z, m_sq, m_sk, m_sv, m_ab, m_tail, _ = _proj(
        meta_tokens.astype(F32), gin, bin_, w_r, zero_conv, wconv, N_META, N_META)
    p_qkv, p_z, p_sq, p_sk, p_sv, p_ab, p_tail, p_h = _proj(
        xp, gin, bin_, w_r, m_tail, wconv, PROMPT_PROJ_TILE, None)
    s_qkv, s_z, s_sq, s_sk, s_sv, s_ab, s_tail, s_h = _proj(
        xs, gin, bin_, w_r, conv_s, wconv, ROW_TILE, len_s)

    zero_s = jnp.zeros((1, HEADS, HEAD_DIM, HEAD_DIM), F32)
    _, s_meta = _delta(m_qkv, m_z, m_ab, zero_s, delta_consts, 1, N_META, N_META, 1, 1)
    p_dn, p_s = _delta(p_qkv, p_z, p_ab, s_meta, delta_consts, 1, len_p, chunk_p, *DELTA_PROMPT)
    s_dn, s_s = _delta(s_qkv, s_z, s_ab, state_delta[0].astype(F32), delta_consts,
                       n_seq_s, len_s, len_s, *DELTA_SAMPLE)

    span = WINDOW
    lead = jnp.zeros((span - N_META, KV_W), F32)
    units_p = SWA_UNITS_PROMPT
    rows_p = units_p * chunk_p
    prev_blocks = rows_p // span
    kv_specs_p = [pl.BlockSpec((span, KV_W), lambda i: (0, 0)),
                  pl.BlockSpec((span, KV_W), lambda i: (jnp.maximum(i * prev_blocks - 1, 0), 0)),
                  pl.BlockSpec((rows_p, KV_W), lambda i: (i, 0))] * 2
    p_sw = _swa(table, sinks, _bucket_map(chunk_p, span + chunk_p, span), p_sq,
                [jnp.concatenate([lead, m_sk], axis=0), p_sk, p_sk,
                 jnp.concatenate([lead, m_sv], axis=0), p_sv, p_sv],
                kv_specs_p, len_p // rows_p, chunk_p, units_p, True, "swa_prompt")
    cache_t = lambda c: (jnp.transpose(c[0].astype(F32), (0, 2, 3, 1))
                         .reshape(n_seq_s * KV_W, cache_len))
    ck, cv = cache_t(cache_swa_k), cache_t(cache_swa_v)
    units_s = SWA_UNITS_SAMPLE
    kv_specs_s = [pl.BlockSpec((units_s * KV_W, cache_len), lambda i: (i, 0)),
                  pl.BlockSpec((units_s * len_s, KV_W), lambda i: (i, 0))] * 2
    s_sw = _swa(table, sinks, _bucket_map(len_s, cache_len + len_s, cache_len), s_sq,
                [ck, s_sk, cv, s_sv], kv_specs_s, n_seq_s // units_s, len_s, units_s, False,
                "swa_sample")

    y_p, y_s = _post((p_h, p_dn, p_sw), (s_h, s_dn, s_sw), post_vecs, post_w, ROW_TILE)

    kv_shape = lambda n, length: (1, n, length, SWA_KV, HEAD_DIM)
    return (y_p[None], y_s.reshape(x_sample.shape),
            p_s[None], p_tail[:, -(CONV_W - 1):][None],
            p_sk[-WINDOW:].reshape(kv_shape(1, WINDOW)), p_sv[-WINDOW:].reshape(kv_shape(1, WINDOW)),
            s_s[None], s_tail[:, -(CONV_W - 1):][None],
            s_sk.reshape(kv_shape(n_seq_s, len_s)), s_sv.reshape(kv_shape(n_seq_s, len_s)))
```

```python
import functools
import math

import jax
import jax.numpy as jnp
from jax import lax
from jax.experimental import pallas as pl
from jax.experimental.pallas import tpu as pltpu

F32 = jnp.float32
BF16 = jnp.bfloat16

D_MODEL = 1024
N_META = 16
HEADS = 8
HEAD_DIM = 64
DN_W = HEADS * HEAD_DIM
CONV_W = 4
CONV_CH = 3 * DN_W
SWA_HEADS = 8
SWA_KV = 2
SWA_G = SWA_HEADS // SWA_KV
KV_W = SWA_KV * HEAD_DIM
WINDOW = 128
N_BUCKETS = 32
MAX_DIST = 128
D_FF = 4 * D_MODEL
DEPTH = 1
DEEP_ALPHA = (2 * DEPTH) ** 0.25
LN_EPS = 1e-5
NORM_EPS = 1e-6
LOG2E = math.log2(math.e)
LANES = 128
SUBLANES = 8
VMEM_LIMIT = 56 * 1024 * 1024

OFF_Z = CONV_CH
OFF_A = OFF_Z + DN_W
OFF_B = OFF_A + HEADS
OFF_SQ = OFF_B + HEADS
OFF_SK = OFF_SQ + SWA_HEADS * HEAD_DIM
PROJ_SPLITS = (CONV_CH, DN_W, SWA_HEADS * HEAD_DIM, KV_W, KV_W, LANES)

ROW_TILE = 512
PROMPT_PROJ_TILE = 1024
DELTA_CHUNK = 64
DELTA_PROMPT = (4, 2)
DELTA_SAMPLE = (8, 2)
SWA_UNITS_PROMPT = 16
SWA_UNITS_SAMPLE = 8
SWA_BATCH = 4


def _layer_norm(x, g, b):
    mu = jnp.mean(x, axis=-1, keepdims=True)
    xc = x - mu
    var = jnp.mean(xc * xc, axis=-1, keepdims=True)
    return xc * lax.rsqrt(var + LN_EPS) * g + b


def _mm(a, b):
    return jnp.dot(a.astype(BF16), b.astype(BF16), preferred_element_type=F32)


def _mm_nt(a, b):
    return lax.dot_general(a.astype(BF16), b.astype(BF16), (((1,), (1,)), ((), ())),
                           preferred_element_type=F32)


def _split3(x):
    x1 = x.astype(BF16)
    r1 = x - x1.astype(F32)
    x2 = r1.astype(BF16)
    x3 = (r1 - x2.astype(F32)).astype(BF16)
    return x1, x2, x3


def _dot_exact_lhs(x, sel):
    return jnp.dot(jnp.concatenate(_split3(x), axis=1), jnp.concatenate([sel] * 3, axis=0),
                   preferred_element_type=F32)


CONV_BLOCK = 256


def _sigmoid(x):
    return 1.0 / (1.0 + jnp.exp(-x))


def _proj_kernel(x_ref, g_ref, b_ref, w_ref, conv0_ref, wconv_ref,
                 y_ref, z_ref, sq_ref, sk_ref, sv_ref, ab_ref, tail_ref, h_ref, xbuf, *, seq_len):
    tm = x_ref.shape[0]
    h_ref[...] = _layer_norm(x_ref[...], g_ref[...], b_ref[...])
    h = h_ref[...].astype(BF16)

    chained = seq_len is None
    n_seq = 1 if chained else tm // seq_len
    L = tm if chained else seq_len
    if chained:
        @pl.when(pl.program_id(0) == 0)
        def _():
            xbuf[0, 0:SUBLANES, :] = conv0_ref[0]
    else:
        for s in range(n_seq):
            xbuf[s, 0:SUBLANES, :] = conv0_ref[s]
    blocks = [slice(cb, cb + CONV_BLOCK) for cb in range(0, CONV_CH, CONV_BLOCK)]

    def project(cols):
        raw = _mm_nt(h, w_ref[cols, :])
        for s in range(n_seq):
            xbuf[s, SUBLANES:SUBLANES + L, cols] = raw[s * L:(s + 1) * L, :]

    def conv(cols):
        for s in range(n_seq):
            ext = xbuf[s, :, cols]
            w0, w1, w2, w3 = (0.5 * wconv_ref[j:j + 1, cols] for j in range(CONV_W))
            ext1 = pltpu.roll(ext, 1, axis=0)
            half = (w3 * ext + w2 * ext1)[SUBLANES:, :]
            half = half + pltpu.roll(w1 * ext + w0 * ext1, 2, axis=0)[SUBLANES:, :]
            y_ref[s * L:(s + 1) * L, cols] = half * (1.0 + jnp.tanh(half))
            tail_ref[s, :, cols] = xbuf[s, L:L + SUBLANES, cols]
            if chained:
                xbuf[s, 0:SUBLANES, cols] = xbuf[s, L:L + SUBLANES, cols]

    head = lambda hd: w_ref[OFF_SQ + hd * HEAD_DIM:OFF_SQ + (hd + 1) * HEAD_DIM, :]
    w_sq = jnp.concatenate([head(kv * SWA_G + g) for g in range(SWA_G) for kv in range(SWA_KV)],
                           axis=0)
    w_ab = jnp.concatenate([w_ref[OFF_A:OFF_SQ, :],
                            jnp.zeros((LANES - 2 * HEADS, D_MODEL), BF16)], axis=0)
    for o_ref, rhs in ((z_ref, w_ref[OFF_Z:OFF_A, :]), (sq_ref, w_sq),
                       (sk_ref, w_ref[OFF_SK:OFF_SK + KV_W, :]),
                       (sv_ref, w_ref[OFF_SK + KV_W:OFF_SK + 2 * KV_W, :]), (ab_ref, w_ab)):
        o_ref[...] = _mm_nt(h, rhs)
    project(blocks[0])
    for bi, cols in enumerate(blocks):
        if bi + 1 < len(blocks):
            project(blocks[bi + 1])
        conv(cols)


def _proj(x, g, b, weights, conv0, wconv, tm, seq_len):
    rows = x.shape[0]
    chained = seq_len is None
    n_tile_seq = 1 if chained else tm // seq_len
    n_seq = 1 if chained else rows // seq_len
    const = lambda i: (0, 0)
    row_blk = lambda width: pl.BlockSpec((tm, width), lambda i: (i, 0))
    seq_blk = pl.BlockSpec((n_tile_seq, SUBLANES, CONV_CH),
                           (lambda i: (0, 0, 0)) if chained else (lambda i: (i, 0, 0)))
    return pl.pallas_call(
        functools.partial(_proj_kernel, seq_len=seq_len),
        out_shape=[jax.ShapeDtypeStruct((rows, width), F32) for width in PROJ_SPLITS]
                  + [jax.ShapeDtypeStruct((n_seq, SUBLANES, CONV_CH), F32),
                     jax.ShapeDtypeStruct((rows, D_MODEL), F32)],
        grid=(rows // tm,),
        in_specs=[row_blk(D_MODEL), pl.BlockSpec((1, D_MODEL), const),
                  pl.BlockSpec((1, D_MODEL), const)]
                 + [pl.BlockSpec(weights.shape, const, pipeline_mode=pl.Buffered(1)),
                    seq_blk, pl.BlockSpec((SUBLANES, CONV_CH), const)],
        out_specs=[row_blk(width) for width in PROJ_SPLITS] + [seq_blk, row_blk(D_MODEL)],
        scratch_shapes=[pltpu.VMEM((n_tile_seq, SUBLANES + (tm if chained else seq_len), CONV_CH),
                                   F32)],
        compiler_params=pltpu.CompilerParams(dimension_semantics=("arbitrary",),
                                             vmem_limit_bytes=VMEM_LIMIT),
        name="ln_in_proj",
    )(x, g, b, weights, conv0, wconv)


GROUP_HEADS = 4


def _softplus(x):
    return jnp.maximum(x, 0.0) + jnp.log1p(jnp.exp(-jnp.abs(x)))


def _delta_kernel(qkv_ref, z_ref, ab_ref, s0_ref, alog_ref, dtb_ref,
                  normw_ref, seg_ref, expand_ref, o_ref, s_ref, sbd,
                  *, chunk, units, groups, sequential):
    C, U = chunk, units
    RG = U * C
    PAIRS = HEADS // GROUP_HEADS
    PW = GROUP_HEADS * HEAD_DIM
    step = pl.program_id(0)
    last_step = pl.num_programs(0) - 1
    expand = expand_ref[...]

    def seg_sum(t):
        tb = t.astype(BF16)
        return jnp.concatenate(
            [jnp.dot(tb[:, g * PW:(g + 1) * PW], seg_ref[g * PW:(g + 1) * PW, g * PW:(g + 1) * PW],
                     preferred_element_type=F32) for g in range(PAIRS)], axis=1)

    ri = lax.broadcasted_iota(jnp.int32, (C, PW), 0)
    ci = lax.broadcasted_iota(jnp.int32, (C, PW), 1) % HEAD_DIM
    causal = (ri >= ci) & (ci < C)
    strict = (ri > ci) & (ci < C)
    eye = (ri == ci).astype(F32)
    lane_head = lax.broadcasted_iota(jnp.int32, (1, PW), 1) // HEAD_DIM
    rs = lax.broadcasted_iota(jnp.int32, (LANES, LANES), 0)
    cs = lax.broadcasted_iota(jnp.int32, (LANES, LANES), 1)
    half_mask = ((rs // HEAD_DIM) == (cs // HEAD_DIM)).astype(F32)
    row_pad = [jnp.zeros((HEAD_DIM - C, PW), BF16)] if C < HEAD_DIM else []
    lane_pad = [jnp.zeros((1, HEAD_DIM - C), F32)] if C < HEAD_DIM else []
    rr = lax.broadcasted_iota(jnp.int32, (RG, RG), 0)
    cc = lax.broadcasted_iota(jnp.int32, (RG, RG), 1)
    tril = ((rr // C == cc // C) & (rr >= cc)).astype(BF16)

    def stack(y):
        yb = y.astype(BF16)
        zero = jnp.zeros_like(yb)
        pad = row_pad if y.shape[0] < HEAD_DIM else []
        blocks = []
        for h in range(GROUP_HEADS):
            blocks += [jnp.where(lane_head == h, yb, zero)] + pad
        return jnp.concatenate(blocks, axis=0)

    items = [(u, p) for u in range(U) for p in range(PAIRS)]
    sl = lambda u, p: (slice(u * C, (u + 1) * C), slice(p * PW, (p + 1) * PW))

    def front(gi):
        rg = slice(gi * RG, (gi + 1) * RG)
        q, k, v = (qkv_ref[rg, j * DN_W:(j + 1) * DN_W] for j in range(3))
        qn = q * (lax.rsqrt(seg_sum(q * q) + NORM_EPS) * HEAD_DIM ** -0.5)
        kn = k * lax.rsqrt(seg_sum(k * k) + NORM_EPS)
        yield

        ab = ab_ref[rg, :]
        g = -jnp.exp(alog_ref[...]) * _softplus(ab + dtb_ref[...])
        gs = jnp.dot(tril, jnp.concatenate(_split3(g), axis=1), preferred_element_type=F32)
        G = (gs[:, :LANES] + gs[:, LANES:2 * LANES] + gs[:, 2 * LANES:]) * LOG2E
        GT = G.T
        G_x = _dot_exact_lhs(G, expand)
        beta = _sigmoid(pltpu.roll(ab, LANES - HEADS, axis=1))
        b_hi = beta.astype(BF16)
        b_lo = (beta - b_hi.astype(F32)).astype(BF16)
        beta_x = jnp.dot(jnp.concatenate([b_hi, b_lo], axis=1),
                         jnp.concatenate([expand, expand], axis=0), preferred_element_type=F32)
        glast_x = jnp.concatenate(
            [jnp.broadcast_to(G_x[(u + 1) * C - 1:(u + 1) * C, :], (C, DN_W)) for u in range(U)],
            axis=0)
        yield
        eg_x = jnp.exp2(G_x)
        kbeta = kn * beta_x
        return dict(qn=qn, kn=kn, kbeta=kbeta, bk=kbeta * eg_x, bv=v * beta_x, qd=qn * eg_x,
                    kd=kn * jnp.exp2(glast_x - G_x), G_x=G_x, GT=GT, eg_x=eg_x)

    def middle(f):
        A, QK = {}, {}
        for (u, p) in items:
            ru, lp = sl(u, p)
            gr = []
            for h in range(p * GROUP_HEADS, (p + 1) * GROUP_HEADS):
                gr += [f["GT"][h:h + 1, ru]] + lane_pad
            gr = jnp.concatenate(gr, axis=1)
            decay = jnp.exp2(jnp.where(causal, f["G_x"][ru, lp] - gr, -jnp.inf))
            r = _mm_nt(jnp.concatenate([f["kbeta"][ru, lp], f["qn"][ru, lp]], axis=0),
                       stack(f["kn"][ru, lp]))
            A[u, p] = jnp.where(strict, r[:C] * decay, 0.0)
            QK[u, p] = r[C:] * decay
        yield

        n_pow = int(math.log2(C)) - 1
        P = {it: eye - A[it] for it in items}
        Ap = {it: _mm(A[it], stack(A[it])) for it in items}
        yield
        for i in range(n_pow):
            if i < n_pow - 1:
                r = {it: _mm(jnp.concatenate([Ap[it], P[it]], axis=0), stack(Ap[it])) for it in items}
                Ap = {it: r[it][:C] for it in items}
                P = {it: P[it] + r[it][C:] for it in items}
            else:
                P = {it: P[it] + _mm(P[it], stack(Ap[it])) for it in items}
            yield

        halves = [slice(hb * LANES, (hb + 1) * LANES) for hb in range(PW // LANES)]
        wu = {it: _mm(P[it], jnp.concatenate([stack(f["bk"][sl(*it)]), stack(f["bv"][sl(*it)])],
                                             axis=1)) for it in items}
        kdT = {(it, hb): f["kd"][sl(*it)][:, cols].T for it in items
               for hb, cols in enumerate(halves)}
        yield
        mn = {(it, hb): _mm(kdT[it, hb], jnp.concatenate(
            [wu[it][:, cols], wu[it][:, PW:][:, cols]], axis=1))
              for it in items for hb, cols in enumerate(halves)}
        yield
        n_t = {key: (half_mask * mn[key][:, LANES:]).T for key in mn}
        out = {}
        for (u, p) in items:
            ru, lp = sl(u, p)
            it = (u, p)
            out[it] = dict(
                W=wu[it][:, :PW], U=wu[it][:, PW:],
                M=diag_blocks([half_mask * mn[it, hb][:, :LANES]
                               for hb in range(len(halves))]).astype(BF16),
                NT=jnp.concatenate([n_t[it, hb][:HEAD_DIM] + n_t[it, hb][HEAD_DIM:]
                                    for hb in range(len(halves))], axis=1),
                QK=QK[it], qd=f["qd"][ru, lp],
                eg_last=f["eg_x"][(u + 1) * C - 1:(u + 1) * C, lp])
        return out

    def diag_blocks(blocks):
        n = len(blocks)
        rows = []
        for i, blk in enumerate(blocks):
            rows.append(jnp.concatenate(
                [blk if j == i else jnp.zeros_like(blk) for j in range(n)], axis=1))
        return jnp.concatenate(rows, axis=0)

    def load_state(ref, prefix, p):
        return jnp.concatenate([ref[prefix + (p * GROUP_HEADS + h,)].T
                                for h in range(GROUP_HEADS)], axis=1)

    def store_state(ref, prefix, p, st):
        for h in range(GROUP_HEADS):
            ref[prefix + (p * GROUP_HEADS + h,)] = st[:, h * HEAD_DIM:(h + 1) * HEAD_DIM].T

    chunks = [(gi, u) for gi in range(groups) for u in range(U)]
    if sequential:
        @pl.when(step == 0)
        def _():
            for p in range(PAIRS):
                sbd[p] = load_state(s0_ref, (), p)
        s_start = {p: sbd[p] for p in range(PAIRS)}
    else:
        s_start = {(idx, p): load_state(s0_ref, (idx,), p)
                   for idx in range(len(chunks)) for p in range(PAIRS)}

    def chain(gi, out, s_in):
        if sequential:
            states = {p: [s_in[p]] for p in range(PAIRS)}
            for u in range(U):
                for p in range(PAIRS):
                    d, s_cur = out[u, p], states[p][-1]
                    states[p].append(s_cur * d["eg_last"] - _mm_nt(s_cur, d["M"]) + d["NT"])
            return ({(u, p): states[p][u] for (u, p) in items},
                    {p: states[p][-1] for p in range(PAIRS)})
        entering = {(u, p): s_in[gi * U + u, p] for (u, p) in items}
        s_out = {}
        for (u, p) in items:
            d, s_cur = out[u, p], entering[u, p]
            s_out[gi * U + u, p] = s_cur * d["eg_last"] - _mm_nt(s_cur, d["M"]) + d["NT"]
        return entering, s_out

    def finish(gi, out, entering):
        r = {it: _mm_nt(jnp.concatenate([out[it]["W"], out[it]["qd"]], axis=0),
                        stack(entering[it])) for it in items}
        yield
        tiles = {it: r[it][C:] + _mm(out[it]["QK"], stack(out[it]["U"] - r[it][:C]))
                 for it in items}
        yield
        rg = slice(gi * RG, (gi + 1) * RG)
        o = jnp.concatenate([jnp.concatenate([tiles[u, p] for p in range(PAIRS)], axis=1)
                             for u in range(U)], axis=0)
        ms = seg_sum(o * o) * (1.0 / HEAD_DIM)
        hz = 0.5 * z_ref[rg, :]
        o_ref[rg, :] = (o * lax.rsqrt(ms + NORM_EPS) * normw_ref[...]
                        * (hz * (1.0 + jnp.tanh(hz)))).astype(BF16)

    def drive(gens):
        results = [None] * len(gens)
        active = list(enumerate(gens))
        while active:
            still = []
            for i, gen in active:
                try:
                    next(gen)
                    still.append((i, gen))
                except StopIteration as stop:
                    results[i] = stop.value
            active = still
        return results

    state = s_start
    s_final = {}
    f_cur = drive([front(0)])[0]
    pending = None
    for gi in range(groups + 1):
        gens = []
        if gi < groups:
            gens.append(middle(f_cur))
        if gi + 1 < groups:
            gens.append(front(gi + 1))
        if pending is not None:
            entering, s_out = chain(gi - 1, pending, state)
            s_final.update(s_out)
            state = s_out if sequential else state
            gens.append(finish(gi - 1, pending, entering))
        res = drive(gens)
        pending = res[0] if gi < groups else None
        f_cur = res[1] if gi + 1 < groups else None

    if sequential:
        for p in range(PAIRS):
            sbd[p] = s_final[p]

        @pl.when(step == last_step)
        def _():
            for p in range(PAIRS):
                store_state(s_ref, (), p, s_final[p])
    else:
        for (idx, p), s_new in s_final.items():
            store_state(s_ref, (idx,), p, s_new)


def _delta(qkv, z, ab, s0, consts, n_seq, seq_len, chunk, units, groups):
    alog, dtb, normw, seg, expand = consts
    sequential = n_seq == 1
    rows = n_seq * seq_len
    n_chunks = units * groups
    blk_rows = n_chunks * chunk
    assert rows % blk_rows == 0 and (sequential or seq_len == chunk)
    const2 = lambda shape: pl.BlockSpec(shape, lambda i: (0, 0))
    tok = lambda width: pl.BlockSpec((blk_rows, width), lambda i: (i, 0))
    group_w = GROUP_HEADS * HEAD_DIM
    if sequential:
        state = pl.BlockSpec((None, HEADS, HEAD_DIM, HEAD_DIM), lambda i: (0, 0, 0, 0))
    else:
        state = pl.BlockSpec((n_chunks, HEADS, HEAD_DIM, HEAD_DIM), lambda i: (i, 0, 0, 0))
    return pl.pallas_call(
        functools.partial(_delta_kernel, chunk=chunk, units=units, groups=groups,
                          sequential=sequential),
        out_shape=[jax.ShapeDtypeStruct((rows, DN_W), BF16),
                   jax.ShapeDtypeStruct((n_seq, HEADS, HEAD_DIM, HEAD_DIM), F32)],
        grid=(rows // blk_rows,),
        in_specs=[tok(CONV_CH), tok(DN_W), tok(LANES), state,
                  const2((1, LANES)), const2((1, LANES)),
                  const2((1, DN_W)), const2((DN_W, DN_W)), const2((LANES, DN_W))],
        out_specs=[tok(DN_W), state],
        scratch_shapes=[pltpu.VMEM((HEADS // GROUP_HEADS, HEAD_DIM, group_w), F32)],
        compiler_params=pltpu.CompilerParams(
            dimension_semantics=("arbitrary",), vmem_limit_bytes=VMEM_LIMIT,
        ),
        name=f"delta_c{chunk}",
    )(qkv, z, ab, s0, alog, dtb, normw, seg, expand)


SWA_W = SWA_HEADS * HEAD_DIM


def _swa_kernel(*refs, n_q, units, prompt, batch=SWA_BATCH):
    if prompt:
        (table_ref, sinks_ref, bucket_ref, q_ref, km_ref, kp_ref, kc_ref, vm_ref, vp_ref, vc_ref,
         o_ref, bias_ref) = refs
    else:
        (table_ref, sinks_ref, bucket_ref, q_ref, kp_ref, kc_ref, vp_ref, vc_ref,
         o_ref, bias_ref) = refs
    n_k = bias_ref.shape[-1]
    n_grp = SWA_W // KV_W
    step = pl.program_id(0)

    @pl.when(step == 0)
    def _():
        bucket = bucket_ref[...]
        for h in range(SWA_HEADS):
            acc = jnp.zeros((n_q, n_k), F32)
            for b in range(N_BUCKETS):
                acc = jnp.where(bucket == b, table_ref[b * SWA_HEADS + h], acc)
            kv, gi = divmod(h, SWA_G)
            bias_ref[kv, gi * n_q:(gi + 1) * n_q, :] = acc * LOG2E

    lane = lax.broadcasted_iota(jnp.int32, (1, KV_W), 1)
    kv_mask = [(lane < HEAD_DIM).astype(F32), (lane >= HEAD_DIM).astype(F32)]
    if prompt:
        first = step == 0
        k_all = jnp.concatenate([jnp.where(first, km_ref[...], kp_ref[...]), kc_ref[...]], axis=0)
        v_all = jnp.concatenate([jnp.where(first, vm_ref[...], vp_ref[...]), vc_ref[...]], axis=0)
        k_kv = [(k_all * m).astype(BF16) for m in kv_mask]
        v_kv = [(v_all * m).astype(BF16) for m in kv_mask]
        keys = lambda j, kv: k_kv[kv][j * n_q:j * n_q + n_k]
        vals = lambda j, kv: v_kv[kv][j * n_q:j * n_q + n_k]
    else:
        n_c = n_k - n_q
        sub = lax.broadcasted_iota(jnp.int32, (KV_W, 1), 0)
        row_mask = [(sub < HEAD_DIM).astype(F32), (sub >= HEAD_DIM).astype(F32)]
        past = lambda ref, j, kv: (ref[j * KV_W:(j + 1) * KV_W, :] * row_mask[kv]).astype(BF16)
        new = lambda ref, j, kv: (ref[j * n_q:(j + 1) * n_q, :] * kv_mask[kv]).astype(BF16)
        ones_c, ones_q = jnp.ones((KV_W, n_c), BF16), jnp.ones((n_q, KV_W), BF16)

    q = (q_ref[...] * (HEAD_DIM ** -0.5 * LOG2E)).astype(BF16)
    ones_k = jnp.ones((n_k, KV_W), BF16)
    sink = [jnp.concatenate([jnp.full((n_q, KV_W), sinks_ref[kv * SWA_G + gi] * LOG2E, F32)
                             for gi in range(SWA_G)], axis=0) for kv in range(SWA_KV)]
    rows4 = SWA_G * n_q
    for j0 in range(0, units, batch):
        js = range(j0, min(j0 + batch, units))
        items = [(j, kv) for j in js for kv in range(SWA_KV)]
        s = {}
        for j in js:
            rows = slice(j * n_q, (j + 1) * n_q)
            q4 = jnp.concatenate([q[rows, g * KV_W:(g + 1) * KV_W] for g in range(n_grp)], axis=0)
            for kv in range(SWA_KV):
                if prompt:
                    sj = _mm_nt(q4, keys(j, kv))
                else:
                    sj = jnp.concatenate([_mm(q4, past(kp_ref, j, kv)),
                                          _mm_nt(q4, new(kc_ref, j, kv))], axis=1)
                sj = sj + bias_ref[kv]
                if prompt and j * n_q < WINDOW - N_META:
                    key_pos = j * n_q + lax.broadcasted_iota(jnp.int32, (1, n_k), 1)
                    n_invalid = jnp.where(first, WINDOW - N_META, 0)
                    sj = jnp.where(key_pos < n_invalid, -jnp.inf, sj)
                s[j, kv] = sj
        m = {it: jnp.maximum(jnp.broadcast_to(jnp.max(s[it], axis=-1, keepdims=True),
                                              (rows4, KV_W)), sink[it[1]]) for it in items}
        p = {it: jnp.exp2(s[it] - jnp.concatenate([m[it], m[it][:, :n_k - KV_W]], axis=1))
             for it in items}
        if prompt:
            pv = {it: _mm(p[it], jnp.concatenate([vals(*it), ones_k], axis=1)) for it in items}
        else:
            pb = {it: p[it].astype(BF16) for it in items}
            pv = {it: _mm_nt(pb[it][:, :n_c], jnp.concatenate([past(vp_ref, *it), ones_c], axis=0))
                  + _mm(pb[it][:, n_c:], jnp.concatenate([new(vc_ref, *it), ones_q], axis=1))
                  for it in items}
        o = {it: pv[it][:, :KV_W] * (1.0 / (pv[it][:, KV_W:] + jnp.exp2(sink[it[1]] - m[it])))
             for it in items}
        for j in js:
            o2 = o[j, 0] + o[j, 1]
            for g in range(n_grp):
                o_ref[j * n_q:(j + 1) * n_q, g * KV_W:(g + 1) * KV_W] = (
                    o2[g * n_q:(g + 1) * n_q, :].astype(BF16))


def _swa(table, sinks, bucket, q, kv_args, kv_specs, n_steps, n_q, units, prompt, name):
    n_k = bucket.shape[1]
    smem = pl.BlockSpec(memory_space=pltpu.SMEM)
    rows = units * n_q
    return pl.pallas_call(
        functools.partial(_swa_kernel, n_q=n_q, units=units, prompt=prompt),
        out_shape=jax.ShapeDtypeStruct((n_steps * rows, SWA_W), BF16),
        grid=(n_steps,),
        in_specs=[smem, smem, pl.BlockSpec((n_q, n_k), lambda i: (0, 0)),
                  pl.BlockSpec((rows, SWA_W), lambda i: (i, 0))] + list(kv_specs),
        out_specs=pl.BlockSpec((rows, SWA_W), lambda i: (i, 0)),
        scratch_shapes=[pltpu.VMEM((SWA_KV, SWA_G * n_q, n_k), F32)],
        compiler_params=pltpu.CompilerParams(dimension_semantics=("arbitrary",),
                                             vmem_limit_bytes=VMEM_LIMIT),
        name=name,
    )(table, sinks, bucket, q, *kv_args)


def _t5_bucket(rel):
    half = N_BUCKETS // 2
    max_exact = half // 2
    n = jnp.abs(rel)
    large = max_exact + (jnp.log(jnp.maximum(n, 1).astype(F32) / max_exact)
                         / math.log(MAX_DIST / max_exact) * (half - max_exact)).astype(jnp.int32)
    large = jnp.minimum(large, half - 1)
    return jnp.where(rel > 0, half, 0) + jnp.where(n < max_exact, n, large)


def _bucket_map(n_q, n_k, key_offset):
    rel = (jnp.arange(n_k)[None, :] - key_offset) - jnp.arange(n_q)[:, None]
    return _t5_bucket(rel).astype(jnp.int32)


FF_BLOCK = 1024


FF_STAGE = 512


def _post_kernel(hp_ref, dnp_ref, swp_ref, hs_ref, dns_ref, sws_ref,
                 wod_ref, wos_ref, g1_ref, b1_ref, w1_hbm, w2_hbm, g2_ref, b2_ref,
                 yp_ref, ys_ref, w1_ref, w2_ref, stage1, stage2, sems, *, n_prompt_tiles):
    step = pl.program_id(0)
    is_prompt = step < n_prompt_tiles
    n_stage = D_FF // FF_STAGE

    def copies(c):
        slot, blk = c % 2, pl.ds(c * FF_STAGE, FF_STAGE)
        return (pltpu.make_async_copy(w1_hbm.at[0, :, blk], stage1.at[slot], sems.at[0, slot]),
                pltpu.make_async_copy(w2_hbm.at[0, blk, :], stage2.at[slot], sems.at[1, slot]))

    def start(c):
        for cp in copies(c):
            cp.start()

    def fetch(j):
        per = FF_BLOCK // FF_STAGE
        for c in range(j * per, (j + 1) * per):
            for cp in copies(c):
                cp.wait()
            w1_ref[:, c * FF_STAGE:(c + 1) * FF_STAGE] = stage1[c % 2].astype(BF16)
            w2_ref[c * FF_STAGE:(c + 1) * FF_STAGE, :] = stage2[c % 2].astype(BF16)
            if c + 2 < n_stage:
                start(c + 2)

    @pl.when(step == 0)
    def _():
        start(0)
        start(1)
        _post_tile(hp_ref, dnp_ref, swp_ref, wod_ref, wos_ref, g1_ref, b1_ref,
                   w1_ref, w2_ref, g2_ref, b2_ref, yp_ref, fetch)

    @pl.when(jnp.logical_and(step > 0, is_prompt))
    def _():
        _post_tile(hp_ref, dnp_ref, swp_ref, wod_ref, wos_ref, g1_ref, b1_ref,
                   w1_ref, w2_ref, g2_ref, b2_ref, yp_ref)

    @pl.when(jnp.logical_not(is_prompt))
    def _():
        _post_tile(hs_ref, dns_ref, sws_ref, wod_ref, wos_ref, g1_ref, b1_ref,
                   w1_ref, w2_ref, g2_ref, b2_ref, ys_ref)


def _post_tile(h_ref, dn_ref, sw_ref, wod_ref, wos_ref, g1_ref, b1_ref,
               w1_ref, w2_ref, g2_ref, b2_ref, y_ref, fetch=None):
    mix = (jnp.dot(dn_ref[...], wod_ref[...], preferred_element_type=F32)
           + jnp.dot(sw_ref[...], wos_ref[...], preferred_element_type=F32))
    h1 = _layer_norm(DEEP_ALPHA * h_ref[...] + mix, g1_ref[...], b1_ref[...])
    h1b = h1.astype(BF16)
    f = jnp.zeros_like(h1)
    for j in range(D_FF // FF_BLOCK):
        if fetch is not None:
            fetch(j)
        a = jnp.dot(h1b, w1_ref[:, j * FF_BLOCK:(j + 1) * FF_BLOCK], preferred_element_type=F32)
        a = jnp.square(jnp.maximum(a, 0.0)).astype(BF16)
        f = f + jnp.dot(a, w2_ref[j * FF_BLOCK:(j + 1) * FF_BLOCK, :], preferred_element_type=F32)
    y_ref[...] = _layer_norm(DEEP_ALPHA * h1 + f, g2_ref[...], b2_ref[...])


def _post(prompt, sample, vecs, weights, tm):
    g1, b1, g2, b2 = vecs
    wo_dn, wo_sw, w1, w2 = weights
    n_p, n_s = prompt[0].shape[0] // tm, sample[0].shape[0] // tm
    assert n_p >= 1
    hbm = pl.BlockSpec(memory_space=pl.ANY)
    const = lambda i: (0, 0)
    vec = pl.BlockSpec((1, D_MODEL), const)
    weight = lambda wt: pl.BlockSpec(wt.shape, const, pipeline_mode=pl.Buffered(1))
    blk_p = lambda width: pl.BlockSpec((tm, width), lambda i: (jnp.minimum(i, n_p - 1), 0))
    blk_s = lambda width: pl.BlockSpec((tm, width), lambda i: (jnp.maximum(i - n_p, 0), 0))
    widths = (D_MODEL, DN_W, SWA_W)
    return pl.pallas_call(
        functools.partial(_post_kernel, n_prompt_tiles=n_p),
        out_shape=[jax.ShapeDtypeStruct((n_p * tm, D_MODEL), F32),
                   jax.ShapeDtypeStruct((n_s * tm, D_MODEL), F32)],
        grid=(n_p + n_s,),
        in_specs=[blk_p(wd) for wd in widths] + [blk_s(wd) for wd in widths]
                 + [weight(wo_dn), weight(wo_sw), vec, vec, hbm, hbm, vec, vec],
        out_specs=[blk_p(D_MODEL), blk_s(D_MODEL)],
        scratch_shapes=[pltpu.VMEM((D_MODEL, D_FF), BF16), pltpu.VMEM((D_FF, D_MODEL), BF16),
                        pltpu.VMEM((2, D_MODEL, FF_STAGE), F32),
                        pltpu.VMEM((2, FF_STAGE, D_MODEL), F32),
                        pltpu.SemaphoreType.DMA((2, 2))],
        compiler_params=pltpu.CompilerParams(dimension_semantics=("arbitrary",),
                                             vmem_limit_bytes=VMEM_LIMIT),
        name="post_mlp",
    )(*prompt, *sample, wo_dn, wo_sw, g1, b1, w1, w2, g2, b2)


def kernel(x_prompt, x_sample, state_delta, state_conv, cache_swa_k, cache_swa_v, meta_tokens, ln_in_g, ln_in_b, w_in, w_conv, dn_a_log, dn_dt_bias, dn_norm, swa_sinks, rel_bias_table, w_o, ln1_g, ln1_b, w_ff1, w_ff2, ln2_g, ln2_b):
    assert w_in.shape[0] == DEPTH == 1
    n_seq_s, len_s = x_sample.shape[0], x_sample.shape[1]
    len_p = x_prompt.shape[1]
    cache_len = cache_swa_k.shape[2]
    chunk_p = DELTA_CHUNK
    row = lambda t: t.reshape(1, -1).astype(F32)

    w_r = jnp.transpose(w_in[0]).astype(BF16)
    gin, bin_ = row(ln_in_g), row(ln_in_b)
    pad_lanes = lambda t: jnp.pad(t.reshape(1, -1).astype(F32), ((0, 0), (0, LANES - HEADS)))
    lane_head = jnp.arange(DN_W) // HEAD_DIM
    seg = (lane_head[:, None] == lane_head[None, :]).astype(BF16)
    expand = (jnp.arange(LANES)[:, None] == lane_head[None, :]).astype(BF16)
    wconv = jnp.pad(w_conv[0].astype(F32), ((0, SUBLANES - CONV_W), (0, 0)))
    delta_consts = (pad_lanes(dn_a_log[0]), pad_lanes(dn_dt_bias[0]),
                    jnp.tile(dn_norm[0].astype(F32), HEADS).reshape(1, DN_W), seg, expand)
    post_vecs = (row(ln1_g[0]), row(ln1_b[0]), row(ln2_g[0]), row(ln2_b[0]))
    wo_sw = (w_o[0][DN_W:].astype(BF16).reshape(SWA_KV, SWA_G, HEAD_DIM, D_MODEL)
             .transpose(1, 0, 2, 3).reshape(SWA_W, D_MODEL))
    post_w = (w_o[0][:DN_W].astype(BF16), wo_sw, w_ff1.astype(F32), w_ff2.astype(F32))
    table = rel_bias_table.astype(F32).reshape(-1)
    sinks = swa_sinks[0].astype(F32)

    xp = x_prompt[0]
    xs = x_sample.reshape(n_seq_s * len_s, D_MODEL)

    conv_pad = SUBLANES - (CONV_W - 1)
    zero_conv = jnp.zeros((1, SUBLANES, CONV_CH), F32)
    conv_s = jnp.pad(state_conv[0].astype(F32), ((0, 0), (conv_pad, 0), (0, 0)))
    m_qkv, m_z, m_sq, m_sk, m_sv, m_ab, m_tail, _ = _proj(
        meta_tokens.astype(F32), gin, bin_, w_r, zero_conv, wconv, N_META, N_META)
    p_qkv, p_z, p_sq, p_sk, p_sv, p_ab, p_tail, p_h = _proj(
        xp, gin, bin_, w_r, m_tail, wconv, PROMPT_PROJ_TILE, None)
    s_qkv, s_z, s_sq, s_sk, s_sv, s_ab, s_tail, s_h = _proj(
        xs, gin, bin_, w_r, conv_s, wconv, ROW_TILE, len_s)

    zero_s = jnp.zeros((1, HEADS, HEAD_DIM, HEAD_DIM), F32)
    _, s_meta = _delta(m_qkv, m_z, m_ab, zero_s, delta_consts, 1, N_META, N_META, 1, 1)
    p_dn, p_s = _delta(p_qkv, p_z, p_ab, s_meta, delta_consts, 1, len_p, chunk_p, *DELTA_PROMPT)
    s_dn, s_s = _delta(s_qkv, s_z, s_ab, state_delta[0].astype(F32), delta_consts,
                       n_seq_s, len_s, len_s, *DELTA_SAMPLE)

    span = WINDOW
    lead = jnp.zeros((span - N_META, KV_W), F32)
    units_p = SWA_UNITS_PROMPT
    rows_p = units_p * chunk_p
    prev_blocks = rows_p // span
    kv_specs_p = [pl.BlockSpec((span, KV_W), lambda i: (0, 0)),
                  pl.BlockSpec((span, KV_W), lambda i: (jnp.maximum(i * prev_blocks - 1, 0), 0)),
                  pl.BlockSpec((rows_p, KV_W), lambda i: (i, 0))] * 2
    p_sw = _swa(table, sinks, _bucket_map(chunk_p, span + chunk_p, span), p_sq,
                [jnp.concatenate([lead, m_sk], axis=0), p_sk, p_sk,
                 jnp.concatenate([lead, m_sv], axis=0), p_sv, p_sv],
                kv_specs_p, len_p // rows_p, chunk_p, units_p, True, "swa_prompt")
    cache_t = lambda c: (jnp.transpose(c[0].astype(F32), (0, 2, 3, 1))
                         .reshape(n_seq_s * KV_W, cache_len))
    ck, cv = cache_t(cache_swa_k), cache_t(cache_swa_v)
    units_s = SWA_UNITS_SAMPLE
    kv_specs_s = [pl.BlockSpec((units_s * KV_W, cache_len), lambda i: (i, 0)),
                  pl.BlockSpec((units_s * len_s, KV_W), lambda i: (i, 0))] * 2
    s_sw = _swa(table, sinks, _bucket_map(len_s, cache_len + len_s, cache_len), s_sq,
                [ck, s_sk, cv, s_sv], kv_specs_s, n_seq_s // units_s, len_s, units_s, False,
                "swa_sample")

    y_p, y_s = _post((p_h, p_dn, p_sw), (s_h, s_dn, s_sw), post_vecs, post_w, ROW_TILE)

    kv_shape = lambda n, length: (1, n, length, SWA_KV, HEAD_DIM)
    return (y_p[None], y_s.reshape(x_sample.shape),
            p_s[None], p_tail[:, -(CONV_W - 1):][None],
            p_sk[-WINDOW:].reshape(kv_shape(1, WINDOW)), p_sv[-WINDOW:].reshape(kv_shape(1, WINDOW)),
            s_s[None], s_tail[:, -(CONV_W - 1):][None],
            s_sk.reshape(kv_shape(n_seq_s, len_s)), s_sv.reshape(kv_shape(n_seq_s, len_s)))
```

```python
import functools
import math

import jax
import jax.numpy as jnp
import numpy as np
from jax import lax
from jax.experimental import pallas as pl
from jax.experimental.pallas import tpu as pltpu

F32 = jnp.float32
BF16 = jnp.bfloat16

D_MODEL = 1024
N_META = 16
HEADS = 8
HEAD_DIM = 64
DN_W = HEADS * HEAD_DIM
CONV_W = 4
CONV_CH = 3 * DN_W
SWA_HEADS = 8
SWA_KV = 2
SWA_G = SWA_HEADS // SWA_KV
KV_W = SWA_KV * HEAD_DIM
WINDOW = 128
N_BUCKETS = 32
MAX_DIST = 128
D_FF = 4 * D_MODEL
DEPTH = 1
DEEP_ALPHA = (2 * DEPTH) ** 0.25
LN_EPS = 1e-5
NORM_EPS = 1e-6
LOG2E = math.log2(math.e)
LANES = 128
SUBLANES = 8
VMEM_LIMIT = 56 * 1024 * 1024

OFF_Z = CONV_CH
OFF_A = OFF_Z + DN_W
OFF_B = OFF_A + HEADS
OFF_SQ = OFF_B + HEADS
OFF_SK = OFF_SQ + SWA_HEADS * HEAD_DIM
PROJ_SPLITS = (CONV_CH, DN_W, SWA_HEADS * HEAD_DIM, KV_W, KV_W, LANES)

ROW_TILE = 512
PROMPT_PROJ_TILE = 1024
DELTA_CHUNK = 64
DELTA_PROMPT = (4, 2)
DELTA_SAMPLE = (8, 2)
SWA_UNITS_PROMPT = 16
SWA_UNITS_SAMPLE = 8
SWA_BATCH = 4


def _layer_norm(x, g, b):
    mu = jnp.mean(x, axis=-1, keepdims=True)
    xc = x - mu
    var = jnp.mean(xc * xc, axis=-1, keepdims=True)
    return xc * lax.rsqrt(var + LN_EPS) * g + b


def _mm(a, b):
    return jnp.dot(a.astype(BF16), b.astype(BF16), preferred_element_type=F32)


def _mm_nt(a, b):
    return lax.dot_general(a.astype(BF16), b.astype(BF16), (((1,), (1,)), ((), ())),
                           preferred_element_type=F32)


def _split3(x):
    x1 = x.astype(BF16)
    r1 = x - x1.astype(F32)
    x2 = r1.astype(BF16)
    x3 = (r1 - x2.astype(F32)).astype(BF16)
    return x1, x2, x3


def _dot_exact_lhs(x, sel):
    return jnp.dot(jnp.concatenate(_split3(x), axis=1), jnp.concatenate([sel] * 3, axis=0),
                   preferred_element_type=F32)


CONV_BLOCK = 256


def _sigmoid(x):
    return 1.0 / (1.0 + jnp.exp(-x))


def _proj_kernel(x_ref, g_ref, b_ref, w_ref, wsq_ref, wkv_ref, wab_ref, conv0_ref, wconv_ref,
                 y_ref, z_ref, sq_ref, sk_ref, sv_ref, ab_ref, tail_ref, h_ref, xbuf, *, seq_len):
    tm = x_ref.shape[0]
    h_ref[...] = _layer_norm(x_ref[...], g_ref[...], b_ref[...])
    h = h_ref[...].astype(BF16)

    chained = seq_len is None
    n_seq = 1 if chained else tm // seq_len
    L = tm if chained else seq_len
    if chained:
        @pl.when(pl.program_id(0) == 0)
        def _():
            xbuf[0, 0:SUBLANES, :] = conv0_ref[0]
    else:
        for s in range(n_seq):
            xbuf[s, 0:SUBLANES, :] = conv0_ref[s]
    blocks = [slice(cb, cb + CONV_BLOCK) for cb in range(0, CONV_CH, CONV_BLOCK)]

    def project(cols):
        raw = _mm_nt(h, w_ref[cols, :])
        for s in range(n_seq):
            xbuf[s, SUBLANES:SUBLANES + L, cols] = raw[s * L:(s + 1) * L, :]

    def conv(cols):
        for s in range(n_seq):
            ext = xbuf[s, :, cols]
            w0, w1, w2, w3 = (0.5 * wconv_ref[j:j + 1, cols] for j in range(CONV_W))
            ext1 = pltpu.roll(ext, 1, axis=0)
            half = (w3 * ext + w2 * ext1)[SUBLANES:, :]
            half = half + pltpu.roll(w1 * ext + w0 * ext1, 2, axis=0)[SUBLANES:, :]
            y_ref[s * L:(s + 1) * L, cols] = half * (1.0 + jnp.tanh(half))
            tail_ref[s, :, cols] = xbuf[s, L:L + SUBLANES, cols]
            if chained:
                xbuf[s, 0:SUBLANES, cols] = xbuf[s, L:L + SUBLANES, cols]

    for o_ref, rhs in ((z_ref, w_ref[CONV_CH:, :]), (sq_ref, wsq_ref[...]),
                       (sk_ref, wkv_ref[:KV_W, :]), (sv_ref, wkv_ref[KV_W:, :]),
                       (ab_ref, wab_ref[...])):
        o_ref[...] = _mm_nt(h, rhs)
    project(blocks[0])
    for bi, cols in enumerate(blocks):
        if bi + 1 < len(blocks):
            project(blocks[bi + 1])
        conv(cols)


def _proj(x, g, b, weights, conv0, wconv, tm, seq_len):
    rows = x.shape[0]
    w_main, w_sq, w_kv, w_ab = weights
    chained = seq_len is None
    n_tile_seq = 1 if chained else tm // seq_len
    n_seq = 1 if chained else rows // seq_len
    const = lambda i: (0, 0)
    row_blk = lambda width: pl.BlockSpec((tm, width), lambda i: (i, 0))
    seq_blk = pl.BlockSpec((n_tile_seq, SUBLANES, CONV_CH),
                           (lambda i: (0, 0, 0)) if chained else (lambda i: (i, 0, 0)))
    return pl.pallas_call(
        functools.partial(_proj_kernel, seq_len=seq_len),
        out_shape=[jax.ShapeDtypeStruct((rows, width), F32) for width in PROJ_SPLITS]
                  + [jax.ShapeDtypeStruct((n_seq, SUBLANES, CONV_CH), F32),
                     jax.ShapeDtypeStruct((rows, D_MODEL), F32)],
        grid=(rows // tm,),
        in_specs=[row_blk(D_MODEL), pl.BlockSpec((1, D_MODEL), const),
                  pl.BlockSpec((1, D_MODEL), const)]
                 + [pl.BlockSpec(wt.shape, const, pipeline_mode=pl.Buffered(1)) for wt in weights]
                 + [seq_blk, pl.BlockSpec((SUBLANES, CONV_CH), const)],
        out_specs=[row_blk(width) for width in PROJ_SPLITS] + [seq_blk, row_blk(D_MODEL)],
        scratch_shapes=[pltpu.VMEM((n_tile_seq, SUBLANES + (tm if chained else seq_len), CONV_CH),
                                   F32)],
        compiler_params=pltpu.CompilerParams(dimension_semantics=("arbitrary",),
                                             vmem_limit_bytes=VMEM_LIMIT),
        name="ln_in_proj",
    )(x, g, b, w_main, w_sq, w_kv, w_ab, conv0, wconv)


GROUP_HEADS = 4


def _softplus(x):
    return jnp.maximum(x, 0.0) + jnp.log1p(jnp.exp(-jnp.abs(x)))


def _delta_kernel(qkv_ref, z_ref, ab_ref, s0_ref, alog_ref, dtb_ref,
                  normw_ref, seg_ref, expand_ref, o_ref, s_ref, sbd,
                  *, chunk, units, groups, sequential):
    C, U = chunk, units
    RG = U * C
    PAIRS = HEADS // GROUP_HEADS
    PW = GROUP_HEADS * HEAD_DIM
    step = pl.program_id(0)
    last_step = pl.num_programs(0) - 1
    expand = expand_ref[...]

    def seg_sum(t):
        tb = t.astype(BF16)
        return jnp.concatenate(
            [jnp.dot(tb[:, g * PW:(g + 1) * PW], seg_ref[g * PW:(g + 1) * PW, g * PW:(g + 1) * PW],
                     preferred_element_type=F32) for g in range(PAIRS)], axis=1)

    ri = lax.broadcasted_iota(jnp.int32, (C, PW), 0)
    ci = lax.broadcasted_iota(jnp.int32, (C, PW), 1) % HEAD_DIM
    causal = (ri >= ci) & (ci < C)
    strict = (ri > ci) & (ci < C)
    eye = (ri == ci).astype(F32)
    lane_head = lax.broadcasted_iota(jnp.int32, (1, PW), 1) // HEAD_DIM
    rs = lax.broadcasted_iota(jnp.int32, (LANES, LANES), 0)
    cs = lax.broadcasted_iota(jnp.int32, (LANES, LANES), 1)
    half_mask = ((rs // HEAD_DIM) == (cs // HEAD_DIM)).astype(F32)
    row_pad = [jnp.zeros((HEAD_DIM - C, PW), BF16)] if C < HEAD_DIM else []
    lane_pad = [jnp.zeros((1, HEAD_DIM - C), F32)] if C < HEAD_DIM else []
    rr = lax.broadcasted_iota(jnp.int32, (RG, RG), 0)
    cc = lax.broadcasted_iota(jnp.int32, (RG, RG), 1)
    tril = ((rr // C == cc // C) & (rr >= cc)).astype(BF16)

    def stack(y):
        yb = y.astype(BF16)
        zero = jnp.zeros_like(yb)
        pad = row_pad if y.shape[0] < HEAD_DIM else []
        blocks = []
        for h in range(GROUP_HEADS):
            blocks += [jnp.where(lane_head == h, yb, zero)] + pad
        return jnp.concatenate(blocks, axis=0)

    items = [(u, p) for u in range(U) for p in range(PAIRS)]
    sl = lambda u, p: (slice(u * C, (u + 1) * C), slice(p * PW, (p + 1) * PW))

    def front(gi):
        rg = slice(gi * RG, (gi + 1) * RG)
        q, k, v = (qkv_ref[rg, j * DN_W:(j + 1) * DN_W] for j in range(3))
        qn = q * (lax.rsqrt(seg_sum(q * q) + NORM_EPS) * HEAD_DIM ** -0.5)
        kn = k * lax.rsqrt(seg_sum(k * k) + NORM_EPS)
        yield

        ab = ab_ref[rg, :]
        g = -jnp.exp(alog_ref[...]) * _softplus(ab + dtb_ref[...])
        gs = jnp.dot(tril, jnp.concatenate(_split3(g), axis=1), preferred_element_type=F32)
        G = (gs[:, :LANES] + gs[:, LANES:2 * LANES] + gs[:, 2 * LANES:]) * LOG2E
        GT = G.T
        G_x = _dot_exact_lhs(G, expand)
        beta = _sigmoid(pltpu.roll(ab, LANES - HEADS, axis=1))
        b_hi = beta.astype(BF16)
        b_lo = (beta - b_hi.astype(F32)).astype(BF16)
        beta_x = jnp.dot(jnp.concatenate([b_hi, b_lo], axis=1),
                         jnp.concatenate([expand, expand], axis=0), preferred_element_type=F32)
        glast_x = jnp.concatenate(
            [jnp.broadcast_to(G_x[(u + 1) * C - 1:(u + 1) * C, :], (C, DN_W)) for u in range(U)],
            axis=0)
        yield
        eg_x = jnp.exp2(G_x)
        kbeta = kn * beta_x
        return dict(qn=qn, kn=kn, kbeta=kbeta, bk=kbeta * eg_x, bv=v * beta_x, qd=qn * eg_x,
                    kd=kn * jnp.exp2(glast_x - G_x), G_x=G_x, GT=GT, eg_x=eg_x)

    def middle(f):
        A, QK = {}, {}
        for (u, p) in items:
            ru, lp = sl(u, p)
            gr = []
            for h in range(p * GROUP_HEADS, (p + 1) * GROUP_HEADS):
                gr += [f["GT"][h:h + 1, ru]] + lane_pad
            gr = jnp.concatenate(gr, axis=1)
            decay = jnp.exp2(jnp.where(causal, f["G_x"][ru, lp] - gr, -jnp.inf))
            r = _mm_nt(jnp.concatenate([f["kbeta"][ru, lp], f["qn"][ru, lp]], axis=0),
                       stack(f["kn"][ru, lp]))
            A[u, p] = jnp.where(strict, r[:C] * decay, 0.0)
            QK[u, p] = r[C:] * decay
        yield

        n_pow = int(math.log2(C)) - 1
        P = {it: eye - A[it] for it in items}
        Ap = {it: _mm(A[it], stack(A[it])) for it in items}
        yield
        for i in range(n_pow):
            if i < n_pow - 1:
                r = {it: _mm(jnp.concatenate([Ap[it], P[it]], axis=0), stack(Ap[it])) for it in items}
                Ap = {it: r[it][:C] for it in items}
                P = {it: P[it] + r[it][C:] for it in items}
            else:
                P = {it: P[it] + _mm(P[it], stack(Ap[it])) for it in items}
            yield

        halves = [slice(hb * LANES, (hb + 1) * LANES) for hb in range(PW // LANES)]
        wu = {it: _mm(P[it], jnp.concatenate([stack(f["bk"][sl(*it)]), stack(f["bv"][sl(*it)])],
                                             axis=1)) for it in items}
        kdT = {(it, hb): f["kd"][sl(*it)][:, cols].T for it in items
               for hb, cols in enumerate(halves)}
        yield
        mn = {(it, hb): _mm(kdT[it, hb], jnp.concatenate(
            [wu[it][:, cols], wu[it][:, PW:][:, cols]], axis=1))
              for it in items for hb, cols in enumerate(halves)}
        yield
        n_t = {key: (half_mask * mn[key][:, LANES:]).T for key in mn}
        out = {}
        for (u, p) in items:
            ru, lp = sl(u, p)
            it = (u, p)
            out[it] = dict(
                W=wu[it][:, :PW], U=wu[it][:, PW:],
                M=diag_blocks([half_mask * mn[it, hb][:, :LANES]
                               for hb in range(len(halves))]).astype(BF16),
                NT=jnp.concatenate([n_t[it, hb][:HEAD_DIM] + n_t[it, hb][HEAD_DIM:]
                                    for hb in range(len(halves))], axis=1),
                QK=QK[it], qd=f["qd"][ru, lp],
                eg_last=f["eg_x"][(u + 1) * C - 1:(u + 1) * C, lp])
        return out

    def diag_blocks(blocks):
        n = len(blocks)
        rows = []
        for i, blk in enumerate(blocks):
            rows.append(jnp.concatenate(
                [blk if j == i else jnp.zeros_like(blk) for j in range(n)], axis=1))
        return jnp.concatenate(rows, axis=0)

    def load_state(ref, prefix, p):
        return jnp.concatenate([ref[prefix + (p * GROUP_HEADS + h,)].T
                                for h in range(GROUP_HEADS)], axis=1)

    def store_state(ref, prefix, p, st):
        for h in range(GROUP_HEADS):
            ref[prefix + (p * GROUP_HEADS + h,)] = st[:, h * HEAD_DIM:(h + 1) * HEAD_DIM].T

    chunks = [(gi, u) for gi in range(groups) for u in range(U)]
    if sequential:
        @pl.when(step == 0)
        def _():
            for p in range(PAIRS):
                sbd[p] = load_state(s0_ref, (), p)
        s_start = {p: sbd[p] for p in range(PAIRS)}
    else:
        s_start = {(idx, p): load_state(s0_ref, (idx,), p)
                   for idx in range(len(chunks)) for p in range(PAIRS)}

    def chain(gi, out, s_in):
        if sequential:
            states = {p: [s_in[p]] for p in range(PAIRS)}
            for u in range(U):
                for p in range(PAIRS):
                    d, s_cur = out[u, p], states[p][-1]
                    states[p].append(s_cur * d["eg_last"] - _mm_nt(s_cur, d["M"]) + d["NT"])
            return ({(u, p): states[p][u] for (u, p) in items},
                    {p: states[p][-1] for p in range(PAIRS)})
        entering = {(u, p): s_in[gi * U + u, p] for (u, p) in items}
        s_out = {}
        for (u, p) in items:
            d, s_cur = out[u, p], entering[u, p]
            s_out[gi * U + u, p] = s_cur * d["eg_last"] - _mm_nt(s_cur, d["M"]) + d["NT"]
        return entering, s_out

    def finish(gi, out, entering):
        r = {it: _mm_nt(jnp.concatenate([out[it]["W"], out[it]["qd"]], axis=0),
                        stack(entering[it])) for it in items}
        yield
        tiles = {it: r[it][C:] + _mm(out[it]["QK"], stack(out[it]["U"] - r[it][:C]))
                 for it in items}
        yield
        rg = slice(gi * RG, (gi + 1) * RG)
        o = jnp.concatenate([jnp.concatenate([tiles[u, p] for p in range(PAIRS)], axis=1)
                             for u in range(U)], axis=0)
        ms = seg_sum(o * o) * (1.0 / HEAD_DIM)
        hz = 0.5 * z_ref[rg, :]
        o_ref[rg, :] = (o * lax.rsqrt(ms + NORM_EPS) * normw_ref[...]
                        * (hz * (1.0 + jnp.tanh(hz)))).astype(BF16)

    def drive(gens):
        results = [None] * len(gens)
        active = list(enumerate(gens))
        while active:
            still = []
            for i, gen in active:
                try:
                    next(gen)
                    still.append((i, gen))
                except StopIteration as stop:
                    results[i] = stop.value
            active = still
        return results

    state = s_start
    s_final = {}
    f_cur = drive([front(0)])[0]
    pending = None
    for gi in range(groups + 1):
        gens = []
        if gi < groups:
            gens.append(middle(f_cur))
        if gi + 1 < groups:
            gens.append(front(gi + 1))
        if pending is not None:
            entering, s_out = chain(gi - 1, pending, state)
            s_final.update(s_out)
            state = s_out if sequential else state
            gens.append(finish(gi - 1, pending, entering))
        res = drive(gens)
        pending = res[0] if gi < groups else None
        f_cur = res[1] if gi + 1 < groups else None

    if sequential:
        for p in range(PAIRS):
            sbd[p] = s_final[p]

        @pl.when(step == last_step)
        def _():
            for p in range(PAIRS):
                store_state(s_ref, (), p, s_final[p])
    else:
        for (idx, p), s_new in s_final.items():
            store_state(s_ref, (idx,), p, s_new)


def _delta(qkv, z, ab, s0, consts, n_seq, seq_len, chunk, units, groups):
    alog, dtb, normw, seg, expand = consts
    sequential = n_seq == 1
    rows = n_seq * seq_len
    n_chunks = units * groups
    blk_rows = n_chunks * chunk
    assert rows % blk_rows == 0 and (sequential or seq_len == chunk)
    const2 = lambda shape: pl.BlockSpec(shape, lambda i: (0, 0))
    tok = lambda width: pl.BlockSpec((blk_rows, width), lambda i: (i, 0))
    group_w = GROUP_HEADS * HEAD_DIM
    if sequential:
        state = pl.BlockSpec((None, HEADS, HEAD_DIM, HEAD_DIM), lambda i: (0, 0, 0, 0))
    else:
        state = pl.BlockSpec((n_chunks, HEADS, HEAD_DIM, HEAD_DIM), lambda i: (i, 0, 0, 0))
    return pl.pallas_call(
        functools.partial(_delta_kernel, chunk=chunk, units=units, groups=groups,
                          sequential=sequential),
        out_shape=[jax.ShapeDtypeStruct((rows, DN_W), BF16),
                   jax.ShapeDtypeStruct((n_seq, HEADS, HEAD_DIM, HEAD_DIM), F32)],
        grid=(rows // blk_rows,),
        in_specs=[tok(CONV_CH), tok(DN_W), tok(LANES), state,
                  const2((1, LANES)), const2((1, LANES)),
                  const2((1, DN_W)), const2((DN_W, DN_W)), const2((LANES, DN_W))],
        out_specs=[tok(DN_W), state],
        scratch_shapes=[pltpu.VMEM((HEADS // GROUP_HEADS, HEAD_DIM, group_w), F32)],
        compiler_params=pltpu.CompilerParams(
            dimension_semantics=("arbitrary",), vmem_limit_bytes=VMEM_LIMIT,
        ),
        name=f"delta_c{chunk}",
    )(qkv, z, ab, s0, alog, dtb, normw, seg, expand)


SWA_W = SWA_HEADS * HEAD_DIM


def _swa_kernel(*refs, n_q, units, prompt, batch=SWA_BATCH):
    if prompt:
        (table_ref, sinks_ref, bucket_ref, q_ref, km_ref, kp_ref, kc_ref, vm_ref, vp_ref, vc_ref,
         o_ref, bias_ref) = refs
    else:
        (table_ref, sinks_ref, bucket_ref, q_ref, kp_ref, kc_ref, vp_ref, vc_ref,
         o_ref, bias_ref) = refs
    n_k = bias_ref.shape[-1]
    n_grp = SWA_W // KV_W
    step = pl.program_id(0)

    @pl.when(step == 0)
    def _():
        bucket = bucket_ref[...]
        for h in range(SWA_HEADS):
            acc = jnp.zeros((n_q, n_k), F32)
            for b in range(N_BUCKETS):
                acc = jnp.where(bucket == b, table_ref[b * SWA_HEADS + h], acc)
            kv, gi = divmod(h, SWA_G)
            bias_ref[kv, gi * n_q:(gi + 1) * n_q, :] = acc * LOG2E

    lane = lax.broadcasted_iota(jnp.int32, (1, KV_W), 1)
    kv_mask = [(lane < HEAD_DIM).astype(F32), (lane >= HEAD_DIM).astype(F32)]
    if prompt:
        first = step == 0
        k_all = jnp.concatenate([jnp.where(first, km_ref[...], kp_ref[...]), kc_ref[...]], axis=0)
        v_all = jnp.concatenate([jnp.where(first, vm_ref[...], vp_ref[...]), vc_ref[...]], axis=0)
        k_kv = [(k_all * m).astype(BF16) for m in kv_mask]
        v_kv = [(v_all * m).astype(BF16) for m in kv_mask]
        keys = lambda j, kv: k_kv[kv][j * n_q:j * n_q + n_k]
        vals = lambda j, kv: v_kv[kv][j * n_q:j * n_q + n_k]
    else:
        n_c = n_k - n_q
        sub = lax.broadcasted_iota(jnp.int32, (KV_W, 1), 0)
        row_mask = [(sub < HEAD_DIM).astype(F32), (sub >= HEAD_DIM).astype(F32)]
        past = lambda ref, j, kv: (ref[j * KV_W:(j + 1) * KV_W, :] * row_mask[kv]).astype(BF16)
        new = lambda ref, j, kv: (ref[j * n_q:(j + 1) * n_q, :] * kv_mask[kv]).astype(BF16)
        ones_c, ones_q = jnp.ones((KV_W, n_c), BF16), jnp.ones((n_q, KV_W), BF16)

    q = (q_ref[...] * (HEAD_DIM ** -0.5 * LOG2E)).astype(BF16)
    ones_k = jnp.ones((n_k, KV_W), BF16)
    sink = [jnp.concatenate([jnp.full((n_q, KV_W), sinks_ref[kv * SWA_G + gi] * LOG2E, F32)
                             for gi in range(SWA_G)], axis=0) for kv in range(SWA_KV)]
    rows4 = SWA_G * n_q
    for j0 in range(0, units, batch):
        js = range(j0, min(j0 + batch, units))
        items = [(j, kv) for j in js for kv in range(SWA_KV)]
        s = {}
        for j in js:
            rows = slice(j * n_q, (j + 1) * n_q)
            q4 = jnp.concatenate([q[rows, g * KV_W:(g + 1) * KV_W] for g in range(n_grp)], axis=0)
            for kv in range(SWA_KV):
                if prompt:
                    sj = _mm_nt(q4, keys(j, kv))
                else:
                    sj = jnp.concatenate([_mm(q4, past(kp_ref, j, kv)),
                                          _mm_nt(q4, new(kc_ref, j, kv))], axis=1)
                sj = sj + bias_ref[kv]
                if prompt and j * n_q < WINDOW - N_META:
                    key_pos = j * n_q + lax.broadcasted_iota(jnp.int32, (1, n_k), 1)
                    n_invalid = jnp.where(first, WINDOW - N_META, 0)
                    sj = jnp.where(key_pos < n_invalid, -jnp.inf, sj)
                s[j, kv] = sj
        m = {it: jnp.maximum(jnp.broadcast_to(jnp.max(s[it], axis=-1, keepdims=True),
                                              (rows4, KV_W)), sink[it[1]]) for it in items}
        p = {it: jnp.exp2(s[it] - jnp.concatenate([m[it], m[it][:, :n_k - KV_W]], axis=1))
             for it in items}
        if prompt:
            pv = {it: _mm(p[it], jnp.concatenate([vals(*it), ones_k], axis=1)) for it in items}
        else:
            pb = {it: p[it].astype(BF16) for it in items}
            pv = {it: _mm_nt(pb[it][:, :n_c], jnp.concatenate([past(vp_ref, *it), ones_c], axis=0))
                  + _mm(pb[it][:, n_c:], jnp.concatenate([new(vc_ref, *it), ones_q], axis=1))
                  for it in items}
        o = {it: pv[it][:, :KV_W] * (1.0 / (pv[it][:, KV_W:] + jnp.exp2(sink[it[1]] - m[it])))
             for it in items}
        for j in js:
            o2 = o[j, 0] + o[j, 1]
            for g in range(n_grp):
                o_ref[j * n_q:(j + 1) * n_q, g * KV_W:(g + 1) * KV_W] = (
                    o2[g * n_q:(g + 1) * n_q, :].astype(BF16))


def _swa(table, sinks, bucket, q, kv_args, kv_specs, n_steps, n_q, units, prompt, name):
    n_k = bucket.shape[1]
    smem = pl.BlockSpec(memory_space=pltpu.SMEM)
    rows = units * n_q
    return pl.pallas_call(
        functools.partial(_swa_kernel, n_q=n_q, units=units, prompt=prompt),
        out_shape=jax.ShapeDtypeStruct((n_steps * rows, SWA_W), BF16),
        grid=(n_steps,),
        in_specs=[smem, smem, pl.BlockSpec((n_q, n_k), lambda i: (0, 0)),
                  pl.BlockSpec((rows, SWA_W), lambda i: (i, 0))] + list(kv_specs),
        out_specs=pl.BlockSpec((rows, SWA_W), lambda i: (i, 0)),
        scratch_shapes=[pltpu.VMEM((SWA_KV, SWA_G * n_q, n_k), F32)],
        compiler_params=pltpu.CompilerParams(dimension_semantics=("arbitrary",),
                                             vmem_limit_bytes=VMEM_LIMIT),
        name=name,
    )(table, sinks, bucket, q, *kv_args)


def _t5_bucket(rel):
    half = N_BUCKETS // 2
    max_exact = half // 2
    n = jnp.abs(rel)
    large = max_exact + (jnp.log(jnp.maximum(n, 1).astype(F32) / max_exact)
                         / math.log(MAX_DIST / max_exact) * (half - max_exact)).astype(jnp.int32)
    large = jnp.minimum(large, half - 1)
    return jnp.where(rel > 0, half, 0) + jnp.where(n < max_exact, n, large)


def _bucket_map(n_q, n_k, key_offset):
    rel = (jnp.arange(n_k)[None, :] - key_offset) - jnp.arange(n_q)[:, None]
    return _t5_bucket(rel).astype(jnp.int32)


FF_BLOCK = 1024


FF_STAGE = 512


def _post_kernel(hp_ref, dnp_ref, swp_ref, hs_ref, dns_ref, sws_ref,
                 wod_ref, wos_ref, g1_ref, b1_ref, w1_hbm, w2_hbm, g2_ref, b2_ref,
                 yp_ref, ys_ref, w1_ref, w2_ref, stage1, stage2, sems, *, n_prompt_tiles):
    step = pl.program_id(0)
    is_prompt = step < n_prompt_tiles
    n_stage = D_FF // FF_STAGE

    def copies(c):
        slot, blk = c % 2, pl.ds(c * FF_STAGE, FF_STAGE)
        return (pltpu.make_async_copy(w1_hbm.at[0, :, blk], stage1.at[slot], sems.at[0, slot]),
                pltpu.make_async_copy(w2_hbm.at[0, blk, :], stage2.at[slot], sems.at[1, slot]))

    def start(c):
        for cp in copies(c):
            cp.start()

    def fetch(j):
        per = FF_BLOCK // FF_STAGE
        for c in range(j * per, (j + 1) * per):
            for cp in copies(c):
                cp.wait()
            w1_ref[:, c * FF_STAGE:(c + 1) * FF_STAGE] = stage1[c % 2].astype(BF16)
            w2_ref[c * FF_STAGE:(c + 1) * FF_STAGE, :] = stage2[c % 2].astype(BF16)
            if c + 2 < n_stage:
                start(c + 2)

    @pl.when(step == 0)
    def _():
        start(0)
        start(1)
        _post_tile(hp_ref, dnp_ref, swp_ref, wod_ref, wos_ref, g1_ref, b1_ref,
                   w1_ref, w2_ref, g2_ref, b2_ref, yp_ref, fetch)

    @pl.when(jnp.logical_and(step > 0, is_prompt))
    def _():
        _post_tile(hp_ref, dnp_ref, swp_ref, wod_ref, wos_ref, g1_ref, b1_ref,
                   w1_ref, w2_ref, g2_ref, b2_ref, yp_ref)

    @pl.when(jnp.logical_not(is_prompt))
    def _():
        _post_tile(hs_ref, dns_ref, sws_ref, wod_ref, wos_ref, g1_ref, b1_ref,
                   w1_ref, w2_ref, g2_ref, b2_ref, ys_ref)


def _post_tile(h_ref, dn_ref, sw_ref, wod_ref, wos_ref, g1_ref, b1_ref,
               w1_ref, w2_ref, g2_ref, b2_ref, y_ref, fetch=None):
    mix = (jnp.dot(dn_ref[...], wod_ref[...], preferred_element_type=F32)
           + jnp.dot(sw_ref[...], wos_ref[...], preferred_element_type=F32))
    h1 = _layer_norm(DEEP_ALPHA * h_ref[...] + mix, g1_ref[...], b1_ref[...])
    h1b = h1.astype(BF16)
    f = jnp.zeros_like(h1)
    for j in range(D_FF // FF_BLOCK):
        if fetch is not None:
            fetch(j)
        a = jnp.dot(h1b, w1_ref[:, j * FF_BLOCK:(j + 1) * FF_BLOCK], preferred_element_type=F32)
        a = jnp.square(jnp.maximum(a, 0.0)).astype(BF16)
        f = f + jnp.dot(a, w2_ref[j * FF_BLOCK:(j + 1) * FF_BLOCK, :], preferred_element_type=F32)
    y_ref[...] = _layer_norm(DEEP_ALPHA * h1 + f, g2_ref[...], b2_ref[...])


def _post(prompt, sample, vecs, weights, tm):
    g1, b1, g2, b2 = vecs
    wo_dn, wo_sw, w1, w2 = weights
    n_p, n_s = prompt[0].shape[0] // tm, sample[0].shape[0] // tm
    assert n_p >= 1
    hbm = pl.BlockSpec(memory_space=pl.ANY)
    const = lambda i: (0, 0)
    vec = pl.BlockSpec((1, D_MODEL), const)
    weight = lambda wt: pl.BlockSpec(wt.shape, const, pipeline_mode=pl.Buffered(1))
    blk_p = lambda width: pl.BlockSpec((tm, width), lambda i: (jnp.minimum(i, n_p - 1), 0))
    blk_s = lambda width: pl.BlockSpec((tm, width), lambda i: (jnp.maximum(i - n_p, 0), 0))
    widths = (D_MODEL, DN_W, SWA_W)
    return pl.pallas_call(
        functools.partial(_post_kernel, n_prompt_tiles=n_p),
        out_shape=[jax.ShapeDtypeStruct((n_p * tm, D_MODEL), F32),
                   jax.ShapeDtypeStruct((n_s * tm, D_MODEL), F32)],
        grid=(n_p + n_s,),
        in_specs=[blk_p(wd) for wd in widths] + [blk_s(wd) for wd in widths]
                 + [weight(wo_dn), weight(wo_sw), vec, vec, hbm, hbm, vec, vec],
        out_specs=[blk_p(D_MODEL), blk_s(D_MODEL)],
        scratch_shapes=[pltpu.VMEM((D_MODEL, D_FF), BF16), pltpu.VMEM((D_FF, D_MODEL), BF16),
                        pltpu.VMEM((2, D_MODEL, FF_STAGE), F32),
                        pltpu.VMEM((2, FF_STAGE, D_MODEL), F32),
                        pltpu.SemaphoreType.DMA((2, 2))],
        compiler_params=pltpu.CompilerParams(dimension_semantics=("arbitrary",),
                                             vmem_limit_bytes=VMEM_LIMIT),
        name="post_mlp",
    )(*prompt, *sample, wo_dn, wo_sw, g1, b1, w1, w2, g2, b2)


def kernel(x_prompt, x_sample, state_delta, state_conv, cache_swa_k, cache_swa_v, meta_tokens, ln_in_g, ln_in_b, w_in, w_conv, dn_a_log, dn_dt_bias, dn_norm, swa_sinks, rel_bias_table, w_o, ln1_g, ln1_b, w_ff1, w_ff2, ln2_g, ln2_b):
    assert w_in.shape[0] == DEPTH == 1
    n_seq_s, len_s = x_sample.shape[0], x_sample.shape[1]
    len_p = x_prompt.shape[1]
    cache_len = cache_swa_k.shape[2]
    chunk_p = DELTA_CHUNK
    row = lambda t: t.reshape(1, -1).astype(F32)

    w = jnp.transpose(w_in[0])
    w_sq = (w[OFF_SQ:OFF_SK].astype(BF16).reshape(SWA_KV, SWA_G, HEAD_DIM, D_MODEL)
            .transpose(1, 0, 2, 3).reshape(SWA_W, D_MODEL))
    w_r = (w[:OFF_A].astype(BF16), w_sq, w[OFF_SK:].astype(BF16),
           jnp.pad(w[OFF_A:OFF_SQ].astype(BF16), ((0, LANES - 2 * HEADS), (0, 0))))
    gin, bin_ = row(ln_in_g), row(ln_in_b)
    pad_lanes = lambda t: jnp.pad(t.reshape(1, -1).astype(F32), ((0, 0), (0, LANES - HEADS)))
    lane_head = np.arange(DN_W) // HEAD_DIM
    seg = jnp.asarray(lane_head[:, None] == lane_head[None, :], BF16)
    expand = jnp.asarray(np.arange(LANES)[:, None] == lane_head[None, :], BF16)
    wconv = jnp.pad(w_conv[0].astype(F32), ((0, SUBLANES - CONV_W), (0, 0)))
    delta_consts = (pad_lanes(dn_a_log[0]), pad_lanes(dn_dt_bias[0]),
                    jnp.tile(dn_norm[0].astype(F32), HEADS).reshape(1, DN_W), seg, expand)
    post_vecs = (row(ln1_g[0]), row(ln1_b[0]), row(ln2_g[0]), row(ln2_b[0]))
    wo_sw = (w_o[0][DN_W:].astype(BF16).reshape(SWA_KV, SWA_G, HEAD_DIM, D_MODEL)
             .transpose(1, 0, 2, 3).reshape(SWA_W, D_MODEL))
    post_w = (w_o[0][:DN_W].astype(BF16), wo_sw, w_ff1.astype(F32), w_ff2.astype(F32))
    table = rel_bias_table.astype(F32).reshape(-1)
    sinks = swa_sinks[0].astype(F32)

    xp = x_prompt[0]
    xs = x_sample.reshape(n_seq_s * len_s, D_MODEL)

    conv_pad = SUBLANES - (CONV_W - 1)
    zero_conv = jnp.zeros((1, SUBLANES, CONV_CH), F32)
    conv_s = jnp.pad(state_conv[0].astype(F32), ((0, 0), (conv_pad, 0), (0, 0)))
    m_qkv, m_z, m_sq, m_sk, m_sv, m_ab, m_tail, _ = _proj(
        meta_tokens.astype(F32), gin, bin_, w_r, zero_conv, wconv, N_META, N_META)
    p_qkv, p_z, p_sq, p_sk, p_sv, p_ab, p_tail, p_h = _proj(
        xp, gin, bin_, w_r, m_tail, wconv, PROMPT_PROJ_TILE, None)
    s_qkv, s_z, s_sq, s_sk, s_sv, s_ab, s_tail, s_h = _proj(
        xs, gin, bin_, w_r, conv_s, wconv, ROW_TILE, len_s)

    zero_s = jnp.zeros((1, HEADS, HEAD_DIM, HEAD_DIM), F32)
    _, s_meta = _delta(m_qkv, m_z, m_ab, zero_s, delta_consts, 1, N_META, N_META, 1, 1)
    p_dn, p_s = _delta(p_qkv, p_z, p_ab, s_meta, delta_consts, 1, len_p, chunk_p, *DELTA_PROMPT)
    s_dn, s_s = _delta(s_qkv, s_z, s_ab, state_delta[0].astype(F32), delta_consts,
                       n_seq_s, len_s, len_s, *DELTA_SAMPLE)

    span = WINDOW
    lead = jnp.zeros((span - N_META, KV_W), F32)
    units_p = SWA_UNITS_PROMPT
    rows_p = units_p * chunk_p
    prev_blocks = rows_p // span
    kv_specs_p = [pl.BlockSpec((span, KV_W), lambda i: (0, 0)),
                  pl.BlockSpec((span, KV_W), lambda i: (jnp.maximum(i * prev_blocks - 1, 0), 0)),
                  pl.BlockSpec((rows_p, KV_W), lambda i: (i, 0))] * 2
    p_sw = _swa(table, sinks, _bucket_map(chunk_p, span + chunk_p, span), p_sq,
                [jnp.concatenate([lead, m_sk], axis=0), p_sk, p_sk,
                 jnp.concatenate([lead, m_sv], axis=0), p_sv, p_sv],
                kv_specs_p, len_p // rows_p, chunk_p, units_p, True, "swa_prompt")
    cache_t = lambda c: (jnp.transpose(c[0].astype(F32), (0, 2, 3, 1))
                         .reshape(n_seq_s * KV_W, cache_len))
    ck, cv = cache_t(cache_swa_k), cache_t(cache_swa_v)
    units_s = SWA_UNITS_SAMPLE
    kv_specs_s = [pl.BlockSpec((units_s * KV_W, cache_len), lambda i: (i, 0)),
                  pl.BlockSpec((units_s * len_s, KV_W), lambda i: (i, 0))] * 2
    s_sw = _swa(table, sinks, _bucket_map(len_s, cache_len + len_s, cache_len), s_sq,
                [ck, s_sk, cv, s_sv], kv_specs_s, n_seq_s // units_s, len_s, units_s, False,
                "swa_sample")

    y_p, y_s = _post((p_h, p_dn, p_sw), (s_h, s_dn, s_sw), post_vecs, post_w, ROW_TILE)

    kv_shape = lambda n, length: (1, n, length, SWA_KV, HEAD_DIM)
    return (y_p[None], y_s.reshape(x_sample.shape),
            p_s[None], p_tail[:, -(CONV_W - 1):][None],
            p_sk[-WINDOW:].reshape(kv_shape(1, WINDOW)), p_sv[-WINDOW:].reshape(kv_shape(1, WINDOW)),
            s_s[None], s_tail[:, -(CONV_W - 1):][None],
            s_sk.reshape(kv_shape(n_seq_s, len_s)), s_sv.reshape(kv_shape(n_seq_s, len_s)))
```
